```python
import math
import jax, jax.numpy as jnp
from jax import lax
import numpy as np

D_MODEL = 1024
BATCH = 2
SEQ = 8192
DEPTH = 4

N_MIXERS = 2
N_CONV_LAYERS = (DEPTH + N_MIXERS - 1) // N_MIXERS
N_ATTN_LAYERS = DEPTH // N_MIXERS
CONV_WIDTH = 3
ATT_HEADS = 8
ATT_HEAD_DIM = D_MODEL // (2 * ATT_HEADS)
Q_BLOCK = 128
PEER_HEADS = 8
PEER_KEYS = 128
PEER_N_EXPERTS = PEER_KEYS * PEER_KEYS
PEER_TOPK = 16
PEER_QK_DIM = 256
PEER_HALF = PEER_QK_DIM // 2
PEER_CHUNK = 128
DEEPNORM_ALPHA = (2.0 * DEPTH) ** 0.25
DEEPNORM_BETA = (8.0 * DEPTH) ** -0.25
LN_EPS = 1e-5

kernel_name = "hybrid_shortconv_diffattn_peer_deepnorm_adaln"


def layer_norm(x, g, b):
    xf = x.astype(jnp.float32)
    mu = jnp.mean(xf, axis=-1, keepdims=True)
    var = jnp.mean(jnp.square(xf - mu), axis=-1, keepdims=True)
    return ((xf - mu) * lax.rsqrt(var + LN_EPS)).astype(x.dtype) * g + b


def rms_norm(x, g):
    xf = x.astype(jnp.float32)
    return (xf * lax.rsqrt(jnp.mean(xf * xf, axis=-1, keepdims=True) + LN_EPS)).astype(x.dtype) * g


def ada_mod(c, w, b):
    m = jax.nn.silu(c) @ w + b
    shift, scale, gate = jnp.split(m, 3, axis=-1)
    return shift[:, None, :], 1.0 + scale[:, None, :], 1.0 + gate[:, None, :]


def lambda_init_fn(layer_idx):
    return 0.8 - 0.6 * math.exp(-0.3 * layer_idx)


def short_conv_mixer(h, w_in, conv_w, w_out):
    S = h.shape[1]
    gb, gc, xv = jnp.split(h @ w_in, 3, axis=-1)
    z = gc * xv
    zp = jnp.pad(z, ((0, 0), (CONV_WIDTH - 1, 0), (0, 0)))
    zc = conv_w[0] * zp[:, 0:S] + conv_w[1] * zp[:, 1:S + 1] + conv_w[2] * zp[:, 2:S + 2]
    return (gb * zc) @ w_out


def diff_attention(h, w_qkv, lam, subln_g, w_o, lambda_init):
    B, S, _ = h.shape
    H, dh = ATT_HEADS, ATT_HEAD_DIM
    q, k, v = jnp.split(h @ w_qkv, 3, axis=-1)
    q = q.reshape(B, S, H, 2, dh) * (dh ** -0.5)
    k = k.reshape(B, S, H, 2, dh)
    v = v.reshape(B, S, H, 2 * dh)
    lam_f = lam.astype(jnp.float32)
    lam_full = (jnp.exp(jnp.sum(lam_f[0] * lam_f[1])) - jnp.exp(jnp.sum(lam_f[2] * lam_f[3]))
                + lambda_init)
    slopes = 2.0 ** (-(8.0 / H) * jnp.arange(1, H + 1, dtype=jnp.float32))
    kpos = jnp.arange(S)
    nb = S // Q_BLOCK
    qb = q.reshape(B, nb, Q_BLOCK, H, 2, dh).transpose(1, 0, 2, 3, 4, 5)

    def block(args):
        qblk, bi = args
        qpos = bi * Q_BLOCK + jnp.arange(Q_BLOCK)
        s = jnp.einsum('bqhpd,bkhpd->bhpqk', qblk, k).astype(jnp.float32)
        dist = (qpos[:, None] - kpos[None, :]).astype(jnp.float32)
        s = s - slopes[None, :, None, None, None] * dist
        s = jnp.where(dist >= 0, s, -jnp.inf)
        p = jax.nn.softmax(s, axis=-1)
        a = p[:, :, 0] - lam_full * p[:, :, 1]
        return jnp.einsum('bhqk,bkhe->bqhe', a.astype(v.dtype), v)

    o = lax.map(block, (qb, jnp.arange(nb)))
    o = o.transpose(1, 0, 2, 3, 4).reshape(B, S, H, 2 * dh)
    o = rms_norm(o, subln_g) * (1.0 - lambda_init)
    return o.reshape(B, S, H * 2 * dh) @ w_o


def peer_ffn(h, w_q, sub_keys, u, v):
    B, S, D = h.shape
    T = B * S
    hc = h.reshape(T // PEER_CHUNK, PEER_CHUNK, D)

    def chunk(xc):
        q = (xc @ w_q).reshape(PEER_CHUNK, PEER_HEADS, 2, PEER_HALF)
        s = jnp.einsum('thpd,pnd->thpn', q, sub_keys).astype(jnp.float32)
        sv, si = lax.top_k(s, PEER_TOPK)
        cand_s = (sv[:, :, 0, :, None] + sv[:, :, 1, None, :]).reshape(
            PEER_CHUNK, PEER_HEADS, PEER_TOPK * PEER_TOPK)
        cand_i = (si[:, :, 0, :, None] * PEER_KEYS + si[:, :, 1, None, :]).reshape(
            PEER_CHUNK, PEER_HEADS, PEER_TOPK * PEER_TOPK)
        top_s, pos = lax.top_k(cand_s, PEER_TOPK)
        idx = jnp.take_along_axis(cand_i, pos, axis=-1)
        g = jax.nn.softmax(top_s, axis=-1).astype(xc.dtype)
        act = jax.nn.gelu(jnp.einsum('td,thkd->thk', xc, u[idx]), approximate=False)
        return jnp.einsum('thk,thkd->td', g * act, v[idx])

    return lax.map(chunk, hc).reshape(B, S, D)


def setup_inputs(seed: int = 0) -> dict:
    key = jax.random.key(seed)
    ks = jax.random.split(key, 20)
    D = D_MODEL
    nrm = jax.random.normal
    f32 = jnp.float32
    x = nrm(ks[0], (BATCH, SEQ, D), f32)
    c = nrm(ks[1], (BATCH, D), f32)
    ada_w = nrm(ks[2], (DEPTH, 2, D, 3 * D), f32) * (0.1 * D ** -0.5)
    ada_b = nrm(ks[3], (DEPTH, 2, 3 * D), f32) * 0.02
    ln_g = 1.0 + 0.02 * nrm(ks[4], (DEPTH, 2, D), f32)
    ln_b = 0.02 * nrm(ks[5], (DEPTH, 2, D), f32)
    val_scale = jnp.concatenate([jnp.ones((2 * D,), f32), jnp.full((D,), DEEPNORM_BETA, f32)])
    conv_w_in = nrm(ks[6], (N_CONV_LAYERS, D, 3 * D), f32) * (D ** -0.5) * val_scale
    conv_w = nrm(ks[7], (N_CONV_LAYERS, CONV_WIDTH, D), f32) * (CONV_WIDTH ** -0.5)
    conv_w_out = nrm(ks[8], (N_CONV_LAYERS, D, D), f32) * (D ** -0.5) * DEEPNORM_BETA
    attn_w_qkv = nrm(ks[9], (N_ATTN_LAYERS, D, 3 * D), f32) * (D ** -0.5) * val_scale
    attn_lambda = 0.1 * nrm(ks[10], (N_ATTN_LAYERS, 4, ATT_HEAD_DIM), f32)
    attn_subln_g = 1.0 + 0.02 * nrm(ks[11], (N_ATTN_LAYERS, 2 * ATT_HEAD_DIM), f32)
    attn_w_o = nrm(ks[12], (N_ATTN_LAYERS, D, D), f32) * (D ** -0.5) * DEEPNORM_BETA
    peer_w_q = nrm(ks[13], (DEPTH, D, PEER_HEADS * PEER_QK_DIM), f32) * (D ** -0.5)
    peer_sub_keys = nrm(ks[14], (DEPTH, 2, PEER_KEYS, PEER_HALF), f32) * (PEER_HALF ** -0.5)
    peer_u = nrm(ks[15], (DEPTH, PEER_N_EXPERTS, D), f32) * (D ** -0.5) * DEEPNORM_BETA
    peer_v = nrm(ks[16], (DEPTH, PEER_N_EXPERTS, D), f32) * (PEER_HEADS ** -0.5) * DEEPNORM_BETA
    return {"x": x, "c": c, "ada_w": ada_w, "ada_b": ada_b, "ln_g": ln_g, "ln_b": ln_b,
            "conv_w_in": conv_w_in, "conv_w": conv_w, "conv_w_out": conv_w_out,
            "attn_w_qkv": attn_w_qkv, "attn_lambda": attn_lambda,
            "attn_subln_g": attn_subln_g, "attn_w_o": attn_w_o,
            "peer_w_q": peer_w_q, "peer_sub_keys": peer_sub_keys,
            "peer_u": peer_u, "peer_v": peer_v}


def reference(x, c, ada_w, ada_b, ln_g, ln_b, conv_w_in, conv_w, conv_w_out,
              attn_w_qkv, attn_lambda, attn_subln_g, attn_w_o,
              peer_w_q, peer_sub_keys, peer_u, peer_v):
    for i in range(DEPTH):
        j = i // N_MIXERS
        shift, scale, gate = ada_mod(c, ada_w[i, 0], ada_b[i, 0])
        h = x * scale + shift
        if i % N_MIXERS == 0:
            f = short_conv_mixer(h, conv_w_in[j], conv_w[j], conv_w_out[j])
        else:
            f = diff_attention(h, attn_w_qkv[j], attn_lambda[j], attn_subln_g[j],
                               attn_w_o[j], lambda_init_fn(i))
        x = layer_norm(DEEPNORM_ALPHA * x + gate * f, ln_g[i, 0], ln_b[i, 0])
        shift, scale, gate = ada_mod(c, ada_w[i, 1], ada_b[i, 1])
        h = x * scale + shift
        f = peer_ffn(h, peer_w_q[i], peer_sub_keys[i], peer_u[i], peer_v[i])
        x = layer_norm(DEEPNORM_ALPHA * x + gate * f, ln_g[i, 1], ln_b[i, 1])
    return x
```

```python
import functools
import math

import jax
import jax.numpy as jnp
from jax import lax
from jax.experimental import pallas as pl
from jax.experimental.pallas import tpu as pltpu

F32 = jnp.float32
BF16 = jnp.bfloat16

N_MIXERS = 2
ATT_HEADS = 8
PEER_HEADS = 8
PEER_KEYS = 128
PEER_TOPK = 16
LN_EPS = 1e-5

LANES = 128
SUBLANES = 8
VMEM_LIMIT_BYTES = 56 * 1024 * 1024

N_CAND_GROUPS = 10
N_CAND = N_CAND_GROUPS * SUBLANES
W_PITCH = 72


def _layer_norm(r, g, b):
    mu = jnp.mean(r, axis=-1, keepdims=True)
    d = r - mu
    var = jnp.mean(d * d, axis=-1, keepdims=True)
    return d * lax.rsqrt(var + LN_EPS) * g + b


def _split_mod(mod, d):
    return mod[:, :d], 1.0 + mod[:, d:2 * d], 1.0 + mod[:, 2 * d:]


def _params(sem):
    return pltpu.CompilerParams(dimension_semantics=sem, vmem_limit_bytes=VMEM_LIMIT_BYTES)


def _const_spec(shape):
    nd = len(shape)
    return pl.BlockSpec(shape, lambda *_: (0,) * nd, pipeline_mode=pl.Buffered(1))


def _ada_kernel(c_ref, w_ref, b_ref, o_ref):
    c = c_ref[...]
    sc = c / (1.0 + jnp.exp(-c))
    o_ref[0] = jnp.dot(sc, w_ref[0], precision=lax.Precision.HIGHEST,
                       preferred_element_type=F32) + b_ref[0]


def _ada_mods(c, ada_w, ada_b):
    depth, _, d, d3 = ada_w.shape
    nb = c.shape[0]
    nmat = depth * 2
    tn = 1024
    w = ada_w.reshape(nmat, d, d3)
    b = ada_b.reshape(nmat, 1, d3)
    out = pl.pallas_call(
        _ada_kernel,
        grid=(nmat, d3 // tn),
        in_specs=[
            pl.BlockSpec((nb, d), lambda m, n: (0, 0)),
            pl.BlockSpec((1, d, tn), lambda m, n: (m, 0, n)),
            pl.BlockSpec((1, 1, tn), lambda m, n: (m, 0, n)),
        ],
        out_specs=pl.BlockSpec((1, nb, tn), lambda m, n: (m, 0, n)),
        out_shape=jax.ShapeDtypeStruct((nmat, nb, d3), F32),
        compiler_params=_params(("arbitrary", "arbitrary")),
        name="ada_mods",
    )(c, w, b)
    return out.reshape(nmat * nb, 1, d3)


def _conv_kernel(x_ref, mod_ref, win_ref, cw_ref, wout_ref, g_ref, b_ref, o_ref, zbuf,
                 *, tiles_per_seq, alpha):
    i = pl.program_id(0)
    tm, d = x_ref.shape
    x = x_ref[...]
    shift, scale, gate = _split_mod(mod_ref[0], d)
    h = (x * scale + shift).astype(BF16)
    hw = jnp.dot(h, win_ref[...], preferred_element_type=F32)
    gb = hw[:, :d]
    z = hw[:, d:2 * d] * hw[:, 2 * d:]

    @pl.when(i % tiles_per_seq == 0)
    def _():
        zbuf[0:SUBLANES, :] = jnp.zeros((SUBLANES, d), F32)

    zbuf[SUBLANES:, :] = z
    z1 = zbuf[pl.ds(SUBLANES - 1, tm), :]
    z2 = zbuf[pl.ds(SUBLANES - 2, tm), :]
    cw = cw_ref[...]
    zc = cw[2:3] * z + cw[1:2] * z1 + cw[0:1] * z2
    zbuf[0:SUBLANES, :] = zbuf[tm:tm + SUBLANES, :]
    y = jnp.dot((gb * zc).astype(BF16), wout_ref[...], preferred_element_type=F32)
    o_ref[...] = _layer_norm(alpha * x + gate * y, g_ref[...], b_ref[...])


def _conv_layer(x, mods, mod_row, w_in, conv_w, w_out, ln_g, ln_b, *, seq, alpha, tm=512):
    t, d = x.shape
    tiles_per_seq = seq // tm
    return pl.pallas_call(
        functools.partial(_conv_kernel, tiles_per_seq=tiles_per_seq, alpha=alpha),
        grid=(t // tm,),
        in_specs=[
            pl.BlockSpec((tm, d), lambda i: (i, 0)),
            pl.BlockSpec((1, 1, 3 * d), lambda i: (mod_row + i // tiles_per_seq, 0, 0)),
            _const_spec((d, 3 * d)),
            _const_spec((conv_w.shape[0], d)),
            _const_spec((d, d)),
            _const_spec((1, d)),
            _const_spec((1, d)),
        ],
        out_specs=pl.BlockSpec((tm, d), lambda i: (i, 0)),
        out_shape=jax.ShapeDtypeStruct((t, d), F32),
        scratch_shapes=[pltpu.VMEM((tm + SUBLANES, d), F32)],
        compiler_params=_params(("arbitrary",)),
        name="conv_mixer",
    )(x, mods, w_in.astype(BF16), conv_w, w_out.astype(BF16), ln_g, ln_b)


def _qkv_kernel(x_ref, mod_ref, w_ref, q_ref, k_ref, v_ref, *, q_scale):
    tm, d = x_ref.shape
    shift, scale, _ = _split_mod(mod_ref[0], d)
    h = (x_ref[...] * scale + shift).astype(BF16)
    qkv = jnp.dot(h, w_ref[...], preferred_element_type=F32)
    q_ref[...] = (qkv[:, :d] * q_scale).astype(BF16)
    k_ref[...] = qkv[:, d:2 * d].astype(BF16)
    v_ref[...] = qkv[:, 2 * d:].astype(BF16)


def _qkv_proj(x, mods, mod_row, w_qkv, *, seq, q_scale, tm=512):
    t, d = x.shape
    tiles_per_seq = seq // tm
    out = jax.ShapeDtypeStruct((t, d), BF16)
    return pl.pallas_call(
        functools.partial(_qkv_kernel, q_scale=q_scale),
        grid=(t // tm,),
        in_specs=[
            pl.BlockSpec((tm, d), lambda i: (i, 0)),
            pl.BlockSpec((1, 1, 3 * d), lambda i: (mod_row + i // tiles_per_seq, 0, 0)),
            _const_spec((d, 3 * d)),
        ],
        out_specs=[pl.BlockSpec((tm, d), lambda i: (i, 0))] * 3,
        out_shape=[out, out, out],
        compiler_params=_params(("arbitrary",)),
        name="attn_qkv",
    )(x, mods, w_qkv.astype(BF16))


def _attn_kernel(slopes_ref, q_ref, k_ref, v_ref, lam_ref, g_ref, o_ref, m_ref, l_ref, acc_ref,
                 *, tq, lambda_init):
    hd = pl.program_id(1)
    qi = pl.program_id(2)
    dh2 = q_ref.shape[1]
    dh = dh2 // 2
    slope = slopes_ref[hd]

    q = q_ref[...]
    lane = lax.broadcasted_iota(jnp.int32, (tq, dh2), 1)
    zero = jnp.zeros_like(q)
    q2 = jnp.concatenate([jnp.where(lane < dh, q, zero), jnp.where(lane >= dh, q, zero)], axis=0)

    row = lax.broadcasted_iota(jnp.int32, (2 * tq, tq), 0)
    col = lax.broadcasted_iota(jnp.int32, (2 * tq, tq), 1)
    rq = jnp.where(row >= tq, row - tq, row)
    rel = rq - col
    bias0 = rel.astype(F32) * slope

    m_ref[...] = jnp.full(m_ref.shape, -jnp.inf, F32)
    l_ref[...] = jnp.zeros(l_ref.shape, F32)
    acc_ref[...] = jnp.zeros(acc_ref.shape, F32)

    def step(j, diagonal):
        start = pl.multiple_of(j * tq, tq)
        kb = k_ref[pl.ds(start, tq), :]
        vb = v_ref[pl.ds(start, tq), :]
        s = lax.dot_general(q2, kb, (((1,), (1,)), ((), ())), preferred_element_type=F32)
        off = ((qi - j) * tq).astype(F32) * slope
        s = s - (bias0 + off)
        if diagonal:
            s = jnp.where(rel >= 0, s, -jnp.inf)
        m_old = m_ref[...]
        m_new = jnp.maximum(m_old, jnp.max(s, axis=1, keepdims=True))
        corr = jnp.exp(m_old - m_new)
        p = jnp.exp(s - m_new)
        l_ref[...] = corr * l_ref[...] + jnp.sum(p, axis=1, keepdims=True)
        acc_ref[...] = corr * acc_ref[...] + jnp.dot(p.astype(BF16), vb, preferred_element_type=F32)
        m_ref[...] = m_new

    def body(j, carry):
        step(j, False)
        return carry

    lax.fori_loop(0, qi, body, 0)
    step(qi, True)

    lam = lam_ref[...]
    lam_full = (jnp.exp(jnp.sum(lam[0:1] * lam[1:2], axis=1, keepdims=True))
                - jnp.exp(jnp.sum(lam[2:3] * lam[3:4], axis=1, keepdims=True)) + lambda_init)
    on = acc_ref[...] / l_ref[...]
    o = on[:tq] - lam_full * on[tq:]
    o = o * lax.rsqrt(jnp.mean(o * o, axis=-1, keepdims=True) + LN_EPS) * g_ref[...]
    o_ref[...] = (o * (1.0 - lambda_init)).astype(o_ref.dtype)


def _diff_attention(q, k, v, lam, subln_g, *, batch, seq, lambda_init, tq=256):
    t, d = q.shape
    dh2 = d // ATT_HEADS
    nq = seq // tq
    slopes = 2.0 ** (-(8.0 / ATT_HEADS) * jnp.arange(1, ATT_HEADS + 1, dtype=F32))
    return pl.pallas_call(
        functools.partial(_attn_kernel, tq=tq, lambda_init=lambda_init),
        grid=(batch, ATT_HEADS, nq),
        in_specs=[
            pl.BlockSpec(memory_space=pltpu.SMEM),
            pl.BlockSpec((tq, dh2), lambda b, h, i: (b * nq + i, h)),
            pl.BlockSpec((seq, dh2), lambda b, h, i: (b, h)),
            pl.BlockSpec((seq, dh2), lambda b, h, i: (b, h)),
            pl.BlockSpec(lam.shape, lambda b, h, i: (0, 0)),
            pl.BlockSpec((1, dh2), lambda b, h, i: (0, 0)),
        ],
        out_specs=pl.BlockSpec((tq, dh2), lambda b, h, i: (b * nq + i, h)),
        out_shape=jax.ShapeDtypeStruct((t, d), BF16),
        scratch_shapes=[
            pltpu.VMEM((2 * tq, 1), F32),
            pltpu.VMEM((2 * tq, 1), F32),
            pltpu.VMEM((2 * tq, dh2), F32),
        ],
        compiler_params=_params(("arbitrary", "arbitrary", "arbitrary")),
        name="diff_attention",
    )(slopes, q, k, v, lam, subln_g.reshape(1, dh2))


def _oproj_kernel(o_ref, x_ref, mod_ref, w_ref, g_ref, b_ref, out_ref, *, alpha):
    tm, d = x_ref.shape
    _, _, gate = _split_mod(mod_ref[0], d)
    y = jnp.dot(o_ref[...], w_ref[...], preferred_element_type=F32)
    out_ref[...] = _layer_norm(alpha * x_ref[...] + gate * y, g_ref[...], b_ref[...])


def _out_proj(o, x, mods, mod_row, w_o, ln_g, ln_b, *, seq, alpha, tm=512):
    t, d = x.shape
    tiles_per_seq = seq // tm
    return pl.pallas_call(
        functools.partial(_oproj_kernel, alpha=alpha),
        grid=(t // tm,),
        in_specs=[
            pl.BlockSpec((tm, d), lambda i: (i, 0)),
            pl.BlockSpec((tm, d), lambda i: (i, 0)),
            pl.BlockSpec((1, 1, 3 * d), lambda i: (mod_row + i // tiles_per_seq, 0, 0)),
            _const_spec((d, d)),
            _const_spec((1, d)),
            _const_spec((1, d)),
        ],
        out_specs=pl.BlockSpec((tm, d), lambda i: (i, 0)),
        out_shape=jax.ShapeDtypeStruct((t, d), F32),
        compiler_params=_params(("arbitrary",)),
        name="attn_out_proj",
    )(o, x, mods, w_o.astype(BF16), ln_g, ln_b)


def _top16_rows(s):
    n, tm = s.shape
    rows = lax.broadcasted_iota(jnp.int32, (n, tm), 0).astype(F32)
    vals, idxs = [], []
    for _ in range(PEER_TOPK):
        m = jnp.max(s, axis=0, keepdims=True)
        idx = jnp.min(jnp.where(s == m, rows, float(n)), axis=0, keepdims=True)
        s = jnp.where(rows == idx, -jnp.inf, s)
        vals.append(m)
        idxs.append(idx)
    return vals, idxs


def _candidate_layout(tm):
    slot = lax.broadcasted_iota(jnp.int32, (N_CAND, tm), 0)
    grp = slot // SUBLANES
    r = slot % SUBLANES
    a = jnp.where(grp <= 1, 0, jnp.where(grp == N_CAND_GROUPS - 1, SUBLANES + r, grp - 1))
    b = jnp.where(grp == 0, r, jnp.where(grp == 1, SUBLANES + r,
                                         jnp.where(grp == N_CAND_GROUPS - 1, 0, r)))
    valid = (a + 1) * (b + 1) <= PEER_TOPK
    flat = (a * PEER_TOPK + b).astype(F32)
    return valid, flat


def _candidates(rows0, rows1):
    lo1 = jnp.concatenate(rows1[:SUBLANES], axis=0)
    hi1 = jnp.concatenate(rows1[SUBLANES:], axis=0)
    hi0 = jnp.concatenate(rows0[SUBLANES:], axis=0)
    groups = [rows0[0] + lo1, rows0[0] + hi1]
    groups += [rows0[a] + lo1 for a in range(1, SUBLANES)]
    groups.append(hi0 + rows1[0])
    return jnp.concatenate(groups, axis=0)


def _route_kernel(x_ref, mod_ref, wq_ref, keys_ref, h_ref, code_ref, gate_ref):
    tm, d = x_ref.shape
    shift, scale, _ = _split_mod(mod_ref[0], d)
    h = (x_ref[...] * scale + shift).astype(BF16)
    h_ref[...] = h
    q = jnp.dot(h, wq_ref[...], preferred_element_type=F32).astype(BF16)
    valid, flat = _candidate_layout(tm)
    half = PEER_KEYS
    for hd in range(PEER_HEADS):
        vals, idxs = [], []
        for p in range(2):
            qhp = q[:, (2 * hd + p) * half:(2 * hd + p + 1) * half]
            st = lax.dot_general(keys_ref[p], qhp, (((1,), (1,)), ((), ())),
                                 preferred_element_type=F32)
            v, ix = _top16_rows(st)
            vals.append(v)
            idxs.append(ix)
        cand = jnp.where(valid, _candidates(vals[0], vals[1]), -jnp.inf)
        code = _candidates([ix * float(PEER_KEYS) for ix in idxs[0]], idxs[1])
        top_s, top_code = [], []
        for _ in range(PEER_TOPK):
            m = jnp.max(cand, axis=0, keepdims=True)
            fmin = jnp.min(jnp.where(cand == m, flat, float(PEER_TOPK * PEER_TOPK)),
                           axis=0, keepdims=True)
            hit = flat == fmin
            top_code.append(jnp.sum(jnp.where(hit, code, 0.0), axis=0, keepdims=True))
            cand = jnp.where(hit, -jnp.inf, cand)
            top_s.append(m)
        ts = jnp.concatenate(top_s, axis=0)
        e = jnp.exp(ts - top_s[0])
        gate_ref[hd * PEER_TOPK:(hd + 1) * PEER_TOPK, :] = e / jnp.sum(e, axis=0, keepdims=True)
        code_ref[hd * PEER_TOPK:(hd + 1) * PEER_TOPK, :] = (
            jnp.concatenate(top_code, axis=0).astype(jnp.int32))


def _peer_route(x, mods, mod_row, w_q, sub_keys, *, seq, tm=256):
    t, d = x.shape
    tiles_per_seq = seq // tm
    nq = w_q.shape[1]
    nhk = PEER_HEADS * PEER_TOPK
    return pl.pallas_call(
        _route_kernel,
        grid=(t // tm,),
        in_specs=[
            pl.BlockSpec((tm, d), lambda i: (i, 0)),
            pl.BlockSpec((1, 1, 3 * d), lambda i: (mod_row + i // tiles_per_seq, 0, 0)),
            _const_spec((d, nq)),
            _const_spec(sub_keys.shape),
        ],
        out_specs=[
            pl.BlockSpec((tm, d), lambda i: (i, 0)),
            pl.BlockSpec((nhk, tm), lambda i: (0, i)),
            pl.BlockSpec((nhk, tm), lambda i: (0, i)),
        ],
        out_shape=[
            jax.ShapeDtypeStruct((t, d), BF16),
            jax.ShapeDtypeStruct((nhk, t), jnp.int32),
            jax.ShapeDtypeStruct((nhk, t), F32),
        ],
        compiler_params=_params(("arbitrary",)),
        name="peer_route",
    )(x, mods, w_q.astype(BF16), sub_keys.astype(BF16))


def _expert_kernel(h_ref, x_ref, code_ref, gate_ref, mod_ref, ut_ref, v_ref, g_ref, b_ref,
                   o_ref, ci_ref, cj_ref, cg_ref, w_ref, acc_ref, *, alpha):
    j = pl.program_id(1)
    nj = pl.num_programs(1)
    tm, d = x_ref.shape
    tn = ut_ref.shape[1]
    nhk = code_ref.shape[0]

    @pl.when(j == 0)
    def _():
        code = code_ref[...].T
        ci_ref[...] = code // PEER_KEYS
        cj_ref[...] = code % PEER_KEYS
        cg_ref[...] = gate_ref[...].T
        key = lax.broadcasted_iota(jnp.int32, (PEER_KEYS, nhk), 0)

        def token(t, carry):
            ri = ci_ref[pl.ds(t, 1), :]
            rj = cj_ref[pl.ds(t, 1), :]
            rg = cg_ref[pl.ds(t, 1), :]
            pt = jnp.where(key == ri, rg, 0.0).astype(BF16)
            qt = jnp.where(key == rj, 1.0, 0.0).astype(BF16)
            wt = lax.dot_general(pt, qt, (((1,), (1,)), ((), ())), preferred_element_type=F32)
            packed = pltpu.bitcast(wt.astype(BF16), jnp.uint32)
            w_ref[pl.ds(pl.multiple_of(t * W_PITCH, SUBLANES), PEER_KEYS // 2), :] = packed
            return carry

        lax.fori_loop(0, tm, token, 0)
        acc_ref[...] = jnp.zeros(acc_ref.shape, F32)

    a = jnp.dot(h_ref[...], ut_ref[...], preferred_element_type=F32)
    act = 0.5 * a * (1.0 + lax.erf(a * (1.0 / math.sqrt(2.0))))
    pieces = []
    for r in range(tn // (2 * PEER_KEYS)):
        words = w_ref[pl.ds(j * (tn // (2 * PEER_KEYS)) + r, tm, stride=W_PITCH), :]
        even = pltpu.bitcast(words << 16, F32)
        odd = pltpu.bitcast(words & jnp.uint32(0xFFFF0000), F32)
        c0 = 2 * r * PEER_KEYS
        pieces.append((even * act[:, c0:c0 + PEER_KEYS]).astype(BF16))
        pieces.append((odd * act[:, c0 + PEER_KEYS:c0 + 2 * PEER_KEYS]).astype(BF16))
    z = jnp.concatenate(pieces, axis=1)
    acc_ref[...] += jnp.dot(z, v_ref[...], preferred_element_type=F32)

    @pl.when(j == nj - 1)
    def _():
        _, _, gate = _split_mod(mod_ref[0], d)
        o_ref[...] = _layer_norm(alpha * x_ref[...] + gate * acc_ref[...], g_ref[...], b_ref[...])


def _peer_experts(h, x, code, gates, mods, mod_row, u_t, v, ln_g, ln_b, *, seq, alpha,
                  tm=512, tn=512):
    t, d = x.shape
    n = v.shape[0]
    nhk = code.shape[0]
    tiles_per_seq = seq // tm
    return pl.pallas_call(
        functools.partial(_expert_kernel, alpha=alpha),
        grid=(t // tm, n // tn),
        in_specs=[
            pl.BlockSpec((tm, d), lambda i, j: (i, 0)),
            pl.BlockSpec((tm, d), lambda i, j: (i, 0)),
            pl.BlockSpec((nhk, tm), lambda i, j: (0, i)),
            pl.BlockSpec((nhk, tm), lambda i, j: (0, i)),
            pl.BlockSpec((1, 1, 3 * d), lambda i, j: (mod_row + i // tiles_per_seq, 0, 0)),
            pl.BlockSpec((d, tn), lambda i, j: (0, j)),
            pl.BlockSpec((tn, d), lambda i, j: (j, 0)),
            _const_spec((1, d)),
            _const_spec((1, d)),
        ],
        out_specs=pl.BlockSpec((tm, d), lambda i, j: (i, 0)),
        out_shape=jax.ShapeDtypeStruct((t, d), F32),
        scratch_shapes=[
            pltpu.VMEM((tm, nhk), jnp.int32),
            pltpu.VMEM((tm, nhk), jnp.int32),
            pltpu.VMEM((tm, nhk), F32),
            pltpu.VMEM((tm * W_PITCH, PEER_KEYS), jnp.uint32),
            pltpu.VMEM((tm, d), F32),
        ],
        compiler_params=_params(("arbitrary", "arbitrary")),
        name="peer_experts",
    )(h, x, code, gates, mods, u_t, v, ln_g, ln_b)


def _lambda_init(layer_idx):
    return 0.8 - 0.6 * math.exp(-0.3 * layer_idx)


def kernel(x, c, ada_w, ada_b, ln_g, ln_b, conv_w_in, conv_w, conv_w_out, attn_w_qkv, attn_lambda,
           attn_subln_g, attn_w_o, peer_w_q, peer_sub_keys, peer_u, peer_v):
    batch, seq, d = x.shape
    depth = ada_w.shape[0]
    alpha = (2.0 * depth) ** 0.25
    head_dim = d // (2 * ATT_HEADS)
    mods = _ada_mods(c, ada_w, ada_b)
    xt = x.reshape(batch * seq, d)
    for i in range(depth):
        j = i // N_MIXERS
        row = (2 * i) * batch
        g0, b0 = ln_g[i, 0].reshape(1, d), ln_b[i, 0].reshape(1, d)
        if i % N_MIXERS == 0:
            xt = _conv_layer(xt, mods, row, conv_w_in[j], conv_w[j], conv_w_out[j], g0, b0,
                             seq=seq, alpha=alpha)
        else:
            q, k, v = _qkv_proj(xt, mods, row, attn_w_qkv[j], seq=seq, q_scale=head_dim ** -0.5)
            o = _diff_attention(q, k, v, attn_lambda[j], attn_subln_g[j], batch=batch, seq=seq,
                                lambda_init=_lambda_init(i))
            xt = _out_proj(o, xt, mods, row, attn_w_o[j], g0, b0, seq=seq, alpha=alpha)
        row = (2 * i + 1) * batch
        g1, b1 = ln_g[i, 1].reshape(1, d), ln_b[i, 1].reshape(1, d)
        h, code, gates = _peer_route(xt, mods, row, peer_w_q[i], peer_sub_keys[i], seq=seq)
        xt = _peer_experts(h, xt, code, gates, mods, row, peer_u[i].astype(BF16).T,
                           peer_v[i].astype(BF16), g1, b1, seq=seq, alpha=alpha)
    return xt.reshape(batch, seq, d)
```

```python
import functools
import math

import jax
import jax.numpy as jnp
from jax import lax
from jax.experimental import pallas as pl
from jax.experimental.pallas import tpu as pltpu

F32 = jnp.float32
BF16 = jnp.bfloat16

N_MIXERS = 2
ATT_HEADS = 8
PEER_HEADS = 8
PEER_KEYS = 128
PEER_TOPK = 16
LN_EPS = 1e-5

LANES = 128
SUBLANES = 8
VMEM_LIMIT_BYTES = 56 * 1024 * 1024

N_CAND_GROUPS = 10
N_CAND = N_CAND_GROUPS * SUBLANES
W_PITCH = 72


def _layer_norm(r, g, b):
    mu = jnp.mean(r, axis=-1, keepdims=True)
    d = r - mu
    var = jnp.mean(d * d, axis=-1, keepdims=True)
    return d * lax.rsqrt(var + LN_EPS) * g + b


def _split_mod(mod, d):
    return mod[:, :d], 1.0 + mod[:, d:2 * d], 1.0 + mod[:, 2 * d:]


def _params(sem):
    return pltpu.CompilerParams(dimension_semantics=sem, vmem_limit_bytes=VMEM_LIMIT_BYTES)


def _const_spec(shape):
    nd = len(shape)
    return pl.BlockSpec(shape, lambda *_: (0,) * nd, pipeline_mode=pl.Buffered(1))


def _ada_kernel(c_ref, w_ref, b_ref, o_ref):
    c = c_ref[...]
    sc = c / (1.0 + jnp.exp(-c))
    o_ref[0] = jnp.dot(sc, w_ref[0], precision=lax.Precision.HIGHEST,
                       preferred_element_type=F32) + b_ref[0]


def _ada_mods(c, ada_w, ada_b):
    depth, _, d, d3 = ada_w.shape
    nb = c.shape[0]
    nmat = depth * 2
    tn = 1024
    w = ada_w.reshape(nmat, d, d3)
    b = ada_b.reshape(nmat, 1, d3)
    out = pl.pallas_call(
        _ada_kernel,
        grid=(nmat, d3 // tn),
        in_specs=[
            pl.BlockSpec((nb, d), lambda m, n: (0, 0)),
            pl.BlockSpec((1, d, tn), lambda m, n: (m, 0, n)),
            pl.BlockSpec((1, 1, tn), lambda m, n: (m, 0, n)),
        ],
        out_specs=pl.BlockSpec((1, nb, tn), lambda m, n: (m, 0, n)),
        out_shape=jax.ShapeDtypeStruct((nmat, nb, d3), F32),
        compiler_params=_params(("arbitrary", "arbitrary")),
        name="ada_mods",
    )(c, w, b)
    return out.reshape(nmat * nb, 1, d3)


def _conv_kernel(x_ref, mod_ref, win_ref, cw_ref, wout_ref, g_ref, b_ref, o_ref, zbuf,
                 *, tiles_per_seq, alpha):
    i = pl.program_id(0)
    tm, d = x_ref.shape
    x = x_ref[...]
    shift, scale, gate = _split_mod(mod_ref[0], d)
    h = (x * scale + shift).astype(BF16)
    hw = jnp.dot(h, win_ref[...], preferred_element_type=F32)
    gb = hw[:, :d]
    z = hw[:, d:2 * d] * hw[:, 2 * d:]

    @pl.when(i % tiles_per_seq == 0)
    def _():
        zbuf[0:SUBLANES, :] = jnp.zeros((SUBLANES, d), F32)

    zbuf[SUBLANES:, :] = z
    z1 = zbuf[pl.ds(SUBLANES - 1, tm), :]
    z2 = zbuf[pl.ds(SUBLANES - 2, tm), :]
    cw = cw_ref[...]
    zc = cw[2:3] * z + cw[1:2] * z1 + cw[0:1] * z2
    zbuf[0:SUBLANES, :] = zbuf[tm:tm + SUBLANES, :]
    y = jnp.dot((gb * zc).astype(BF16), wout_ref[...], preferred_element_type=F32)
    o_ref[...] = _layer_norm(alpha * x + gate * y, g_ref[...], b_ref[...])


def _conv_layer(x, mods, mod_row, w_in, conv_w, w_out, ln_g, ln_b, *, seq, alpha, tm=512):
    t, d = x.shape
    tiles_per_seq = seq // tm
    return pl.pallas_call(
        functools.partial(_conv_kernel, tiles_per_seq=tiles_per_seq, alpha=alpha),
        grid=(t // tm,),
        in_specs=[
            pl.BlockSpec((tm, d), lambda i: (i, 0)),
            pl.BlockSpec((1, 1, 3 * d), lambda i: (mod_row + i // tiles_per_seq, 0, 0)),
            _const_spec((d, 3 * d)),
            _const_spec((conv_w.shape[0], d)),
            _const_spec((d, d)),
            _const_spec((1, d)),
            _const_spec((1, d)),
        ],
        out_specs=pl.BlockSpec((tm, d), lambda i: (i, 0)),
        out_shape=jax.ShapeDtypeStruct((t, d), F32),
        scratch_shapes=[pltpu.VMEM((tm + SUBLANES, d), F32)],
        compiler_params=_params(("arbitrary",)),
        name="conv_mixer",
    )(x, mods, w_in.astype(BF16), conv_w, w_out.astype(BF16), ln_g, ln_b)


def _qkv_kernel(x_ref, mod_ref, w_ref, qt_ref, k_ref, vt_ref, *, q_scale):
    tm, d = x_ref.shape
    shift, scale, _ = _split_mod(mod_ref[0], d)
    h = (x_ref[...] * scale + shift).astype(BF16)
    qkv = jnp.dot(h, w_ref[...], preferred_element_type=F32)
    qt_ref[...] = (qkv[:, :d] * q_scale).T.astype(BF16)
    k_ref[...] = qkv[:, d:2 * d].astype(BF16)
    vt_ref[...] = qkv[:, 2 * d:].T.astype(BF16)


def _qkv_proj(x, mods, mod_row, w_qkv, *, seq, q_scale, tm=512):
    t, d = x.shape
    tiles_per_seq = seq // tm
    return pl.pallas_call(
        functools.partial(_qkv_kernel, q_scale=q_scale),
        grid=(t // tm,),
        in_specs=[
            pl.BlockSpec((tm, d), lambda i: (i, 0)),
            pl.BlockSpec((1, 1, 3 * d), lambda i: (mod_row + i // tiles_per_seq, 0, 0)),
            _const_spec((d, 3 * d)),
        ],
        out_specs=[
            pl.BlockSpec((d, tm), lambda i: (0, i)),
            pl.BlockSpec((tm, d), lambda i: (i, 0)),
            pl.BlockSpec((d, tm), lambda i: (0, i)),
        ],
        out_shape=[
            jax.ShapeDtypeStruct((d, t), BF16),
            jax.ShapeDtypeStruct((t, d), BF16),
            jax.ShapeDtypeStruct((d, t), BF16),
        ],
        compiler_params=_params(("arbitrary",)),
        name="attn_qkv",
    )(x, mods, w_qkv.astype(BF16))


def _attn_kernel(slopes_ref, qt_ref, k_ref, vt_ref, lam_ref, g_ref, o_ref, qa_ref, m_ref, acc_ref,
                 *, tq, heads, lambda_init):
    hg = pl.program_id(1)
    qi = pl.program_id(2)
    dh2 = qt_ref.shape[0] // heads
    dh = dh2 // 2
    ones_rows = 2 * SUBLANES

    feat = lax.broadcasted_iota(jnp.int32, (dh2, tq), 0)
    arow = lax.broadcasted_iota(jnp.int32, (dh2, 2 * tq), 0)
    acol = lax.broadcasted_iota(jnp.int32, (dh2, 2 * tq), 1)
    r = jnp.where(acol >= tq, acol - tq, acol).astype(F32)
    for g in range(heads):
        slope = slopes_ref[hg * heads + g]
        qt = qt_ref[g * dh2:(g + 1) * dh2, :]
        zero = jnp.zeros_like(qt)
        qa_ref[g, :dh2, :] = jnp.concatenate(
            [jnp.where(feat < dh, qt, zero), jnp.where(feat >= dh, qt, zero)], axis=1)
        qa_ref[g, dh2:, :] = jnp.where(arow == 0, -slope * r,
                                       jnp.where(arow == 1, slope, 0.0)).astype(BF16)
    kcol = lax.broadcasted_iota(jnp.int32, (tq, dh2), 1)
    krow = lax.broadcasted_iota(jnp.int32, (tq, dh2), 0).astype(F32)
    k_extra = jnp.where(kcol == 0, 1.0, jnp.where(kcol == 1, krow, 0.0)).astype(BF16)
    ones = jnp.ones((ones_rows, tq), BF16)

    m_ref[...] = jnp.full(m_ref.shape, -jnp.inf, F32)
    acc_ref[...] = jnp.zeros(acc_ref.shape, F32)

    def step(j, diagonal):
        start = pl.multiple_of(j * tq, tq)
        head_cols = [slice(g * dh2, (g + 1) * dh2) for g in range(heads)]
        scores = [jnp.dot(jnp.concatenate([k_ref[pl.ds(start, tq), cols], k_extra], axis=1),
                          qa_ref[g], preferred_element_type=F32)
                  for g, cols in enumerate(head_cols)]
        probs, corrs = [], []
        for g, s in enumerate(scores):
            if diagonal:
                mrow = lax.broadcasted_iota(jnp.int32, (tq, 2 * tq), 0)
                mcol = lax.broadcasted_iota(jnp.int32, (tq, 2 * tq), 1)
                s = jnp.where(mrow <= jnp.where(mcol >= tq, mcol - tq, mcol), s, -jnp.inf)
            off = ((qi - j) * tq).astype(F32) * slopes_ref[hg * heads + g]
            m_old = m_ref[g]
            m_new = jnp.maximum(m_old, jnp.max(s, axis=0, keepdims=True) - off)
            probs.append(jnp.exp(s - (m_new + off)).astype(BF16))
            corrs.append(jnp.exp(m_old - m_new))
            m_ref[g] = m_new
        for g, cols in enumerate(head_cols):
            v_aug = jnp.concatenate([vt_ref[cols, pl.ds(start, tq)], ones], axis=0)
            acc_ref[g] = corrs[g] * acc_ref[g] + jnp.dot(v_aug, probs[g],
                                                         preferred_element_type=F32)

    def body(j, carry):
        step(j, False)
        return carry

    lax.fori_loop(0, qi, body, 0)
    step(qi, True)

    lam = lam_ref[...]
    lam_full = (jnp.exp(jnp.sum(lam[0:1] * lam[1:2], axis=1, keepdims=True))
                - jnp.exp(jnp.sum(lam[2:3] * lam[3:4], axis=1, keepdims=True)) + lambda_init)
    for g in range(heads):
        acc = acc_ref[g]
        on = acc[:dh2] / acc[dh2:dh2 + 1]
        o = on[:, :tq] - lam_full * on[:, tq:]
        o = o * lax.rsqrt(jnp.mean(o * o, axis=0, keepdims=True) + LN_EPS) * g_ref[...]
        o_ref[:, g * dh2:(g + 1) * dh2] = (o * (1.0 - lambda_init)).T.astype(o_ref.dtype)


def _diff_attention(qt, k, vt, lam, subln_g, *, batch, seq, lambda_init, tq=256, heads=4):
    d, t = qt.shape
    dh2 = d // ATT_HEADS
    gw = heads * dh2
    nq = seq // tq
    slopes = 2.0 ** (-(8.0 / ATT_HEADS) * jnp.arange(1, ATT_HEADS + 1, dtype=F32))
    return pl.pallas_call(
        functools.partial(_attn_kernel, tq=tq, heads=heads, lambda_init=lambda_init),
        grid=(batch, ATT_HEADS // heads, nq),
        in_specs=[
            pl.BlockSpec(memory_space=pltpu.SMEM),
            pl.BlockSpec((gw, tq), lambda b, h, i: (h, b * nq + i)),
            pl.BlockSpec((seq, gw), lambda b, h, i: (b, h)),
            pl.BlockSpec((gw, seq), lambda b, h, i: (h, b)),
            pl.BlockSpec(lam.shape, lambda b, h, i: (0, 0)),
            pl.BlockSpec((dh2, 1), lambda b, h, i: (0, 0)),
        ],
        out_specs=pl.BlockSpec((tq, gw), lambda b, h, i: (b * nq + i, h)),
        out_shape=jax.ShapeDtypeStruct((t, d), BF16),
        scratch_shapes=[
            pltpu.VMEM((heads, 2 * dh2, 2 * tq), BF16),
            pltpu.VMEM((heads, 1, 2 * tq), F32),
            pltpu.VMEM((heads, dh2 + 2 * SUBLANES, 2 * tq), F32),
        ],
        compiler_params=_params(("arbitrary", "arbitrary", "arbitrary")),
        name="diff_attention",
    )(slopes, qt, k, vt, lam, subln_g.reshape(dh2, 1))


def _oproj_kernel(o_ref, x_ref, mod_ref, w_ref, g_ref, b_ref, out_ref, *, alpha):
    tm, d = x_ref.shape
    _, _, gate = _split_mod(mod_ref[0], d)
    y = jnp.dot(o_ref[...], w_ref[...], preferred_element_type=F32)
    out_ref[...] = _layer_norm(alpha * x_ref[...] + gate * y, g_ref[...], b_ref[...])


def _out_proj(o, x, mods, mod_row, w_o, ln_g, ln_b, *, seq, alpha, tm=512):
    t, d = x.shape
    tiles_per_seq = seq // tm
    return pl.pallas_call(
        functools.partial(_oproj_kernel, alpha=alpha),
        grid=(t // tm,),
        in_specs=[
            pl.BlockSpec((tm, d), lambda i: (i, 0)),
            pl.BlockSpec((tm, d), lambda i: (i, 0)),
            pl.BlockSpec((1, 1, 3 * d), lambda i: (mod_row + i // tiles_per_seq, 0, 0)),
            _const_spec((d, d)),
            _const_spec((1, d)),
            _const_spec((1, d)),
        ],
        out_specs=pl.BlockSpec((tm, d), lambda i: (i, 0)),
        out_shape=jax.ShapeDtypeStruct((t, d), F32),
        compiler_params=_params(("arbitrary",)),
        name="attn_out_proj",
    )(o, x, mods, w_o.astype(BF16), ln_g, ln_b)


def _top16_rows(s):
    n, tm = s.shape
    rows = lax.broadcasted_iota(jnp.int32, (n, tm), 0).astype(F32)
    vals, idxs = [], []
    for _ in range(PEER_TOPK):
        m = jnp.max(s, axis=0, keepdims=True)
        idx = jnp.min(jnp.where(s == m, rows, float(n)), axis=0, keepdims=True)
        s = jnp.where(rows == idx, -jnp.inf, s)
        vals.append(m)
        idxs.append(idx)
    return vals, idxs


def _candidate_layout(tm):
    slot = lax.broadcasted_iota(jnp.int32, (N_CAND, tm), 0)
    grp = slot // SUBLANES
    r = slot % SUBLANES
    a = jnp.where(grp <= 1, 0, jnp.where(grp == N_CAND_GROUPS - 1, SUBLANES + r, grp - 1))
    b = jnp.where(grp == 0, r, jnp.where(grp == 1, SUBLANES + r,
                                         jnp.where(grp == N_CAND_GROUPS - 1, 0, r)))
    valid = (a + 1) * (b + 1) <= PEER_TOPK
    flat = (a * PEER_TOPK + b).astype(F32)
    return valid, flat


def _candidates(rows0, rows1):
    lo1 = jnp.concatenate(rows1[:SUBLANES], axis=0)
    hi1 = jnp.concatenate(rows1[SUBLANES:], axis=0)
    hi0 = jnp.concatenate(rows0[SUBLANES:], axis=0)
    groups = [rows0[0] + lo1, rows0[0] + hi1]
    groups += [rows0[a] + lo1 for a in range(1, SUBLANES)]
    groups.append(hi0 + rows1[0])
    return jnp.concatenate(groups, axis=0)


def _route_kernel(x_ref, mod_ref, wq_ref, keys_ref, h_ref, code_ref, gate_ref):
    tm, d = x_ref.shape
    shift, scale, _ = _split_mod(mod_ref[0], d)
    h = (x_ref[...] * scale + shift).astype(BF16)
    h_ref[...] = h
    q = jnp.dot(h, wq_ref[...], preferred_element_type=F32).astype(BF16)
    valid, flat = _candidate_layout(tm)
    half = PEER_KEYS
    for hd in range(PEER_HEADS):
        vals, idxs = [], []
        for p in range(2):
            qhp = q[:, (2 * hd + p) * half:(2 * hd + p + 1) * half]
            st = lax.dot_general(keys_ref[p], qhp, (((1,), (1,)), ((), ())),
                                 preferred_element_type=F32)
            v, ix = _top16_rows(st)
            vals.append(v)
            idxs.append(ix)
        cand = jnp.where(valid, _candidates(vals[0], vals[1]), -jnp.inf)
        code = _candidates([ix * float(PEER_KEYS) for ix in idxs[0]], idxs[1])
        top_s, top_code = [], []
        for _ in range(PEER_TOPK):
            m = jnp.max(cand, axis=0, keepdims=True)
            fmin = jnp.min(jnp.where(cand == m, flat, float(PEER_TOPK * PEER_TOPK)),
                           axis=0, keepdims=True)
            hit = flat == fmin
            top_code.append(jnp.sum(jnp.where(hit, code, 0.0), axis=0, keepdims=True))
            cand = jnp.where(hit, -jnp.inf, cand)
            top_s.append(m)
        ts = jnp.concatenate(top_s, axis=0)
        e = jnp.exp(ts - top_s[0])
        gate_ref[hd * PEER_TOPK:(hd + 1) * PEER_TOPK, :] = e / jnp.sum(e, axis=0, keepdims=True)
        code_ref[hd * PEER_TOPK:(hd + 1) * PEER_TOPK, :] = (
            jnp.concatenate(top_code, axis=0).astype(jnp.int32))


def _peer_route(x, mods, mod_row, w_q, sub_keys, *, seq, tm=256):
    t, d = x.shape
    tiles_per_seq = seq // tm
    nq = w_q.shape[1]
    nhk = PEER_HEADS * PEER_TOPK
    return pl.pallas_call(
        _route_kernel,
        grid=(t // tm,),
        in_specs=[
            pl.BlockSpec((tm, d), lambda i: (i, 0)),
            pl.BlockSpec((1, 1, 3 * d), lambda i: (mod_row + i // tiles_per_seq, 0, 0)),
            _const_spec((d, nq)),
            _const_spec(sub_keys.shape),
        ],
        out_specs=[
            pl.BlockSpec((tm, d), lambda i: (i, 0)),
            pl.BlockSpec((nhk, tm), lambda i: (0, i)),
            pl.BlockSpec((nhk, tm), lambda i: (0, i)),
        ],
        out_shape=[
            jax.ShapeDtypeStruct((t, d), BF16),
            jax.ShapeDtypeStruct((nhk, t), jnp.int32),
            jax.ShapeDtypeStruct((nhk, t), F32),
        ],
        compiler_params=_params(("arbitrary",)),
        name="peer_route",
    )(x, mods, w_q.astype(BF16), sub_keys.astype(BF16))


def _expert_kernel(h_ref, x_ref, code_ref, gate_ref, mod_ref, ut_ref, v_ref, g_ref, b_ref,
                   o_ref, ci_ref, cj_ref, cg_ref, w_ref, acc_ref, *, alpha):
    j = pl.program_id(1)
    nj = pl.num_programs(1)
    tm, d = x_ref.shape
    tn = ut_ref.shape[1]
    nhk = code_ref.shape[0]

    @pl.when(j == 0)
    def _():
        code = code_ref[...].T
        ci_ref[...] = code // PEER_KEYS
        cj_ref[...] = code % PEER_KEYS
        cg_ref[...] = gate_ref[...].T
        row = lax.broadcasted_iota(jnp.int32, (PEER_KEYS, nhk), 0)
        key_i = jnp.where(row < PEER_KEYS // 2, 2 * row, 2 * row - (PEER_KEYS - 1))
        key_j = row

        def token_group(grp, carry):
            base = pl.multiple_of(grp * SUBLANES, SUBLANES)
            for u in range(SUBLANES):
                t = base + u
                ri = ci_ref[pl.ds(t, 1), :]
                rj = cj_ref[pl.ds(t, 1), :]
                rg = cg_ref[pl.ds(t, 1), :]
                pt = jnp.where(key_i == ri, rg, 0.0).astype(BF16)
                qt = jnp.where(key_j == rj, 1.0, 0.0).astype(BF16)
                wt = lax.dot_general(pt, qt, (((1,), (1,)), ((), ())),
                                     preferred_element_type=F32)
                wt = wt.astype(BF16).astype(F32)
                lo = lax.bitcast_convert_type(wt[:PEER_KEYS // 2], jnp.uint32) >> 16
                hi = lax.bitcast_convert_type(wt[PEER_KEYS // 2:], jnp.uint32) & jnp.uint32(0xFFFF0000)
                w_ref[pl.ds(pl.multiple_of(t * W_PITCH, SUBLANES), PEER_KEYS // 2), :] = lo | hi
            return carry

        lax.fori_loop(0, tm // SUBLANES, token_group, 0)
        acc_ref[...] = jnp.zeros(acc_ref.shape, F32)

    a = jnp.dot(h_ref[...], ut_ref[...], preferred_element_type=F32)
    act = 0.5 * a * (1.0 + lax.erf(a * (1.0 / math.sqrt(2.0))))
    pieces = []
    for r in range(tn // (2 * PEER_KEYS)):
        words = w_ref[pl.ds(j * (tn // (2 * PEER_KEYS)) + r, tm, stride=W_PITCH), :]
        even = lax.bitcast_convert_type(words << 16, F32)
        odd = lax.bitcast_convert_type(words & jnp.uint32(0xFFFF0000), F32)
        c0 = 2 * r * PEER_KEYS
        pieces.append((even * act[:, c0:c0 + PEER_KEYS]).astype(BF16))
        pieces.append((odd * act[:, c0 + PEER_KEYS:c0 + 2 * PEER_KEYS]).astype(BF16))
    z = jnp.concatenate(pieces, axis=1)
    acc_ref[...] += jnp.dot(z, v_ref[...], preferred_element_type=F32)

    @pl.when(j == nj - 1)
    def _():
        _, _, gate = _split_mod(mod_ref[0], d)
        o_ref[...] = _layer_norm(alpha * x_ref[...] + gate * acc_ref[...], g_ref[...], b_ref[...])


def _peer_experts(h, x, code, gates, mods, mod_row, u_t, v, ln_g, ln_b, *, seq, alpha,
                  tm=512, tn=512):
    t, d = x.shape
    n = v.shape[0]
    nhk = code.shape[0]
    tiles_per_seq = seq // tm
    return pl.pallas_call(
        functools.partial(_expert_kernel, alpha=alpha),
        grid=(t // tm, n // tn),
        in_specs=[
            pl.BlockSpec((tm, d), lambda i, j: (i, 0)),
            pl.BlockSpec((tm, d), lambda i, j: (i, 0)),
            pl.BlockSpec((nhk, tm), lambda i, j: (0, i)),
            pl.BlockSpec((nhk, tm), lambda i, j: (0, i)),
            pl.BlockSpec((1, 1, 3 * d), lambda i, j: (mod_row + i // tiles_per_seq, 0, 0)),
            pl.BlockSpec((d, tn), lambda i, j: (0, j)),
            pl.BlockSpec((tn, d), lambda i, j: (j, 0)),
            _const_spec((1, d)),
            _const_spec((1, d)),
        ],
        out_specs=pl.BlockSpec((tm, d), lambda i, j: (i, 0)),
        out_shape=jax.ShapeDtypeStruct((t, d), F32),
        scratch_shapes=[
            pltpu.VMEM((tm, nhk), jnp.int32),
            pltpu.VMEM((tm, nhk), jnp.int32),
            pltpu.VMEM((tm, nhk), F32),
            pltpu.VMEM((tm * W_PITCH, PEER_KEYS), jnp.uint32),
            pltpu.VMEM((tm, d), F32),
        ],
        compiler_params=_params(("arbitrary", "arbitrary")),
        name="peer_experts",
    )(h, x, code, gates, mods, u_t, v, ln_g, ln_b)


def _lambda_init(layer_idx):
    return 0.8 - 0.6 * math.exp(-0.3 * layer_idx)


def kernel(x, c, ada_w, ada_b, ln_g, ln_b, conv_w_in, conv_w, conv_w_out, attn_w_qkv, attn_lambda,
           attn_subln_g, attn_w_o, peer_w_q, peer_sub_keys, peer_u, peer_v):
    batch, seq, d = x.shape
    depth = ada_w.shape[0]
    alpha = (2.0 * depth) ** 0.25
    head_dim = d // (2 * ATT_HEADS)
    mods = _ada_mods(c, ada_w, ada_b)
    xt = x.reshape(batch * seq, d)
    for i in range(depth):
        j = i // N_MIXERS
        row = (2 * i) * batch
        g0, b0 = ln_g[i, 0].reshape(1, d), ln_b[i, 0].reshape(1, d)
        if i % N_MIXERS == 0:
            xt = _conv_layer(xt, mods, row, conv_w_in[j], conv_w[j], conv_w_out[j], g0, b0,
                             seq=seq, alpha=alpha)
        else:
            qt, k, vt = _qkv_proj(xt, mods, row, attn_w_qkv[j], seq=seq, q_scale=head_dim ** -0.5)
            o = _diff_attention(qt, k, vt, attn_lambda[j], attn_subln_g[j], batch=batch, seq=seq,
                                lambda_init=_lambda_init(i))
            xt = _out_proj(o, xt, mods, row, attn_w_o[j], g0, b0, seq=seq, alpha=alpha)
        row = (2 * i + 1) * batch
        g1, b1 = ln_g[i, 1].reshape(1, d), ln_b[i, 1].reshape(1, d)
        h, code, gates = _peer_route(xt, mods, row, peer_w_q[i], peer_sub_keys[i], seq=seq)
        xt = _peer_experts(h, xt, code, gates, mods, row, peer_u[i].astype(BF16).T,
                           peer_v[i].astype(BF16), g1, b1, seq=seq, alpha=alpha)
    return xt.reshape(batch, seq, d)
```

```python
import functools
import math

import jax
import jax.numpy as jnp
from jax import lax
from jax.experimental import pallas as pl
from jax.experimental.pallas import tpu as pltpu

F32 = jnp.float32
BF16 = jnp.bfloat16

N_MIXERS = 2
ATT_HEADS = 8
PEER_HEADS = 8
PEER_KEYS = 128
PEER_TOPK = 16
LN_EPS = 1e-5

LANES = 128
SUBLANES = 8
VMEM_LIMIT_BYTES = 56 * 1024 * 1024
BF16_EXACT_INT = 256

N_CAND_GROUPS = 10
N_CAND = N_CAND_GROUPS * SUBLANES
W_PITCH = 72
W_GROUP = 32


def _layer_norm(r, g, b):
    mu = jnp.mean(r, axis=-1, keepdims=True)
    d = r - mu
    var = jnp.mean(d * d, axis=-1, keepdims=True)
    return d * lax.rsqrt(var + LN_EPS) * g + b


def _split_mod(mod, d):
    return mod[:, :d], 1.0 + mod[:, d:2 * d], 1.0 + mod[:, 2 * d:]


def _params(sem):
    return pltpu.CompilerParams(dimension_semantics=sem, vmem_limit_bytes=VMEM_LIMIT_BYTES)


def _const_spec(shape):
    nd = len(shape)
    return pl.BlockSpec(shape, lambda *_: (0,) * nd, pipeline_mode=pl.Buffered(1))


def _ada_kernel(c_ref, w_ref, b_ref, o_ref):
    c = c_ref[...]
    sc = c / (1.0 + jnp.exp(-c))
    o_ref[0] = jnp.dot(sc, w_ref[0], precision=lax.Precision.HIGHEST,
                       preferred_element_type=F32) + b_ref[0]


def _ada_mods(c, ada_w, ada_b):
    depth, _, d, d3 = ada_w.shape
    nb = c.shape[0]
    nmat = depth * 2
    tn = 1024
    w = ada_w.reshape(nmat, d, d3)
    b = ada_b.reshape(nmat, 1, d3)
    out = pl.pallas_call(
        _ada_kernel,
        grid=(nmat, d3 // tn),
        in_specs=[
            pl.BlockSpec((nb, d), lambda m, n: (0, 0)),
            pl.BlockSpec((1, d, tn), lambda m, n: (m, 0, n)),
            pl.BlockSpec((1, 1, tn), lambda m, n: (m, 0, n)),
        ],
        out_specs=pl.BlockSpec((1, nb, tn), lambda m, n: (m, 0, n)),
        out_shape=jax.ShapeDtypeStruct((nmat, nb, d3), F32),
        compiler_params=_params(("arbitrary", "arbitrary")),
        name="ada_mods",
    )(c, w, b)
    return out.reshape(nmat * nb, 1, d3)


def _conv_kernel(x_ref, mod_ref, win_ref, cw_ref, wout_ref, g_ref, b_ref, o_ref, zbuf,
                 *, tiles_per_seq, alpha):
    i = pl.program_id(0)
    tm, d = x_ref.shape
    x = x_ref[...]
    shift, scale, gate = _split_mod(mod_ref[0], d)
    h = (x * scale + shift).astype(BF16)
    hw = jnp.dot(h, win_ref[...], preferred_element_type=F32)
    gb = hw[:, :d]
    z = hw[:, d:2 * d] * hw[:, 2 * d:]

    @pl.when(i % tiles_per_seq == 0)
    def _():
        zbuf[0:SUBLANES, :] = jnp.zeros((SUBLANES, d), F32)

    zbuf[SUBLANES:, :] = z
    z1 = zbuf[pl.ds(SUBLANES - 1, tm), :]
    z2 = zbuf[pl.ds(SUBLANES - 2, tm), :]
    cw = cw_ref[...]
    zc = cw[2:3] * z + cw[1:2] * z1 + cw[0:1] * z2
    zbuf[0:SUBLANES, :] = zbuf[tm:tm + SUBLANES, :]
    y = jnp.dot((gb * zc).astype(BF16), wout_ref[...], preferred_element_type=F32)
    o_ref[...] = _layer_norm(alpha * x + gate * y, g_ref[...], b_ref[...])


def _conv_layer(x, mods, mod_row, w_in, conv_w, w_out, ln_g, ln_b, *, seq, alpha, tm=512):
    t, d = x.shape
    tiles_per_seq = seq // tm
    return pl.pallas_call(
        functools.partial(_conv_kernel, tiles_per_seq=tiles_per_seq, alpha=alpha),
        grid=(t // tm,),
        in_specs=[
            pl.BlockSpec((tm, d), lambda i: (i, 0)),
            pl.BlockSpec((1, 1, 3 * d), lambda i: (mod_row + i // tiles_per_seq, 0, 0)),
            _const_spec((d, 3 * d)),
            _const_spec((conv_w.shape[0], d)),
            _const_spec((d, d)),
            _const_spec((1, d)),
            _const_spec((1, d)),
        ],
        out_specs=pl.BlockSpec((tm, d), lambda i: (i, 0)),
        out_shape=jax.ShapeDtypeStruct((t, d), F32),
        scratch_shapes=[pltpu.VMEM((tm + SUBLANES, d), F32)],
        compiler_params=_params(("arbitrary",)),
        name="conv_mixer",
    )(x, mods, w_in.astype(BF16), conv_w, w_out.astype(BF16), ln_g, ln_b)


def _qkv_kernel(x_ref, mod_ref, w_ref, qt_ref, k_ref, vt_ref, *, q_scale):
    tm, d = x_ref.shape
    shift, scale, _ = _split_mod(mod_ref[0], d)
    h = (x_ref[...] * scale + shift).astype(BF16)
    qkv = jnp.dot(h, w_ref[...], preferred_element_type=F32)
    qt_ref[...] = (qkv[:, :d] * q_scale).T.astype(BF16)
    k_ref[...] = qkv[:, d:2 * d].astype(BF16)
    vt_ref[...] = qkv[:, 2 * d:].T.astype(BF16)


def _qkv_proj(x, mods, mod_row, w_qkv, *, seq, q_scale, tm=512):
    t, d = x.shape
    tiles_per_seq = seq // tm
    return pl.pallas_call(
        functools.partial(_qkv_kernel, q_scale=q_scale),
        grid=(t // tm,),
        in_specs=[
            pl.BlockSpec((tm, d), lambda i: (i, 0)),
            pl.BlockSpec((1, 1, 3 * d), lambda i: (mod_row + i // tiles_per_seq, 0, 0)),
            _const_spec((d, 3 * d)),
        ],
        out_specs=[
            pl.BlockSpec((d, tm), lambda i: (0, i)),
            pl.BlockSpec((tm, d), lambda i: (i, 0)),
            pl.BlockSpec((d, tm), lambda i: (0, i)),
        ],
        out_shape=[
            jax.ShapeDtypeStruct((d, t), BF16),
            jax.ShapeDtypeStruct((t, d), BF16),
            jax.ShapeDtypeStruct((d, t), BF16),
        ],
        compiler_params=_params(("arbitrary",)),
        name="attn_qkv",
    )(x, mods, w_qkv.astype(BF16))


def _attn_kernel(slopes_ref, qt_ref, k_ref, vt_ref, lam_ref, g_ref, o_ref, qa_ref, m_ref, acc_ref,
                 *, tq, tk, heads, lambda_init):
    hg = pl.program_id(1)
    qi = pl.program_id(2)
    dh2 = qt_ref.shape[0] // heads
    dh = dh2 // 2
    ones_rows = 2 * SUBLANES
    blocks_per_tile = tq // tk

    feat = lax.broadcasted_iota(jnp.int32, (dh2, tq), 0)
    arow = lax.broadcasted_iota(jnp.int32, (dh2, 2 * tq), 0)
    acol = lax.broadcasted_iota(jnp.int32, (dh2, 2 * tq), 1)
    r = jnp.where(acol >= tq, acol - tq, acol)
    r_lo = (r % BF16_EXACT_INT).astype(F32)
    r_hi = (r - r % BF16_EXACT_INT).astype(F32)
    for g in range(heads):
        slope = slopes_ref[hg * heads + g]
        qt = qt_ref[g * dh2:(g + 1) * dh2, :]
        zero = jnp.zeros_like(qt)
        qa_ref[g, :dh2, :] = jnp.concatenate(
            [jnp.where(feat < dh, qt, zero), jnp.where(feat >= dh, qt, zero)], axis=1)
        qa_ref[g, dh2:, :] = jnp.where(
            arow == 0, -slope * r_lo,
            jnp.where(arow == 1, -slope * r_hi, jnp.where(arow == 2, slope, 0.0))).astype(BF16)
    kcol = lax.broadcasted_iota(jnp.int32, (tk, dh2), 1)
    krow = lax.broadcasted_iota(jnp.int32, (tk, dh2), 0).astype(F32)
    k_extra = jnp.where(kcol <= 1, 1.0, jnp.where(kcol == 2, krow, 0.0)).astype(BF16)
    ones = jnp.ones((ones_rows, tk), BF16)

    m_ref[...] = jnp.full(m_ref.shape, -jnp.inf, F32)
    acc_ref[...] = jnp.zeros(acc_ref.shape, F32)

    def step(j, diagonal):
        start = pl.multiple_of(j * tk, tk)
        delta = qi * tq - j * tk
        head_cols = [slice(g * dh2, (g + 1) * dh2) for g in range(heads)]
        scores = [jnp.dot(jnp.concatenate([k_ref[pl.ds(start, tk), cols], k_extra], axis=1),
                          qa_ref[g], preferred_element_type=F32)
                  for g, cols in enumerate(head_cols)]
        probs, corrs = [], []
        for g, s in enumerate(scores):
            if diagonal:
                mrow = lax.broadcasted_iota(jnp.int32, (tk, 2 * tq), 0)
                mcol = lax.broadcasted_iota(jnp.int32, (tk, 2 * tq), 1)
                mq = jnp.where(mcol >= tq, mcol - tq, mcol)
                s = jnp.where(mrow - mq <= delta, s, -jnp.inf)
            off = delta.astype(F32) * slopes_ref[hg * heads + g]
            m_old = m_ref[g]
            m_new = jnp.maximum(m_old, jnp.max(s, axis=0, keepdims=True) - off)
            probs.append(jnp.exp(s - (m_new + off)).astype(BF16))
            corrs.append(jnp.exp(m_old - m_new))
            m_ref[g] = m_new
        for g, cols in enumerate(head_cols):
            v_aug = jnp.concatenate([vt_ref[cols, pl.ds(start, tk)], ones], axis=0)
            acc_ref[g] = corrs[g] * acc_ref[g] + jnp.dot(v_aug, probs[g],
                                                         preferred_element_type=F32)

    def body(j, carry):
        step(j, False)
        return carry

    lax.fori_loop(0, qi * blocks_per_tile, body, 0)
    for d in range(blocks_per_tile):
        step(qi * blocks_per_tile + d, True)

    lam = lam_ref[...]
    lam_full = (jnp.exp(jnp.sum(lam[0:1] * lam[1:2], axis=1, keepdims=True))
                - jnp.exp(jnp.sum(lam[2:3] * lam[3:4], axis=1, keepdims=True)) + lambda_init)
    for g in range(heads):
        acc = acc_ref[g]
        on = acc[:dh2] / acc[dh2:dh2 + 1]
        o = on[:, :tq] - lam_full * on[:, tq:]
        o = o * lax.rsqrt(jnp.mean(o * o, axis=0, keepdims=True) + LN_EPS) * g_ref[...]
        o_ref[:, g * dh2:(g + 1) * dh2] = (o * (1.0 - lambda_init)).T.astype(o_ref.dtype)


def _diff_attention(qt, k, vt, lam, subln_g, *, batch, seq, lambda_init, tq=512, tk=256, heads=4):
    d, t = qt.shape
    dh2 = d // ATT_HEADS
    gw = heads * dh2
    nq = seq // tq
    assert tq % tk == 0 and tk <= BF16_EXACT_INT and tq < BF16_EXACT_INT * BF16_EXACT_INT
    slopes = 2.0 ** (-(8.0 / ATT_HEADS) * jnp.arange(1, ATT_HEADS + 1, dtype=F32))
    return pl.pallas_call(
        functools.partial(_attn_kernel, tq=tq, tk=tk, heads=heads, lambda_init=lambda_init),
        grid=(batch, ATT_HEADS // heads, nq),
        in_specs=[
            pl.BlockSpec(memory_space=pltpu.SMEM),
            pl.BlockSpec((gw, tq), lambda b, h, i: (h, b * nq + i)),
            pl.BlockSpec((seq, gw), lambda b, h, i: (b, h)),
            pl.BlockSpec((gw, seq), lambda b, h, i: (h, b)),
            pl.BlockSpec(lam.shape, lambda b, h, i: (0, 0)),
            pl.BlockSpec((dh2, 1), lambda b, h, i: (0, 0)),
        ],
        out_specs=pl.BlockSpec((tq, gw), lambda b, h, i: (b * nq + i, h)),
        out_shape=jax.ShapeDtypeStruct((t, d), BF16),
        scratch_shapes=[
            pltpu.VMEM((heads, 2 * dh2, 2 * tq), BF16),
            pltpu.VMEM((heads, 1, 2 * tq), F32),
            pltpu.VMEM((heads, dh2 + 2 * SUBLANES, 2 * tq), F32),
        ],
        compiler_params=_params(("arbitrary", "arbitrary", "arbitrary")),
        name="diff_attention",
    )(slopes, qt, k, vt, lam, subln_g.reshape(dh2, 1))


def _oproj_kernel(o_ref, x_ref, mod_ref, w_ref, g_ref, b_ref, out_ref, *, alpha):
    tm, d = x_ref.shape
    _, _, gate = _split_mod(mod_ref[0], d)
    y = jnp.dot(o_ref[...], w_ref[...], preferred_element_type=F32)
    out_ref[...] = _layer_norm(alpha * x_ref[...] + gate * y, g_ref[...], b_ref[...])


def _out_proj(o, x, mods, mod_row, w_o, ln_g, ln_b, *, seq, alpha, tm=512):
    t, d = x.shape
    tiles_per_seq = seq // tm
    return pl.pallas_call(
        functools.partial(_oproj_kernel, alpha=alpha),
        grid=(t // tm,),
        in_specs=[
            pl.BlockSpec((tm, d), lambda i: (i, 0)),
            pl.BlockSpec((tm, d), lambda i: (i, 0)),
            pl.BlockSpec((1, 1, 3 * d), lambda i: (mod_row + i // tiles_per_seq, 0, 0)),
            _const_spec((d, d)),
            _const_spec((1, d)),
            _const_spec((1, d)),
        ],
        out_specs=pl.BlockSpec((tm, d), lambda i: (i, 0)),
        out_shape=jax.ShapeDtypeStruct((t, d), F32),
        compiler_params=_params(("arbitrary",)),
        name="attn_out_proj",
    )(o, x, mods, w_o.astype(BF16), ln_g, ln_b)


def _top16_rows(s):
    n, tm = s.shape
    rows = lax.broadcasted_iota(jnp.int32, (n, tm), 0).astype(F32)
    vals, idxs = [], []
    for _ in range(PEER_TOPK):
        m = jnp.max(s, axis=0, keepdims=True)
        idx = jnp.min(jnp.where(s == m, rows, float(n)), axis=0, keepdims=True)
        s = jnp.where(rows == idx, -jnp.inf, s)
        vals.append(m)
        idxs.append(idx)
    return vals, idxs


def _candidate_layout(tm):
    slot = lax.broadcasted_iota(jnp.int32, (N_CAND, tm), 0)
    grp = slot // SUBLANES
    r = slot % SUBLANES
    a = jnp.where(grp <= 1, 0, jnp.where(grp == N_CAND_GROUPS - 1, SUBLANES + r, grp - 1))
    b = jnp.where(grp == 0, r, jnp.where(grp == 1, SUBLANES + r,
                                         jnp.where(grp == N_CAND_GROUPS - 1, 0, r)))
    valid = (a + 1) * (b + 1) <= PEER_TOPK
    flat = (a * PEER_TOPK + b).astype(F32)
    return valid, flat


def _candidates(rows0, rows1):
    lo1 = jnp.concatenate(rows1[:SUBLANES], axis=0)
    hi1 = jnp.concatenate(rows1[SUBLANES:], axis=0)
    hi0 = jnp.concatenate(rows0[SUBLANES:], axis=0)
    groups = [rows0[0] + lo1, rows0[0] + hi1]
    groups += [rows0[a] + lo1 for a in range(1, SUBLANES)]
    groups.append(hi0 + rows1[0])
    return jnp.concatenate(groups, axis=0)


def _route_kernel(x_ref, mod_ref, wq_ref, keys_ref, h_ref, code_ref, gate_ref):
    tm, d = x_ref.shape
    shift, scale, _ = _split_mod(mod_ref[0], d)
    h = (x_ref[...] * scale + shift).astype(BF16)
    h_ref[...] = h
    q = jnp.dot(h, wq_ref[...], preferred_element_type=F32).astype(BF16)
    valid, flat = _candidate_layout(tm)
    half = PEER_KEYS
    for hd in range(PEER_HEADS):
        vals, idxs = [], []
        for p in range(2):
            qhp = q[:, (2 * hd + p) * half:(2 * hd + p + 1) * half]
            st = lax.dot_general(keys_ref[p], qhp, (((1,), (1,)), ((), ())),
                                 preferred_element_type=F32)
            v, ix = _top16_rows(st)
            vals.append(v)
            idxs.append(ix)
        cand = jnp.where(valid, _candidates(vals[0], vals[1]), -jnp.inf)
        code = _candidates([ix * float(PEER_KEYS) for ix in idxs[0]], idxs[1])
        top_s, top_code = [], []
        for _ in range(PEER_TOPK):
            m = jnp.max(cand, axis=0, keepdims=True)
            fmin = jnp.min(jnp.where(cand == m, flat, float(PEER_TOPK * PEER_TOPK)),
                           axis=0, keepdims=True)
            hit = flat == fmin
            top_code.append(jnp.sum(jnp.where(hit, code, 0.0), axis=0, keepdims=True))
            cand = jnp.where(hit, -jnp.inf, cand)
            top_s.append(m)
        ts = jnp.concatenate(top_s, axis=0)
        e = jnp.exp(ts - top_s[0])
        gate_ref[hd * PEER_TOPK:(hd + 1) * PEER_TOPK, :] = e / jnp.sum(e, axis=0, keepdims=True)
        code_ref[hd * PEER_TOPK:(hd + 1) * PEER_TOPK, :] = (
            jnp.concatenate(top_code, axis=0).astype(jnp.int32))


def _peer_route(x, mods, mod_row, w_q, sub_keys, *, seq, tm=256):
    t, d = x.shape
    tiles_per_seq = seq // tm
    nq = w_q.shape[1]
    nhk = PEER_HEADS * PEER_TOPK
    return pl.pallas_call(
        _route_kernel,
        grid=(t // tm,),
        in_specs=[
            pl.BlockSpec((tm, d), lambda i: (i, 0)),
            pl.BlockSpec((1, 1, 3 * d), lambda i: (mod_row + i // tiles_per_seq, 0, 0)),
            _const_spec((d, nq)),
            _const_spec(sub_keys.shape),
        ],
        out_specs=[
            pl.BlockSpec((tm, d), lambda i: (i, 0)),
            pl.BlockSpec((nhk, tm), lambda i: (0, i)),
            pl.BlockSpec((nhk, tm), lambda i: (0, i)),
        ],
        out_shape=[
            jax.ShapeDtypeStruct((t, d), BF16),
            jax.ShapeDtypeStruct((nhk, t), jnp.int32),
            jax.ShapeDtypeStruct((nhk, t), F32),
        ],
        compiler_params=_params(("arbitrary",)),
        name="peer_route",
    )(x, mods, w_q.astype(BF16), sub_keys.astype(BF16))


def _expert_kernel(h_ref, x_ref, code_ref, gate_ref, mod_ref, ut_ref, v_ref, g_ref, b_ref,
                   o_ref, ci_ref, cj_ref, cg_ref, w_ref, acc_ref, *, alpha):
    j = pl.program_id(1)
    nj = pl.num_programs(1)
    tm, d = x_ref.shape
    tn = ut_ref.shape[1]
    nhk = code_ref.shape[0]

    @pl.when(j == 0)
    def _():
        code = code_ref[...].T
        ci_ref[...] = code // PEER_KEYS
        cj_ref[...] = code % PEER_KEYS
        cg_ref[...] = gate_ref[...].T
        row = lax.broadcasted_iota(jnp.int32, (PEER_KEYS, nhk), 0)
        key_i = jnp.where(row < PEER_KEYS // 2, 2 * row, 2 * row - (PEER_KEYS - 1)).astype(BF16)
        key_j = row.astype(BF16)
        zero = jnp.zeros((PEER_KEYS, nhk), BF16)
        one = jnp.ones((PEER_KEYS, nhk), BF16)

        def token_group(grp, carry):
            base = pl.multiple_of(grp * W_GROUP, W_GROUP)
            for u in range(W_GROUP):
                t = base + u
                ri = ci_ref[pl.ds(t, 1), :].astype(BF16)
                rj = cj_ref[pl.ds(t, 1), :].astype(BF16)
                rg = cg_ref[pl.ds(t, 1), :].astype(BF16)
                pt = jnp.where(key_i == ri, jnp.broadcast_to(rg, zero.shape), zero)
                qt = jnp.where(key_j == rj, one, zero)
                wt = lax.dot_general(pt, qt, (((1,), (1,)), ((), ())),
                                     preferred_element_type=F32)
                lo = lax.bitcast_convert_type(wt[:PEER_KEYS // 2], jnp.uint32) >> 16
                hi = lax.bitcast_convert_type(wt[PEER_KEYS // 2:], jnp.uint32) & jnp.uint32(0xFFFF0000)
                w_ref[pl.ds(pl.multiple_of(t * W_PITCH, SUBLANES), PEER_KEYS // 2), :] = lo | hi
            return carry

        lax.fori_loop(0, tm // W_GROUP, token_group, 0)
        acc_ref[...] = jnp.zeros(acc_ref.shape, F32)

    a = jnp.dot(h_ref[...], ut_ref[...], preferred_element_type=F32)
    act = 0.5 * a * (1.0 + lax.erf(a * (1.0 / math.sqrt(2.0))))
    pieces = []
    for r in range(tn // (2 * PEER_KEYS)):
        words = w_ref[pl.ds(j * (tn // (2 * PEER_KEYS)) + r, tm, stride=W_PITCH), :]
        even = lax.bitcast_convert_type(words << 16, F32)
        odd = lax.bitcast_convert_type(words & jnp.uint32(0xFFFF0000), F32)
        c0 = 2 * r * PEER_KEYS
        pieces.append((even * act[:, c0:c0 + PEER_KEYS]).astype(BF16))
        pieces.append((odd * act[:, c0 + PEER_KEYS:c0 + 2 * PEER_KEYS]).astype(BF16))
    z = jnp.concatenate(pieces, axis=1)
    acc_ref[...] += jnp.dot(z, v_ref[...], preferred_element_type=F32)

    @pl.when(j == nj - 1)
    def _():
        _, _, gate = _split_mod(mod_ref[0], d)
        o_ref[...] = _layer_norm(alpha * x_ref[...] + gate * acc_ref[...], g_ref[...], b_ref[...])


def _peer_experts(h, x, code, gates, mods, mod_row, u_t, v, ln_g, ln_b, *, seq, alpha,
                  tm=512, tn=1024):
    t, d = x.shape
    n = v.shape[0]
    nhk = code.shape[0]
    tiles_per_seq = seq // tm
    return pl.pallas_call(
        functools.partial(_expert_kernel, alpha=alpha),
        grid=(t // tm, n // tn),
        in_specs=[
            pl.BlockSpec((tm, d), lambda i, j: (i, 0)),
            pl.BlockSpec((tm, d), lambda i, j: (i, 0)),
            pl.BlockSpec((nhk, tm), lambda i, j: (0, i)),
            pl.BlockSpec((nhk, tm), lambda i, j: (0, i)),
            pl.BlockSpec((1, 1, 3 * d), lambda i, j: (mod_row + i // tiles_per_seq, 0, 0)),
            pl.BlockSpec((d, tn), lambda i, j: (0, j)),
            pl.BlockSpec((tn, d), lambda i, j: (j, 0)),
            _const_spec((1, d)),
            _const_spec((1, d)),
        ],
        out_specs=pl.BlockSpec((tm, d), lambda i, j: (i, 0)),
        out_shape=jax.ShapeDtypeStruct((t, d), F32),
        scratch_shapes=[
            pltpu.VMEM((tm, nhk), jnp.int32),
            pltpu.VMEM((tm, nhk), jnp.int32),
            pltpu.VMEM((tm, nhk), F32),
            pltpu.VMEM((tm * W_PITCH, PEER_KEYS), jnp.uint32),
            pltpu.VMEM((tm, d), F32),
        ],
        compiler_params=_params(("arbitrary", "arbitrary")),
        name="peer_experts",
    )(h, x, code, gates, mods, u_t, v, ln_g, ln_b)


def _lambda_init(layer_idx):
    return 0.8 - 0.6 * math.exp(-0.3 * layer_idx)


def kernel(x, c, ada_w, ada_b, ln_g, ln_b, conv_w_in, conv_w, conv_w_out, attn_w_qkv, attn_lambda,
           attn_subln_g, attn_w_o, peer_w_q, peer_sub_keys, peer_u, peer_v):
    batch, seq, d = x.shape
    depth = ada_w.shape[0]
    alpha = (2.0 * depth) ** 0.25
    head_dim = d // (2 * ATT_HEADS)
    mods = _ada_mods(c, ada_w, ada_b)
    xt = x.reshape(batch * seq, d)
    for i in range(depth):
        j = i // N_MIXERS
        row = (2 * i) * batch
        g0, b0 = ln_g[i, 0].reshape(1, d), ln_b[i, 0].reshape(1, d)
        if i % N_MIXERS == 0:
            xt = _conv_layer(xt, mods, row, conv_w_in[j], conv_w[j], conv_w_out[j], g0, b0,
                             seq=seq, alpha=alpha)
        else:
            qt, k, vt = _qkv_proj(xt, mods, row, attn_w_qkv[j], seq=seq, q_scale=head_dim ** -0.5)
            o = _diff_attention(qt, k, vt, attn_lambda[j], attn_subln_g[j], batch=batch, seq=seq,
                                lambda_init=_lambda_init(i))
            xt = _out_proj(o, xt, mods, row, attn_w_o[j], g0, b0, seq=seq, alpha=alpha)
        row = (2 * i + 1) * batch
        g1, b1 = ln_g[i, 1].reshape(1, d), ln_b[i, 1].reshape(1, d)
        h, code, gates = _peer_route(xt, mods, row, peer_w_q[i], peer_sub_keys[i], seq=seq)
        xt = _peer_experts(h, xt, code, gates, mods, row, peer_u[i].astype(BF16).T,
                           peer_v[i].astype(BF16), g1, b1, seq=seq, alpha=alpha)
    return xt.reshape(batch, seq, d)
```

```python
import functools
import math

import jax
import jax.numpy as jnp
from jax import lax
from jax.experimental import pallas as pl
from jax.experimental.pallas import tpu as pltpu

F32 = jnp.float32
BF16 = jnp.bfloat16

N_MIXERS = 2
ATT_HEADS = 8
PEER_HEADS = 8
PEER_KEYS = 128
PEER_TOPK = 16
LN_EPS = 1e-5

LANES = 128
SUBLANES = 8
VMEM_LIMIT_BYTES = 56 * 1024 * 1024
BF16_EXACT_INT = 256
EXP_UNDERFLOW = 106.0
NORM_SLACK = 1.02

N_CAND_GROUPS = 10
N_CAND = N_CAND_GROUPS * SUBLANES
W_PITCH = 72
W_GROUP = 32


def _layer_norm(r, g, b):
    mu = jnp.mean(r, axis=-1, keepdims=True)
    d = r - mu
    var = jnp.mean(d * d, axis=-1, keepdims=True)
    return d * lax.rsqrt(var + LN_EPS) * g + b


def _split_mod(mod, d):
    return mod[:, :d], 1.0 + mod[:, d:2 * d], 1.0 + mod[:, 2 * d:]


def _params(sem):
    return pltpu.CompilerParams(dimension_semantics=sem, vmem_limit_bytes=VMEM_LIMIT_BYTES)


def _const_spec(shape):
    nd = len(shape)
    return pl.BlockSpec(shape, lambda *_: (0,) * nd, pipeline_mode=pl.Buffered(1))


def _ada_kernel(c_ref, w_ref, b_ref, o_ref):
    c = c_ref[...]
    sc = c / (1.0 + jnp.exp(-c))
    o_ref[0] = jnp.dot(sc, w_ref[0], precision=lax.Precision.HIGHEST,
                       preferred_element_type=F32) + b_ref[0]


def _ada_mods(c, ada_w, ada_b):
    depth, _, d, d3 = ada_w.shape
    nb = c.shape[0]
    nmat = depth * 2
    tn = 1024
    w = ada_w.reshape(nmat, d, d3)
    b = ada_b.reshape(nmat, 1, d3)
    out = pl.pallas_call(
        _ada_kernel,
        grid=(nmat, d3 // tn),
        in_specs=[
            pl.BlockSpec((nb, d), lambda m, n: (0, 0)),
            pl.BlockSpec((1, d, tn), lambda m, n: (m, 0, n)),
            pl.BlockSpec((1, 1, tn), lambda m, n: (m, 0, n)),
        ],
        out_specs=pl.BlockSpec((1, nb, tn), lambda m, n: (m, 0, n)),
        out_shape=jax.ShapeDtypeStruct((nmat, nb, d3), F32),
        compiler_params=_params(("arbitrary", "arbitrary")),
        name="ada_mods",
    )(c, w, b)
    return out.reshape(nmat * nb, 1, d3)


def _conv_kernel(x_ref, mod_ref, win_ref, cw_ref, wout_ref, g_ref, b_ref, o_ref, zbuf,
                 *, tiles_per_seq, alpha):
    i = pl.program_id(0)
    tm, d = x_ref.shape
    x = x_ref[...]
    shift, scale, gate = _split_mod(mod_ref[0], d)
    h = (x * scale + shift).astype(BF16)
    hw = jnp.dot(h, win_ref[...], preferred_element_type=F32)
    gb = hw[:, :d]
    z = hw[:, d:2 * d] * hw[:, 2 * d:]

    @pl.when(i % tiles_per_seq == 0)
    def _():
        zbuf[0:SUBLANES, :] = jnp.zeros((SUBLANES, d), F32)

    zbuf[SUBLANES:, :] = z
    z1 = zbuf[pl.ds(SUBLANES - 1, tm), :]
    z2 = zbuf[pl.ds(SUBLANES - 2, tm), :]
    cw = cw_ref[...]
    zc = cw[2:3] * z + cw[1:2] * z1 + cw[0:1] * z2
    zbuf[0:SUBLANES, :] = zbuf[tm:tm + SUBLANES, :]
    y = jnp.dot((gb * zc).astype(BF16), wout_ref[...], preferred_element_type=F32)
    o_ref[...] = _layer_norm(alpha * x + gate * y, g_ref[...], b_ref[...])


def _conv_layer(x, mods, mod_row, w_in, conv_w, w_out, ln_g, ln_b, *, seq, alpha, tm=512):
    t, d = x.shape
    tiles_per_seq = seq // tm
    return pl.pallas_call(
        functools.partial(_conv_kernel, tiles_per_seq=tiles_per_seq, alpha=alpha),
        grid=(t // tm,),
        in_specs=[
            pl.BlockSpec((tm, d), lambda i: (i, 0)),
            pl.BlockSpec((1, 1, 3 * d), lambda i: (mod_row + i // tiles_per_seq, 0, 0)),
            _const_spec((d, 3 * d)),
            _const_spec((conv_w.shape[0], d)),
            _const_spec((d, d)),
            _const_spec((1, d)),
            _const_spec((1, d)),
        ],
        out_specs=pl.BlockSpec((tm, d), lambda i: (i, 0)),
        out_shape=jax.ShapeDtypeStruct((t, d), F32),
        scratch_shapes=[pltpu.VMEM((tm + SUBLANES, d), F32)],
        compiler_params=_params(("arbitrary",)),
        name="conv_mixer",
    )(x, mods, w_in.astype(BF16), conv_w, w_out.astype(BF16), ln_g, ln_b)


def _qkv_kernel(x_ref, mod_ref, w_ref, qt_ref, k_ref, vt_ref, *, q_scale):
    tm, d = x_ref.shape
    shift, scale, _ = _split_mod(mod_ref[0], d)
    h = (x_ref[...] * scale + shift).astype(BF16)
    qkv = jnp.dot(h, w_ref[...], preferred_element_type=F32)
    qt_ref[...] = (qkv[:, :d] * q_scale).T.astype(BF16)
    k_ref[...] = qkv[:, d:2 * d].astype(BF16)
    vt_ref[...] = qkv[:, 2 * d:].T.astype(BF16)


def _qkv_proj(x, mods, mod_row, w_qkv, *, seq, q_scale, tm=512):
    t, d = x.shape
    tiles_per_seq = seq // tm
    return pl.pallas_call(
        functools.partial(_qkv_kernel, q_scale=q_scale),
        grid=(t // tm,),
        in_specs=[
            pl.BlockSpec((tm, d), lambda i: (i, 0)),
            pl.BlockSpec((1, 1, 3 * d), lambda i: (mod_row + i // tiles_per_seq, 0, 0)),
            _const_spec((d, 3 * d)),
        ],
        out_specs=[
            pl.BlockSpec((d, tm), lambda i: (0, i)),
            pl.BlockSpec((tm, d), lambda i: (i, 0)),
            pl.BlockSpec((d, tm), lambda i: (0, i)),
        ],
        out_shape=[
            jax.ShapeDtypeStruct((d, t), BF16),
            jax.ShapeDtypeStruct((t, d), BF16),
            jax.ShapeDtypeStruct((d, t), BF16),
        ],
        compiler_params=_params(("arbitrary",)),
        name="attn_qkv",
    )(x, mods, w_qkv.astype(BF16))


def _attn_kernel(slopes_ref, qt_ref, k_ref, vt_ref, lam_ref, g_ref, o_ref, qa_ref, m_ref, acc_ref,
                 kn_ref, *, tq, tk, heads, lambda_init):
    hg = pl.program_id(1)
    qi = pl.program_id(2)
    dh2 = qt_ref.shape[0] // heads
    dh = dh2 // 2
    ones_rows = 2 * SUBLANES
    blocks_per_tile = tq // tk

    feat = lax.broadcasted_iota(jnp.int32, (dh2, tq), 0)
    arow = lax.broadcasted_iota(jnp.int32, (dh2, 2 * tq), 0)
    acol = lax.broadcasted_iota(jnp.int32, (dh2, 2 * tq), 1)
    r = jnp.where(acol >= tq, acol - tq, acol)
    r_lo = (r % BF16_EXACT_INT).astype(F32)
    r_hi = (r - r % BF16_EXACT_INT).astype(F32)
    for g in range(heads):
        slope = slopes_ref[hg * heads + g]
        qt = qt_ref[g * dh2:(g + 1) * dh2, :]
        zero = jnp.zeros_like(qt)
        qa_ref[g, :dh2, :] = jnp.concatenate(
            [jnp.where(feat < dh, qt, zero), jnp.where(feat >= dh, qt, zero)], axis=1)
        qa_ref[g, dh2:, :] = jnp.where(
            arow == 0, -slope * r_lo,
            jnp.where(arow == 1, -slope * r_hi, jnp.where(arow == 2, slope, 0.0))).astype(BF16)
    kcol = lax.broadcasted_iota(jnp.int32, (tk, dh2), 1)
    krow = lax.broadcasted_iota(jnp.int32, (tk, dh2), 0).astype(F32)
    k_extra = jnp.where(kcol <= 1, 1.0, jnp.where(kcol == 2, krow, 0.0)).astype(BF16)
    ones = jnp.ones((ones_rows, tk), BF16)

    m_ref[...] = jnp.full(m_ref.shape, -jnp.inf, F32)
    acc_ref[...] = jnp.zeros(acc_ref.shape, F32)

    @pl.when(qi == 0)
    def _():
        ones_sq = jnp.ones((dh2, LANES), BF16)
        for g in range(heads):
            def chunk(c, best, g=g):
                kc = k_ref[pl.ds(pl.multiple_of(c * tq, tq), tq), g * dh2:(g + 1) * dh2].astype(F32)
                rows = jnp.dot((kc * kc).astype(BF16), ones_sq, preferred_element_type=F32)
                return jnp.maximum(best, jnp.max(rows, axis=0, keepdims=True))
            kn_ref[g] = lax.fori_loop(0, k_ref.shape[0] // tq, chunk, jnp.zeros((1, LANES), F32))

    def step(j, diagonal):
        start = pl.multiple_of(j * tk, tk)
        delta = qi * tq - j * tk
        head_cols = [slice(g * dh2, (g + 1) * dh2) for g in range(heads)]
        scores, maxima = [], []
        for g, cols in enumerate(head_cols):
            s = jnp.dot(jnp.concatenate([k_ref[pl.ds(start, tk), cols], k_extra], axis=1),
                        qa_ref[g], preferred_element_type=F32)
            if diagonal:
                mrow = lax.broadcasted_iota(jnp.int32, (tk, 2 * tq), 0)
                mcol = lax.broadcasted_iota(jnp.int32, (tk, 2 * tq), 1)
                mq = jnp.where(mcol >= tq, mcol - tq, mcol)
                s = jnp.where(mrow - mq <= delta, s, -jnp.inf)
            scores.append(s)
            maxima.append(jnp.max(s, axis=0, keepdims=True))
        probs, corrs = [], []
        for g, s in enumerate(scores):
            off = delta.astype(F32) * slopes_ref[hg * heads + g]
            m_old = m_ref[g]
            m_new = jnp.maximum(m_old, maxima[g] - off)
            probs.append(jnp.exp(s - (m_new + off)).astype(BF16))
            corrs.append(jnp.exp(m_old - m_new))
            m_ref[g] = m_new
        for g, cols in enumerate(head_cols):
            v_aug = jnp.concatenate([vt_ref[cols, pl.ds(start, tk)], ones], axis=0)
            acc_ref[g] = corrs[g] * acc_ref[g] + jnp.dot(v_aug, probs[g],
                                                         preferred_element_type=F32)

    first_diag = qi * blocks_per_tile
    for d in range(blocks_per_tile):
        step(first_diag + d, True)

    needed = jnp.zeros((1, 2 * tq), F32)
    for g in range(heads):
        qf = qa_ref[g, :dh2, :].astype(F32)
        q_norm = jnp.sqrt(jnp.sum(qf * qf, axis=0, keepdims=True))
        k_norm = jnp.sqrt(kn_ref[g][:, :1])
        reach = (q_norm * k_norm * NORM_SLACK - m_ref[g] + EXP_UNDERFLOW) / slopes_ref[hg * heads + g]
        needed = jnp.maximum(needed, jnp.floor((reach - 1.0) / tk) + 1.0)
    needed = jnp.clip(needed, 0.0, first_diag.astype(F32))
    n_blocks = jnp.max(needed.astype(jnp.int32))

    def body(i, carry):
        step(first_diag - 1 - i, False)
        return carry

    lax.fori_loop(0, n_blocks, body, 0)

    lam = lam_ref[...]
    lam_full = (jnp.exp(jnp.sum(lam[0:1] * lam[1:2], axis=1, keepdims=True))
                - jnp.exp(jnp.sum(lam[2:3] * lam[3:4], axis=1, keepdims=True)) + lambda_init)
    for g in range(heads):
        acc = acc_ref[g]
        on = acc[:dh2] / acc[dh2:dh2 + 1]
        o = on[:, :tq] - lam_full * on[:, tq:]
        o = o * lax.rsqrt(jnp.mean(o * o, axis=0, keepdims=True) + LN_EPS) * g_ref[...]
        o_ref[:, g * dh2:(g + 1) * dh2] = (o * (1.0 - lambda_init)).T.astype(o_ref.dtype)


def _diff_attention(qt, k, vt, lam, subln_g, *, batch, seq, lambda_init, tq=512, tk=256, heads=4):
    d, t = qt.shape
    dh2 = d // ATT_HEADS
    gw = heads * dh2
    nq = seq // tq
    assert tq % tk == 0 and tk <= BF16_EXACT_INT and tq < BF16_EXACT_INT * BF16_EXACT_INT
    slopes = 2.0 ** (-(8.0 / ATT_HEADS) * jnp.arange(1, ATT_HEADS + 1, dtype=F32))
    return pl.pallas_call(
        functools.partial(_attn_kernel, tq=tq, tk=tk, heads=heads, lambda_init=lambda_init),
        grid=(batch, ATT_HEADS // heads, nq),
        in_specs=[
            pl.BlockSpec(memory_space=pltpu.SMEM),
            pl.BlockSpec((gw, tq), lambda b, h, i: (h, b * nq + i)),
            pl.BlockSpec((seq, gw), lambda b, h, i: (b, h)),
            pl.BlockSpec((gw, seq), lambda b, h, i: (h, b)),
            pl.BlockSpec(lam.shape, lambda b, h, i: (0, 0)),
            pl.BlockSpec((dh2, 1), lambda b, h, i: (0, 0)),
        ],
        out_specs=pl.BlockSpec((tq, gw), lambda b, h, i: (b * nq + i, h)),
        out_shape=jax.ShapeDtypeStruct((t, d), BF16),
        scratch_shapes=[
            pltpu.VMEM((heads, 2 * dh2, 2 * tq), BF16),
            pltpu.VMEM((heads, 1, 2 * tq), F32),
            pltpu.VMEM((heads, dh2 + 2 * SUBLANES, 2 * tq), F32),
            pltpu.VMEM((heads, 1, LANES), F32),
        ],
        compiler_params=_params(("arbitrary", "arbitrary", "arbitrary")),
        name="diff_attention",
    )(slopes, qt, k, vt, lam, subln_g.reshape(dh2, 1))


def _oproj_kernel(o_ref, x_ref, mod_ref, w_ref, g_ref, b_ref, out_ref, *, alpha):
    tm, d = x_ref.shape
    _, _, gate = _split_mod(mod_ref[0], d)
    y = jnp.dot(o_ref[...], w_ref[...], preferred_element_type=F32)
    out_ref[...] = _layer_norm(alpha * x_ref[...] + gate * y, g_ref[...], b_ref[...])


def _out_proj(o, x, mods, mod_row, w_o, ln_g, ln_b, *, seq, alpha, tm=512):
    t, d = x.shape
    tiles_per_seq = seq // tm
    return pl.pallas_call(
        functools.partial(_oproj_kernel, alpha=alpha),
        grid=(t // tm,),
        in_specs=[
            pl.BlockSpec((tm, d), lambda i: (i, 0)),
            pl.BlockSpec((tm, d), lambda i: (i, 0)),
            pl.BlockSpec((1, 1, 3 * d), lambda i: (mod_row + i // tiles_per_seq, 0, 0)),
            _const_spec((d, d)),
            _const_spec((1, d)),
            _const_spec((1, d)),
        ],
        out_specs=pl.BlockSpec((tm, d), lambda i: (i, 0)),
        out_shape=jax.ShapeDtypeStruct((t, d), F32),
        compiler_params=_params(("arbitrary",)),
        name="attn_out_proj",
    )(o, x, mods, w_o.astype(BF16), ln_g, ln_b)


def _top16_rows(s):
    n, tm = s.shape
    rows = lax.broadcasted_iota(jnp.int32, (n, tm), 0).astype(F32)
    vals, idxs = [], []
    for _ in range(PEER_TOPK):
        m = jnp.max(s, axis=0, keepdims=True)
        idx = jnp.min(jnp.where(s == m, rows, float(n)), axis=0, keepdims=True)
        s = jnp.where(rows == idx, -jnp.inf, s)
        vals.append(m)
        idxs.append(idx)
    return vals, idxs


def _candidate_layout(tm):
    slot = lax.broadcasted_iota(jnp.int32, (N_CAND, tm), 0)
    grp = slot // SUBLANES
    r = slot % SUBLANES
    a = jnp.where(grp <= 1, 0, jnp.where(grp == N_CAND_GROUPS - 1, SUBLANES + r, grp - 1))
    b = jnp.where(grp == 0, r, jnp.where(grp == 1, SUBLANES + r,
                                         jnp.where(grp == N_CAND_GROUPS - 1, 0, r)))
    valid = (a + 1) * (b + 1) <= PEER_TOPK
    flat = (a * PEER_TOPK + b).astype(F32)
    return valid, flat


def _candidates(rows0, rows1):
    lo1 = jnp.concatenate(rows1[:SUBLANES], axis=0)
    hi1 = jnp.concatenate(rows1[SUBLANES:], axis=0)
    hi0 = jnp.concatenate(rows0[SUBLANES:], axis=0)
    groups = [rows0[0] + lo1, rows0[0] + hi1]
    groups += [rows0[a] + lo1 for a in range(1, SUBLANES)]
    groups.append(hi0 + rows1[0])
    return jnp.concatenate(groups, axis=0)


def _route_kernel(x_ref, mod_ref, wq_ref, keys_ref, h_ref, code_ref, gate_ref):
    tm, d = x_ref.shape
    shift, scale, _ = _split_mod(mod_ref[0], d)
    h = (x_ref[...] * scale + shift).astype(BF16)
    h_ref[...] = h
    q = jnp.dot(h, wq_ref[...], preferred_element_type=F32).astype(BF16)
    valid, flat = _candidate_layout(tm)
    half = PEER_KEYS
    for hd in range(PEER_HEADS):
        vals, idxs = [], []
        for p in range(2):
            qhp = q[:, (2 * hd + p) * half:(2 * hd + p + 1) * half]
            st = lax.dot_general(keys_ref[p], qhp, (((1,), (1,)), ((), ())),
                                 preferred_element_type=F32)
            v, ix = _top16_rows(st)
            vals.append(v)
            idxs.append(ix)
        cand = jnp.where(valid, _candidates(vals[0], vals[1]), -jnp.inf)
        code = _candidates([ix * float(PEER_KEYS) for ix in idxs[0]], idxs[1])
        top_s, top_code = [], []
        for _ in range(PEER_TOPK):
            m = jnp.max(cand, axis=0, keepdims=True)
            fmin = jnp.min(jnp.where(cand == m, flat, float(PEER_TOPK * PEER_TOPK)),
                           axis=0, keepdims=True)
            hit = flat == fmin
            top_code.append(jnp.sum(jnp.where(hit, code, 0.0), axis=0, keepdims=True))
            cand = jnp.where(hit, -jnp.inf, cand)
            top_s.append(m)
        ts = jnp.concatenate(top_s, axis=0)
        e = jnp.exp(ts - top_s[0])
        gate_ref[hd * PEER_TOPK:(hd + 1) * PEER_TOPK, :] = e / jnp.sum(e, axis=0, keepdims=True)
        code_ref[hd * PEER_TOPK:(hd + 1) * PEER_TOPK, :] = (
            jnp.concatenate(top_code, axis=0).astype(jnp.int32))


def _peer_route(x, mods, mod_row, w_q, sub_keys, *, seq, tm=256):
    t, d = x.shape
    tiles_per_seq = seq // tm
    nq = w_q.shape[1]
    nhk = PEER_HEADS * PEER_TOPK
    return pl.pallas_call(
        _route_kernel,
        grid=(t // tm,),
        in_specs=[
            pl.BlockSpec((tm, d), lambda i: (i, 0)),
            pl.BlockSpec((1, 1, 3 * d), lambda i: (mod_row + i // tiles_per_seq, 0, 0)),
            _const_spec((d, nq)),
            _const_spec(sub_keys.shape),
        ],
        out_specs=[
            pl.BlockSpec((tm, d), lambda i: (i, 0)),
            pl.BlockSpec((nhk, tm), lambda i: (0, i)),
            pl.BlockSpec((nhk, tm), lambda i: (0, i)),
        ],
        out_shape=[
            jax.ShapeDtypeStruct((t, d), BF16),
            jax.ShapeDtypeStruct((nhk, t), jnp.int32),
            jax.ShapeDtypeStruct((nhk, t), F32),
        ],
        compiler_params=_params(("arbitrary",)),
        name="peer_route",
    )(x, mods, w_q.astype(BF16), sub_keys.astype(BF16))


def _expert_kernel(h_ref, x_ref, code_ref, gate_ref, mod_ref, ut_ref, v_ref, g_ref, b_ref,
                   o_ref, ci_ref, cj_ref, cg_ref, w_ref, acc_ref, *, alpha):
    j = pl.program_id(1)
    nj = pl.num_programs(1)
    tm, d = x_ref.shape
    tn = ut_ref.shape[1]
    nhk = code_ref.shape[0]

    @pl.when(j == 0)
    def _():
        code = code_ref[...].T
        ci_ref[...] = code // PEER_KEYS
        cj_ref[...] = code % PEER_KEYS
        cg_ref[...] = gate_ref[...].T
        row = lax.broadcasted_iota(jnp.int32, (PEER_KEYS, nhk), 0)
        key_i = jnp.where(row < PEER_KEYS // 2, 2 * row, 2 * row - (PEER_KEYS - 1)).astype(BF16)
        key_j = row.astype(BF16)
        zero = jnp.zeros((PEER_KEYS, nhk), BF16)
        one = jnp.ones((PEER_KEYS, nhk), BF16)

        def token_group(grp, carry):
            base = pl.multiple_of(grp * W_GROUP, W_GROUP)
            for u in range(W_GROUP):
                t = base + u
                ri = ci_ref[pl.ds(t, 1), :].astype(BF16)
                rj = cj_ref[pl.ds(t, 1), :].astype(BF16)
                rg = cg_ref[pl.ds(t, 1), :].astype(BF16)
                pt = jnp.where(key_i == ri, jnp.broadcast_to(rg, zero.shape), zero)
                qt = jnp.where(key_j == rj, one, zero)
                wt = lax.dot_general(pt, qt, (((1,), (1,)), ((), ())),
                                     preferred_element_type=F32)
                lo = lax.bitcast_convert_type(wt[:PEER_KEYS // 2], jnp.uint32) >> 16
                hi = lax.bitcast_convert_type(wt[PEER_KEYS // 2:], jnp.uint32) & jnp.uint32(0xFFFF0000)
                w_ref[pl.ds(pl.multiple_of(t * W_PITCH, SUBLANES), PEER_KEYS // 2), :] = lo | hi
            return carry

        lax.fori_loop(0, tm // W_GROUP, token_group, 0)
        acc_ref[...] = jnp.zeros(acc_ref.shape, F32)

    a = jnp.dot(h_ref[...], ut_ref[...], preferred_element_type=F32)
    act = 0.5 * a * (1.0 + lax.erf(a * (1.0 / math.sqrt(2.0))))
    pieces = []
    for r in range(tn // (2 * PEER_KEYS)):
        words = w_ref[pl.ds(j * (tn // (2 * PEER_KEYS)) + r, tm, stride=W_PITCH), :]
        even = lax.bitcast_convert_type(words << 16, F32)
        odd = lax.bitcast_convert_type(words & jnp.uint32(0xFFFF0000), F32)
        c0 = 2 * r * PEER_KEYS
        pieces.append((even * act[:, c0:c0 + PEER_KEYS]).astype(BF16))
        pieces.append((odd * act[:, c0 + PEER_KEYS:c0 + 2 * PEER_KEYS]).astype(BF16))
    z = jnp.concatenate(pieces, axis=1)
    acc_ref[...] += jnp.dot(z, v_ref[...], preferred_element_type=F32)

    @pl.when(j == nj - 1)
    def _():
        _, _, gate = _split_mod(mod_ref[0], d)
        o_ref[...] = _layer_norm(alpha * x_ref[...] + gate * acc_ref[...], g_ref[...], b_ref[...])


def _peer_experts(h, x, code, gates, mods, mod_row, u_t, v, ln_g, ln_b, *, seq, alpha,
                  tm=512, tn=1024):
    t, d = x.shape
    n = v.shape[0]
    nhk = code.shape[0]
    tiles_per_seq = seq // tm
    return pl.pallas_call(
        functools.partial(_expert_kernel, alpha=alpha),
        grid=(t // tm, n // tn),
        in_specs=[
            pl.BlockSpec((tm, d), lambda i, j: (i, 0)),
            pl.BlockSpec((tm, d), lambda i, j: (i, 0)),
            pl.BlockSpec((nhk, tm), lambda i, j: (0, i)),
            pl.BlockSpec((nhk, tm), lambda i, j: (0, i)),
            pl.BlockSpec((1, 1, 3 * d), lambda i, j: (mod_row + i // tiles_per_seq, 0, 0)),
            pl.BlockSpec((d, tn), lambda i, j: (0, j)),
            pl.BlockSpec((tn, d), lambda i, j: (j, 0)),
            _const_spec((1, d)),
            _const_spec((1, d)),
        ],
        out_specs=pl.BlockSpec((tm, d), lambda i, j: (i, 0)),
        out_shape=jax.ShapeDtypeStruct((t, d), F32),
        scratch_shapes=[
            pltpu.VMEM((tm, nhk), jnp.int32),
            pltpu.VMEM((tm, nhk), jnp.int32),
            pltpu.VMEM((tm, nhk), F32),
            pltpu.VMEM((tm * W_PITCH, PEER_KEYS), jnp.uint32),
            pltpu.VMEM((tm, d), F32),
        ],
        compiler_params=_params(("arbitrary", "arbitrary")),
        name="peer_experts",
    )(h, x, code, gates, mods, u_t, v, ln_g, ln_b)


def _lambda_init(layer_idx):
    return 0.8 - 0.6 * math.exp(-0.3 * layer_idx)


def kernel(x, c, ada_w, ada_b, ln_g, ln_b, conv_w_in, conv_w, conv_w_out, attn_w_qkv, attn_lambda,
           attn_subln_g, attn_w_o, peer_w_q, peer_sub_keys, peer_u, peer_v):
    batch, seq, d = x.shape
    depth = ada_w.shape[0]
    alpha = (2.0 * depth) ** 0.25
    head_dim = d // (2 * ATT_HEADS)
    mods = _ada_mods(c, ada_w, ada_b)
    xt = x.reshape(batch * seq, d)
    for i in range(depth):
        j = i // N_MIXERS
        row = (2 * i) * batch
        g0, b0 = ln_g[i, 0].reshape(1, d), ln_b[i, 0].reshape(1, d)
        if i % N_MIXERS == 0:
            xt = _conv_layer(xt, mods, row, conv_w_in[j], conv_w[j], conv_w_out[j], g0, b0,
                             seq=seq, alpha=alpha)
        else:
            qt, k, vt = _qkv_proj(xt, mods, row, attn_w_qkv[j], seq=seq, q_scale=head_dim ** -0.5)
            o = _diff_attention(qt, k, vt, attn_lambda[j], attn_subln_g[j], batch=batch, seq=seq,
                                lambda_init=_lambda_init(i))
            xt = _out_proj(o, xt, mods, row, attn_w_o[j], g0, b0, seq=seq, alpha=alpha)
        row = (2 * i + 1) * batch
        g1, b1 = ln_g[i, 1].reshape(1, d), ln_b[i, 1].reshape(1, d)
        h, code, gates = _peer_route(xt, mods, row, peer_w_q[i], peer_sub_keys[i], seq=seq)
        xt = _peer_experts(h, xt, code, gates, mods, row, peer_u[i].astype(BF16).T,
                           peer_v[i].astype(BF16), g1, b1, seq=seq, alpha=alpha)
    return xt.reshape(batch, seq, d)
```

```python
import functools
import math

import jax
import jax.numpy as jnp
from jax import lax
from jax.experimental import pallas as pl
from jax.experimental.pallas import tpu as pltpu

F32 = jnp.float32
BF16 = jnp.bfloat16

N_MIXERS = 2
ATT_HEADS = 8
PEER_HEADS = 8
PEER_KEYS = 128
PEER_TOPK = 16
LN_EPS = 1e-5

LANES = 128
SUBLANES = 8
VMEM_LIMIT_BYTES = 56 * 1024 * 1024
BF16_EXACT_INT = 256
EXP_UNDERFLOW = 106.0
NORM_SLACK = 1.02

N_CAND_GROUPS = 10
N_CAND = N_CAND_GROUPS * SUBLANES
W_PITCH = 72
W_GROUP = 32
ROUTE_UNITS = 16


def _layer_norm(r, g, b):
    mu = jnp.mean(r, axis=-1, keepdims=True)
    d = r - mu
    var = jnp.mean(d * d, axis=-1, keepdims=True)
    return d * lax.rsqrt(var + LN_EPS) * g + b


def _split_mod(mod, d):
    return mod[:, :d], 1.0 + mod[:, d:2 * d], 1.0 + mod[:, 2 * d:]


def _params(sem):
    return pltpu.CompilerParams(dimension_semantics=sem, vmem_limit_bytes=VMEM_LIMIT_BYTES)


def _const_spec(shape):
    nd = len(shape)
    return pl.BlockSpec(shape, lambda *_: (0,) * nd, pipeline_mode=pl.Buffered(1))


def _ada_kernel(c_ref, w_ref, b_ref, o_ref):
    c = c_ref[...]
    sc = c / (1.0 + jnp.exp(-c))
    o_ref[0] = jnp.dot(sc, w_ref[0], precision=lax.Precision.HIGHEST,
                       preferred_element_type=F32) + b_ref[0]


def _ada_mods(c, ada_w, ada_b):
    depth, _, d, d3 = ada_w.shape
    nb = c.shape[0]
    nmat = depth * 2
    tn = 1024
    w = ada_w.reshape(nmat, d, d3)
    b = ada_b.reshape(nmat, 1, d3)
    out = pl.pallas_call(
        _ada_kernel,
        grid=(nmat, d3 // tn),
        in_specs=[
            pl.BlockSpec((nb, d), lambda m, n: (0, 0)),
            pl.BlockSpec((1, d, tn), lambda m, n: (m, 0, n)),
            pl.BlockSpec((1, 1, tn), lambda m, n: (m, 0, n)),
        ],
        out_specs=pl.BlockSpec((1, nb, tn), lambda m, n: (m, 0, n)),
        out_shape=jax.ShapeDtypeStruct((nmat, nb, d3), F32),
        compiler_params=_params(("arbitrary", "arbitrary")),
        name="ada_mods",
    )(c, w, b)
    return out.reshape(nmat * nb, 1, d3)


def _emit_stage_output(r, g_ref, b_ref, modp_ref, o_ref, hp_ref):
    xn = _layer_norm(r, g_ref[...], b_ref[...])
    o_ref[...] = xn
    shift, scale, _ = _split_mod(modp_ref[0], xn.shape[1])
    hp_ref[...] = (xn * scale + shift).astype(BF16)


def _conv_kernel(x_ref, mod_ref, modp_ref, win_ref, cw_ref, wout_ref, g_ref, b_ref, o_ref, hp_ref,
                 zbuf, *, tiles_per_seq, alpha):
    i = pl.program_id(0)
    tm, d = x_ref.shape
    x = x_ref[...]
    shift, scale, gate = _split_mod(mod_ref[0], d)
    h = (x * scale + shift).astype(BF16)
    hw = jnp.dot(h, win_ref[...], preferred_element_type=F32)
    gb = hw[:, :d]
    z = hw[:, d:2 * d] * hw[:, 2 * d:]

    @pl.when(i % tiles_per_seq == 0)
    def _():
        zbuf[0:SUBLANES, :] = jnp.zeros((SUBLANES, d), F32)

    zbuf[SUBLANES:, :] = z
    z1 = zbuf[pl.ds(SUBLANES - 1, tm), :]
    z2 = zbuf[pl.ds(SUBLANES - 2, tm), :]
    cw = cw_ref[...]
    zc = cw[2:3] * z + cw[1:2] * z1 + cw[0:1] * z2
    zbuf[0:SUBLANES, :] = zbuf[tm:tm + SUBLANES, :]
    y = jnp.dot((gb * zc).astype(BF16), wout_ref[...], preferred_element_type=F32)
    _emit_stage_output(alpha * x + gate * y, g_ref, b_ref, modp_ref, o_ref, hp_ref)


def _conv_layer(x, mods, mod_row, peer_row, w_in, conv_w, w_out, ln_g, ln_b, *, seq, alpha, tm=512):
    t, d = x.shape
    tiles_per_seq = seq // tm
    return pl.pallas_call(
        functools.partial(_conv_kernel, tiles_per_seq=tiles_per_seq, alpha=alpha),
        grid=(t // tm,),
        in_specs=[
            pl.BlockSpec((tm, d), lambda i: (i, 0)),
            pl.BlockSpec((1, 1, 3 * d), lambda i: (mod_row + i // tiles_per_seq, 0, 0)),
            pl.BlockSpec((1, 1, 3 * d), lambda i: (peer_row + i // tiles_per_seq, 0, 0)),
            _const_spec((d, 3 * d)),
            _const_spec((conv_w.shape[0], d)),
            _const_spec((d, d)),
            _const_spec((1, d)),
            _const_spec((1, d)),
        ],
        out_specs=[pl.BlockSpec((tm, d), lambda i: (i, 0))] * 2,
        out_shape=[jax.ShapeDtypeStruct((t, d), F32), jax.ShapeDtypeStruct((t, d), BF16)],
        scratch_shapes=[pltpu.VMEM((tm + SUBLANES, d), F32)],
        compiler_params=_params(("arbitrary",)),
        name="conv_mixer",
    )(x, mods, mods, w_in.astype(BF16), conv_w, w_out.astype(BF16), ln_g, ln_b)


def _qkv_kernel(x_ref, mod_ref, w_ref, qt_ref, k_ref, vt_ref, *, q_scale):
    tm, d = x_ref.shape
    shift, scale, _ = _split_mod(mod_ref[0], d)
    h = (x_ref[...] * scale + shift).astype(BF16)
    qkv = jnp.dot(h, w_ref[...], preferred_element_type=F32)
    qt_ref[...] = (qkv[:, :d] * q_scale).T.astype(BF16)
    k_ref[...] = qkv[:, d:2 * d].astype(BF16)
    vt_ref[...] = qkv[:, 2 * d:].T.astype(BF16)


def _qkv_proj(x, mods, mod_row, w_qkv, *, seq, q_scale, tm=512):
    t, d = x.shape
    tiles_per_seq = seq // tm
    return pl.pallas_call(
        functools.partial(_qkv_kernel, q_scale=q_scale),
        grid=(t // tm,),
        in_specs=[
            pl.BlockSpec((tm, d), lambda i: (i, 0)),
            pl.BlockSpec((1, 1, 3 * d), lambda i: (mod_row + i // tiles_per_seq, 0, 0)),
            _const_spec((d, 3 * d)),
        ],
        out_specs=[
            pl.BlockSpec((d, tm), lambda i: (0, i)),
            pl.BlockSpec((tm, d), lambda i: (i, 0)),
            pl.BlockSpec((d, tm), lambda i: (0, i)),
        ],
        out_shape=[
            jax.ShapeDtypeStruct((d, t), BF16),
            jax.ShapeDtypeStruct((t, d), BF16),
            jax.ShapeDtypeStruct((d, t), BF16),
        ],
        compiler_params=_params(("arbitrary",)),
        name="attn_qkv",
    )(x, mods, w_qkv.astype(BF16))


def _attn_kernel(slopes_ref, qt_ref, k_ref, vt_ref, lam_ref, g_ref, o_ref, qa_ref, m_ref, acc_ref,
                 kn_ref, *, tq, tk, heads, lambda_init):
    hg = pl.program_id(1)
    qi = pl.program_id(2)
    dh2 = qt_ref.shape[0] // heads
    dh = dh2 // 2
    ones_rows = 2 * SUBLANES
    blocks_per_tile = tq // tk

    feat = lax.broadcasted_iota(jnp.int32, (dh2, tq), 0)
    arow = lax.broadcasted_iota(jnp.int32, (dh2, 2 * tq), 0)
    acol = lax.broadcasted_iota(jnp.int32, (dh2, 2 * tq), 1)
    r = jnp.where(acol >= tq, acol - tq, acol)
    r_lo = (r % BF16_EXACT_INT).astype(F32)
    r_hi = (r - r % BF16_EXACT_INT).astype(F32)
    for g in range(heads):
        slope = slopes_ref[hg * heads + g]
        qt = qt_ref[g * dh2:(g + 1) * dh2, :]
        zero = jnp.zeros_like(qt)
        qa_ref[g, :dh2, :] = jnp.concatenate(
            [jnp.where(feat < dh, qt, zero), jnp.where(feat >= dh, qt, zero)], axis=1)
        qa_ref[g, dh2:, :] = jnp.where(
            arow == 0, -slope * r_lo,
            jnp.where(arow == 1, -slope * r_hi, jnp.where(arow == 2, slope, 0.0))).astype(BF16)
    kcol = lax.broadcasted_iota(jnp.int32, (tk, dh2), 1)
    krow = lax.broadcasted_iota(jnp.int32, (tk, dh2), 0).astype(F32)
    k_extra = jnp.where(kcol <= 1, 1.0, jnp.where(kcol == 2, krow, 0.0)).astype(BF16)
    ones = jnp.ones((ones_rows, tk), BF16)

    m_ref[...] = jnp.full(m_ref.shape, -jnp.inf, F32)
    acc_ref[...] = jnp.zeros(acc_ref.shape, F32)

    @pl.when(qi == 0)
    def _():
        ones_sq = jnp.ones((dh2, LANES), BF16)
        for g in range(heads):
            def chunk(c, best, g=g):
                kc = k_ref[pl.ds(pl.multiple_of(c * tq, tq), tq), g * dh2:(g + 1) * dh2].astype(F32)
                rows = jnp.dot((kc * kc).astype(BF16), ones_sq, preferred_element_type=F32)
                return jnp.maximum(best, jnp.max(rows, axis=0, keepdims=True))
            kn_ref[g] = lax.fori_loop(0, k_ref.shape[0] // tq, chunk, jnp.zeros((1, LANES), F32))

    def step(j, diagonal):
        start = pl.multiple_of(j * tk, tk)
        delta = qi * tq - j * tk
        head_cols = [slice(g * dh2, (g + 1) * dh2) for g in range(heads)]
        scores, maxima = [], []
        for g, cols in enumerate(head_cols):
            s = jnp.dot(jnp.concatenate([k_ref[pl.ds(start, tk), cols], k_extra], axis=1),
                        qa_ref[g], preferred_element_type=F32)
            if diagonal:
                mrow = lax.broadcasted_iota(jnp.int32, (tk, 2 * tq), 0)
                mcol = lax.broadcasted_iota(jnp.int32, (tk, 2 * tq), 1)
                mq = jnp.where(mcol >= tq, mcol - tq, mcol)
                s = jnp.where(mrow - mq <= delta, s, -jnp.inf)
            scores.append(s)
            maxima.append(jnp.max(s, axis=0, keepdims=True))
        probs, corrs = [], []
        for g, s in enumerate(scores):
            off = delta.astype(F32) * slopes_ref[hg * heads + g]
            m_old = m_ref[g]
            m_new = jnp.maximum(m_old, maxima[g] - off)
            probs.append(jnp.exp(s - (m_new + off)).astype(BF16))
            corrs.append(jnp.exp(m_old - m_new))
            m_ref[g] = m_new
        for g, cols in enumerate(head_cols):
            v_aug = jnp.concatenate([vt_ref[cols, pl.ds(start, tk)], ones], axis=0)
            acc_ref[g] = corrs[g] * acc_ref[g] + jnp.dot(v_aug, probs[g],
                                                         preferred_element_type=F32)

    first_diag = qi * blocks_per_tile
    for d in range(blocks_per_tile):
        step(first_diag + d, True)

    needed = jnp.zeros((1, 2 * tq), F32)
    for g in range(heads):
        qf = qa_ref[g, :dh2, :].astype(F32)
        q_norm = jnp.sqrt(jnp.sum(qf * qf, axis=0, keepdims=True))
        k_norm = jnp.sqrt(kn_ref[g][:, :1])
        reach = (q_norm * k_norm * NORM_SLACK - m_ref[g] + EXP_UNDERFLOW) / slopes_ref[hg * heads + g]
        needed = jnp.maximum(needed, jnp.floor((reach - 1.0) / tk) + 1.0)
    needed = jnp.clip(needed, 0.0, first_diag.astype(F32))
    n_blocks = jnp.max(needed.astype(jnp.int32))

    def body(i, carry):
        step(first_diag - 1 - i, False)
        return carry

    lax.fori_loop(0, n_blocks, body, 0)

    lam = lam_ref[...]
    lam_full = (jnp.exp(jnp.sum(lam[0:1] * lam[1:2], axis=1, keepdims=True))
                - jnp.exp(jnp.sum(lam[2:3] * lam[3:4], axis=1, keepdims=True)) + lambda_init)
    for g in range(heads):
        acc = acc_ref[g]
        on = acc[:dh2] / acc[dh2:dh2 + 1]
        o = on[:, :tq] - lam_full * on[:, tq:]
        o = o * lax.rsqrt(jnp.mean(o * o, axis=0, keepdims=True) + LN_EPS) * g_ref[...]
        o_ref[:, g * dh2:(g + 1) * dh2] = (o * (1.0 - lambda_init)).T.astype(o_ref.dtype)


def _diff_attention(qt, k, vt, lam, subln_g, *, batch, seq, lambda_init, tq=512, tk=256, heads=4):
    d, t = qt.shape
    dh2 = d // ATT_HEADS
    gw = heads * dh2
    nq = seq // tq
    assert tq % tk == 0 and tk <= BF16_EXACT_INT and tq < BF16_EXACT_INT * BF16_EXACT_INT
    slopes = 2.0 ** (-(8.0 / ATT_HEADS) * jnp.arange(1, ATT_HEADS + 1, dtype=F32))
    return pl.pallas_call(
        functools.partial(_attn_kernel, tq=tq, tk=tk, heads=heads, lambda_init=lambda_init),
        grid=(batch, ATT_HEADS // heads, nq),
        in_specs=[
            pl.BlockSpec(memory_space=pltpu.SMEM),
            pl.BlockSpec((gw, tq), lambda b, h, i: (h, b * nq + i)),
            pl.BlockSpec((seq, gw), lambda b, h, i: (b, h)),
            pl.BlockSpec((gw, seq), lambda b, h, i: (h, b)),
            pl.BlockSpec(lam.shape, lambda b, h, i: (0, 0)),
            pl.BlockSpec((dh2, 1), lambda b, h, i: (0, 0)),
        ],
        out_specs=pl.BlockSpec((tq, gw), lambda b, h, i: (b * nq + i, h)),
        out_shape=jax.ShapeDtypeStruct((t, d), BF16),
        scratch_shapes=[
            pltpu.VMEM((heads, 2 * dh2, 2 * tq), BF16),
            pltpu.VMEM((heads, 1, 2 * tq), F32),
            pltpu.VMEM((heads, dh2 + 2 * SUBLANES, 2 * tq), F32),
            pltpu.VMEM((heads, 1, LANES), F32),
        ],
        compiler_params=_params(("arbitrary", "arbitrary", "arbitrary")),
        name="diff_attention",
    )(slopes, qt, k, vt, lam, subln_g.reshape(dh2, 1))


def _oproj_kernel(o_ref, x_ref, mod_ref, modp_ref, w_ref, g_ref, b_ref, out_ref, hp_ref, *, alpha):
    tm, d = x_ref.shape
    _, _, gate = _split_mod(mod_ref[0], d)
    y = jnp.dot(o_ref[...], w_ref[...], preferred_element_type=F32)
    _emit_stage_output(alpha * x_ref[...] + gate * y, g_ref, b_ref, modp_ref, out_ref, hp_ref)


def _out_proj(o, x, mods, mod_row, peer_row, w_o, ln_g, ln_b, *, seq, alpha, tm=512):
    t, d = x.shape
    tiles_per_seq = seq // tm
    return pl.pallas_call(
        functools.partial(_oproj_kernel, alpha=alpha),
        grid=(t // tm,),
        in_specs=[
            pl.BlockSpec((tm, d), lambda i: (i, 0)),
            pl.BlockSpec((tm, d), lambda i: (i, 0)),
            pl.BlockSpec((1, 1, 3 * d), lambda i: (mod_row + i // tiles_per_seq, 0, 0)),
            pl.BlockSpec((1, 1, 3 * d), lambda i: (peer_row + i // tiles_per_seq, 0, 0)),
            _const_spec((d, d)),
            _const_spec((1, d)),
            _const_spec((1, d)),
        ],
        out_specs=[pl.BlockSpec((tm, d), lambda i: (i, 0))] * 2,
        out_shape=[jax.ShapeDtypeStruct((t, d), F32), jax.ShapeDtypeStruct((t, d), BF16)],
        compiler_params=_params(("arbitrary",)),
        name="attn_out_proj",
    )(o, x, mods, mods, w_o.astype(BF16), ln_g, ln_b)


def _top16_rows(s):
    n, tm = s.shape
    rows = lax.broadcasted_iota(jnp.int32, (n, tm), 0).astype(F32)
    vals, idxs = [], []
    tie = None
    for _ in range(PEER_TOPK):
        m = jnp.max(s, axis=0, keepdims=True)
        if tie is not None:
            m = m + tie
        idx = jnp.min(jnp.where(s == m, rows, float(n)), axis=0, keepdims=True)
        s = jnp.where(rows == idx, -jnp.inf, s)
        vals.append(m)
        idxs.append(idx)
        tie = yield
    return vals, idxs


def _candidate_layout(tm):
    slot = lax.broadcasted_iota(jnp.int32, (N_CAND, tm), 0)
    grp = slot // SUBLANES
    r = slot % SUBLANES
    a = jnp.where(grp <= 1, 0, jnp.where(grp == N_CAND_GROUPS - 1, SUBLANES + r, grp - 1))
    b = jnp.where(grp == 0, r, jnp.where(grp == 1, SUBLANES + r,
                                         jnp.where(grp == N_CAND_GROUPS - 1, 0, r)))
    valid = (a + 1) * (b + 1) <= PEER_TOPK
    flat = (a * PEER_TOPK + b).astype(F32)
    return valid, flat


def _candidates(rows0, rows1):
    lo1 = jnp.concatenate(rows1[:SUBLANES], axis=0)
    hi1 = jnp.concatenate(rows1[SUBLANES:], axis=0)
    hi0 = jnp.concatenate(rows0[SUBLANES:], axis=0)
    groups = [rows0[0] + lo1, rows0[0] + hi1]
    groups += [rows0[a] + lo1 for a in range(1, SUBLANES)]
    groups.append(hi0 + rows1[0])
    return jnp.concatenate(groups, axis=0)


def _route_head(q_head, keys_ref, valid, flat):
    scores = [lax.dot_general(keys_ref[p], q_head[:, p * PEER_KEYS:(p + 1) * PEER_KEYS],
                              (((1,), (1,)), ((), ())), preferred_element_type=F32)
              for p in range(2)]
    yield
    vals, idxs = [], []
    for st in scores:
        v, ix = yield from _top16_rows(st)
        vals.append(v)
        idxs.append(ix)
    cand = jnp.where(valid, _candidates(vals[0], vals[1]), -jnp.inf)
    code = _candidates([ix * float(PEER_KEYS) for ix in idxs[0]], idxs[1])
    top_s, top_code = [], []
    tie = None
    for _ in range(PEER_TOPK):
        m = jnp.max(cand, axis=0, keepdims=True)
        if tie is not None:
            m = m + tie
        fmin = jnp.min(jnp.where(cand == m, flat, float(PEER_TOPK * PEER_TOPK)),
                       axis=0, keepdims=True)
        hit = flat == fmin
        top_code.append(jnp.sum(jnp.where(hit, code, 0.0), axis=0, keepdims=True))
        cand = jnp.where(hit, -jnp.inf, cand)
        top_s.append(m)
        tie = yield
    ts = jnp.concatenate(top_s, axis=0)
    e = jnp.exp(ts - top_s[0])
    return (jnp.concatenate(top_code, axis=0).astype(jnp.int32),
            e / jnp.sum(e, axis=0, keepdims=True))


ROUTE_YIELDS = 3 * PEER_TOPK


def _zero_row(x, width):
    bits = lax.bitcast_convert_type(x[:1, :LANES], jnp.uint32)
    zero = lax.bitcast_convert_type((bits >> 16) >> 16, F32)
    return jnp.concatenate([zero] * (width // LANES), axis=1)


def _peer_kernel(h_ref, hn_ref, x_ref, mod_ref, wq_ref, keys_ref, ut_ref, v_ref, g_ref, b_ref,
                 o_ref, q_ref, code_ref, gate_ref, ci_ref, cj_ref, cg_ref, w_ref, acc_ref, hs_ref,
                 *, alpha):
    i = pl.program_id(0)
    j = pl.program_id(1)
    nj = pl.num_programs(1)
    tm, d = x_ref.shape
    tn = ut_ref.shape[1]
    nhk = PEER_HEADS * PEER_TOPK
    parts = ROUTE_UNITS // PEER_HEADS
    tp = tm // parts
    slot = i % 2
    valid, flat = _candidate_layout(tp)

    def prepare_queries(h):
        q = jnp.dot(h, wq_ref[...], preferred_element_type=F32).astype(BF16)
        for hd in range(PEER_HEADS):
            q_ref[hd] = q[:, hd * 2 * PEER_KEYS:(hd + 1) * 2 * PEER_KEYS]

    def route_unit(u, dst):
        hd = u // parts
        part = u % parts
        rows = pl.ds(pl.multiple_of(part * tp, tp), tp)
        code, gate = yield from _route_head(q_ref[hd, rows, :], keys_ref, valid, flat)
        krows = pl.ds(pl.multiple_of(hd * PEER_TOPK, PEER_TOPK), PEER_TOPK)
        code_ref[dst, part, krows, :] = code
        gate_ref[dst, part, krows, :] = gate

    @pl.when((i == 0) & (j == 0))
    def _():
        prepare_queries(h_ref[...])

        def unit(u, carry):
            for _ in route_unit(u, 0):
                pass
            return carry

        lax.fori_loop(0, ROUTE_UNITS, unit, 0)

    @pl.when(j == 0)
    def _():
        for part in range(parts):
            code = code_ref[slot, part].T
            ci_ref[part * tp:(part + 1) * tp, :] = code // PEER_KEYS
            cj_ref[part * tp:(part + 1) * tp, :] = code % PEER_KEYS
            cg_ref[part * tp:(part + 1) * tp, :] = gate_ref[slot, part].T
        row = lax.broadcasted_iota(jnp.int32, (PEER_KEYS, nhk), 0)
        key_i = jnp.where(row < PEER_KEYS // 2, 2 * row, 2 * row - (PEER_KEYS - 1)).astype(BF16)
        key_j = row.astype(BF16)
        zero = jnp.zeros((PEER_KEYS, nhk), BF16)
        one = jnp.ones((PEER_KEYS, nhk), BF16)

        def token_group(grp, carry):
            base = pl.multiple_of(grp * W_GROUP, W_GROUP)
            for u in range(W_GROUP):
                t = base + u
                ri = ci_ref[pl.ds(t, 1), :].astype(BF16)
                rj = cj_ref[pl.ds(t, 1), :].astype(BF16)
                rg = cg_ref[pl.ds(t, 1), :].astype(BF16)
                pt = jnp.where(key_i == ri, jnp.broadcast_to(rg, zero.shape), zero)
                qt = jnp.where(key_j == rj, one, zero)
                wt = lax.dot_general(pt, qt, (((1,), (1,)), ((), ())),
                                     preferred_element_type=F32)
                lo = lax.bitcast_convert_type(wt[:PEER_KEYS // 2], jnp.uint32) >> 16
                hi = lax.bitcast_convert_type(wt[PEER_KEYS // 2:], jnp.uint32) & jnp.uint32(0xFFFF0000)
                w_ref[pl.ds(pl.multiple_of(t * W_PITCH, SUBLANES), PEER_KEYS // 2), :] = lo | hi
            return carry

        lax.fori_loop(0, tm // W_GROUP, token_group, 0)
        acc_ref[...] = jnp.zeros(acc_ref.shape, F32)
        hs_ref[...] = h_ref[...]
        prepare_queries(hn_ref[...])

    route = route_unit(j, 1 - slot)
    next(route)
    chunk = 2 * PEER_KEYS
    n_chunks = tn // chunk
    per_chunk = -(-ROUTE_YIELDS // (2 * n_chunks))

    def advance(after):
        try:
            route.send(None if after is None else _zero_row(after, tp))
            for _ in range(per_chunk - 1):
                next(route)
        except StopIteration:
            pass

    advance(None)
    h = hs_ref[...]
    pieces = []
    for r in range(n_chunks):
        c0 = r * chunk
        a = jnp.dot(h, ut_ref[:, c0:c0 + chunk], preferred_element_type=F32)
        advance(a)
        act = 0.5 * a * (1.0 + lax.erf(a * (1.0 / math.sqrt(2.0))))
        words = w_ref[pl.ds(j * n_chunks + r, tm, stride=W_PITCH), :]
        even = lax.bitcast_convert_type(words << 16, F32)
        odd = lax.bitcast_convert_type(words & jnp.uint32(0xFFFF0000), F32)
        pieces.append((even * act[:, :PEER_KEYS]).astype(BF16))
        pieces.append((odd * act[:, PEER_KEYS:]).astype(BF16))
    z = jnp.concatenate(pieces, axis=1)
    for c in range(d // chunk):
        cols = slice(c * chunk, (c + 1) * chunk)
        zv = jnp.dot(z, v_ref[:, cols], preferred_element_type=F32)
        acc_ref[:, cols] += zv
        advance(zv)
    for _ in route:
        pass

    @pl.when(j == nj - 1)
    def _():
        _, _, gate = _split_mod(mod_ref[0], d)
        o_ref[...] = _layer_norm(alpha * x_ref[...] + gate * acc_ref[...], g_ref[...], b_ref[...])


def _peer_layer(h, x, mods, mod_row, w_q, sub_keys, u_t, v, ln_g, ln_b, *, seq, alpha, tm=512):
    t, d = x.shape
    n = v.shape[0]
    tn = n // ROUTE_UNITS
    nhk = PEER_HEADS * PEER_TOPK
    parts = ROUTE_UNITS // PEER_HEADS
    tiles_per_seq = seq // tm
    last = t // tm - 1
    return pl.pallas_call(
        functools.partial(_peer_kernel, alpha=alpha),
        grid=(t // tm, n // tn),
        in_specs=[
            pl.BlockSpec((tm, d), lambda i, j: (i, 0)),
            pl.BlockSpec((tm, d), lambda i, j: (jnp.minimum(i + 1, last), 0)),
            pl.BlockSpec((tm, d), lambda i, j: (i, 0)),
            pl.BlockSpec((1, 1, 3 * d), lambda i, j: (mod_row + i // tiles_per_seq, 0, 0)),
            _const_spec(w_q.shape),
            _const_spec(sub_keys.shape),
            pl.BlockSpec((d, tn), lambda i, j: (0, j)),
            pl.BlockSpec((tn, d), lambda i, j: (j, 0)),
            _const_spec((1, d)),
            _const_spec((1, d)),
        ],
        out_specs=pl.BlockSpec((tm, d), lambda i, j: (i, 0)),
        out_shape=jax.ShapeDtypeStruct((t, d), F32),
        scratch_shapes=[
            pltpu.VMEM((PEER_HEADS, tm, 2 * PEER_KEYS), BF16),
            pltpu.VMEM((2, parts, nhk, tm // parts), jnp.int32),
            pltpu.VMEM((2, parts, nhk, tm // parts), F32),
            pltpu.VMEM((tm, nhk), jnp.int32),
            pltpu.VMEM((tm, nhk), jnp.int32),
            pltpu.VMEM((tm, nhk), F32),
            pltpu.VMEM((tm * W_PITCH, PEER_KEYS), jnp.uint32),
            pltpu.VMEM((tm, d), F32),
            pltpu.VMEM((tm, d), BF16),
        ],
        compiler_params=_params(("arbitrary", "arbitrary")),
        name="peer_layer",
    )(h, h, x, mods, w_q.astype(BF16), sub_keys.astype(BF16), u_t, v, ln_g, ln_b)


def _lambda_init(layer_idx):
    return 0.8 - 0.6 * math.exp(-0.3 * layer_idx)


def kernel(x, c, ada_w, ada_b, ln_g, ln_b, conv_w_in, conv_w, conv_w_out, attn_w_qkv, attn_lambda,
           attn_subln_g, attn_w_o, peer_w_q, peer_sub_keys, peer_u, peer_v):
    batch, seq, d = x.shape
    depth = ada_w.shape[0]
    alpha = (2.0 * depth) ** 0.25
    head_dim = d // (2 * ATT_HEADS)
    mods = _ada_mods(c, ada_w, ada_b)
    xt = x.reshape(batch * seq, d)
    for i in range(depth):
        j = i // N_MIXERS
        row = (2 * i) * batch
        prow = (2 * i + 1) * batch
        g0, b0 = ln_g[i, 0].reshape(1, d), ln_b[i, 0].reshape(1, d)
        if i % N_MIXERS == 0:
            xt, h = _conv_layer(xt, mods, row, prow, conv_w_in[j], conv_w[j], conv_w_out[j], g0, b0,
                                seq=seq, alpha=alpha)
        else:
            qt, k, vt = _qkv_proj(xt, mods, row, attn_w_qkv[j], seq=seq, q_scale=head_dim ** -0.5)
            o = _diff_attention(qt, k, vt, attn_lambda[j], attn_subln_g[j], batch=batch, seq=seq,
                                lambda_init=_lambda_init(i))
            xt, h = _out_proj(o, xt, mods, row, prow, attn_w_o[j], g0, b0, seq=seq, alpha=alpha)
        g1, b1 = ln_g[i, 1].reshape(1, d), ln_b[i, 1].reshape(1, d)
        xt = _peer_layer(h, xt, mods, prow, peer_w_q[i], peer_sub_keys[i],
                         peer_u[i].astype(BF16).T, peer_v[i].astype(BF16), g1, b1,
                         seq=seq, alpha=alpha)
    return xt.reshape(batch, seq, d)
```

```python
import functools
import math

import jax
import jax.numpy as jnp
from jax import lax
from jax.experimental import pallas as pl
from jax.experimental.pallas import tpu as pltpu

F32 = jnp.float32
BF16 = jnp.bfloat16

N_MIXERS = 2
ATT_HEADS = 8
PEER_HEADS = 8
PEER_KEYS = 128
PEER_TOPK = 16
LN_EPS = 1e-5

LANES = 128
SUBLANES = 8
VMEM_LIMIT_BYTES = 56 * 1024 * 1024
BF16_EXACT_INT = 256
EXP_UNDERFLOW = 106.0
NORM_SLACK = 1.02

N_CAND_GROUPS = 10
N_CAND = N_CAND_GROUPS * SUBLANES
W_PITCH = 136
W_GROUP = 32
ROUTE_UNITS = 16


def _layer_norm(r, g, b):
    mu = jnp.mean(r, axis=-1, keepdims=True)
    d = r - mu
    var = jnp.mean(d * d, axis=-1, keepdims=True)
    return d * lax.rsqrt(var + LN_EPS) * g + b


def _split_mod(mod, d):
    return mod[:, :d], 1.0 + mod[:, d:2 * d], 1.0 + mod[:, 2 * d:]


def _params(sem):
    return pltpu.CompilerParams(dimension_semantics=sem, vmem_limit_bytes=VMEM_LIMIT_BYTES)


def _const_spec(shape):
    nd = len(shape)
    return pl.BlockSpec(shape, lambda *_: (0,) * nd, pipeline_mode=pl.Buffered(1))


def _ada_kernel(c_ref, w_ref, b_ref, o_ref):
    c = c_ref[...]
    sc = c / (1.0 + jnp.exp(-c))
    o_ref[0] = jnp.dot(sc, w_ref[0], precision=lax.Precision.HIGHEST,
                       preferred_element_type=F32) + b_ref[0]


def _ada_mods(c, ada_w, ada_b):
    depth, _, d, d3 = ada_w.shape
    nb = c.shape[0]
    nmat = depth * 2
    tn = 1024
    w = ada_w.reshape(nmat, d, d3)
    b = ada_b.reshape(nmat, 1, d3)
    out = pl.pallas_call(
        _ada_kernel,
        grid=(nmat, d3 // tn),
        in_specs=[
            pl.BlockSpec((nb, d), lambda m, n: (0, 0)),
            pl.BlockSpec((1, d, tn), lambda m, n: (m, 0, n)),
            pl.BlockSpec((1, 1, tn), lambda m, n: (m, 0, n)),
        ],
        out_specs=pl.BlockSpec((1, nb, tn), lambda m, n: (m, 0, n)),
        out_shape=jax.ShapeDtypeStruct((nmat, nb, d3), F32),
        compiler_params=_params(("arbitrary", "arbitrary")),
        name="ada_mods",
    )(c, w, b)
    return out.reshape(nmat * nb, 1, d3)


def _emit_stage_output(r, g_ref, b_ref, modp_ref, o_ref, hp_ref):
    xn = _layer_norm(r, g_ref[...], b_ref[...])
    o_ref[...] = xn
    shift, scale, _ = _split_mod(modp_ref[0], xn.shape[1])
    hp_ref[...] = (xn * scale + shift).astype(BF16)


def _conv_kernel(x_ref, mod_ref, modp_ref, win_ref, cw_ref, wout_ref, g_ref, b_ref, o_ref, hp_ref,
                 zbuf, *, tiles_per_seq, alpha):
    i = pl.program_id(0)
    tm, d = x_ref.shape
    x = x_ref[...]
    shift, scale, gate = _split_mod(mod_ref[0], d)
    h = (x * scale + shift).astype(BF16)
    hw = jnp.dot(h, win_ref[...], preferred_element_type=F32)
    gb = hw[:, :d]
    z = hw[:, d:2 * d] * hw[:, 2 * d:]

    @pl.when(i % tiles_per_seq == 0)
    def _():
        zbuf[0:SUBLANES, :] = jnp.zeros((SUBLANES, d), F32)

    zbuf[SUBLANES:, :] = z
    z1 = zbuf[pl.ds(SUBLANES - 1, tm), :]
    z2 = zbuf[pl.ds(SUBLANES - 2, tm), :]
    cw = cw_ref[...]
    zc = cw[2:3] * z + cw[1:2] * z1 + cw[0:1] * z2
    zbuf[0:SUBLANES, :] = zbuf[tm:tm + SUBLANES, :]
    y = jnp.dot((gb * zc).astype(BF16), wout_ref[...], preferred_element_type=F32)
    _emit_stage_output(alpha * x + gate * y, g_ref, b_ref, modp_ref, o_ref, hp_ref)


def _conv_layer(x, mods, mod_row, peer_row, w_in, conv_w, w_out, ln_g, ln_b, *, seq, alpha, tm=512):
    t, d = x.shape
    tiles_per_seq = seq // tm
    return pl.pallas_call(
        functools.partial(_conv_kernel, tiles_per_seq=tiles_per_seq, alpha=alpha),
        grid=(t // tm,),
        in_specs=[
            pl.BlockSpec((tm, d), lambda i: (i, 0)),
            pl.BlockSpec((1, 1, 3 * d), lambda i: (mod_row + i // tiles_per_seq, 0, 0)),
            pl.BlockSpec((1, 1, 3 * d), lambda i: (peer_row + i // tiles_per_seq, 0, 0)),
            _const_spec((d, 3 * d)),
            _const_spec((conv_w.shape[0], d)),
            _const_spec((d, d)),
            _const_spec((1, d)),
            _const_spec((1, d)),
        ],
        out_specs=[pl.BlockSpec((tm, d), lambda i: (i, 0))] * 2,
        out_shape=[jax.ShapeDtypeStruct((t, d), F32), jax.ShapeDtypeStruct((t, d), BF16)],
        scratch_shapes=[pltpu.VMEM((tm + SUBLANES, d), F32)],
        compiler_params=_params(("arbitrary",)),
        name="conv_mixer",
    )(x, mods, mods, w_in.astype(BF16), conv_w, w_out.astype(BF16), ln_g, ln_b)


def _qkv_kernel(x_ref, mod_ref, w_ref, qt_ref, k_ref, vt_ref, *, q_scale):
    tm, d = x_ref.shape
    shift, scale, _ = _split_mod(mod_ref[0], d)
    h = (x_ref[...] * scale + shift).astype(BF16)
    qkv = jnp.dot(h, w_ref[...], preferred_element_type=F32)
    qt_ref[...] = (qkv[:, :d] * q_scale).T.astype(BF16)
    k_ref[...] = qkv[:, d:2 * d].astype(BF16)
    vt_ref[...] = qkv[:, 2 * d:].T.astype(BF16)


def _qkv_proj(x, mods, mod_row, w_qkv, *, seq, q_scale, tm=512):
    t, d = x.shape
    tiles_per_seq = seq // tm
    return pl.pallas_call(
        functools.partial(_qkv_kernel, q_scale=q_scale),
        grid=(t // tm,),
        in_specs=[
            pl.BlockSpec((tm, d), lambda i: (i, 0)),
            pl.BlockSpec((1, 1, 3 * d), lambda i: (mod_row + i // tiles_per_seq, 0, 0)),
            _const_spec((d, 3 * d)),
        ],
        out_specs=[
            pl.BlockSpec((d, tm), lambda i: (0, i)),
            pl.BlockSpec((tm, d), lambda i: (i, 0)),
            pl.BlockSpec((d, tm), lambda i: (0, i)),
        ],
        out_shape=[
            jax.ShapeDtypeStruct((d, t), BF16),
            jax.ShapeDtypeStruct((t, d), BF16),
            jax.ShapeDtypeStruct((d, t), BF16),
        ],
        compiler_params=_params(("arbitrary",)),
        name="attn_qkv",
    )(x, mods, w_qkv.astype(BF16))


def _attn_kernel(slopes_ref, qt_ref, k_ref, vt_ref, lam_ref, g_ref, o_ref, qa_ref, m_ref, acc_ref,
                 kn_ref, *, tq, tk, heads, lambda_init):
    hg = pl.program_id(1)
    qi = pl.program_id(2)
    dh2 = qt_ref.shape[0] // heads
    dh = dh2 // 2
    ones_rows = 2 * SUBLANES
    blocks_per_tile = tq // tk

    feat = lax.broadcasted_iota(jnp.int32, (dh2, tq), 0)
    arow = lax.broadcasted_iota(jnp.int32, (dh2, 2 * tq), 0)
    acol = lax.broadcasted_iota(jnp.int32, (dh2, 2 * tq), 1)
    r = jnp.where(acol >= tq, acol - tq, acol)
    r_lo = (r % BF16_EXACT_INT).astype(F32)
    r_hi = (r - r % BF16_EXACT_INT).astype(F32)
    for g in range(heads):
        slope = slopes_ref[hg * heads + g]
        qt = qt_ref[g * dh2:(g + 1) * dh2, :]
        zero = jnp.zeros_like(qt)
        qa_ref[g, :dh2, :] = jnp.concatenate(
            [jnp.where(feat < dh, qt, zero), jnp.where(feat >= dh, qt, zero)], axis=1)
        qa_ref[g, dh2:, :] = jnp.where(
            arow == 0, -slope * r_lo,
            jnp.where(arow == 1, -slope * r_hi, jnp.where(arow == 2, slope, 0.0))).astype(BF16)
    kcol = lax.broadcasted_iota(jnp.int32, (tk, dh2), 1)
    krow = lax.broadcasted_iota(jnp.int32, (tk, dh2), 0).astype(F32)
    k_extra = jnp.where(kcol <= 1, 1.0, jnp.where(kcol == 2, krow, 0.0)).astype(BF16)
    ones = jnp.ones((ones_rows, tk), BF16)

    m_ref[...] = jnp.full(m_ref.shape, -jnp.inf, F32)
    acc_ref[...] = jnp.zeros(acc_ref.shape, F32)

    @pl.when(qi == 0)
    def _():
        ones_sq = jnp.ones((dh2, LANES), BF16)
        for g in range(heads):
            def chunk(c, best, g=g):
                kc = k_ref[pl.ds(pl.multiple_of(c * tq, tq), tq), g * dh2:(g + 1) * dh2].astype(F32)
                rows = jnp.dot((kc * kc).astype(BF16), ones_sq, preferred_element_type=F32)
                return jnp.maximum(best, jnp.max(rows, axis=0, keepdims=True))
            kn_ref[g] = lax.fori_loop(0, k_ref.shape[0] // tq, chunk, jnp.zeros((1, LANES), F32))

    def step(j, diagonal):
        start = pl.multiple_of(j * tk, tk)
        delta = qi * tq - j * tk
        head_cols = [slice(g * dh2, (g + 1) * dh2) for g in range(heads)]
        scores, maxima = [], []
        for g, cols in enumerate(head_cols):
            s = jnp.dot(jnp.concatenate([k_ref[pl.ds(start, tk), cols], k_extra], axis=1),
                        qa_ref[g], preferred_element_type=F32)
            if diagonal:
                mrow = lax.broadcasted_iota(jnp.int32, (tk, 2 * tq), 0)
                mcol = lax.broadcasted_iota(jnp.int32, (tk, 2 * tq), 1)
                mq = jnp.where(mcol >= tq, mcol - tq, mcol)
                s = jnp.where(mrow - mq <= delta, s, -jnp.inf)
            scores.append(s)
            maxima.append(jnp.max(s, axis=0, keepdims=True))
        probs, corrs = [], []
        for g, s in enumerate(scores):
            off = delta.astype(F32) * slopes_ref[hg * heads + g]
            m_old = m_ref[g]
            m_new = jnp.maximum(m_old, maxima[g] - off)
            probs.append(jnp.exp(s - (m_new + off)).astype(BF16))
            corrs.append(jnp.exp(m_old - m_new))
            m_ref[g] = m_new
        for g, cols in enumerate(head_cols):
            v_aug = jnp.concatenate([vt_ref[cols, pl.ds(start, tk)], ones], axis=0)
            acc_ref[g] = corrs[g] * acc_ref[g] + jnp.dot(v_aug, probs[g],
                                                         preferred_element_type=F32)

    first_diag = qi * blocks_per_tile
    for d in range(blocks_per_tile):
        step(first_diag + d, True)

    needed = jnp.zeros((1, 2 * tq), F32)
    for g in range(heads):
        qf = qa_ref[g, :dh2, :].astype(F32)
        q_norm = jnp.sqrt(jnp.sum(qf * qf, axis=0, keepdims=True))
        k_norm = jnp.sqrt(kn_ref[g][:, :1])
        reach = (q_norm * k_norm * NORM_SLACK - m_ref[g] + EXP_UNDERFLOW) / slopes_ref[hg * heads + g]
        needed = jnp.maximum(needed, jnp.floor((reach - 1.0) / tk) + 1.0)
    needed = jnp.clip(needed, 0.0, first_diag.astype(F32))
    n_blocks = jnp.max(needed.astype(jnp.int32))

    def body(i, carry):
        step(first_diag - 1 - i, False)
        return carry

    lax.fori_loop(0, n_blocks, body, 0)

    lam = lam_ref[...]
    lam_full = (jnp.exp(jnp.sum(lam[0:1] * lam[1:2], axis=1, keepdims=True))
                - jnp.exp(jnp.sum(lam[2:3] * lam[3:4], axis=1, keepdims=True)) + lambda_init)
    for g in range(heads):
        acc = acc_ref[g]
        on = acc[:dh2] / acc[dh2:dh2 + 1]
        o = on[:, :tq] - lam_full * on[:, tq:]
        o = o * lax.rsqrt(jnp.mean(o * o, axis=0, keepdims=True) + LN_EPS) * g_ref[...]
        o_ref[:, g * dh2:(g + 1) * dh2] = (o * (1.0 - lambda_init)).T.astype(o_ref.dtype)


def _diff_attention(qt, k, vt, lam, subln_g, *, batch, seq, lambda_init, tq=512, tk=256, heads=4):
    d, t = qt.shape
    dh2 = d // ATT_HEADS
    gw = heads * dh2
    nq = seq // tq
    assert tq % tk == 0 and tk <= BF16_EXACT_INT and tq < BF16_EXACT_INT * BF16_EXACT_INT
    slopes = 2.0 ** (-(8.0 / ATT_HEADS) * jnp.arange(1, ATT_HEADS + 1, dtype=F32))
    return pl.pallas_call(
        functools.partial(_attn_kernel, tq=tq, tk=tk, heads=heads, lambda_init=lambda_init),
        grid=(batch, ATT_HEADS // heads, nq),
        in_specs=[
            pl.BlockSpec(memory_space=pltpu.SMEM),
            pl.BlockSpec((gw, tq), lambda b, h, i: (h, b * nq + i)),
            pl.BlockSpec((seq, gw), lambda b, h, i: (b, h)),
            pl.BlockSpec((gw, seq), lambda b, h, i: (h, b)),
            pl.BlockSpec(lam.shape, lambda b, h, i: (0, 0)),
            pl.BlockSpec((dh2, 1), lambda b, h, i: (0, 0)),
        ],
        out_specs=pl.BlockSpec((tq, gw), lambda b, h, i: (b * nq + i, h)),
        out_shape=jax.ShapeDtypeStruct((t, d), BF16),
        scratch_shapes=[
            pltpu.VMEM((heads, 2 * dh2, 2 * tq), BF16),
            pltpu.VMEM((heads, 1, 2 * tq), F32),
            pltpu.VMEM((heads, dh2 + 2 * SUBLANES, 2 * tq), F32),
            pltpu.VMEM((heads, 1, LANES), F32),
        ],
        compiler_params=_params(("arbitrary", "arbitrary", "arbitrary")),
        name="diff_attention",
    )(slopes, qt, k, vt, lam, subln_g.reshape(dh2, 1))


def _oproj_kernel(o_ref, x_ref, mod_ref, modp_ref, w_ref, g_ref, b_ref, out_ref, hp_ref, *, alpha):
    tm, d = x_ref.shape
    _, _, gate = _split_mod(mod_ref[0], d)
    y = jnp.dot(o_ref[...], w_ref[...], preferred_element_type=F32)
    _emit_stage_output(alpha * x_ref[...] + gate * y, g_ref, b_ref, modp_ref, out_ref, hp_ref)


def _out_proj(o, x, mods, mod_row, peer_row, w_o, ln_g, ln_b, *, seq, alpha, tm=512):
    t, d = x.shape
    tiles_per_seq = seq // tm
    return pl.pallas_call(
        functools.partial(_oproj_kernel, alpha=alpha),
        grid=(t // tm,),
        in_specs=[
            pl.BlockSpec((tm, d), lambda i: (i, 0)),
            pl.BlockSpec((tm, d), lambda i: (i, 0)),
            pl.BlockSpec((1, 1, 3 * d), lambda i: (mod_row + i // tiles_per_seq, 0, 0)),
            pl.BlockSpec((1, 1, 3 * d), lambda i: (peer_row + i // tiles_per_seq, 0, 0)),
            _const_spec((d, d)),
            _const_spec((1, d)),
            _const_spec((1, d)),
        ],
        out_specs=[pl.BlockSpec((tm, d), lambda i: (i, 0))] * 2,
        out_shape=[jax.ShapeDtypeStruct((t, d), F32), jax.ShapeDtypeStruct((t, d), BF16)],
        compiler_params=_params(("arbitrary",)),
        name="attn_out_proj",
    )(o, x, mods, mods, w_o.astype(BF16), ln_g, ln_b)


def _top16_rows(s):
    n, tm = s.shape
    rows = lax.broadcasted_iota(jnp.int32, (n, tm), 0).astype(F32)
    vals, idxs = [], []
    tie = None
    for _ in range(PEER_TOPK):
        m = jnp.max(s, axis=0, keepdims=True)
        if tie is not None:
            m = m + tie
        idx = jnp.min(jnp.where(s == m, rows, float(n)), axis=0, keepdims=True)
        s = jnp.where(rows == idx, -jnp.inf, s)
        vals.append(m)
        idxs.append(idx)
        tie = yield
    return vals, idxs


def _candidate_layout(tm):
    slot = lax.broadcasted_iota(jnp.int32, (N_CAND, tm), 0)
    grp = slot // SUBLANES
    r = slot % SUBLANES
    a = jnp.where(grp <= 1, 0, jnp.where(grp == N_CAND_GROUPS - 1, SUBLANES + r, grp - 1))
    b = jnp.where(grp == 0, r, jnp.where(grp == 1, SUBLANES + r,
                                         jnp.where(grp == N_CAND_GROUPS - 1, 0, r)))
    valid = (a + 1) * (b + 1) <= PEER_TOPK
    flat = (a * PEER_TOPK + b).astype(F32)
    return valid, flat


def _candidates(rows0, rows1):
    lo1 = jnp.concatenate(rows1[:SUBLANES], axis=0)
    hi1 = jnp.concatenate(rows1[SUBLANES:], axis=0)
    hi0 = jnp.concatenate(rows0[SUBLANES:], axis=0)
    groups = [rows0[0] + lo1, rows0[0] + hi1]
    groups += [rows0[a] + lo1 for a in range(1, SUBLANES)]
    groups.append(hi0 + rows1[0])
    return jnp.concatenate(groups, axis=0)


def _route_head(q_head, keys_ref, valid, flat):
    scores = [lax.dot_general(keys_ref[p], q_head[:, p * PEER_KEYS:(p + 1) * PEER_KEYS],
                              (((1,), (1,)), ((), ())), preferred_element_type=F32)
              for p in range(2)]
    yield
    vals, idxs = [], []
    for st in scores:
        v, ix = yield from _top16_rows(st)
        vals.append(v)
        idxs.append(ix)
    cand = jnp.where(valid, _candidates(vals[0], vals[1]), -jnp.inf)
    code = _candidates([ix * float(PEER_KEYS) for ix in idxs[0]], idxs[1])
    top_s, top_code = [], []
    tie = None
    for _ in range(PEER_TOPK):
        m = jnp.max(cand, axis=0, keepdims=True)
        if tie is not None:
            m = m + tie
        fmin = jnp.min(jnp.where(cand == m, flat, float(PEER_TOPK * PEER_TOPK)),
                       axis=0, keepdims=True)
        hit = flat == fmin
        top_code.append(jnp.sum(jnp.where(hit, code, 0.0), axis=0, keepdims=True))
        cand = jnp.where(hit, -jnp.inf, cand)
        top_s.append(m)
        tie = yield
    ts = jnp.concatenate(top_s, axis=0)
    e = jnp.exp(ts - top_s[0])
    return (jnp.concatenate(top_code, axis=0).astype(jnp.int32),
            e / jnp.sum(e, axis=0, keepdims=True))


ROUTE_YIELDS = 3 * PEER_TOPK


def _zero_row(x, width):
    bits = lax.bitcast_convert_type(x[:1, :LANES], jnp.uint32)
    zero = lax.bitcast_convert_type((bits >> 16) >> 16, F32)
    return jnp.concatenate([zero] * (width // LANES), axis=1)


def _peer_kernel(h_ref, hn_ref, x_ref, mod_ref, wq_ref, keys_ref, ut_ref, v_ref, g_ref, b_ref,
                 o_ref, q_ref, code_ref, gate_ref, ci_ref, cj_ref, cg_ref, w_ref, acc_ref, hs_ref,
                 *, alpha):
    i = pl.program_id(0)
    j = pl.program_id(1)
    nj = pl.num_programs(1)
    tm, d = x_ref.shape
    tn = ut_ref.shape[1]
    nhk = PEER_HEADS * PEER_TOPK
    parts = ROUTE_UNITS // PEER_HEADS
    tp = tm // parts
    slot = i % 2
    valid, flat = _candidate_layout(tp)

    def prepare_queries(h):
        q = jnp.dot(h, wq_ref[...], preferred_element_type=F32).astype(BF16)
        for hd in range(PEER_HEADS):
            q_ref[hd] = q[:, hd * 2 * PEER_KEYS:(hd + 1) * 2 * PEER_KEYS]

    def route_unit(u, dst):
        hd = u // parts
        part = u % parts
        rows = pl.ds(pl.multiple_of(part * tp, tp), tp)
        code, gate = yield from _route_head(q_ref[hd, rows, :], keys_ref, valid, flat)
        krows = pl.ds(pl.multiple_of(hd * PEER_TOPK, PEER_TOPK), PEER_TOPK)
        code_ref[dst, part, krows, :] = code
        gate_ref[dst, part, krows, :] = gate

    @pl.when((i == 0) & (j == 0))
    def _():
        prepare_queries(h_ref[...])

        def unit(u, carry):
            for _ in route_unit(u, 0):
                pass
            return carry

        lax.fori_loop(0, ROUTE_UNITS, unit, 0)

    @pl.when(j == 0)
    def _():
        for part in range(parts):
            code = code_ref[slot, part].T
            ci_ref[part * tp:(part + 1) * tp, :] = code // PEER_KEYS
            cj_ref[part * tp:(part + 1) * tp, :] = code % PEER_KEYS
            cg_ref[part * tp:(part + 1) * tp, :] = gate_ref[slot, part].T
        key = lax.broadcasted_iota(jnp.int32, (PEER_KEYS, nhk), 0).astype(BF16)
        zero = jnp.zeros((PEER_KEYS, nhk), BF16)
        one = jnp.ones((PEER_KEYS, nhk), BF16)

        def gate_bits(t):
            ri = ci_ref[pl.ds(t, 1), :].astype(BF16)
            rj = cj_ref[pl.ds(t, 1), :].astype(BF16)
            rg = cg_ref[pl.ds(t, 1), :].astype(BF16)
            pt = jnp.where(key == ri, jnp.broadcast_to(rg, zero.shape), zero)
            qt = jnp.where(key == rj, one, zero)
            wt = lax.dot_general(pt, qt, (((1,), (1,)), ((), ())), preferred_element_type=F32)
            return lax.bitcast_convert_type(wt, jnp.uint32)

        def token_group(grp, carry):
            base = pl.multiple_of(grp * W_GROUP, W_GROUP)
            for u in range(0, W_GROUP, 2):
                packed = (gate_bits(base + u) >> 16) | (gate_bits(base + u + 1) & jnp.uint32(0xFFFF0000))
                pair = (base + u) // 2
                w_ref[pl.ds(pl.multiple_of(pair * W_PITCH, SUBLANES), PEER_KEYS), :] = packed
            return carry

        lax.fori_loop(0, tm // W_GROUP, token_group, 0)
        acc_ref[...] = jnp.zeros(acc_ref.shape, F32)
        hs_ref[...] = h_ref[...]
        prepare_queries(hn_ref[...])

    route = route_unit(j, 1 - slot)
    next(route)
    chunk = 2 * PEER_KEYS
    n_chunks = tn // chunk
    per_chunk = -(-ROUTE_YIELDS // (2 * n_chunks))

    def advance(after):
        try:
            route.send(None if after is None else _zero_row(after, tp))
            for _ in range(per_chunk - 1):
                next(route)
        except StopIteration:
            pass

    advance(None)
    h = hs_ref[...]
    pieces = []
    for r in range(n_chunks):
        c0 = r * chunk
        a = jnp.dot(h, ut_ref[:, c0:c0 + chunk], preferred_element_type=F32)
        advance(a)
        act = (a * (1.0 + lax.erf(a * (1.0 / math.sqrt(2.0))))).astype(BF16)
        for e in range(chunk // PEER_KEYS):
            first_key = (j * n_chunks + r) * (chunk // PEER_KEYS) + e
            words = w_ref[pl.ds(first_key, tm // 2, stride=W_PITCH), :]
            gates = pltpu.bitcast(words, BF16)
            pieces.append(gates * act[:, e * PEER_KEYS:(e + 1) * PEER_KEYS])
    z = jnp.concatenate(pieces, axis=1)
    for c in range(d // chunk):
        cols = slice(c * chunk, (c + 1) * chunk)
        zv = jnp.dot(z, v_ref[:, cols], preferred_element_type=F32)
        acc_ref[:, cols] += zv
        advance(zv)
    for _ in route:
        pass

    @pl.when(j == nj - 1)
    def _():
        _, _, gate = _split_mod(mod_ref[0], d)
        o_ref[...] = _layer_norm(alpha * x_ref[...] + gate * acc_ref[...], g_ref[...], b_ref[...])


def _peer_layer(h, x, mods, mod_row, w_q, sub_keys, u_t, v, ln_g, ln_b, *, seq, alpha, tm=512):
    t, d = x.shape
    n = v.shape[0]
    tn = n // ROUTE_UNITS
    nhk = PEER_HEADS * PEER_TOPK
    parts = ROUTE_UNITS // PEER_HEADS
    tiles_per_seq = seq // tm
    last = t // tm - 1
    return pl.pallas_call(
        functools.partial(_peer_kernel, alpha=alpha),
        grid=(t // tm, n // tn),
        in_specs=[
            pl.BlockSpec((tm, d), lambda i, j: (i, 0)),
            pl.BlockSpec((tm, d), lambda i, j: (jnp.minimum(i + 1, last), 0)),
            pl.BlockSpec((tm, d), lambda i, j: (i, 0)),
            pl.BlockSpec((1, 1, 3 * d), lambda i, j: (mod_row + i // tiles_per_seq, 0, 0)),
            _const_spec(w_q.shape),
            _const_spec(sub_keys.shape),
            pl.BlockSpec((d, tn), lambda i, j: (0, j)),
            pl.BlockSpec((tn, d), lambda i, j: (j, 0)),
            _const_spec((1, d)),
            _const_spec((1, d)),
        ],
        out_specs=pl.BlockSpec((tm, d), lambda i, j: (i, 0)),
        out_shape=jax.ShapeDtypeStruct((t, d), F32),
        scratch_shapes=[
            pltpu.VMEM((PEER_HEADS, tm, 2 * PEER_KEYS), BF16),
            pltpu.VMEM((2, parts, nhk, tm // parts), jnp.int32),
            pltpu.VMEM((2, parts, nhk, tm // parts), F32),
            pltpu.VMEM((tm, nhk), jnp.int32),
            pltpu.VMEM((tm, nhk), jnp.int32),
            pltpu.VMEM((tm, nhk), F32),
            pltpu.VMEM((tm // 2 * W_PITCH, PEER_KEYS), jnp.uint32),
            pltpu.VMEM((tm, d), F32),
            pltpu.VMEM((tm, d), BF16),
        ],
        compiler_params=_params(("arbitrary", "arbitrary")),
        name="peer_layer",
    )(h, h, x, mods, w_q.astype(BF16), sub_keys.astype(BF16), u_t, v, ln_g, ln_b)


def _lambda_init(layer_idx):
    return 0.8 - 0.6 * math.exp(-0.3 * layer_idx)


def kernel(x, c, ada_w, ada_b, ln_g, ln_b, conv_w_in, conv_w, conv_w_out, attn_w_qkv, attn_lambda,
           attn_subln_g, attn_w_o, peer_w_q, peer_sub_keys, peer_u, peer_v):
    batch, seq, d = x.shape
    depth = ada_w.shape[0]
    alpha = (2.0 * depth) ** 0.25
    head_dim = d // (2 * ATT_HEADS)
    mods = _ada_mods(c, ada_w, ada_b)
    xt = x.reshape(batch * seq, d)
    for i in range(depth):
        j = i // N_MIXERS
        row = (2 * i) * batch
        prow = (2 * i + 1) * batch
        g0, b0 = ln_g[i, 0].reshape(1, d), ln_b[i, 0].reshape(1, d)
        if i % N_MIXERS == 0:
            xt, h = _conv_layer(xt, mods, row, prow, conv_w_in[j], conv_w[j], conv_w_out[j], g0, b0,
                                seq=seq, alpha=alpha)
        else:
            qt, k, vt = _qkv_proj(xt, mods, row, attn_w_qkv[j], seq=seq, q_scale=head_dim ** -0.5)
            o = _diff_attention(qt, k, vt, attn_lambda[j], attn_subln_g[j], batch=batch, seq=seq,
                                lambda_init=_lambda_init(i))
            xt, h = _out_proj(o, xt, mods, row, prow, attn_w_o[j], g0, b0, seq=seq, alpha=alpha)
        g1, b1 = ln_g[i, 1].reshape(1, d), ln_b[i, 1].reshape(1, d)
        xt = _peer_layer(h, xt, mods, prow, peer_w_q[i], peer_sub_keys[i],
                         peer_u[i].T.astype(BF16), (0.5 * peer_v[i]).astype(BF16), g1, b1,
                         seq=seq, alpha=alpha)
    return xt.reshape(batch, seq, d)
```

```python
import functools
import math

import jax
import jax.numpy as jnp
from jax import lax
from jax.experimental import pallas as pl
from jax.experimental.pallas import tpu as pltpu

F32 = jnp.float32
BF16 = jnp.bfloat16

N_MIXERS = 2
ATT_HEADS = 8
PEER_HEADS = 8
PEER_KEYS = 128
PEER_TOPK = 16
LN_EPS = 1e-5

LANES = 128
SUBLANES = 8
VMEM_LIMIT_BYTES = 56 * 1024 * 1024
BF16_EXACT_INT = 256
EXP_UNDERFLOW = 106.0
NORM_SLACK = 1.02

N_CAND_GROUPS = 10
N_CAND = N_CAND_GROUPS * SUBLANES
W_PITCH = 136
ROUTE_UNITS = 16


def _layer_norm(r, g, b):
    mu = jnp.mean(r, axis=-1, keepdims=True)
    d = r - mu
    var = jnp.mean(d * d, axis=-1, keepdims=True)
    return d * lax.rsqrt(var + LN_EPS) * g + b


def _split_mod(mod, d):
    return mod[:, :d], 1.0 + mod[:, d:2 * d], 1.0 + mod[:, 2 * d:]


def _params(sem):
    return pltpu.CompilerParams(dimension_semantics=sem, vmem_limit_bytes=VMEM_LIMIT_BYTES)


def _const_spec(shape):
    nd = len(shape)
    return pl.BlockSpec(shape, lambda *_: (0,) * nd, pipeline_mode=pl.Buffered(1))


def _ada_kernel(c_ref, w_ref, b_ref, o_ref):
    c = c_ref[...]
    sc = c / (1.0 + jnp.exp(-c))
    o_ref[0] = jnp.dot(sc, w_ref[0], precision=lax.Precision.HIGHEST,
                       preferred_element_type=F32) + b_ref[0]


def _ada_mods(c, ada_w, ada_b):
    depth, _, d, d3 = ada_w.shape
    nb = c.shape[0]
    nmat = depth * 2
    tn = 1024
    w = ada_w.reshape(nmat, d, d3)
    b = ada_b.reshape(nmat, 1, d3)
    out = pl.pallas_call(
        _ada_kernel,
        grid=(nmat, d3 // tn),
        in_specs=[
            pl.BlockSpec((nb, d), lambda m, n: (0, 0)),
            pl.BlockSpec((1, d, tn), lambda m, n: (m, 0, n)),
            pl.BlockSpec((1, 1, tn), lambda m, n: (m, 0, n)),
        ],
        out_specs=pl.BlockSpec((1, nb, tn), lambda m, n: (m, 0, n)),
        out_shape=jax.ShapeDtypeStruct((nmat, nb, d3), F32),
        compiler_params=_params(("arbitrary", "arbitrary")),
        name="ada_mods",
    )(c, w, b)
    return out.reshape(nmat * nb, 1, d3)


def _emit_stage_output(r, g_ref, b_ref, modp_ref, o_ref, hp_ref):
    xn = _layer_norm(r, g_ref[...], b_ref[...])
    o_ref[...] = xn
    shift, scale, _ = _split_mod(modp_ref[0], xn.shape[1])
    hp_ref[...] = (xn * scale + shift).astype(BF16)


def _conv_kernel(x_ref, mod_ref, modp_ref, win_ref, cw_ref, wout_ref, g_ref, b_ref, o_ref, hp_ref,
                 zbuf, *, tiles_per_seq, alpha):
    i = pl.program_id(0)
    tm, d = x_ref.shape
    x = x_ref[...]
    shift, scale, gate = _split_mod(mod_ref[0], d)
    h = (x * scale + shift).astype(BF16)
    hw = jnp.dot(h, win_ref[...], preferred_element_type=F32)
    gb = hw[:, :d]
    z = hw[:, d:2 * d] * hw[:, 2 * d:]

    @pl.when(i % tiles_per_seq == 0)
    def _():
        zbuf[0:SUBLANES, :] = jnp.zeros((SUBLANES, d), F32)

    zbuf[SUBLANES:, :] = z
    z1 = zbuf[pl.ds(SUBLANES - 1, tm), :]
    z2 = zbuf[pl.ds(SUBLANES - 2, tm), :]
    cw = cw_ref[...]
    zc = cw[2:3] * z + cw[1:2] * z1 + cw[0:1] * z2
    zbuf[0:SUBLANES, :] = zbuf[tm:tm + SUBLANES, :]
    y = jnp.dot((gb * zc).astype(BF16), wout_ref[...], preferred_element_type=F32)
    _emit_stage_output(alpha * x + gate * y, g_ref, b_ref, modp_ref, o_ref, hp_ref)


def _conv_layer(x, mods, mod_row, peer_row, w_in, conv_w, w_out, ln_g, ln_b, *, seq, alpha, tm=512):
    t, d = x.shape
    tiles_per_seq = seq // tm
    return pl.pallas_call(
        functools.partial(_conv_kernel, tiles_per_seq=tiles_per_seq, alpha=alpha),
        grid=(t // tm,),
        in_specs=[
            pl.BlockSpec((tm, d), lambda i: (i, 0)),
            pl.BlockSpec((1, 1, 3 * d), lambda i: (mod_row + i // tiles_per_seq, 0, 0)),
            pl.BlockSpec((1, 1, 3 * d), lambda i: (peer_row + i // tiles_per_seq, 0, 0)),
            _const_spec((d, 3 * d)),
            _const_spec((conv_w.shape[0], d)),
            _const_spec((d, d)),
            _const_spec((1, d)),
            _const_spec((1, d)),
        ],
        out_specs=[pl.BlockSpec((tm, d), lambda i: (i, 0))] * 2,
        out_shape=[jax.ShapeDtypeStruct((t, d), F32), jax.ShapeDtypeStruct((t, d), BF16)],
        scratch_shapes=[pltpu.VMEM((tm + SUBLANES, d), F32)],
        compiler_params=_params(("arbitrary",)),
        name="conv_mixer",
    )(x, mods, mods, w_in.astype(BF16), conv_w, w_out.astype(BF16), ln_g, ln_b)


def _qkv_kernel(x_ref, mod_ref, w_ref, qt_ref, k_ref, vt_ref, *, q_scale):
    tm, d = x_ref.shape
    shift, scale, _ = _split_mod(mod_ref[0], d)
    h = (x_ref[...] * scale + shift).astype(BF16)
    qkv = jnp.dot(h, w_ref[...], preferred_element_type=F32)
    qt_ref[...] = (qkv[:, :d] * q_scale).T.astype(BF16)
    k_ref[...] = qkv[:, d:2 * d].astype(BF16)
    vt_ref[...] = qkv[:, 2 * d:].T.astype(BF16)


def _qkv_proj(x, mods, mod_row, w_qkv, *, seq, q_scale, tm=512):
    t, d = x.shape
    tiles_per_seq = seq // tm
    return pl.pallas_call(
        functools.partial(_qkv_kernel, q_scale=q_scale),
        grid=(t // tm,),
        in_specs=[
            pl.BlockSpec((tm, d), lambda i: (i, 0)),
            pl.BlockSpec((1, 1, 3 * d), lambda i: (mod_row + i // tiles_per_seq, 0, 0)),
            _const_spec((d, 3 * d)),
        ],
        out_specs=[
            pl.BlockSpec((d, tm), lambda i: (0, i)),
            pl.BlockSpec((tm, d), lambda i: (i, 0)),
            pl.BlockSpec((d, tm), lambda i: (0, i)),
        ],
        out_shape=[
            jax.ShapeDtypeStruct((d, t), BF16),
            jax.ShapeDtypeStruct((t, d), BF16),
            jax.ShapeDtypeStruct((d, t), BF16),
        ],
        compiler_params=_params(("arbitrary",)),
        name="attn_qkv",
    )(x, mods, w_qkv.astype(BF16))


def _attn_kernel(slopes_ref, qt_ref, k_ref, vt_ref, lam_ref, g_ref, o_ref, qa_ref, m_ref, acc_ref,
                 kn_ref, *, tq, tk, heads, lambda_init):
    hg = pl.program_id(1)
    qi = pl.program_id(2)
    dh2 = qt_ref.shape[0] // heads
    dh = dh2 // 2
    ones_rows = 2 * SUBLANES
    blocks_per_tile = tq // tk

    feat = lax.broadcasted_iota(jnp.int32, (dh2, tq), 0)
    arow = lax.broadcasted_iota(jnp.int32, (dh2, 2 * tq), 0)
    acol = lax.broadcasted_iota(jnp.int32, (dh2, 2 * tq), 1)
    r = jnp.where(acol >= tq, acol - tq, acol)
    r_lo = (r % BF16_EXACT_INT).astype(F32)
    r_hi = (r - r % BF16_EXACT_INT).astype(F32)
    for g in range(heads):
        slope = slopes_ref[hg * heads + g]
        qt = qt_ref[g * dh2:(g + 1) * dh2, :]
        zero = jnp.zeros_like(qt)
        qa_ref[g, :dh2, :] = jnp.concatenate(
            [jnp.where(feat < dh, qt, zero), jnp.where(feat >= dh, qt, zero)], axis=1)
        qa_ref[g, dh2:, :] = jnp.where(
            arow == 0, -slope * r_lo,
            jnp.where(arow == 1, -slope * r_hi, jnp.where(arow == 2, slope, 0.0))).astype(BF16)
    kcol = lax.broadcasted_iota(jnp.int32, (tk, dh2), 1)
    krow = lax.broadcasted_iota(jnp.int32, (tk, dh2), 0).astype(F32)
    k_extra = jnp.where(kcol <= 1, 1.0, jnp.where(kcol == 2, krow, 0.0)).astype(BF16)
    ones = jnp.ones((ones_rows, tk), BF16)

    m_ref[...] = jnp.full(m_ref.shape, -jnp.inf, F32)
    acc_ref[...] = jnp.zeros(acc_ref.shape, F32)

    @pl.when(qi == 0)
    def _():
        ones_sq = jnp.ones((dh2, LANES), BF16)
        for g in range(heads):
            def chunk(c, best, g=g):
                kc = k_ref[pl.ds(pl.multiple_of(c * tq, tq), tq), g * dh2:(g + 1) * dh2].astype(F32)
                rows = jnp.dot((kc * kc).astype(BF16), ones_sq, preferred_element_type=F32)
                return jnp.maximum(best, jnp.max(rows, axis=0, keepdims=True))
            kn_ref[g] = lax.fori_loop(0, k_ref.shape[0] // tq, chunk, jnp.zeros((1, LANES), F32))

    def step(j, diagonal):
        start = pl.multiple_of(j * tk, tk)
        delta = qi * tq - j * tk
        head_cols = [slice(g * dh2, (g + 1) * dh2) for g in range(heads)]
        scores, maxima = [], []
        for g, cols in enumerate(head_cols):
            s = jnp.dot(jnp.concatenate([k_ref[pl.ds(start, tk), cols], k_extra], axis=1),
                        qa_ref[g], preferred_element_type=F32)
            if diagonal:
                mrow = lax.broadcasted_iota(jnp.int32, (tk, 2 * tq), 0)
                mcol = lax.broadcasted_iota(jnp.int32, (tk, 2 * tq), 1)
                mq = jnp.where(mcol >= tq, mcol - tq, mcol)
                s = jnp.where(mrow - mq <= delta, s, -jnp.inf)
            scores.append(s)
            maxima.append(jnp.max(s, axis=0, keepdims=True))
        probs, corrs = [], []
        for g, s in enumerate(scores):
            off = delta.astype(F32) * slopes_ref[hg * heads + g]
            m_old = m_ref[g]
            m_new = jnp.maximum(m_old, maxima[g] - off)
            probs.append(jnp.exp(s - (m_new + off)).astype(BF16))
            corrs.append(jnp.exp(m_old - m_new))
            m_ref[g] = m_new
        for g, cols in enumerate(head_cols):
            v_aug = jnp.concatenate([vt_ref[cols, pl.ds(start, tk)], ones], axis=0)
            acc_ref[g] = corrs[g] * acc_ref[g] + jnp.dot(v_aug, probs[g],
                                                         preferred_element_type=F32)

    first_diag = qi * blocks_per_tile
    for d in range(blocks_per_tile):
        step(first_diag + d, True)

    needed = jnp.zeros((1, 2 * tq), F32)
    for g in range(heads):
        qf = qa_ref[g, :dh2, :].astype(F32)
        q_norm = jnp.sqrt(jnp.sum(qf * qf, axis=0, keepdims=True))
        k_norm = jnp.sqrt(kn_ref[g][:, :1])
        reach = (q_norm * k_norm * NORM_SLACK - m_ref[g] + EXP_UNDERFLOW) / slopes_ref[hg * heads + g]
        needed = jnp.maximum(needed, jnp.floor((reach - 1.0) / tk) + 1.0)
    needed = jnp.clip(needed, 0.0, first_diag.astype(F32))
    n_blocks = jnp.max(needed.astype(jnp.int32))

    def body(i, carry):
        step(first_diag - 1 - i, False)
        return carry

    lax.fori_loop(0, n_blocks, body, 0)

    lam = lam_ref[...]
    lam_full = (jnp.exp(jnp.sum(lam[0:1] * lam[1:2], axis=1, keepdims=True))
                - jnp.exp(jnp.sum(lam[2:3] * lam[3:4], axis=1, keepdims=True)) + lambda_init)
    for g in range(heads):
        acc = acc_ref[g]
        on = acc[:dh2] / acc[dh2:dh2 + 1]
        o = on[:, :tq] - lam_full * on[:, tq:]
        o = o * lax.rsqrt(jnp.mean(o * o, axis=0, keepdims=True) + LN_EPS) * g_ref[...]
        o_ref[:, g * dh2:(g + 1) * dh2] = (o * (1.0 - lambda_init)).T.astype(o_ref.dtype)


def _diff_attention(qt, k, vt, lam, subln_g, *, batch, seq, lambda_init, tq=512, tk=256, heads=4):
    d, t = qt.shape
    dh2 = d // ATT_HEADS
    gw = heads * dh2
    nq = seq // tq
    assert tq % tk == 0 and tk <= BF16_EXACT_INT and tq < BF16_EXACT_INT * BF16_EXACT_INT
    slopes = 2.0 ** (-(8.0 / ATT_HEADS) * jnp.arange(1, ATT_HEADS + 1, dtype=F32))
    return pl.pallas_call(
        functools.partial(_attn_kernel, tq=tq, tk=tk, heads=heads, lambda_init=lambda_init),
        grid=(batch, ATT_HEADS // heads, nq),
        in_specs=[
            pl.BlockSpec(memory_space=pltpu.SMEM),
            pl.BlockSpec((gw, tq), lambda b, h, i: (h, b * nq + i)),
            pl.BlockSpec((seq, gw), lambda b, h, i: (b, h)),
            pl.BlockSpec((gw, seq), lambda b, h, i: (h, b)),
            pl.BlockSpec(lam.shape, lambda b, h, i: (0, 0)),
            pl.BlockSpec((dh2, 1), lambda b, h, i: (0, 0)),
        ],
        out_specs=pl.BlockSpec((tq, gw), lambda b, h, i: (b * nq + i, h)),
        out_shape=jax.ShapeDtypeStruct((t, d), BF16),
        scratch_shapes=[
            pltpu.VMEM((heads, 2 * dh2, 2 * tq), BF16),
            pltpu.VMEM((heads, 1, 2 * tq), F32),
            pltpu.VMEM((heads, dh2 + 2 * SUBLANES, 2 * tq), F32),
            pltpu.VMEM((heads, 1, LANES), F32),
        ],
        compiler_params=_params(("arbitrary", "arbitrary", "arbitrary")),
        name="diff_attention",
    )(slopes, qt, k, vt, lam, subln_g.reshape(dh2, 1))


def _oproj_kernel(o_ref, x_ref, mod_ref, modp_ref, w_ref, g_ref, b_ref, out_ref, hp_ref, *, alpha):
    tm, d = x_ref.shape
    _, _, gate = _split_mod(mod_ref[0], d)
    y = jnp.dot(o_ref[...], w_ref[...], preferred_element_type=F32)
    _emit_stage_output(alpha * x_ref[...] + gate * y, g_ref, b_ref, modp_ref, out_ref, hp_ref)


def _out_proj(o, x, mods, mod_row, peer_row, w_o, ln_g, ln_b, *, seq, alpha, tm=512):
    t, d = x.shape
    tiles_per_seq = seq // tm
    return pl.pallas_call(
        functools.partial(_oproj_kernel, alpha=alpha),
        grid=(t // tm,),
        in_specs=[
            pl.BlockSpec((tm, d), lambda i: (i, 0)),
            pl.BlockSpec((tm, d), lambda i: (i, 0)),
            pl.BlockSpec((1, 1, 3 * d), lambda i: (mod_row + i // tiles_per_seq, 0, 0)),
            pl.BlockSpec((1, 1, 3 * d), lambda i: (peer_row + i // tiles_per_seq, 0, 0)),
            _const_spec((d, d)),
            _const_spec((1, d)),
            _const_spec((1, d)),
        ],
        out_specs=[pl.BlockSpec((tm, d), lambda i: (i, 0))] * 2,
        out_shape=[jax.ShapeDtypeStruct((t, d), F32), jax.ShapeDtypeStruct((t, d), BF16)],
        compiler_params=_params(("arbitrary",)),
        name="attn_out_proj",
    )(o, x, mods, mods, w_o.astype(BF16), ln_g, ln_b)


def _top16_rows(s):
    n, tm = s.shape
    rows = lax.broadcasted_iota(jnp.int32, (n, tm), 0).astype(F32)
    vals, idxs = [], []
    tie = None
    for _ in range(PEER_TOPK):
        m = jnp.max(s, axis=0, keepdims=True)
        if tie is not None:
            m = m + tie
        idx = jnp.min(jnp.where(s == m, rows, float(n)), axis=0, keepdims=True)
        s = jnp.where(rows == idx, -jnp.inf, s)
        vals.append(m)
        idxs.append(idx)
        tie = yield
    return vals, idxs


def _candidate_layout(tm):
    slot = lax.broadcasted_iota(jnp.int32, (N_CAND, tm), 0)
    grp = slot // SUBLANES
    r = slot % SUBLANES
    a = jnp.where(grp <= 1, 0, jnp.where(grp == N_CAND_GROUPS - 1, SUBLANES + r, grp - 1))
    b = jnp.where(grp == 0, r, jnp.where(grp == 1, SUBLANES + r,
                                         jnp.where(grp == N_CAND_GROUPS - 1, 0, r)))
    valid = (a + 1) * (b + 1) <= PEER_TOPK
    flat = (a * PEER_TOPK + b).astype(F32)
    return valid, flat


def _candidates(rows0, rows1):
    lo1 = jnp.concatenate(rows1[:SUBLANES], axis=0)
    hi1 = jnp.concatenate(rows1[SUBLANES:], axis=0)
    hi0 = jnp.concatenate(rows0[SUBLANES:], axis=0)
    groups = [rows0[0] + lo1, rows0[0] + hi1]
    groups += [rows0[a] + lo1 for a in range(1, SUBLANES)]
    groups.append(hi0 + rows1[0])
    return jnp.concatenate(groups, axis=0)


def _route_head(q_head, keys_ref, valid, flat):
    scores = [lax.dot_general(keys_ref[p], q_head[:, p * PEER_KEYS:(p + 1) * PEER_KEYS],
                              (((1,), (1,)), ((), ())), preferred_element_type=F32)
              for p in range(2)]
    yield
    vals, idxs = [], []
    for st in scores:
        v, ix = yield from _top16_rows(st)
        vals.append(v)
        idxs.append(ix)
    cand = jnp.where(valid, _candidates(vals[0], vals[1]), -jnp.inf)
    code = _candidates([ix * float(PEER_KEYS) for ix in idxs[0]], idxs[1])
    top_s, top_code = [], []
    tie = None
    for _ in range(PEER_TOPK):
        m = jnp.max(cand, axis=0, keepdims=True)
        if tie is not None:
            m = m + tie
        fmin = jnp.min(jnp.where(cand == m, flat, float(PEER_TOPK * PEER_TOPK)),
                       axis=0, keepdims=True)
        hit = flat == fmin
        top_code.append(jnp.sum(jnp.where(hit, code, 0.0), axis=0, keepdims=True))
        cand = jnp.where(hit, -jnp.inf, cand)
        top_s.append(m)
        tie = yield
    ts = jnp.concatenate(top_s, axis=0)
    e = jnp.exp(ts - top_s[0])
    return (jnp.concatenate(top_code, axis=0).astype(jnp.int32),
            e / jnp.sum(e, axis=0, keepdims=True))


ROUTE_YIELDS = 3 * PEER_TOPK


def _zero_row(x, width):
    bits = lax.bitcast_convert_type(x[:1, :LANES], jnp.uint32)
    zero = lax.bitcast_convert_type((bits >> 16) >> 16, F32)
    return jnp.concatenate([zero] * (width // LANES), axis=1)


def _peer_kernel(h_ref, hn_ref, x_ref, mod_ref, wq_ref, keys_ref, ut_ref, v_ref, g_ref, b_ref,
                 o_ref, q_ref, code_ref, gate_ref, ci_ref, cg_ref, w_ref, acc_ref, hs_ref, us_ref,
                 *, alpha):
    i = pl.program_id(0)
    j = pl.program_id(1)
    nj = pl.num_programs(1)
    tm, d = x_ref.shape
    tn = ut_ref.shape[1]
    nhk = PEER_HEADS * PEER_TOPK
    parts = ROUTE_UNITS // PEER_HEADS
    tp = tm // parts
    slot = i % 2
    valid, flat = _candidate_layout(tp)

    def prepare_queries(h):
        q = jnp.dot(h, wq_ref[...], preferred_element_type=F32).astype(BF16)
        for hd in range(PEER_HEADS):
            q_ref[hd] = q[:, hd * 2 * PEER_KEYS:(hd + 1) * 2 * PEER_KEYS]

    def route_unit(u, dst):
        hd = u // parts
        part = u % parts
        rows = pl.ds(pl.multiple_of(part * tp, tp), tp)
        code, gate = yield from _route_head(q_ref[hd, rows, :], keys_ref, valid, flat)
        krows = pl.ds(pl.multiple_of(hd * PEER_TOPK, PEER_TOPK), PEER_TOPK)
        for g in range(tp // LANES):
            lanes = slice(g * LANES, (g + 1) * LANES)
            code_ref[dst, part * (tp // LANES) + g, krows, :] = code[:, lanes]
            gate_ref[dst, part * (tp // LANES) + g, krows, :] = gate[:, lanes]

    @pl.when((i == 0) & (j == 0))
    def _():
        prepare_queries(h_ref[...])

        def unit(u, carry):
            for _ in route_unit(u, 0):
                pass
            return carry

        lax.fori_loop(0, ROUTE_UNITS, unit, 0)

    @pl.when(j == 0)
    def _():
        for grp in range(tm // LANES):
            code = code_ref[slot, grp].T
            ci_ref[grp * LANES:(grp + 1) * LANES, :] = code // PEER_KEYS
            cg_ref[grp * LANES:(grp + 1) * LANES, :] = gate_ref[slot, grp].T
        key_rows = lax.broadcasted_iota(jnp.int32, (PEER_KEYS, nhk), 0).astype(BF16)
        key_lanes = lax.broadcasted_iota(jnp.int32, (nhk, PEER_KEYS), 1).astype(BF16)
        zero = jnp.zeros((PEER_KEYS, nhk), BF16)
        one = jnp.ones((nhk, PEER_KEYS), BF16)

        def token_group(grp, carry):
            base = pl.multiple_of(grp * LANES, LANES)
            second = (code_ref[slot, grp] % PEER_KEYS).astype(F32)

            def gate_bits(u):
                ri = ci_ref[pl.ds(base + u, 1), :].astype(BF16)
                rg = cg_ref[pl.ds(base + u, 1), :].astype(BF16)
                cj = jnp.broadcast_to(second[:, u:u + 1], (nhk, PEER_KEYS)).astype(BF16)
                pt = jnp.where(key_rows == ri, jnp.broadcast_to(rg, zero.shape), zero)
                qm = jnp.where(key_lanes == cj, one, jnp.zeros_like(one))
                wt = jnp.dot(pt, qm, preferred_element_type=F32)
                return lax.bitcast_convert_type(wt, jnp.uint32)

            for u in range(0, LANES, 2):
                packed = (gate_bits(u) >> 16) | (gate_bits(u + 1) & jnp.uint32(0xFFFF0000))
                pair = (base + u) // 2
                w_ref[pl.ds(pl.multiple_of(pair * W_PITCH, SUBLANES), PEER_KEYS), :] = packed
            return carry

        lax.fori_loop(0, tm // LANES, token_group, 0)
        acc_ref[...] = jnp.zeros(acc_ref.shape, F32)
        hs_ref[...] = h_ref[...]
        prepare_queries(hn_ref[...])

    route = route_unit(j, 1 - slot)
    next(route)
    chunk = 2 * PEER_KEYS
    n_chunks = tn // chunk

    def advance(after, count):
        try:
            route.send(None if after is None else _zero_row(after, tp))
            for _ in range(count - 1):
                next(route)
        except StopIteration:
            pass

    halves = 2
    th = tm // halves
    per_piece = -(-ROUTE_YIELDS // (2 * halves * n_chunks))

    def expert_inputs(half):
        rows = slice(half * th, (half + 1) * th)
        h = hs_ref[rows, :]
        pieces = []
        for r in range(n_chunks):
            c0 = r * chunk
            a = jnp.dot(h, us_ref[:, c0:c0 + chunk], preferred_element_type=F32)
            advance(a, per_piece)
            act = (a * (1.0 + lax.erf(a * (1.0 / math.sqrt(2.0))))).astype(BF16)
            for e in range(chunk // PEER_KEYS):
                first_key = (j * n_chunks + r) * (chunk // PEER_KEYS) + e
                words = w_ref[pl.ds(half * (th // 2) * W_PITCH + first_key, th // 2, stride=W_PITCH), :]
                gates = pltpu.bitcast(words, BF16)
                pieces.append(gates * act[:, e * PEER_KEYS:(e + 1) * PEER_KEYS])
        return jnp.concatenate(pieces, axis=1)

    def expert_outputs(half, z):
        rows = slice(half * th, (half + 1) * th)
        for c in range(d // chunk):
            cols = slice(c * chunk, (c + 1) * chunk)
            zv = jnp.dot(z, v_ref[:, cols], preferred_element_type=F32)
            acc_ref[rows, cols] += zv
            advance(zv, per_piece)

    us_ref[...] = ut_ref[...]
    advance(None, per_piece)
    z_prev = expert_inputs(0)
    for half in range(1, halves):
        z_next = expert_inputs(half)
        expert_outputs(half - 1, z_prev)
        z_prev = z_next
    expert_outputs(halves - 1, z_prev)
    for _ in route:
        pass

    @pl.when(j == nj - 1)
    def _():
        _, _, gate = _split_mod(mod_ref[0], d)
        o_ref[...] = _layer_norm(alpha * x_ref[...] + gate * acc_ref[...], g_ref[...], b_ref[...])


def _peer_layer(h, x, mods, mod_row, w_q, sub_keys, u_t, v, ln_g, ln_b, *, seq, alpha, tm=512):
    t, d = x.shape
    n = v.shape[0]
    tn = n // ROUTE_UNITS
    nhk = PEER_HEADS * PEER_TOPK
    parts = ROUTE_UNITS // PEER_HEADS
    tiles_per_seq = seq // tm
    last = t // tm - 1
    return pl.pallas_call(
        functools.partial(_peer_kernel, alpha=alpha),
        grid=(t // tm, n // tn),
        in_specs=[
            pl.BlockSpec((tm, d), lambda i, j: (i, 0)),
            pl.BlockSpec((tm, d), lambda i, j: (jnp.minimum(i + 1, last), 0)),
            pl.BlockSpec((tm, d), lambda i, j: (i, 0)),
            pl.BlockSpec((1, 1, 3 * d), lambda i, j: (mod_row + i // tiles_per_seq, 0, 0)),
            _const_spec(w_q.shape),
            _const_spec(sub_keys.shape),
            pl.BlockSpec((d, tn), lambda i, j: (0, j)),
            pl.BlockSpec((tn, d), lambda i, j: (j, 0)),
            _const_spec((1, d)),
            _const_spec((1, d)),
        ],
        out_specs=pl.BlockSpec((tm, d), lambda i, j: (i, 0)),
        out_shape=jax.ShapeDtypeStruct((t, d), F32),
        scratch_shapes=[
            pltpu.VMEM((PEER_HEADS, tm, 2 * PEER_KEYS), BF16),
            pltpu.VMEM((2, tm // LANES, nhk, LANES), jnp.int32),
            pltpu.VMEM((2, tm // LANES, nhk, LANES), F32),
            pltpu.VMEM((tm, nhk), jnp.int32),
            pltpu.VMEM((tm, nhk), F32),
            pltpu.VMEM((tm // 2 * W_PITCH, PEER_KEYS), jnp.uint32),
            pltpu.VMEM((tm, d), F32),
            pltpu.VMEM((tm, d), BF16),
            pltpu.VMEM((d, tn), BF16),
        ],
        compiler_params=_params(("arbitrary", "arbitrary")),
        name="peer_layer",
    )(h, h, x, mods, w_q.astype(BF16), sub_keys.astype(BF16), u_t, v, ln_g, ln_b)


def _lambda_init(layer_idx):
    return 0.8 - 0.6 * math.exp(-0.3 * layer_idx)


def kernel(x, c, ada_w, ada_b, ln_g, ln_b, conv_w_in, conv_w, conv_w_out, attn_w_qkv, attn_lambda,
           attn_subln_g, attn_w_o, peer_w_q, peer_sub_keys, peer_u, peer_v):
    batch, seq, d = x.shape
    depth = ada_w.shape[0]
    alpha = (2.0 * depth) ** 0.25
    head_dim = d // (2 * ATT_HEADS)
    mods = _ada_mods(c, ada_w, ada_b)
    xt = x.reshape(batch * seq, d)
    for i in range(depth):
        j = i // N_MIXERS
        row = (2 * i) * batch
        prow = (2 * i + 1) * batch
        g0, b0 = ln_g[i, 0].reshape(1, d), ln_b[i, 0].reshape(1, d)
        if i % N_MIXERS == 0:
            xt, h = _conv_layer(xt, mods, row, prow, conv_w_in[j], conv_w[j], conv_w_out[j], g0, b0,
                                seq=seq, alpha=alpha)
        else:
            qt, k, vt = _qkv_proj(xt, mods, row, attn_w_qkv[j], seq=seq, q_scale=head_dim ** -0.5)
            o = _diff_attention(qt, k, vt, attn_lambda[j], attn_subln_g[j], batch=batch, seq=seq,
                                lambda_init=_lambda_init(i))
            xt, h = _out_proj(o, xt, mods, row, prow, attn_w_o[j], g0, b0, seq=seq, alpha=alpha)
        g1, b1 = ln_g[i, 1].reshape(1, d), ln_b[i, 1].reshape(1, d)
        xt = _peer_layer(h, xt, mods, prow, peer_w_q[i], peer_sub_keys[i],
                         peer_u[i].T.astype(BF16), (0.5 * peer_v[i]).astype(BF16), g1, b1,
                         seq=seq, alpha=alpha)
    return xt.reshape(batch, seq, d)
```

```python
import functools
import math

import jax
import jax.numpy as jnp
from jax import lax
from jax.experimental import pallas as pl
from jax.experimental.pallas import tpu as pltpu

F32 = jnp.float32
BF16 = jnp.bfloat16

N_MIXERS = 2
ATT_HEADS = 8
PEER_HEADS = 8
PEER_KEYS = 128
PEER_TOPK = 16
LN_EPS = 1e-5

LANES = 128
SUBLANES = 8
VMEM_LIMIT_BYTES = 56 * 1024 * 1024
BF16_EXACT_INT = 256
EXP_UNDERFLOW = 106.0
NORM_SLACK = 1.02

N_CAND_GROUPS = 10
N_CAND = N_CAND_GROUPS * SUBLANES
W_PITCH = 136
ROUTE_UNITS = 16


def _layer_norm(r, g, b):
    mu = jnp.mean(r, axis=-1, keepdims=True)
    d = r - mu
    var = jnp.mean(d * d, axis=-1, keepdims=True)
    return d * lax.rsqrt(var + LN_EPS) * g + b


def _split_mod(mod, d):
    return mod[:, :d], 1.0 + mod[:, d:2 * d], 1.0 + mod[:, 2 * d:]


def _params(sem):
    return pltpu.CompilerParams(dimension_semantics=sem, vmem_limit_bytes=VMEM_LIMIT_BYTES)


def _const_spec(shape):
    nd = len(shape)
    return pl.BlockSpec(shape, lambda *_: (0,) * nd, pipeline_mode=pl.Buffered(1))


def _ada_kernel(c_ref, w_ref, b_ref, o_ref):
    c = c_ref[...]
    sc = c / (1.0 + jnp.exp(-c))
    o_ref[0] = jnp.dot(sc, w_ref[0], precision=lax.Precision.HIGHEST,
                       preferred_element_type=F32) + b_ref[0]


def _ada_mods(c, ada_w, ada_b):
    depth, _, d, d3 = ada_w.shape
    nb = c.shape[0]
    nmat = depth * 2
    tn = 1024
    w = ada_w.reshape(nmat, d, d3)
    b = ada_b.reshape(nmat, 1, d3)
    out = pl.pallas_call(
        _ada_kernel,
        grid=(nmat, d3 // tn),
        in_specs=[
            pl.BlockSpec((nb, d), lambda m, n: (0, 0)),
            pl.BlockSpec((1, d, tn), lambda m, n: (m, 0, n)),
            pl.BlockSpec((1, 1, tn), lambda m, n: (m, 0, n)),
        ],
        out_specs=pl.BlockSpec((1, nb, tn), lambda m, n: (m, 0, n)),
        out_shape=jax.ShapeDtypeStruct((nmat, nb, d3), F32),
        compiler_params=_params(("arbitrary", "arbitrary")),
        name="ada_mods",
    )(c, w, b)
    return out.reshape(nmat * nb, 1, d3)


def _emit_stage_output(r, g_ref, b_ref, modp_ref, o_ref, hp_ref):
    xn = _layer_norm(r, g_ref[...], b_ref[...])
    o_ref[...] = xn
    shift, scale, _ = _split_mod(modp_ref[0], xn.shape[1])
    hp_ref[...] = (xn * scale + shift).astype(BF16)


def _conv_kernel(x_ref, mod_ref, modp_ref, win_ref, cw_ref, wout_ref, g_ref, b_ref, o_ref, hp_ref,
                 zbuf, *, tiles_per_seq, alpha):
    i = pl.program_id(0)
    tm, d = x_ref.shape
    x = x_ref[...]
    shift, scale, gate = _split_mod(mod_ref[0], d)
    h = (x * scale + shift).astype(BF16)
    hw = jnp.dot(h, win_ref[...], preferred_element_type=F32)
    gb = hw[:, :d]
    z = hw[:, d:2 * d] * hw[:, 2 * d:]

    @pl.when(i % tiles_per_seq == 0)
    def _():
        zbuf[0:SUBLANES, :] = jnp.zeros((SUBLANES, d), F32)

    zbuf[SUBLANES:, :] = z
    z1 = zbuf[pl.ds(SUBLANES - 1, tm), :]
    z2 = zbuf[pl.ds(SUBLANES - 2, tm), :]
    cw = cw_ref[...]
    zc = cw[2:3] * z + cw[1:2] * z1 + cw[0:1] * z2
    zbuf[0:SUBLANES, :] = zbuf[tm:tm + SUBLANES, :]
    y = jnp.dot((gb * zc).astype(BF16), wout_ref[...], preferred_element_type=F32)
    _emit_stage_output(alpha * x + gate * y, g_ref, b_ref, modp_ref, o_ref, hp_ref)


def _conv_layer(x, mods, mod_row, peer_row, w_in, conv_w, w_out, ln_g, ln_b, *, seq, alpha, tm=512):
    t, d = x.shape
    tiles_per_seq = seq // tm
    return pl.pallas_call(
        functools.partial(_conv_kernel, tiles_per_seq=tiles_per_seq, alpha=alpha),
        grid=(t // tm,),
        in_specs=[
            pl.BlockSpec((tm, d), lambda i: (i, 0)),
            pl.BlockSpec((1, 1, 3 * d), lambda i: (mod_row + i // tiles_per_seq, 0, 0)),
            pl.BlockSpec((1, 1, 3 * d), lambda i: (peer_row + i // tiles_per_seq, 0, 0)),
            _const_spec((d, 3 * d)),
            _const_spec((conv_w.shape[0], d)),
            _const_spec((d, d)),
            _const_spec((1, d)),
            _const_spec((1, d)),
        ],
        out_specs=[pl.BlockSpec((tm, d), lambda i: (i, 0))] * 2,
        out_shape=[jax.ShapeDtypeStruct((t, d), F32), jax.ShapeDtypeStruct((t, d), BF16)],
        scratch_shapes=[pltpu.VMEM((tm + SUBLANES, d), F32)],
        compiler_params=_params(("arbitrary",)),
        name="conv_mixer",
    )(x, mods, mods, w_in.astype(BF16), conv_w, w_out.astype(BF16), ln_g, ln_b)


def _qkv_kernel(x_ref, mod_ref, w_ref, qt_ref, k_ref, vt_ref, *, q_scale):
    tm, d = x_ref.shape
    shift, scale, _ = _split_mod(mod_ref[0], d)
    h = (x_ref[...] * scale + shift).astype(BF16)
    qkv = jnp.dot(h, w_ref[...], preferred_element_type=F32)
    qt_ref[...] = (qkv[:, :d] * q_scale).T.astype(BF16)
    k_ref[...] = qkv[:, d:2 * d].astype(BF16)
    vt_ref[...] = qkv[:, 2 * d:].T.astype(BF16)


def _qkv_proj(x, mods, mod_row, w_qkv, *, seq, q_scale, tm=512):
    t, d = x.shape
    tiles_per_seq = seq // tm
    return pl.pallas_call(
        functools.partial(_qkv_kernel, q_scale=q_scale),
        grid=(t // tm,),
        in_specs=[
            pl.BlockSpec((tm, d), lambda i: (i, 0)),
            pl.BlockSpec((1, 1, 3 * d), lambda i: (mod_row + i // tiles_per_seq, 0, 0)),
            _const_spec((d, 3 * d)),
        ],
        out_specs=[
            pl.BlockSpec((d, tm), lambda i: (0, i)),
            pl.BlockSpec((tm, d), lambda i: (i, 0)),
            pl.BlockSpec((d, tm), lambda i: (0, i)),
        ],
        out_shape=[
            jax.ShapeDtypeStruct((d, t), BF16),
            jax.ShapeDtypeStruct((t, d), BF16),
            jax.ShapeDtypeStruct((d, t), BF16),
        ],
        compiler_params=_params(("arbitrary",)),
        name="attn_qkv",
    )(x, mods, w_qkv.astype(BF16))


def _attn_kernel(slopes_ref, qt_ref, k_ref, vt_ref, lam_ref, g_ref, o_ref, qa_ref, m_ref, acc_ref,
                 kn_ref, *, tq, tk, heads, lambda_init):
    hg = pl.program_id(1)
    qi = pl.program_id(2)
    dh2 = qt_ref.shape[0] // heads
    dh = dh2 // 2
    ones_rows = 2 * SUBLANES
    blocks_per_tile = tq // tk

    feat = lax.broadcasted_iota(jnp.int32, (dh2, tq), 0)
    arow = lax.broadcasted_iota(jnp.int32, (dh2, 2 * tq), 0)
    acol = lax.broadcasted_iota(jnp.int32, (dh2, 2 * tq), 1)
    r = jnp.where(acol >= tq, acol - tq, acol)
    r_lo = (r % BF16_EXACT_INT).astype(F32)
    r_hi = (r - r % BF16_EXACT_INT).astype(F32)
    for g in range(heads):
        slope = slopes_ref[hg * heads + g]
        qt = qt_ref[g * dh2:(g + 1) * dh2, :]
        zero = jnp.zeros_like(qt)
        qa_ref[g, :dh2, :] = jnp.concatenate(
            [jnp.where(feat < dh, qt, zero), jnp.where(feat >= dh, qt, zero)], axis=1)
        qa_ref[g, dh2:, :] = jnp.where(
            arow == 0, -slope * r_lo,
            jnp.where(arow == 1, -slope * r_hi, jnp.where(arow == 2, slope, 0.0))).astype(BF16)
    kcol = lax.broadcasted_iota(jnp.int32, (tk, dh2), 1)
    krow = lax.broadcasted_iota(jnp.int32, (tk, dh2), 0).astype(F32)
    k_extra = jnp.where(kcol <= 1, 1.0, jnp.where(kcol == 2, krow, 0.0)).astype(BF16)
    ones = jnp.ones((ones_rows, tk), BF16)

    m_ref[...] = jnp.full(m_ref.shape, -jnp.inf, F32)
    acc_ref[...] = jnp.zeros(acc_ref.shape, F32)

    @pl.when(qi == 0)
    def _():
        ones_sq = jnp.ones((dh2, LANES), BF16)
        for g in range(heads):
            def chunk(c, best, g=g):
                kc = k_ref[pl.ds(pl.multiple_of(c * tq, tq), tq), g * dh2:(g + 1) * dh2].astype(F32)
                rows = jnp.dot((kc * kc).astype(BF16), ones_sq, preferred_element_type=F32)
                return jnp.maximum(best, jnp.max(rows, axis=0, keepdims=True))
            kn_ref[g] = lax.fori_loop(0, k_ref.shape[0] // tq, chunk, jnp.zeros((1, LANES), F32))

    def step(j, diagonal):
        start = pl.multiple_of(j * tk, tk)
        delta = qi * tq - j * tk
        head_cols = [slice(g * dh2, (g + 1) * dh2) for g in range(heads)]
        scores, maxima = [], []
        for g, cols in enumerate(head_cols):
            s = jnp.dot(jnp.concatenate([k_ref[pl.ds(start, tk), cols], k_extra], axis=1),
                        qa_ref[g], preferred_element_type=F32)
            if diagonal:
                mrow = lax.broadcasted_iota(jnp.int32, (tk, 2 * tq), 0)
                mcol = lax.broadcasted_iota(jnp.int32, (tk, 2 * tq), 1)
                mq = jnp.where(mcol >= tq, mcol - tq, mcol)
                s = jnp.where(mrow - mq <= delta, s, -jnp.inf)
            scores.append(s)
            maxima.append(jnp.max(s, axis=0, keepdims=True))
        probs, corrs = [], []
        for g, s in enumerate(scores):
            off = delta.astype(F32) * slopes_ref[hg * heads + g]
            m_old = m_ref[g]
            m_new = jnp.maximum(m_old, maxima[g] - off)
            probs.append(jnp.exp(s - (m_new + off)).astype(BF16))
            corrs.append(jnp.exp(m_old - m_new))
            m_ref[g] = m_new
        for g, cols in enumerate(head_cols):
            v_aug = jnp.concatenate([vt_ref[cols, pl.ds(start, tk)], ones], axis=0)
            acc_ref[g] = corrs[g] * acc_ref[g] + jnp.dot(v_aug, probs[g],
                                                         preferred_element_type=F32)

    first_diag = qi * blocks_per_tile
    for d in range(blocks_per_tile):
        step(first_diag + d, True)

    needed = jnp.zeros((1, 2 * tq), F32)
    for g in range(heads):
        qf = qa_ref[g, :dh2, :].astype(F32)
        q_norm = jnp.sqrt(jnp.sum(qf * qf, axis=0, keepdims=True))
        k_norm = jnp.sqrt(kn_ref[g][:, :1])
        reach = (q_norm * k_norm * NORM_SLACK - m_ref[g] + EXP_UNDERFLOW) / slopes_ref[hg * heads + g]
        needed = jnp.maximum(needed, jnp.floor((reach - 1.0) / tk) + 1.0)
    needed = jnp.clip(needed, 0.0, first_diag.astype(F32))
    n_blocks = jnp.max(needed.astype(jnp.int32))

    def body(i, carry):
        step(first_diag - 1 - i, False)
        return carry

    lax.fori_loop(0, n_blocks, body, 0)

    lam = lam_ref[...]
    lam_full = (jnp.exp(jnp.sum(lam[0:1] * lam[1:2], axis=1, keepdims=True))
                - jnp.exp(jnp.sum(lam[2:3] * lam[3:4], axis=1, keepdims=True)) + lambda_init)
    for g in range(heads):
        acc = acc_ref[g]
        on = acc[:dh2] / acc[dh2:dh2 + 1]
        o = on[:, :tq] - lam_full * on[:, tq:]
        o = o * lax.rsqrt(jnp.mean(o * o, axis=0, keepdims=True) + LN_EPS) * g_ref[...]
        o_ref[:, g * dh2:(g + 1) * dh2] = (o * (1.0 - lambda_init)).T.astype(o_ref.dtype)


def _diff_attention(qt, k, vt, lam, subln_g, *, batch, seq, lambda_init, tq=512, tk=256, heads=4):
    d, t = qt.shape
    dh2 = d // ATT_HEADS
    gw = heads * dh2
    nq = seq // tq
    assert tq % tk == 0 and tk <= BF16_EXACT_INT and tq < BF16_EXACT_INT * BF16_EXACT_INT
    slopes = 2.0 ** (-(8.0 / ATT_HEADS) * jnp.arange(1, ATT_HEADS + 1, dtype=F32))
    return pl.pallas_call(
        functools.partial(_attn_kernel, tq=tq, tk=tk, heads=heads, lambda_init=lambda_init),
        grid=(batch, ATT_HEADS // heads, nq),
        in_specs=[
            pl.BlockSpec(memory_space=pltpu.SMEM),
            pl.BlockSpec((gw, tq), lambda b, h, i: (h, b * nq + i)),
            pl.BlockSpec((seq, gw), lambda b, h, i: (b, h)),
            pl.BlockSpec((gw, seq), lambda b, h, i: (h, b)),
            pl.BlockSpec(lam.shape, lambda b, h, i: (0, 0)),
            pl.BlockSpec((dh2, 1), lambda b, h, i: (0, 0)),
        ],
        out_specs=pl.BlockSpec((tq, gw), lambda b, h, i: (b * nq + i, h)),
        out_shape=jax.ShapeDtypeStruct((t, d), BF16),
        scratch_shapes=[
            pltpu.VMEM((heads, 2 * dh2, 2 * tq), BF16),
            pltpu.VMEM((heads, 1, 2 * tq), F32),
            pltpu.VMEM((heads, dh2 + 2 * SUBLANES, 2 * tq), F32),
            pltpu.VMEM((heads, 1, LANES), F32),
        ],
        compiler_params=_params(("arbitrary", "arbitrary", "arbitrary")),
        name="diff_attention",
    )(slopes, qt, k, vt, lam, subln_g.reshape(dh2, 1))


def _oproj_kernel(o_ref, x_ref, mod_ref, modp_ref, w_ref, g_ref, b_ref, out_ref, hp_ref, *, alpha):
    tm, d = x_ref.shape
    _, _, gate = _split_mod(mod_ref[0], d)
    y = jnp.dot(o_ref[...], w_ref[...], preferred_element_type=F32)
    _emit_stage_output(alpha * x_ref[...] + gate * y, g_ref, b_ref, modp_ref, out_ref, hp_ref)


def _out_proj(o, x, mods, mod_row, peer_row, w_o, ln_g, ln_b, *, seq, alpha, tm=512):
    t, d = x.shape
    tiles_per_seq = seq // tm
    return pl.pallas_call(
        functools.partial(_oproj_kernel, alpha=alpha),
        grid=(t // tm,),
        in_specs=[
            pl.BlockSpec((tm, d), lambda i: (i, 0)),
            pl.BlockSpec((tm, d), lambda i: (i, 0)),
            pl.BlockSpec((1, 1, 3 * d), lambda i: (mod_row + i // tiles_per_seq, 0, 0)),
            pl.BlockSpec((1, 1, 3 * d), lambda i: (peer_row + i // tiles_per_seq, 0, 0)),
            _const_spec((d, d)),
            _const_spec((1, d)),
            _const_spec((1, d)),
        ],
        out_specs=[pl.BlockSpec((tm, d), lambda i: (i, 0))] * 2,
        out_shape=[jax.ShapeDtypeStruct((t, d), F32), jax.ShapeDtypeStruct((t, d), BF16)],
        compiler_params=_params(("arbitrary",)),
        name="attn_out_proj",
    )(o, x, mods, mods, w_o.astype(BF16), ln_g, ln_b)


def _top16_rows(s):
    n, tm = s.shape
    rows = lax.broadcasted_iota(jnp.int32, (n, tm), 0).astype(F32)
    vals, idxs = [], []
    tie = None
    for _ in range(PEER_TOPK):
        m = jnp.max(s, axis=0, keepdims=True)
        if tie is not None:
            m = m + tie
        idx = jnp.min(jnp.where(s == m, rows, float(n)), axis=0, keepdims=True)
        s = jnp.where(rows == idx, -jnp.inf, s)
        vals.append(m)
        idxs.append(idx)
        tie = yield
    return vals, idxs


def _candidate_layout(tm):
    slot = lax.broadcasted_iota(jnp.int32, (N_CAND, tm), 0)
    grp = slot // SUBLANES
    r = slot % SUBLANES
    a = jnp.where(grp <= 1, 0, jnp.where(grp == N_CAND_GROUPS - 1, SUBLANES + r, grp - 1))
    b = jnp.where(grp == 0, r, jnp.where(grp == 1, SUBLANES + r,
                                         jnp.where(grp == N_CAND_GROUPS - 1, 0, r)))
    valid = (a + 1) * (b + 1) <= PEER_TOPK
    flat = (a * PEER_TOPK + b).astype(F32)
    return valid, flat


def _candidates(rows0, rows1):
    lo1 = jnp.concatenate(rows1[:SUBLANES], axis=0)
    hi1 = jnp.concatenate(rows1[SUBLANES:], axis=0)
    hi0 = jnp.concatenate(rows0[SUBLANES:], axis=0)
    groups = [rows0[0] + lo1, rows0[0] + hi1]
    groups += [rows0[a] + lo1 for a in range(1, SUBLANES)]
    groups.append(hi0 + rows1[0])
    return jnp.concatenate(groups, axis=0)


def _route_head(q_head, keys_ref, valid, flat):
    scores = [lax.dot_general(keys_ref[p], q_head[:, p * PEER_KEYS:(p + 1) * PEER_KEYS],
                              (((1,), (1,)), ((), ())), preferred_element_type=F32)
              for p in range(2)]
    yield
    vals, idxs = [], []
    for st in scores:
        v, ix = yield from _top16_rows(st)
        vals.append(v)
        idxs.append(ix)
    cand = jnp.where(valid, _candidates(vals[0], vals[1]), -jnp.inf)
    code = _candidates([ix * float(PEER_KEYS) for ix in idxs[0]], idxs[1])
    top_s, top_code = [], []
    tie = None
    for _ in range(PEER_TOPK):
        m = jnp.max(cand, axis=0, keepdims=True)
        if tie is not None:
            m = m + tie
        fmin = jnp.min(jnp.where(cand == m, flat, float(PEER_TOPK * PEER_TOPK)),
                       axis=0, keepdims=True)
        hit = flat == fmin
        top_code.append(jnp.sum(jnp.where(hit, code, 0.0), axis=0, keepdims=True))
        cand = jnp.where(hit, -jnp.inf, cand)
        top_s.append(m)
        tie = yield
    ts = jnp.concatenate(top_s, axis=0)
    e = jnp.exp(ts - top_s[0])
    return (jnp.concatenate(top_code, axis=0).astype(jnp.int32),
            e / jnp.sum(e, axis=0, keepdims=True))


ROUTE_YIELDS = 3 * PEER_TOPK


def _zero_row(x, width):
    bits = lax.bitcast_convert_type(x[:1, :LANES], jnp.uint32)
    zero = lax.bitcast_convert_type((bits >> 16) >> 16, F32)
    return jnp.concatenate([zero] * (width // LANES), axis=1)


def _peer_kernel(h_ref, hn_ref, x_ref, mod_ref, wq_ref, keys_ref, ut_ref, v_ref, g_ref, b_ref,
                 o_ref, q_ref, code_ref, gate_ref, ci_ref, cg_ref, w_ref, acc_ref, hs_ref, us_ref,
                 *, alpha):
    i = pl.program_id(0)
    j = pl.program_id(1)
    nj = pl.num_programs(1)
    tm, d = x_ref.shape
    tn = ut_ref.shape[1]
    nhk = PEER_HEADS * PEER_TOPK
    parts = ROUTE_UNITS // PEER_HEADS
    tp = tm // parts
    slot = i % 2
    valid, flat = _candidate_layout(tp)

    def prepare_queries(h):
        q = jnp.dot(h, wq_ref[...], preferred_element_type=F32).astype(BF16)
        for hd in range(PEER_HEADS):
            q_ref[hd] = q[:, hd * 2 * PEER_KEYS:(hd + 1) * 2 * PEER_KEYS]

    def route_unit(u, dst):
        hd = u // parts
        part = u % parts
        rows = pl.ds(pl.multiple_of(part * tp, tp), tp)
        code, gate = yield from _route_head(q_ref[hd, rows, :], keys_ref, valid, flat)
        krows = pl.ds(pl.multiple_of(hd * PEER_TOPK, PEER_TOPK), PEER_TOPK)
        for g in range(tp // LANES):
            lanes = slice(g * LANES, (g + 1) * LANES)
            code_ref[dst, part * (tp // LANES) + g, krows, :] = code[:, lanes]
            gate_ref[dst, part * (tp // LANES) + g, krows, :] = gate[:, lanes]

    @pl.when((i == 0) & (j == 0))
    def _():
        prepare_queries(h_ref[...])

        def unit(u, carry):
            for _ in route_unit(u, 0):
                pass
            return carry

        lax.fori_loop(0, ROUTE_UNITS, unit, 0)

    @pl.when(j == 0)
    def _():
        for grp in range(tm // LANES):
            code = code_ref[slot, grp].T
            ci_ref[grp * LANES:(grp + 1) * LANES, :] = code // PEER_KEYS
            cg_ref[grp * LANES:(grp + 1) * LANES, :] = gate_ref[slot, grp].T
        key_rows = lax.broadcasted_iota(jnp.int32, (PEER_KEYS, nhk), 0).astype(BF16)
        key_lanes = lax.broadcasted_iota(jnp.int32, (nhk, PEER_KEYS), 1).astype(BF16)
        zero = jnp.zeros((PEER_KEYS, nhk), BF16)
        one = jnp.ones((nhk, PEER_KEYS), BF16)

        heads_per_group = PEER_HEADS // (tm // LANES)
        qcols = heads_per_group * 2 * PEER_KEYS

        def token_group(grp, carry):
            base = pl.multiple_of(grp * LANES, LANES)
            second = (code_ref[slot, grp] % PEER_KEYS).astype(F32)
            qs = jnp.dot(hn_ref[...], wq_ref[:, pl.ds(pl.multiple_of(grp * qcols, qcols), qcols)],
                         preferred_element_type=F32).astype(BF16)
            for k in range(heads_per_group):
                q_ref[grp * heads_per_group + k] = qs[:, k * 2 * PEER_KEYS:(k + 1) * 2 * PEER_KEYS]

            def gate_bits(u):
                ri = ci_ref[pl.ds(base + u, 1), :].astype(BF16)
                rg = cg_ref[pl.ds(base + u, 1), :].astype(BF16)
                cj = jnp.broadcast_to(second[:, u:u + 1], (nhk, PEER_KEYS)).astype(BF16)
                pt = jnp.where(key_rows == ri, jnp.broadcast_to(rg, zero.shape), zero)
                qm = jnp.where(key_lanes == cj, one, jnp.zeros_like(one))
                wt = jnp.dot(pt, qm, preferred_element_type=F32)
                return lax.bitcast_convert_type(wt, jnp.uint32)

            for u in range(0, LANES, 2):
                packed = (gate_bits(u) >> 16) | (gate_bits(u + 1) & jnp.uint32(0xFFFF0000))
                pair = (base + u) // 2
                w_ref[pl.ds(pl.multiple_of(pair * W_PITCH, SUBLANES), PEER_KEYS), :] = packed
            return carry

        lax.fori_loop(0, tm // LANES, token_group, 0)
        acc_ref[...] = jnp.zeros(acc_ref.shape, F32)
        hs_ref[...] = h_ref[...]

    route = route_unit(j, 1 - slot)
    next(route)
    chunk = 2 * PEER_KEYS
    n_chunks = tn // chunk

    def advance(after, count):
        try:
            route.send(None if after is None else _zero_row(after, tp))
            for _ in range(count - 1):
                next(route)
        except StopIteration:
            pass

    halves = 2
    th = tm // halves
    per_piece = -(-ROUTE_YIELDS // (2 * halves * n_chunks))

    def expert_inputs(half):
        rows = slice(half * th, (half + 1) * th)
        h = hs_ref[rows, :]
        pieces = []
        for r in range(n_chunks):
            c0 = r * chunk
            a = jnp.dot(h, us_ref[:, c0:c0 + chunk], preferred_element_type=F32)
            advance(a, per_piece)
            act = (a * (1.0 + lax.erf(a * (1.0 / math.sqrt(2.0))))).astype(BF16)
            for e in range(chunk // PEER_KEYS):
                first_key = (j * n_chunks + r) * (chunk // PEER_KEYS) + e
                words = w_ref[pl.ds(half * (th // 2) * W_PITCH + first_key, th // 2, stride=W_PITCH), :]
                gates = pltpu.bitcast(words, BF16)
                pieces.append(gates * act[:, e * PEER_KEYS:(e + 1) * PEER_KEYS])
        return jnp.concatenate(pieces, axis=1)

    def expert_outputs(half, z):
        rows = slice(half * th, (half + 1) * th)
        for c in range(d // chunk):
            cols = slice(c * chunk, (c + 1) * chunk)
            zv = jnp.dot(z, v_ref[:, cols], preferred_element_type=F32)
            acc_ref[rows, cols] += zv
            advance(zv, per_piece)

    us_ref[...] = ut_ref[...]
    advance(None, per_piece)
    z_prev = expert_inputs(0)
    for half in range(1, halves):
        z_next = expert_inputs(half)
        expert_outputs(half - 1, z_prev)
        z_prev = z_next
    expert_outputs(halves - 1, z_prev)
    for _ in route:
        pass

    @pl.when(j == nj - 1)
    def _():
        _, _, gate = _split_mod(mod_ref[0], d)
        o_ref[...] = _layer_norm(alpha * x_ref[...] + gate * acc_ref[...], g_ref[...], b_ref[...])


def _peer_layer(h, x, mods, mod_row, w_q, sub_keys, u_t, v, ln_g, ln_b, *, seq, alpha, tm=512):
    t, d = x.shape
    n = v.shape[0]
    tn = n // ROUTE_UNITS
    nhk = PEER_HEADS * PEER_TOPK
    parts = ROUTE_UNITS // PEER_HEADS
    tiles_per_seq = seq // tm
    last = t // tm - 1
    return pl.pallas_call(
        functools.partial(_peer_kernel, alpha=alpha),
        grid=(t // tm, n // tn),
        in_specs=[
            pl.BlockSpec((tm, d), lambda i, j: (i, 0)),
            pl.BlockSpec((tm, d), lambda i, j: (jnp.minimum(i + 1, last), 0)),
            pl.BlockSpec((tm, d), lambda i, j: (i, 0)),
            pl.BlockSpec((1, 1, 3 * d), lambda i, j: (mod_row + i // tiles_per_seq, 0, 0)),
            _const_spec(w_q.shape),
            _const_spec(sub_keys.shape),
            pl.BlockSpec((d, tn), lambda i, j: (0, j)),
            pl.BlockSpec((tn, d), lambda i, j: (j, 0)),
            _const_spec((1, d)),
            _const_spec((1, d)),
        ],
        out_specs=pl.BlockSpec((tm, d), lambda i, j: (i, 0)),
        out_shape=jax.ShapeDtypeStruct((t, d), F32),
        scratch_shapes=[
            pltpu.VMEM((PEER_HEADS, tm, 2 * PEER_KEYS), BF16),
            pltpu.VMEM((2, tm // LANES, nhk, LANES), jnp.int32),
            pltpu.VMEM((2, tm // LANES, nhk, LANES), F32),
            pltpu.VMEM((tm, nhk), jnp.int32),
            pltpu.VMEM((tm, nhk), F32),
            pltpu.VMEM((tm // 2 * W_PITCH, PEER_KEYS), jnp.uint32),
            pltpu.VMEM((tm, d), F32),
            pltpu.VMEM((tm, d), BF16),
            pltpu.VMEM((d, tn), BF16),
        ],
        compiler_params=_params(("arbitrary", "arbitrary")),
        name="peer_layer",
    )(h, h, x, mods, w_q.astype(BF16), sub_keys.astype(BF16), u_t, v, ln_g, ln_b)


def _lambda_init(layer_idx):
    return 0.8 - 0.6 * math.exp(-0.3 * layer_idx)


def kernel(x, c, ada_w, ada_b, ln_g, ln_b, conv_w_in, conv_w, conv_w_out, attn_w_qkv, attn_lambda,
           attn_subln_g, attn_w_o, peer_w_q, peer_sub_keys, peer_u, peer_v):
    batch, seq, d = x.shape
    depth = ada_w.shape[0]
    alpha = (2.0 * depth) ** 0.25
    head_dim = d // (2 * ATT_HEADS)
    mods = _ada_mods(c, ada_w, ada_b)
    xt = x.reshape(batch * seq, d)
    for i in range(depth):
        j = i // N_MIXERS
        row = (2 * i) * batch
        prow = (2 * i + 1) * batch
        g0, b0 = ln_g[i, 0].reshape(1, d), ln_b[i, 0].reshape(1, d)
        if i % N_MIXERS == 0:
            xt, h = _conv_layer(xt, mods, row, prow, conv_w_in[j], conv_w[j], conv_w_out[j], g0, b0,
                                seq=seq, alpha=alpha)
        else:
            qt, k, vt = _qkv_proj(xt, mods, row, attn_w_qkv[j], seq=seq, q_scale=head_dim ** -0.5)
            o = _diff_attention(qt, k, vt, attn_lambda[j], attn_subln_g[j], batch=batch, seq=seq,
                                lambda_init=_lambda_init(i))
            xt, h = _out_proj(o, xt, mods, row, prow, attn_w_o[j], g0, b0, seq=seq, alpha=alpha)
        g1, b1 = ln_g[i, 1].reshape(1, d), ln_b[i, 1].reshape(1, d)
        xt = _peer_layer(h, xt, mods, prow, peer_w_q[i], peer_sub_keys[i],
                         peer_u[i].T.astype(BF16), (0.5 * peer_v[i]).astype(BF16), g1, b1,
                         seq=seq, alpha=alpha)
    return xt.reshape(batch, seq, d)
```

```python
import functools
import math

import jax
import jax.numpy as jnp
from jax import lax
from jax.experimental import pallas as pl
from jax.experimental.pallas import tpu as pltpu

F32 = jnp.float32
BF16 = jnp.bfloat16

N_MIXERS = 2
ATT_HEADS = 8
PEER_HEADS = 8
PEER_KEYS = 128
PEER_TOPK = 16
LN_EPS = 1e-5

LANES = 128
SUBLANES = 8
VMEM_LIMIT_BYTES = 56 * 1024 * 1024
BF16_EXACT_INT = 256
EXP_UNDERFLOW = 106.0
NORM_SLACK = 1.02
FIXED_SHIFT_LIMIT = 60.0

N_CAND_GROUPS = 10
N_CAND = N_CAND_GROUPS * SUBLANES
W_PITCH = 136
ROUTE_UNITS = 16


def _layer_norm(r, g, b):
    mu = jnp.mean(r, axis=-1, keepdims=True)
    d = r - mu
    var = jnp.mean(d * d, axis=-1, keepdims=True)
    return d * lax.rsqrt(var + LN_EPS) * g + b


def _split_mod(mod, d):
    return mod[:, :d], 1.0 + mod[:, d:2 * d], 1.0 + mod[:, 2 * d:]


def _params(sem):
    return pltpu.CompilerParams(dimension_semantics=sem, vmem_limit_bytes=VMEM_LIMIT_BYTES)


def _const_spec(shape):
    nd = len(shape)
    return pl.BlockSpec(shape, lambda *_: (0,) * nd, pipeline_mode=pl.Buffered(1))


def _ada_kernel(c_ref, w_ref, b_ref, o_ref):
    c = c_ref[...]
    sc = c / (1.0 + jnp.exp(-c))
    o_ref[0] = jnp.dot(sc, w_ref[0], precision=lax.Precision.HIGHEST,
                       preferred_element_type=F32) + b_ref[0]


def _ada_mods(c, ada_w, ada_b):
    depth, _, d, d3 = ada_w.shape
    nb = c.shape[0]
    nmat = depth * 2
    tn = 1024
    w = ada_w.reshape(nmat, d, d3)
    b = ada_b.reshape(nmat, 1, d3)
    out = pl.pallas_call(
        _ada_kernel,
        grid=(nmat, d3 // tn),
        in_specs=[
            pl.BlockSpec((nb, d), lambda m, n: (0, 0)),
            pl.BlockSpec((1, d, tn), lambda m, n: (m, 0, n)),
            pl.BlockSpec((1, 1, tn), lambda m, n: (m, 0, n)),
        ],
        out_specs=pl.BlockSpec((1, nb, tn), lambda m, n: (m, 0, n)),
        out_shape=jax.ShapeDtypeStruct((nmat, nb, d3), F32),
        compiler_params=_params(("arbitrary", "arbitrary")),
        name="ada_mods",
    )(c, w, b)
    return out.reshape(nmat * nb, 1, d3)


def _emit_stage_output(r, g_ref, b_ref, modp_ref, o_ref, hp_ref):
    xn = _layer_norm(r, g_ref[...], b_ref[...])
    o_ref[...] = xn
    shift, scale, _ = _split_mod(modp_ref[0], xn.shape[1])
    hp_ref[...] = (xn * scale + shift).astype(BF16)


def _conv_kernel(x_ref, mod_ref, modp_ref, win_ref, cw_ref, wout_ref, g_ref, b_ref, o_ref, hp_ref,
                 zbuf, *, tiles_per_seq, alpha):
    i = pl.program_id(0)
    tm, d = x_ref.shape
    x = x_ref[...]
    shift, scale, gate = _split_mod(mod_ref[0], d)
    h = (x * scale + shift).astype(BF16)
    hw = jnp.dot(h, win_ref[...], preferred_element_type=F32)
    gb = hw[:, :d]
    z = hw[:, d:2 * d] * hw[:, 2 * d:]

    @pl.when(i % tiles_per_seq == 0)
    def _():
        zbuf[0:SUBLANES, :] = jnp.zeros((SUBLANES, d), F32)

    zbuf[SUBLANES:, :] = z
    z1 = zbuf[pl.ds(SUBLANES - 1, tm), :]
    z2 = zbuf[pl.ds(SUBLANES - 2, tm), :]
    cw = cw_ref[...]
    zc = cw[2:3] * z + cw[1:2] * z1 + cw[0:1] * z2
    zbuf[0:SUBLANES, :] = zbuf[tm:tm + SUBLANES, :]
    y = jnp.dot((gb * zc).astype(BF16), wout_ref[...], preferred_element_type=F32)
    _emit_stage_output(alpha * x + gate * y, g_ref, b_ref, modp_ref, o_ref, hp_ref)


def _conv_layer(x, mods, mod_row, peer_row, w_in, conv_w, w_out, ln_g, ln_b, *, seq, alpha, tm=512):
    t, d = x.shape
    tiles_per_seq = seq // tm
    return pl.pallas_call(
        functools.partial(_conv_kernel, tiles_per_seq=tiles_per_seq, alpha=alpha),
        grid=(t // tm,),
        in_specs=[
            pl.BlockSpec((tm, d), lambda i: (i, 0)),
            pl.BlockSpec((1, 1, 3 * d), lambda i: (mod_row + i // tiles_per_seq, 0, 0)),
            pl.BlockSpec((1, 1, 3 * d), lambda i: (peer_row + i // tiles_per_seq, 0, 0)),
            _const_spec((d, 3 * d)),
            _const_spec((conv_w.shape[0], d)),
            _const_spec((d, d)),
            _const_spec((1, d)),
            _const_spec((1, d)),
        ],
        out_specs=[pl.BlockSpec((tm, d), lambda i: (i, 0))] * 2,
        out_shape=[jax.ShapeDtypeStruct((t, d), F32), jax.ShapeDtypeStruct((t, d), BF16)],
        scratch_shapes=[pltpu.VMEM((tm + SUBLANES, d), F32)],
        compiler_params=_params(("arbitrary",)),
        name="conv_mixer",
    )(x, mods, mods, w_in.astype(BF16), conv_w, w_out.astype(BF16), ln_g, ln_b)


def _qkv_kernel(x_ref, mod_ref, w_ref, qt_ref, k_ref, vt_ref, *, q_scale):
    tm, d = x_ref.shape
    shift, scale, _ = _split_mod(mod_ref[0], d)
    h = (x_ref[...] * scale + shift).astype(BF16)
    qkv = jnp.dot(h, w_ref[...], preferred_element_type=F32)
    qt_ref[...] = (qkv[:, :d] * q_scale).T.astype(BF16)
    k_ref[...] = qkv[:, d:2 * d].astype(BF16)
    vt_ref[...] = qkv[:, 2 * d:].T.astype(BF16)


def _qkv_proj(x, mods, mod_row, w_qkv, *, seq, q_scale, tm=512):
    t, d = x.shape
    tiles_per_seq = seq // tm
    return pl.pallas_call(
        functools.partial(_qkv_kernel, q_scale=q_scale),
        grid=(t // tm,),
        in_specs=[
            pl.BlockSpec((tm, d), lambda i: (i, 0)),
            pl.BlockSpec((1, 1, 3 * d), lambda i: (mod_row + i // tiles_per_seq, 0, 0)),
            _const_spec((d, 3 * d)),
        ],
        out_specs=[
            pl.BlockSpec((d, tm), lambda i: (0, i)),
            pl.BlockSpec((tm, d), lambda i: (i, 0)),
            pl.BlockSpec((d, tm), lambda i: (0, i)),
        ],
        out_shape=[
            jax.ShapeDtypeStruct((d, t), BF16),
            jax.ShapeDtypeStruct((t, d), BF16),
            jax.ShapeDtypeStruct((d, t), BF16),
        ],
        compiler_params=_params(("arbitrary",)),
        name="attn_qkv",
    )(x, mods, w_qkv.astype(BF16))


def _attn_kernel(slopes_ref, qt_ref, k_ref, vt_ref, lam_ref, g_ref, o_ref, qa_ref, m_ref, acc_ref,
                 kn_ref, *, tq, tk, heads, lambda_init):
    hg = pl.program_id(1)
    qi = pl.program_id(2)
    dh2 = qt_ref.shape[0] // heads
    dh = dh2 // 2
    ones_rows = 2 * SUBLANES
    blocks_per_tile = tq // tk

    feat = lax.broadcasted_iota(jnp.int32, (dh2, tq), 0)
    arow = lax.broadcasted_iota(jnp.int32, (dh2, 2 * tq), 0)
    acol = lax.broadcasted_iota(jnp.int32, (dh2, 2 * tq), 1)
    r = jnp.where(acol >= tq, acol - tq, acol)
    r_lo = (r % BF16_EXACT_INT).astype(F32)
    r_hi = (r - r % BF16_EXACT_INT).astype(F32)
    for g in range(heads):
        slope = slopes_ref[hg * heads + g]
        qt = qt_ref[g * dh2:(g + 1) * dh2, :]
        zero = jnp.zeros_like(qt)
        qa_ref[g, :dh2, :] = jnp.concatenate(
            [jnp.where(feat < dh, qt, zero), jnp.where(feat >= dh, qt, zero)], axis=1)
        qa_ref[g, dh2:, :] = jnp.where(
            arow == 0, -slope * r_lo,
            jnp.where(arow == 1, -slope * r_hi, jnp.where(arow == 2, slope, 0.0))).astype(BF16)
    kcol = lax.broadcasted_iota(jnp.int32, (tk, dh2), 1)
    krow = lax.broadcasted_iota(jnp.int32, (tk, dh2), 0).astype(F32)
    k_extra = jnp.where(kcol <= 1, 1.0, jnp.where(kcol == 2, krow, 0.0)).astype(BF16)
    ones = jnp.ones((ones_rows, tk), BF16)

    m_ref[...] = jnp.full(m_ref.shape, -jnp.inf, F32)
    acc_ref[...] = jnp.zeros(acc_ref.shape, F32)

    @pl.when(qi == 0)
    def _():
        ones_sq = jnp.ones((dh2, LANES), BF16)
        for g in range(heads):
            def chunk(c, best, g=g):
                kc = k_ref[pl.ds(pl.multiple_of(c * tq, tq), tq), g * dh2:(g + 1) * dh2].astype(F32)
                rows = jnp.dot((kc * kc).astype(BF16), ones_sq, preferred_element_type=F32)
                return jnp.maximum(best, jnp.max(rows, axis=0, keepdims=True))
            kn_ref[g] = lax.fori_loop(0, k_ref.shape[0] // tq, chunk, jnp.zeros((1, LANES), F32))

    def step(j, diagonal):
        start = pl.multiple_of(j * tk, tk)
        delta = qi * tq - j * tk
        head_cols = [slice(g * dh2, (g + 1) * dh2) for g in range(heads)]
        scores, maxima = [], []
        for g, cols in enumerate(head_cols):
            s = jnp.dot(jnp.concatenate([k_ref[pl.ds(start, tk), cols], k_extra], axis=1),
                        qa_ref[g], preferred_element_type=F32)
            if diagonal:
                mrow = lax.broadcasted_iota(jnp.int32, (tk, 2 * tq), 0)
                mcol = lax.broadcasted_iota(jnp.int32, (tk, 2 * tq), 1)
                mq = jnp.where(mcol >= tq, mcol - tq, mcol)
                s = jnp.where(mrow - mq <= delta, s, -jnp.inf)
            scores.append(s)
            maxima.append(jnp.max(s, axis=0, keepdims=True))
        probs, corrs = [], []
        for g, s in enumerate(scores):
            off = delta.astype(F32) * slopes_ref[hg * heads + g]
            m_old = m_ref[g]
            m_new = jnp.maximum(m_old, maxima[g] - off)
            probs.append(jnp.exp(s - (m_new + off)).astype(BF16))
            corrs.append(jnp.exp(m_old - m_new))
            m_ref[g] = m_new
        for g, cols in enumerate(head_cols):
            v_aug = jnp.concatenate([vt_ref[cols, pl.ds(start, tk)], ones], axis=0)
            acc_ref[g] = corrs[g] * acc_ref[g] + jnp.dot(v_aug, probs[g],
                                                         preferred_element_type=F32)

    first_diag = qi * blocks_per_tile
    for d in range(blocks_per_tile):
        step(first_diag + d, True)

    needed = jnp.zeros((1, 2 * tq), F32)
    bound = jnp.zeros((1, 2 * tq), F32)
    for g in range(heads):
        qf = qa_ref[g, :dh2, :].astype(F32)
        q_norm = jnp.sqrt(jnp.sum(qf * qf, axis=0, keepdims=True))
        k_norm = jnp.sqrt(kn_ref[g][:, :1])
        qk = q_norm * k_norm * NORM_SLACK
        reach = (qk - m_ref[g] + EXP_UNDERFLOW) / slopes_ref[hg * heads + g]
        needed = jnp.maximum(needed, jnp.floor((reach - 1.0) / tk) + 1.0)
        bound = jnp.maximum(bound, qk)
    needed = jnp.clip(needed, 0.0, first_diag.astype(F32))
    n_blocks = jnp.max(needed.astype(jnp.int32))

    fixed_shift_ok = jnp.max(bound) * 2.0 < FIXED_SHIFT_LIMIT

    def fixed_shift_step(j):
        start = pl.multiple_of(j * tk, tk)
        delta = qi * tq - j * tk
        scores = [jnp.dot(jnp.concatenate([k_ref[pl.ds(start, tk), g * dh2:(g + 1) * dh2], k_extra],
                                          axis=1), qa_ref[g], preferred_element_type=F32)
                  for g in range(heads)]
        probs = [jnp.exp(s - (m_ref[g] + delta.astype(F32) * slopes_ref[hg * heads + g])).astype(BF16)
                 for g, s in enumerate(scores)]
        for g in range(heads):
            v_aug = jnp.concatenate([vt_ref[g * dh2:(g + 1) * dh2, pl.ds(start, tk)], ones], axis=0)
            acc_ref[g] += jnp.dot(v_aug, probs[g], preferred_element_type=F32)

    @pl.when(fixed_shift_ok)
    def _():
        def body(i, carry):
            fixed_shift_step(first_diag - 1 - i)
            return carry

        lax.fori_loop(0, n_blocks, body, 0)

    @pl.when(jnp.logical_not(fixed_shift_ok))
    def _():
        def body(i, carry):
            step(first_diag - 1 - i, False)
            return carry

        lax.fori_loop(0, n_blocks, body, 0)

    lam = lam_ref[...]
    lam_full = (jnp.exp(jnp.sum(lam[0:1] * lam[1:2], axis=1, keepdims=True))
                - jnp.exp(jnp.sum(lam[2:3] * lam[3:4], axis=1, keepdims=True)) + lambda_init)
    for g in range(heads):
        acc = acc_ref[g]
        on = acc[:dh2] / acc[dh2:dh2 + 1]
        o = on[:, :tq] - lam_full * on[:, tq:]
        o = o * lax.rsqrt(jnp.mean(o * o, axis=0, keepdims=True) + LN_EPS) * g_ref[...]
        o_ref[:, g * dh2:(g + 1) * dh2] = (o * (1.0 - lambda_init)).T.astype(o_ref.dtype)


def _diff_attention(qt, k, vt, lam, subln_g, *, batch, seq, lambda_init, tq=512, tk=256, heads=4):
    d, t = qt.shape
    dh2 = d // ATT_HEADS
    gw = heads * dh2
    nq = seq // tq
    assert tq % tk == 0 and tk <= BF16_EXACT_INT and tq < BF16_EXACT_INT * BF16_EXACT_INT
    slopes = 2.0 ** (-(8.0 / ATT_HEADS) * jnp.arange(1, ATT_HEADS + 1, dtype=F32))
    return pl.pallas_call(
        functools.partial(_attn_kernel, tq=tq, tk=tk, heads=heads, lambda_init=lambda_init),
        grid=(batch, ATT_HEADS // heads, nq),
        in_specs=[
            pl.BlockSpec(memory_space=pltpu.SMEM),
            pl.BlockSpec((gw, tq), lambda b, h, i: (h, b * nq + i)),
            pl.BlockSpec((seq, gw), lambda b, h, i: (b, h)),
            pl.BlockSpec((gw, seq), lambda b, h, i: (h, b)),
            pl.BlockSpec(lam.shape, lambda b, h, i: (0, 0)),
            pl.BlockSpec((dh2, 1), lambda b, h, i: (0, 0)),
        ],
        out_specs=pl.BlockSpec((tq, gw), lambda b, h, i: (b * nq + i, h)),
        out_shape=jax.ShapeDtypeStruct((t, d), BF16),
        scratch_shapes=[
            pltpu.VMEM((heads, 2 * dh2, 2 * tq), BF16),
            pltpu.VMEM((heads, 1, 2 * tq), F32),
            pltpu.VMEM((heads, dh2 + 2 * SUBLANES, 2 * tq), F32),
            pltpu.VMEM((heads, 1, LANES), F32),
        ],
        compiler_params=_params(("arbitrary", "arbitrary", "arbitrary")),
        name="diff_attention",
    )(slopes, qt, k, vt, lam, subln_g.reshape(dh2, 1))


def _oproj_kernel(o_ref, x_ref, mod_ref, modp_ref, w_ref, g_ref, b_ref, out_ref, hp_ref, *, alpha):
    tm, d = x_ref.shape
    _, _, gate = _split_mod(mod_ref[0], d)
    y = jnp.dot(o_ref[...], w_ref[...], preferred_element_type=F32)
    _emit_stage_output(alpha * x_ref[...] + gate * y, g_ref, b_ref, modp_ref, out_ref, hp_ref)


def _out_proj(o, x, mods, mod_row, peer_row, w_o, ln_g, ln_b, *, seq, alpha, tm=512):
    t, d = x.shape
    tiles_per_seq = seq // tm
    return pl.pallas_call(
        functools.partial(_oproj_kernel, alpha=alpha),
        grid=(t // tm,),
        in_specs=[
            pl.BlockSpec((tm, d), lambda i: (i, 0)),
            pl.BlockSpec((tm, d), lambda i: (i, 0)),
            pl.BlockSpec((1, 1, 3 * d), lambda i: (mod_row + i // tiles_per_seq, 0, 0)),
            pl.BlockSpec((1, 1, 3 * d), lambda i: (peer_row + i // tiles_per_seq, 0, 0)),
            _const_spec((d, d)),
            _const_spec((1, d)),
            _const_spec((1, d)),
        ],
        out_specs=[pl.BlockSpec((tm, d), lambda i: (i, 0))] * 2,
        out_shape=[jax.ShapeDtypeStruct((t, d), F32), jax.ShapeDtypeStruct((t, d), BF16)],
        compiler_params=_params(("arbitrary",)),
        name="attn_out_proj",
    )(o, x, mods, mods, w_o.astype(BF16), ln_g, ln_b)


def _top16_rows(s):
    n, tm = s.shape
    rows = lax.broadcasted_iota(jnp.int32, (n, tm), 0).astype(F32)
    vals, idxs = [], []
    tie = None
    for _ in range(PEER_TOPK):
        m = jnp.max(s, axis=0, keepdims=True)
        if tie is not None:
            m = m + tie
        idx = jnp.min(jnp.where(s == m, rows, float(n)), axis=0, keepdims=True)
        s = jnp.where(rows == idx, -jnp.inf, s)
        vals.append(m)
        idxs.append(idx)
        tie = yield
    return vals, idxs


def _candidate_layout(tm):
    slot = lax.broadcasted_iota(jnp.int32, (N_CAND, tm), 0)
    grp = slot // SUBLANES
    r = slot % SUBLANES
    a = jnp.where(grp <= 1, 0, jnp.where(grp == N_CAND_GROUPS - 1, SUBLANES + r, grp - 1))
    b = jnp.where(grp == 0, r, jnp.where(grp == 1, SUBLANES + r,
                                         jnp.where(grp == N_CAND_GROUPS - 1, 0, r)))
    valid = (a + 1) * (b + 1) <= PEER_TOPK
    flat = (a * PEER_TOPK + b).astype(F32)
    return valid, flat


def _candidates(rows0, rows1):
    lo1 = jnp.concatenate(rows1[:SUBLANES], axis=0)
    hi1 = jnp.concatenate(rows1[SUBLANES:], axis=0)
    hi0 = jnp.concatenate(rows0[SUBLANES:], axis=0)
    groups = [rows0[0] + lo1, rows0[0] + hi1]
    groups += [rows0[a] + lo1 for a in range(1, SUBLANES)]
    groups.append(hi0 + rows1[0])
    return jnp.concatenate(groups, axis=0)


def _route_head(q_head, keys_ref, valid, flat):
    scores = [lax.dot_general(keys_ref[p], q_head[:, p * PEER_KEYS:(p + 1) * PEER_KEYS],
                              (((1,), (1,)), ((), ())), preferred_element_type=F32)
              for p in range(2)]
    yield
    vals, idxs = [], []
    for st in scores:
        v, ix = yield from _top16_rows(st)
        vals.append(v)
        idxs.append(ix)
    cand = jnp.where(valid, _candidates(vals[0], vals[1]), -jnp.inf)
    code = _candidates([ix * float(PEER_KEYS) for ix in idxs[0]], idxs[1])
    top_s, top_code = [], []
    tie = None
    for _ in range(PEER_TOPK):
        m = jnp.max(cand, axis=0, keepdims=True)
        if tie is not None:
            m = m + tie
        fmin = jnp.min(jnp.where(cand == m, flat, float(PEER_TOPK * PEER_TOPK)),
                       axis=0, keepdims=True)
        hit = flat == fmin
        top_code.append(jnp.sum(jnp.where(hit, code, 0.0), axis=0, keepdims=True))
        cand = jnp.where(hit, -jnp.inf, cand)
        top_s.append(m)
        tie = yield
    ts = jnp.concatenate(top_s, axis=0)
    e = jnp.exp(ts - top_s[0])
    return (jnp.concatenate(top_code, axis=0).astype(jnp.int32),
            e / jnp.sum(e, axis=0, keepdims=True))


ROUTE_YIELDS = 3 * PEER_TOPK


def _zero_row(x, width):
    bits = lax.bitcast_convert_type(x[:1, :LANES], jnp.uint32)
    zero = lax.bitcast_convert_type((bits >> 16) >> 16, F32)
    return jnp.concatenate([zero] * (width // LANES), axis=1)


def _peer_kernel(h_ref, hn_ref, x_ref, mod_ref, wq_ref, keys_ref, ut_ref, v_ref, g_ref, b_ref,
                 o_ref, q_ref, code_ref, gate_ref, ci_ref, cg_ref, w_ref, acc_ref, hs_ref, us_ref,
                 *, alpha):
    i = pl.program_id(0)
    j = pl.program_id(1)
    nj = pl.num_programs(1)
    tm, d = x_ref.shape
    tn = ut_ref.shape[1]
    nhk = PEER_HEADS * PEER_TOPK
    parts = ROUTE_UNITS // PEER_HEADS
    tp = tm // parts
    slot = i % 2
    valid, flat = _candidate_layout(tp)

    def prepare_queries(h):
        q = jnp.dot(h, wq_ref[...], preferred_element_type=F32).astype(BF16)
        for hd in range(PEER_HEADS):
            q_ref[hd] = q[:, hd * 2 * PEER_KEYS:(hd + 1) * 2 * PEER_KEYS]

    def route_unit(u, dst):
        hd = u // parts
        part = u % parts
        rows = pl.ds(pl.multiple_of(part * tp, tp), tp)
        code, gate = yield from _route_head(q_ref[hd, rows, :], keys_ref, valid, flat)
        krows = pl.ds(pl.multiple_of(hd * PEER_TOPK, PEER_TOPK), PEER_TOPK)
        for g in range(tp // LANES):
            lanes = slice(g * LANES, (g + 1) * LANES)
            code_ref[dst, part * (tp // LANES) + g, krows, :] = code[:, lanes]
            gate_ref[dst, part * (tp // LANES) + g, krows, :] = gate[:, lanes]

    @pl.when((i == 0) & (j == 0))
    def _():
        prepare_queries(h_ref[...])

        def unit(u, carry):
            for _ in route_unit(u, 0):
                pass
            return carry

        lax.fori_loop(0, ROUTE_UNITS, unit, 0)

    @pl.when(j == 0)
    def _():
        for grp in range(tm // LANES):
            code = code_ref[slot, grp].T
            ci_ref[grp * LANES:(grp + 1) * LANES, :] = code // PEER_KEYS
            cg_ref[grp * LANES:(grp + 1) * LANES, :] = gate_ref[slot, grp].T
        key_rows = lax.broadcasted_iota(jnp.int32, (PEER_KEYS, nhk), 0).astype(BF16)
        key_lanes = lax.broadcasted_iota(jnp.int32, (nhk, PEER_KEYS), 1).astype(BF16)
        zero = jnp.zeros((PEER_KEYS, nhk), BF16)
        one = jnp.ones((nhk, PEER_KEYS), BF16)

        def token_group(grp, carry):
            base = pl.multiple_of(grp * LANES, LANES)
            second = (code_ref[slot, grp] % PEER_KEYS).astype(F32)

            def gate_bits(u):
                ri = ci_ref[pl.ds(base + u, 1), :].astype(BF16)
                rg = cg_ref[pl.ds(base + u, 1), :].astype(BF16)
                cj = jnp.broadcast_to(second[:, u:u + 1], (nhk, PEER_KEYS)).astype(BF16)
                pt = jnp.where(key_rows == ri, jnp.broadcast_to(rg, zero.shape), zero)
                qm = jnp.where(key_lanes == cj, one, jnp.zeros_like(one))
                wt = jnp.dot(pt, qm, preferred_element_type=F32)
                return lax.bitcast_convert_type(wt, jnp.uint32)

            for u in range(0, LANES, 2):
                packed = (gate_bits(u) >> 16) | (gate_bits(u + 1) & jnp.uint32(0xFFFF0000))
                pair = (base + u) // 2
                w_ref[pl.ds(pl.multiple_of(pair * W_PITCH, SUBLANES), PEER_KEYS), :] = packed
            return carry

        lax.fori_loop(0, tm // LANES, token_group, 0)
        acc_ref[...] = jnp.zeros(acc_ref.shape, F32)
        hs_ref[...] = h_ref[...]
        prepare_queries(hn_ref[...])

    route = route_unit(j, 1 - slot)
    next(route)
    chunk = 2 * PEER_KEYS
    n_chunks = tn // chunk

    def advance(after, count):
        try:
            route.send(None if after is None else _zero_row(after, tp))
            for _ in range(count - 1):
                next(route)
        except StopIteration:
            pass

    halves = 2
    th = tm // halves
    per_piece = -(-ROUTE_YIELDS // (2 * halves * n_chunks))

    def expert_inputs(half):
        rows = slice(half * th, (half + 1) * th)
        h = hs_ref[rows, :]
        pieces = []
        for r in range(n_chunks):
            c0 = r * chunk
            a = jnp.dot(h, us_ref[:, c0:c0 + chunk], preferred_element_type=F32)
            advance(a, per_piece)
            act = (a * (1.0 + lax.erf(a * (1.0 / math.sqrt(2.0))))).astype(BF16)
            for e in range(chunk // PEER_KEYS):
                first_key = (j * n_chunks + r) * (chunk // PEER_KEYS) + e
                words = w_ref[pl.ds(half * (th // 2) * W_PITCH + first_key, th // 2, stride=W_PITCH), :]
                gates = pltpu.bitcast(words, BF16)
                pieces.append(gates * act[:, e * PEER_KEYS:(e + 1) * PEER_KEYS])
        return jnp.concatenate(pieces, axis=1)

    def expert_outputs(half, z):
        rows = slice(half * th, (half + 1) * th)
        for c in range(d // chunk):
            cols = slice(c * chunk, (c + 1) * chunk)
            zv = jnp.dot(z, v_ref[:, cols], preferred_element_type=F32)
            acc_ref[rows, cols] += zv
            advance(zv, per_piece)

    us_ref[...] = ut_ref[...]
    advance(None, per_piece)
    z_prev = expert_inputs(0)
    for half in range(1, halves):
        z_next = expert_inputs(half)
        expert_outputs(half - 1, z_prev)
        z_prev = z_next
    expert_outputs(halves - 1, z_prev)
    for _ in route:
        pass

    @pl.when(j == nj - 1)
    def _():
        _, _, gate = _split_mod(mod_ref[0], d)
        o_ref[...] = _layer_norm(alpha * x_ref[...] + gate * acc_ref[...], g_ref[...], b_ref[...])


def _peer_layer(h, x, mods, mod_row, w_q, sub_keys, u_t, v, ln_g, ln_b, *, seq, alpha, tm=512):
    t, d = x.shape
    n = v.shape[0]
    tn = n // ROUTE_UNITS
    nhk = PEER_HEADS * PEER_TOPK
    parts = ROUTE_UNITS // PEER_HEADS
    tiles_per_seq = seq // tm
    last = t // tm - 1
    return pl.pallas_call(
        functools.partial(_peer_kernel, alpha=alpha),
        grid=(t // tm, n // tn),
        in_specs=[
            pl.BlockSpec((tm, d), lambda i, j: (i, 0)),
            pl.BlockSpec((tm, d), lambda i, j: (jnp.minimum(i + 1, last), 0)),
            pl.BlockSpec((tm, d), lambda i, j: (i, 0)),
            pl.BlockSpec((1, 1, 3 * d), lambda i, j: (mod_row + i // tiles_per_seq, 0, 0)),
            _const_spec(w_q.shape),
            _const_spec(sub_keys.shape),
            pl.BlockSpec((d, tn), lambda i, j: (0, j)),
            pl.BlockSpec((tn, d), lambda i, j: (j, 0)),
            _const_spec((1, d)),
            _const_spec((1, d)),
        ],
        out_specs=pl.BlockSpec((tm, d), lambda i, j: (i, 0)),
        out_shape=jax.ShapeDtypeStruct((t, d), F32),
        scratch_shapes=[
            pltpu.VMEM((PEER_HEADS, tm, 2 * PEER_KEYS), BF16),
            pltpu.VMEM((2, tm // LANES, nhk, LANES), jnp.int32),
            pltpu.VMEM((2, tm // LANES, nhk, LANES), F32),
            pltpu.VMEM((tm, nhk), jnp.int32),
            pltpu.VMEM((tm, nhk), F32),
            pltpu.VMEM((tm // 2 * W_PITCH, PEER_KEYS), jnp.uint32),
            pltpu.VMEM((tm, d), F32),
            pltpu.VMEM((tm, d), BF16),
            pltpu.VMEM((d, tn), BF16),
        ],
        compiler_params=_params(("arbitrary", "arbitrary")),
        name="peer_layer",
    )(h, h, x, mods, w_q.astype(BF16), sub_keys.astype(BF16), u_t, v, ln_g, ln_b)


def _lambda_init(layer_idx):
    return 0.8 - 0.6 * math.exp(-0.3 * layer_idx)


def kernel(x, c, ada_w, ada_b, ln_g, ln_b, conv_w_in, conv_w, conv_w_out, attn_w_qkv, attn_lambda,
           attn_subln_g, attn_w_o, peer_w_q, peer_sub_keys, peer_u, peer_v):
    batch, seq, d = x.shape
    depth = ada_w.shape[0]
    alpha = (2.0 * depth) ** 0.25
    head_dim = d // (2 * ATT_HEADS)
    mods = _ada_mods(c, ada_w, ada_b)
    xt = x.reshape(batch * seq, d)
    for i in range(depth):
        j = i // N_MIXERS
        row = (2 * i) * batch
        prow = (2 * i + 1) * batch
        g0, b0 = ln_g[i, 0].reshape(1, d), ln_b[i, 0].reshape(1, d)
        if i % N_MIXERS == 0:
            xt, h = _conv_layer(xt, mods, row, prow, conv_w_in[j], conv_w[j], conv_w_out[j], g0, b0,
                                seq=seq, alpha=alpha)
        else:
            qt, k, vt = _qkv_proj(xt, mods, row, attn_w_qkv[j], seq=seq, q_scale=head_dim ** -0.5)
            o = _diff_attention(qt, k, vt, attn_lambda[j], attn_subln_g[j], batch=batch, seq=seq,
                                lambda_init=_lambda_init(i))
            xt, h = _out_proj(o, xt, mods, row, prow, attn_w_o[j], g0, b0, seq=seq, alpha=alpha)
        g1, b1 = ln_g[i, 1].reshape(1, d), ln_b[i, 1].reshape(1, d)
        xt = _peer_layer(h, xt, mods, prow, peer_w_q[i], peer_sub_keys[i],
                         peer_u[i].T.astype(BF16), (0.5 * peer_v[i]).astype(BF16), g1, b1,
                         seq=seq, alpha=alpha)
    return xt.reshape(batch, seq, d)
```

```python
import functools
import math

import jax
import jax.numpy as jnp
from jax import lax
from jax.experimental import pallas as pl
from jax.experimental.pallas import tpu as pltpu

F32 = jnp.float32
BF16 = jnp.bfloat16

N_MIXERS = 2
ATT_HEADS = 8
PEER_HEADS = 8
PEER_KEYS = 128
PEER_TOPK = 16
LN_EPS = 1e-5

LANES = 128
SUBLANES = 8
VMEM_LIMIT_BYTES = 56 * 1024 * 1024
BF16_EXACT_INT = 256
EXP_UNDERFLOW = 106.0
NORM_SLACK = 1.02
FIXED_SHIFT_LIMIT = 60.0

N_CAND_GROUPS = 10
N_CAND = N_CAND_GROUPS * SUBLANES
W_PITCH = 136
ROUTE_UNITS = 16


def _layer_norm(r, g, b):
    mu = jnp.mean(r, axis=-1, keepdims=True)
    d = r - mu
    var = jnp.mean(d * d, axis=-1, keepdims=True)
    return d * lax.rsqrt(var + LN_EPS) * g + b


def _split_mod(mod, d):
    return mod[:, :d], 1.0 + mod[:, d:2 * d], 1.0 + mod[:, 2 * d:]


def _params(sem):
    return pltpu.CompilerParams(dimension_semantics=sem, vmem_limit_bytes=VMEM_LIMIT_BYTES)


def _const_spec(shape):
    nd = len(shape)
    return pl.BlockSpec(shape, lambda *_: (0,) * nd, pipeline_mode=pl.Buffered(1))


def _ada_kernel(c_ref, w_ref, b_ref, o_ref):
    c = c_ref[...]
    sc = c / (1.0 + jnp.exp(-c))
    o_ref[0] = jnp.dot(sc, w_ref[0], precision=lax.Precision.HIGHEST,
                       preferred_element_type=F32) + b_ref[0]


def _ada_mods(c, ada_w, ada_b):
    depth, _, d, d3 = ada_w.shape
    nb = c.shape[0]
    nmat = depth * 2
    tn = 1024
    w = ada_w.reshape(nmat, d, d3)
    b = ada_b.reshape(nmat, 1, d3)
    out = pl.pallas_call(
        _ada_kernel,
        grid=(nmat, d3 // tn),
        in_specs=[
            pl.BlockSpec((nb, d), lambda m, n: (0, 0)),
            pl.BlockSpec((1, d, tn), lambda m, n: (m, 0, n)),
            pl.BlockSpec((1, 1, tn), lambda m, n: (m, 0, n)),
        ],
        out_specs=pl.BlockSpec((1, nb, tn), lambda m, n: (m, 0, n)),
        out_shape=jax.ShapeDtypeStruct((nmat, nb, d3), F32),
        compiler_params=_params(("arbitrary", "arbitrary")),
        name="ada_mods",
    )(c, w, b)
    return out.reshape(nmat * nb, 1, d3)


def _emit_stage_output(r, g_ref, b_ref, modp_ref, o_ref, hp_ref):
    xn = _layer_norm(r, g_ref[...], b_ref[...])
    o_ref[...] = xn
    shift, scale, _ = _split_mod(modp_ref[0], xn.shape[1])
    hp_ref[...] = (xn * scale + shift).astype(BF16)


def _conv_kernel(x_ref, mod_ref, modp_ref, win_ref, cw_ref, wout_ref, g_ref, b_ref, o_ref, hp_ref,
                 zbuf, *, tiles_per_seq, alpha):
    i = pl.program_id(0)
    tm, d = x_ref.shape
    x = x_ref[...]
    shift, scale, gate = _split_mod(mod_ref[0], d)
    h = (x * scale + shift).astype(BF16)
    hw = jnp.dot(h, win_ref[...], preferred_element_type=F32)
    gb = hw[:, :d]
    z = hw[:, d:2 * d] * hw[:, 2 * d:]

    @pl.when(i % tiles_per_seq == 0)
    def _():
        zbuf[0:SUBLANES, :] = jnp.zeros((SUBLANES, d), F32)

    zbuf[SUBLANES:, :] = z
    z1 = zbuf[pl.ds(SUBLANES - 1, tm), :]
    z2 = zbuf[pl.ds(SUBLANES - 2, tm), :]
    cw = cw_ref[...]
    zc = cw[2:3] * z + cw[1:2] * z1 + cw[0:1] * z2
    zbuf[0:SUBLANES, :] = zbuf[tm:tm + SUBLANES, :]
    y = jnp.dot((gb * zc).astype(BF16), wout_ref[...], preferred_element_type=F32)
    _emit_stage_output(alpha * x + gate * y, g_ref, b_ref, modp_ref, o_ref, hp_ref)


def _conv_layer(x, mods, mod_row, peer_row, w_in, conv_w, w_out, ln_g, ln_b, *, seq, alpha, tm=512):
    t, d = x.shape
    tiles_per_seq = seq // tm
    return pl.pallas_call(
        functools.partial(_conv_kernel, tiles_per_seq=tiles_per_seq, alpha=alpha),
        grid=(t // tm,),
        in_specs=[
            pl.BlockSpec((tm, d), lambda i: (i, 0)),
            pl.BlockSpec((1, 1, 3 * d), lambda i: (mod_row + i // tiles_per_seq, 0, 0)),
            pl.BlockSpec((1, 1, 3 * d), lambda i: (peer_row + i // tiles_per_seq, 0, 0)),
            _const_spec((d, 3 * d)),
            _const_spec((conv_w.shape[0], d)),
            _const_spec((d, d)),
            _const_spec((1, d)),
            _const_spec((1, d)),
        ],
        out_specs=[pl.BlockSpec((tm, d), lambda i: (i, 0))] * 2,
        out_shape=[jax.ShapeDtypeStruct((t, d), F32), jax.ShapeDtypeStruct((t, d), BF16)],
        scratch_shapes=[pltpu.VMEM((tm + SUBLANES, d), F32)],
        compiler_params=_params(("arbitrary",)),
        name="conv_mixer",
    )(x, mods, mods, w_in.astype(BF16), conv_w, w_out.astype(BF16), ln_g, ln_b)


def _qkv_kernel(x_ref, mod_ref, w_ref, qt_ref, k_ref, vt_ref, *, q_scale):
    tm, d = x_ref.shape
    shift, scale, _ = _split_mod(mod_ref[0], d)
    h = (x_ref[...] * scale + shift).astype(BF16)
    qkv = jnp.dot(h, w_ref[...], preferred_element_type=F32)
    qt_ref[...] = (qkv[:, :d] * q_scale).T.astype(BF16)
    k_ref[...] = qkv[:, d:2 * d].astype(BF16)
    vt_ref[...] = qkv[:, 2 * d:].T.astype(BF16)


def _qkv_proj(x, mods, mod_row, w_qkv, *, seq, q_scale, tm=512):
    t, d = x.shape
    tiles_per_seq = seq // tm
    return pl.pallas_call(
        functools.partial(_qkv_kernel, q_scale=q_scale),
        grid=(t // tm,),
        in_specs=[
            pl.BlockSpec((tm, d), lambda i: (i, 0)),
            pl.BlockSpec((1, 1, 3 * d), lambda i: (mod_row + i // tiles_per_seq, 0, 0)),
            _const_spec((d, 3 * d)),
        ],
        out_specs=[
            pl.BlockSpec((d, tm), lambda i: (0, i)),
            pl.BlockSpec((tm, d), lambda i: (i, 0)),
            pl.BlockSpec((d, tm), lambda i: (0, i)),
        ],
        out_shape=[
            jax.ShapeDtypeStruct((d, t), BF16),
            jax.ShapeDtypeStruct((t, d), BF16),
            jax.ShapeDtypeStruct((d, t), BF16),
        ],
        compiler_params=_params(("arbitrary",)),
        name="attn_qkv",
    )(x, mods, w_qkv.astype(BF16))


def _attn_kernel(slopes_ref, qt_ref, k_ref, vt_ref, lam_ref, g_ref, o_ref, qa_ref, m_ref, acc_ref,
                 kn_ref, *, tq, tk, heads, lambda_init):
    hg = pl.program_id(1)
    qi = pl.program_id(2)
    dh2 = qt_ref.shape[0] // heads
    dh = dh2 // 2
    ones_rows = 2 * SUBLANES
    blocks_per_tile = tq // tk

    feat = lax.broadcasted_iota(jnp.int32, (dh2, tq), 0)
    arow = lax.broadcasted_iota(jnp.int32, (dh2, 2 * tq), 0)
    acol = lax.broadcasted_iota(jnp.int32, (dh2, 2 * tq), 1)
    r = jnp.where(acol >= tq, acol - tq, acol)
    r_lo = (r % BF16_EXACT_INT).astype(F32)
    r_hi = (r - r % BF16_EXACT_INT).astype(F32)
    for g in range(heads):
        slope = slopes_ref[hg * heads + g]
        qt = qt_ref[g * dh2:(g + 1) * dh2, :]
        zero = jnp.zeros_like(qt)
        qa_ref[g, :dh2, :] = jnp.concatenate(
            [jnp.where(feat < dh, qt, zero), jnp.where(feat >= dh, qt, zero)], axis=1)
        qa_ref[g, dh2:, :] = jnp.where(
            arow == 0, -slope * r_lo,
            jnp.where(arow == 1, -slope * r_hi, jnp.where(arow == 2, slope, 0.0))).astype(BF16)
    kcol = lax.broadcasted_iota(jnp.int32, (tk, dh2), 1)
    krow = lax.broadcasted_iota(jnp.int32, (tk, dh2), 0).astype(F32)
    k_extra = jnp.where(kcol <= 1, 1.0, jnp.where(kcol == 2, krow, 0.0)).astype(BF16)
    ones = jnp.ones((ones_rows, tk), BF16)

    m_ref[...] = jnp.full(m_ref.shape, -jnp.inf, F32)
    acc_ref[...] = jnp.zeros(acc_ref.shape, F32)

    @pl.when(qi == 0)
    def _():
        ones_sq = jnp.ones((dh2, LANES), BF16)
        for g in range(heads):
            def chunk(c, best, g=g):
                kc = k_ref[pl.ds(pl.multiple_of(c * tq, tq), tq), g * dh2:(g + 1) * dh2].astype(F32)
                rows = jnp.dot((kc * kc).astype(BF16), ones_sq, preferred_element_type=F32)
                return jnp.maximum(best, jnp.max(rows, axis=0, keepdims=True))
            kn_ref[g] = lax.fori_loop(0, k_ref.shape[0] // tq, chunk, jnp.zeros((1, LANES), F32))

    def step(j, diagonal):
        start = pl.multiple_of(j * tk, tk)
        delta = qi * tq - j * tk
        head_cols = [slice(g * dh2, (g + 1) * dh2) for g in range(heads)]
        scores, maxima = [], []
        for g, cols in enumerate(head_cols):
            s = jnp.dot(jnp.concatenate([k_ref[pl.ds(start, tk), cols], k_extra], axis=1),
                        qa_ref[g], preferred_element_type=F32)
            if diagonal:
                mrow = lax.broadcasted_iota(jnp.int32, (tk, 2 * tq), 0)
                mcol = lax.broadcasted_iota(jnp.int32, (tk, 2 * tq), 1)
                mq = jnp.where(mcol >= tq, mcol - tq, mcol)
                s = jnp.where(mrow - mq <= delta, s, -jnp.inf)
            scores.append(s)
            maxima.append(jnp.max(s, axis=0, keepdims=True))
        probs, corrs = [], []
        for g, s in enumerate(scores):
            off = delta.astype(F32) * slopes_ref[hg * heads + g]
            m_old = m_ref[g]
            m_new = jnp.maximum(m_old, maxima[g] - off)
            probs.append(jnp.exp(s - (m_new + off)).astype(BF16))
            corrs.append(jnp.exp(m_old - m_new))
            m_ref[g] = m_new
        for g, cols in enumerate(head_cols):
            v_aug = jnp.concatenate([vt_ref[cols, pl.ds(start, tk)], ones], axis=0)
            acc_ref[g] = corrs[g] * acc_ref[g] + jnp.dot(v_aug, probs[g],
                                                         preferred_element_type=F32)

    first_diag = qi * blocks_per_tile
    for d in range(blocks_per_tile):
        step(first_diag + d, True)

    needed = jnp.zeros((1, 2 * tq), F32)
    bound = jnp.zeros((1, 2 * tq), F32)
    for g in range(heads):
        qf = qa_ref[g, :dh2, :].astype(F32)
        q_norm = jnp.sqrt(jnp.sum(qf * qf, axis=0, keepdims=True))
        k_norm = jnp.sqrt(kn_ref[g][:, :1])
        qk = q_norm * k_norm * NORM_SLACK
        reach = (qk - m_ref[g] + EXP_UNDERFLOW) / slopes_ref[hg * heads + g]
        needed = jnp.maximum(needed, jnp.floor((reach - 1.0) / tk) + 1.0)
        bound = jnp.maximum(bound, qk)
    needed = jnp.clip(needed, 0.0, first_diag.astype(F32))
    n_blocks = jnp.max(needed.astype(jnp.int32))

    fixed_shift_ok = jnp.max(bound) * 2.0 < FIXED_SHIFT_LIMIT

    def fixed_shift_step(j):
        start = pl.multiple_of(j * tk, tk)
        delta = qi * tq - j * tk
        scores = [jnp.dot(jnp.concatenate([k_ref[pl.ds(start, tk), g * dh2:(g + 1) * dh2], k_extra],
                                          axis=1), qa_ref[g], preferred_element_type=F32)
                  for g in range(heads)]
        probs = [jnp.exp(s - (m_ref[g] + delta.astype(F32) * slopes_ref[hg * heads + g])).astype(BF16)
                 for g, s in enumerate(scores)]
        for g in range(heads):
            v_aug = jnp.concatenate([vt_ref[g * dh2:(g + 1) * dh2, pl.ds(start, tk)], ones], axis=0)
            acc_ref[g] += jnp.dot(v_aug, probs[g], preferred_element_type=F32)

    @pl.when(fixed_shift_ok)
    def _():
        def body(i, carry):
            fixed_shift_step(first_diag - 1 - i)
            return carry

        lax.fori_loop(0, n_blocks, body, 0)

    @pl.when(jnp.logical_not(fixed_shift_ok))
    def _():
        def body(i, carry):
            step(first_diag - 1 - i, False)
            return carry

        lax.fori_loop(0, n_blocks, body, 0)

    lam = lam_ref[...]
    lam_full = (jnp.exp(jnp.sum(lam[0:1] * lam[1:2], axis=1, keepdims=True))
                - jnp.exp(jnp.sum(lam[2:3] * lam[3:4], axis=1, keepdims=True)) + lambda_init)
    for g in range(heads):
        acc = acc_ref[g]
        on = acc[:dh2] / acc[dh2:dh2 + 1]
        o = on[:, :tq] - lam_full * on[:, tq:]
        o = o * lax.rsqrt(jnp.mean(o * o, axis=0, keepdims=True) + LN_EPS) * g_ref[...]
        o_ref[:, g * dh2:(g + 1) * dh2] = (o * (1.0 - lambda_init)).T.astype(o_ref.dtype)


def _diff_attention(qt, k, vt, lam, subln_g, *, batch, seq, lambda_init, tq=512, tk=256, heads=4):
    d, t = qt.shape
    dh2 = d // ATT_HEADS
    gw = heads * dh2
    nq = seq // tq
    assert tq % tk == 0 and tk <= BF16_EXACT_INT and tq < BF16_EXACT_INT * BF16_EXACT_INT
    slopes = 2.0 ** (-(8.0 / ATT_HEADS) * jnp.arange(1, ATT_HEADS + 1, dtype=F32))
    return pl.pallas_call(
        functools.partial(_attn_kernel, tq=tq, tk=tk, heads=heads, lambda_init=lambda_init),
        grid=(batch, ATT_HEADS // heads, nq),
        in_specs=[
            pl.BlockSpec(memory_space=pltpu.SMEM),
            pl.BlockSpec((gw, tq), lambda b, h, i: (h, b * nq + i)),
            pl.BlockSpec((seq, gw), lambda b, h, i: (b, h)),
            pl.BlockSpec((gw, seq), lambda b, h, i: (h, b)),
            pl.BlockSpec(lam.shape, lambda b, h, i: (0, 0)),
            pl.BlockSpec((dh2, 1), lambda b, h, i: (0, 0)),
        ],
        out_specs=pl.BlockSpec((tq, gw), lambda b, h, i: (b * nq + i, h)),
        out_shape=jax.ShapeDtypeStruct((t, d), BF16),
        scratch_shapes=[
            pltpu.VMEM((heads, 2 * dh2, 2 * tq), BF16),
            pltpu.VMEM((heads, 1, 2 * tq), F32),
            pltpu.VMEM((heads, dh2 + 2 * SUBLANES, 2 * tq), F32),
            pltpu.VMEM((heads, 1, LANES), F32),
        ],
        compiler_params=_params(("arbitrary", "arbitrary", "arbitrary")),
        name="diff_attention",
    )(slopes, qt, k, vt, lam, subln_g.reshape(dh2, 1))


def _oproj_kernel(o_ref, x_ref, mod_ref, modp_ref, w_ref, g_ref, b_ref, out_ref, hp_ref, *, alpha):
    tm, d = x_ref.shape
    _, _, gate = _split_mod(mod_ref[0], d)
    y = jnp.dot(o_ref[...], w_ref[...], preferred_element_type=F32)
    _emit_stage_output(alpha * x_ref[...] + gate * y, g_ref, b_ref, modp_ref, out_ref, hp_ref)


def _out_proj(o, x, mods, mod_row, peer_row, w_o, ln_g, ln_b, *, seq, alpha, tm=512):
    t, d = x.shape
    tiles_per_seq = seq // tm
    return pl.pallas_call(
        functools.partial(_oproj_kernel, alpha=alpha),
        grid=(t // tm,),
        in_specs=[
            pl.BlockSpec((tm, d), lambda i: (i, 0)),
            pl.BlockSpec((tm, d), lambda i: (i, 0)),
            pl.BlockSpec((1, 1, 3 * d), lambda i: (mod_row + i // tiles_per_seq, 0, 0)),
            pl.BlockSpec((1, 1, 3 * d), lambda i: (peer_row + i // tiles_per_seq, 0, 0)),
            _const_spec((d, d)),
            _const_spec((1, d)),
            _const_spec((1, d)),
        ],
        out_specs=[pl.BlockSpec((tm, d), lambda i: (i, 0))] * 2,
        out_shape=[jax.ShapeDtypeStruct((t, d), F32), jax.ShapeDtypeStruct((t, d), BF16)],
        compiler_params=_params(("arbitrary",)),
        name="attn_out_proj",
    )(o, x, mods, mods, w_o.astype(BF16), ln_g, ln_b)


def _top16_rows(s):
    n, tm = s.shape
    rows = lax.broadcasted_iota(jnp.int32, (n, tm), 0).astype(F32)
    vals, idxs = [], []
    tie = None
    for _ in range(PEER_TOPK):
        m = jnp.max(s, axis=0, keepdims=True)
        if tie is not None:
            m = m + tie
        idx = jnp.min(jnp.where(s == m, rows, float(n)), axis=0, keepdims=True)
        s = jnp.where(rows == idx, -jnp.inf, s)
        vals.append(m)
        idxs.append(idx)
        tie = yield
    return vals, idxs


def _candidate_layout(tm):
    slot = lax.broadcasted_iota(jnp.int32, (N_CAND, tm), 0)
    grp = slot // SUBLANES
    r = slot % SUBLANES
    a = jnp.where(grp <= 1, 0, jnp.where(grp == N_CAND_GROUPS - 1, SUBLANES + r, grp - 1))
    b = jnp.where(grp == 0, r, jnp.where(grp == 1, SUBLANES + r,
                                         jnp.where(grp == N_CAND_GROUPS - 1, 0, r)))
    valid = (a + 1) * (b + 1) <= PEER_TOPK
    flat = (a * PEER_TOPK + b).astype(F32)
    return valid, flat


def _candidates(rows0, rows1):
    lo1 = jnp.concatenate(rows1[:SUBLANES], axis=0)
    hi1 = jnp.concatenate(rows1[SUBLANES:], axis=0)
    hi0 = jnp.concatenate(rows0[SUBLANES:], axis=0)
    groups = [rows0[0] + lo1, rows0[0] + hi1]
    groups += [rows0[a] + lo1 for a in range(1, SUBLANES)]
    groups.append(hi0 + rows1[0])
    return jnp.concatenate(groups, axis=0)


def _route_head(q_head, keys_ref, valid, flat):
    scores = [lax.dot_general(keys_ref[p], q_head[:, p * PEER_KEYS:(p + 1) * PEER_KEYS],
                              (((1,), (1,)), ((), ())), preferred_element_type=F32)
              for p in range(2)]
    yield
    vals, idxs = [], []
    for st in scores:
        v, ix = yield from _top16_rows(st)
        vals.append(v)
        idxs.append(ix)
    cand = jnp.where(valid, _candidates(vals[0], vals[1]), -jnp.inf)
    code = _candidates([ix * float(PEER_KEYS) for ix in idxs[0]], idxs[1])
    top_s, top_code = [], []
    tie = None
    for _ in range(PEER_TOPK):
        m = jnp.max(cand, axis=0, keepdims=True)
        if tie is not None:
            m = m + tie
        fmin = jnp.min(jnp.where(cand == m, flat, float(PEER_TOPK * PEER_TOPK)),
                       axis=0, keepdims=True)
        hit = flat == fmin
        top_code.append(jnp.sum(jnp.where(hit, code, 0.0), axis=0, keepdims=True))
        cand = jnp.where(hit, -jnp.inf, cand)
        top_s.append(m)
        tie = yield
    ts = jnp.concatenate(top_s, axis=0)
    e = jnp.exp(ts - top_s[0])
    return (jnp.concatenate(top_code, axis=0).astype(jnp.int32),
            e / jnp.sum(e, axis=0, keepdims=True))


ROUTE_YIELDS = 3 * PEER_TOPK


def _zero_row(x, width):
    bits = lax.bitcast_convert_type(x[:1, :LANES], jnp.uint32)
    zero = lax.bitcast_convert_type((bits >> 16) >> 16, F32)
    return jnp.concatenate([zero] * (width // LANES), axis=1)


def _peer_kernel(h_ref, hn_ref, x_ref, mod_ref, wq_ref, keys_ref, ut_ref, v_ref, g_ref, b_ref,
                 o_ref, q_ref, code_ref, gate_ref, ci_ref, cg_ref, w_ref, acc_ref, hs_ref, us_ref,
                 *, alpha):
    i = pl.program_id(0)
    j = pl.program_id(1)
    nj = pl.num_programs(1)
    tm, d = x_ref.shape
    tn = ut_ref.shape[1]
    nhk = PEER_HEADS * PEER_TOPK
    parts = ROUTE_UNITS // PEER_HEADS
    tp = tm // parts
    slot = i % 2
    valid, flat = _candidate_layout(tp)

    def prepare_queries(h):
        q = jnp.dot(h, wq_ref[...], preferred_element_type=F32).astype(BF16)
        for hd in range(PEER_HEADS):
            q_ref[hd] = q[:, hd * 2 * PEER_KEYS:(hd + 1) * 2 * PEER_KEYS]

    def route_unit(u, dst):
        hd = u // parts
        part = u % parts
        rows = pl.ds(pl.multiple_of(part * tp, tp), tp)
        code, gate = yield from _route_head(q_ref[hd, rows, :], keys_ref, valid, flat)
        krows = pl.ds(pl.multiple_of(hd * PEER_TOPK, PEER_TOPK), PEER_TOPK)
        for g in range(tp // LANES):
            lanes = slice(g * LANES, (g + 1) * LANES)
            code_ref[dst, part * (tp // LANES) + g, krows, :] = code[:, lanes]
            gate_ref[dst, part * (tp // LANES) + g, krows, :] = gate[:, lanes]

    @pl.when((i == 0) & (j == 0))
    def _():
        prepare_queries(h_ref[...])

        def unit(u, carry):
            for _ in route_unit(u, 0):
                pass
            return carry

        lax.fori_loop(0, ROUTE_UNITS, unit, 0)

    @pl.when(j == 0)
    def _():
        for grp in range(tm // LANES):
            code = code_ref[slot, grp].T
            ci_ref[grp * LANES:(grp + 1) * LANES, :] = code // PEER_KEYS
            cg_ref[grp * LANES:(grp + 1) * LANES, :] = gate_ref[slot, grp].T
        key_rows = lax.broadcasted_iota(jnp.int32, (PEER_KEYS, nhk), 0).astype(BF16)
        key_lanes = lax.broadcasted_iota(jnp.int32, (nhk, PEER_KEYS), 1).astype(BF16)
        zero = jnp.zeros((PEER_KEYS, nhk), BF16)
        one = jnp.ones((nhk, PEER_KEYS), BF16)

        def token_group(grp, carry):
            base = pl.multiple_of(grp * LANES, LANES)
            second = (code_ref[slot, grp] % PEER_KEYS).astype(F32)

            def gate_bits(u):
                ri = ci_ref[pl.ds(base + u, 1), :].astype(BF16)
                rg = cg_ref[pl.ds(base + u, 1), :].astype(BF16)
                cj = jnp.broadcast_to(second[:, u:u + 1], (nhk, PEER_KEYS)).astype(BF16)
                pt = jnp.where(key_rows == ri, jnp.broadcast_to(rg, zero.shape), zero)
                qm = jnp.where(key_lanes == cj, one, jnp.zeros_like(one))
                wt = jnp.dot(pt, qm, preferred_element_type=F32)
                return lax.bitcast_convert_type(wt, jnp.uint32)

            for u in range(0, LANES, 2):
                packed = (gate_bits(u) >> 16) | (gate_bits(u + 1) & jnp.uint32(0xFFFF0000))
                pair = (base + u) // 2
                w_ref[pl.ds(pl.multiple_of(pair * W_PITCH, SUBLANES), PEER_KEYS), :] = packed
            return carry

        lax.fori_loop(0, tm // LANES, token_group, 0)
        acc_ref[...] = jnp.zeros(acc_ref.shape, F32)
        hs_ref[...] = h_ref[...]
        prepare_queries(hn_ref[...])

    route = route_unit(j, 1 - slot)
    next(route)
    chunk = 2 * PEER_KEYS
    n_chunks = tn // chunk

    def advance(after, count):
        try:
            route.send(None if after is None else _zero_row(after, tp))
            for _ in range(count - 1):
                next(route)
        except StopIteration:
            pass

    halves = 2
    th = tm // halves
    per_piece = -(-ROUTE_YIELDS // (2 * halves * n_chunks))

    def expert_inputs(half):
        rows = slice(half * th, (half + 1) * th)
        h = hs_ref[rows, :]
        pieces = []
        for r in range(n_chunks):
            c0 = r * chunk
            a = jnp.dot(h, us_ref[:, c0:c0 + chunk], preferred_element_type=F32)
            advance(a, per_piece)
            act = (a * (1.0 + lax.erf(a * (1.0 / math.sqrt(2.0))))).astype(BF16)
            for e in range(chunk // PEER_KEYS):
                first_key = (j * n_chunks + r) * (chunk // PEER_KEYS) + e
                words = w_ref[pl.ds(half * (th // 2) * W_PITCH + first_key, th // 2, stride=W_PITCH), :]
                gates = pltpu.bitcast(words, BF16)
                pieces.append(gates * act[:, e * PEER_KEYS:(e + 1) * PEER_KEYS])
        return jnp.concatenate(pieces, axis=1)

    def expert_outputs(half, z):
        rows = slice(half * th, (half + 1) * th)
        for c in range(d // chunk):
            cols = slice(c * chunk, (c + 1) * chunk)
            zv = jnp.dot(z, vs[:, cols], preferred_element_type=F32)
            acc_ref[rows, cols] += zv
            advance(zv, per_piece)

    us_ref[...] = pltpu.bitcast(ut_ref[...], BF16)
    vs = pltpu.bitcast(v_ref[...], BF16)
    advance(None, per_piece)
    z_prev = expert_inputs(0)
    for half in range(1, halves):
        z_next = expert_inputs(half)
        expert_outputs(half - 1, z_prev)
        z_prev = z_next
    expert_outputs(halves - 1, z_prev)
    for _ in route:
        pass

    @pl.when(j == nj - 1)
    def _():
        _, _, gate = _split_mod(mod_ref[0], d)
        o_ref[...] = _layer_norm(alpha * x_ref[...] + gate * acc_ref[...], g_ref[...], b_ref[...])


def _peer_layer(h, x, mods, mod_row, w_q, sub_keys, u_t, v, ln_g, ln_b, *, seq, alpha, tm=512):
    t, d = x.shape
    n = u_t.shape[1]
    tn = n // ROUTE_UNITS
    nhk = PEER_HEADS * PEER_TOPK
    parts = ROUTE_UNITS // PEER_HEADS
    tiles_per_seq = seq // tm
    last = t // tm - 1
    return pl.pallas_call(
        functools.partial(_peer_kernel, alpha=alpha),
        grid=(t // tm, n // tn),
        in_specs=[
            pl.BlockSpec((tm, d), lambda i, j: (i, 0)),
            pl.BlockSpec((tm, d), lambda i, j: (jnp.minimum(i + 1, last), 0)),
            pl.BlockSpec((tm, d), lambda i, j: (i, 0)),
            pl.BlockSpec((1, 1, 3 * d), lambda i, j: (mod_row + i // tiles_per_seq, 0, 0)),
            _const_spec(w_q.shape),
            _const_spec(sub_keys.shape),
            pl.BlockSpec((d // 2, tn), lambda i, j: (0, j)),
            pl.BlockSpec((tn // 2, d), lambda i, j: (j, 0)),
            _const_spec((1, d)),
            _const_spec((1, d)),
        ],
        out_specs=pl.BlockSpec((tm, d), lambda i, j: (i, 0)),
        out_shape=jax.ShapeDtypeStruct((t, d), F32),
        scratch_shapes=[
            pltpu.VMEM((PEER_HEADS, tm, 2 * PEER_KEYS), BF16),
            pltpu.VMEM((2, tm // LANES, nhk, LANES), jnp.int32),
            pltpu.VMEM((2, tm // LANES, nhk, LANES), F32),
            pltpu.VMEM((tm, nhk), jnp.int32),
            pltpu.VMEM((tm, nhk), F32),
            pltpu.VMEM((tm // 2 * W_PITCH, PEER_KEYS), jnp.uint32),
            pltpu.VMEM((tm, d), F32),
            pltpu.VMEM((tm, d), BF16),
            pltpu.VMEM((d, tn), BF16),
        ],
        compiler_params=_params(("arbitrary", "arbitrary")),
        name="peer_layer",
    )(h, h, x, mods, w_q.astype(BF16), sub_keys.astype(BF16), u_t, v, ln_g, ln_b)


def _packed_rows(w):
    return lax.bitcast_convert_type(jnp.stack([w[0::2], w[1::2]], axis=-1), jnp.uint32)


def _lambda_init(layer_idx):
    return 0.8 - 0.6 * math.exp(-0.3 * layer_idx)


def kernel(x, c, ada_w, ada_b, ln_g, ln_b, conv_w_in, conv_w, conv_w_out, attn_w_qkv, attn_lambda,
           attn_subln_g, attn_w_o, peer_w_q, peer_sub_keys, peer_u, peer_v):
    batch, seq, d = x.shape
    depth = ada_w.shape[0]
    alpha = (2.0 * depth) ** 0.25
    head_dim = d // (2 * ATT_HEADS)
    mods = _ada_mods(c, ada_w, ada_b)
    xt = x.reshape(batch * seq, d)
    for i in range(depth):
        j = i // N_MIXERS
        row = (2 * i) * batch
        prow = (2 * i + 1) * batch
        g0, b0 = ln_g[i, 0].reshape(1, d), ln_b[i, 0].reshape(1, d)
        if i % N_MIXERS == 0:
            xt, h = _conv_layer(xt, mods, row, prow, conv_w_in[j], conv_w[j], conv_w_out[j], g0, b0,
                                seq=seq, alpha=alpha)
        else:
            qt, k, vt = _qkv_proj(xt, mods, row, attn_w_qkv[j], seq=seq, q_scale=head_dim ** -0.5)
            o = _diff_attention(qt, k, vt, attn_lambda[j], attn_subln_g[j], batch=batch, seq=seq,
                                lambda_init=_lambda_init(i))
            xt, h = _out_proj(o, xt, mods, row, prow, attn_w_o[j], g0, b0, seq=seq, alpha=alpha)
        g1, b1 = ln_g[i, 1].reshape(1, d), ln_b[i, 1].reshape(1, d)
        xt = _peer_layer(h, xt, mods, prow, peer_w_q[i], peer_sub_keys[i],
                         _packed_rows(peer_u[i].T.astype(BF16)),
                         _packed_rows((0.5 * peer_v[i]).astype(BF16)), g1, b1,
                         seq=seq, alpha=alpha)
    return xt.reshape(batch, seq, d)
```

```python
import functools
import math

import jax
import jax.numpy as jnp
from jax import lax
from jax.experimental import pallas as pl
from jax.experimental.pallas import tpu as pltpu

F32 = jnp.float32
BF16 = jnp.bfloat16

N_MIXERS = 2
ATT_HEADS = 8
PEER_HEADS = 8
PEER_KEYS = 128
PEER_TOPK = 16
LN_EPS = 1e-5

LANES = 128
SUBLANES = 8
VMEM_LIMIT_BYTES = 56 * 1024 * 1024
BF16_EXACT_INT = 256
EXP_UNDERFLOW = 106.0
NORM_SLACK = 1.02
FIXED_SHIFT_LIMIT = 60.0

N_CAND_GROUPS = 10
N_CAND = N_CAND_GROUPS * SUBLANES
W_PITCH = 136
ROUTE_UNITS = 16


def _layer_norm(r, g, b):
    mu = jnp.mean(r, axis=-1, keepdims=True)
    d = r - mu
    var = jnp.mean(d * d, axis=-1, keepdims=True)
    return d * lax.rsqrt(var + LN_EPS) * g + b


def _split_mod(mod, d):
    return mod[:, :d], 1.0 + mod[:, d:2 * d], 1.0 + mod[:, 2 * d:]


def _params(sem):
    return pltpu.CompilerParams(dimension_semantics=sem, vmem_limit_bytes=VMEM_LIMIT_BYTES)


def _const_spec(shape):
    nd = len(shape)
    return pl.BlockSpec(shape, lambda *_: (0,) * nd, pipeline_mode=pl.Buffered(1))


def _ada_kernel(c_ref, w_ref, b_ref, o_ref):
    c = c_ref[...]
    sc = c / (1.0 + jnp.exp(-c))
    o_ref[0] = jnp.dot(sc, w_ref[0], precision=lax.Precision.HIGHEST,
                       preferred_element_type=F32) + b_ref[0]


def _ada_mods(c, ada_w, ada_b):
    depth, _, d, d3 = ada_w.shape
    nb = c.shape[0]
    nmat = depth * 2
    tn = 1024
    w = ada_w.reshape(nmat, d, d3)
    b = ada_b.reshape(nmat, 1, d3)
    out = pl.pallas_call(
        _ada_kernel,
        grid=(nmat, d3 // tn),
        in_specs=[
            pl.BlockSpec((nb, d), lambda m, n: (0, 0)),
            pl.BlockSpec((1, d, tn), lambda m, n: (m, 0, n)),
            pl.BlockSpec((1, 1, tn), lambda m, n: (m, 0, n)),
        ],
        out_specs=pl.BlockSpec((1, nb, tn), lambda m, n: (m, 0, n)),
        out_shape=jax.ShapeDtypeStruct((nmat, nb, d3), F32),
        compiler_params=_params(("arbitrary", "arbitrary")),
        name="ada_mods",
    )(c, w, b)
    return out.reshape(nmat * nb, 1, d3)


def _emit_stage_output(r, g_ref, b_ref, modp_ref, o_ref, hp_ref):
    xn = _layer_norm(r, g_ref[...], b_ref[...])
    o_ref[...] = xn
    shift, scale, _ = _split_mod(modp_ref[0], xn.shape[1])
    hp_ref[...] = (xn * scale + shift).astype(BF16)


def _conv_kernel(x_ref, mod_ref, modp_ref, win_ref, cw_ref, wout_ref, g_ref, b_ref, o_ref, hp_ref,
                 zbuf, *, tiles_per_seq, alpha):
    i = pl.program_id(0)
    tm, d = x_ref.shape
    x = x_ref[...]
    shift, scale, gate = _split_mod(mod_ref[0], d)
    h = (x * scale + shift).astype(BF16)
    hw = jnp.dot(h, win_ref[...], preferred_element_type=F32)
    gb = hw[:, :d]
    z = hw[:, d:2 * d] * hw[:, 2 * d:]

    @pl.when(i % tiles_per_seq == 0)
    def _():
        zbuf[0:SUBLANES, :] = jnp.zeros((SUBLANES, d), F32)

    zbuf[SUBLANES:, :] = z
    z1 = zbuf[pl.ds(SUBLANES - 1, tm), :]
    z2 = zbuf[pl.ds(SUBLANES - 2, tm), :]
    cw = cw_ref[...]
    zc = cw[2:3] * z + cw[1:2] * z1 + cw[0:1] * z2
    zbuf[0:SUBLANES, :] = zbuf[tm:tm + SUBLANES, :]
    y = jnp.dot((gb * zc).astype(BF16), wout_ref[...], preferred_element_type=F32)
    _emit_stage_output(alpha * x + gate * y, g_ref, b_ref, modp_ref, o_ref, hp_ref)


def _conv_layer(x, mods, mod_row, peer_row, w_in, conv_w, w_out, ln_g, ln_b, *, seq, alpha, tm=512):
    t, d = x.shape
    tiles_per_seq = seq // tm
    return pl.pallas_call(
        functools.partial(_conv_kernel, tiles_per_seq=tiles_per_seq, alpha=alpha),
        grid=(t // tm,),
        in_specs=[
            pl.BlockSpec((tm, d), lambda i: (i, 0)),
            pl.BlockSpec((1, 1, 3 * d), lambda i: (mod_row + i // tiles_per_seq, 0, 0)),
            pl.BlockSpec((1, 1, 3 * d), lambda i: (peer_row + i // tiles_per_seq, 0, 0)),
            _const_spec((d, 3 * d)),
            _const_spec((conv_w.shape[0], d)),
            _const_spec((d, d)),
            _const_spec((1, d)),
            _const_spec((1, d)),
        ],
        out_specs=[pl.BlockSpec((tm, d), lambda i: (i, 0))] * 2,
        out_shape=[jax.ShapeDtypeStruct((t, d), F32), jax.ShapeDtypeStruct((t, d), BF16)],
        scratch_shapes=[pltpu.VMEM((tm + SUBLANES, d), F32)],
        compiler_params=_params(("arbitrary",)),
        name="conv_mixer",
    )(x, mods, mods, w_in.astype(BF16), conv_w, w_out.astype(BF16), ln_g, ln_b)


def _qkv_kernel(x_ref, mod_ref, w_ref, qt_ref, k_ref, vt_ref, *, q_scale):
    tm, d = x_ref.shape
    shift, scale, _ = _split_mod(mod_ref[0], d)
    h = (x_ref[...] * scale + shift).astype(BF16)
    qkv = jnp.dot(h, w_ref[...], preferred_element_type=F32)
    qt_ref[...] = (qkv[:, :d] * q_scale).T.astype(BF16)
    k_ref[...] = qkv[:, d:2 * d].astype(BF16)
    vt_ref[...] = qkv[:, 2 * d:].T.astype(BF16)


def _qkv_proj(x, mods, mod_row, w_qkv, *, seq, q_scale, tm=512):
    t, d = x.shape
    tiles_per_seq = seq // tm
    return pl.pallas_call(
        functools.partial(_qkv_kernel, q_scale=q_scale),
        grid=(t // tm,),
        in_specs=[
            pl.BlockSpec((tm, d), lambda i: (i, 0)),
            pl.BlockSpec((1, 1, 3 * d), lambda i: (mod_row + i // tiles_per_seq, 0, 0)),
            _const_spec((d, 3 * d)),
        ],
        out_specs=[
            pl.BlockSpec((d, tm), lambda i: (0, i)),
            pl.BlockSpec((tm, d), lambda i: (i, 0)),
            pl.BlockSpec((d, tm), lambda i: (0, i)),
        ],
        out_shape=[
            jax.ShapeDtypeStruct((d, t), BF16),
            jax.ShapeDtypeStruct((t, d), BF16),
            jax.ShapeDtypeStruct((d, t), BF16),
        ],
        compiler_params=_params(("arbitrary",)),
        name="attn_qkv",
    )(x, mods, w_qkv.astype(BF16))


def _attn_kernel(slopes_ref, qt_ref, k_ref, vt_ref, lam_ref, g_ref, o_ref, qa_ref, m_ref, acc_ref,
                 kn_ref, *, tq, tk, heads, lambda_init):
    hg = pl.program_id(1)
    qi = pl.program_id(2)
    dh2 = qt_ref.shape[0] // heads
    dh = dh2 // 2
    ones_rows = 2 * SUBLANES
    blocks_per_tile = tq // tk

    feat = lax.broadcasted_iota(jnp.int32, (dh2, tq), 0)
    arow = lax.broadcasted_iota(jnp.int32, (dh2, 2 * tq), 0)
    acol = lax.broadcasted_iota(jnp.int32, (dh2, 2 * tq), 1)
    r = jnp.where(acol >= tq, acol - tq, acol)
    r_lo = (r % BF16_EXACT_INT).astype(F32)
    r_hi = (r - r % BF16_EXACT_INT).astype(F32)
    for g in range(heads):
        slope = slopes_ref[hg * heads + g]
        qt = qt_ref[g * dh2:(g + 1) * dh2, :]
        zero = jnp.zeros_like(qt)
        qa_ref[g, :dh2, :] = jnp.concatenate(
            [jnp.where(feat < dh, qt, zero), jnp.where(feat >= dh, qt, zero)], axis=1)
        qa_ref[g, dh2:, :] = jnp.where(
            arow == 0, -slope * r_lo,
            jnp.where(arow == 1, -slope * r_hi, jnp.where(arow == 2, slope, 0.0))).astype(BF16)
    kcol = lax.broadcasted_iota(jnp.int32, (tk, dh2), 1)
    krow = lax.broadcasted_iota(jnp.int32, (tk, dh2), 0).astype(F32)
    k_extra = jnp.where(kcol <= 1, 1.0, jnp.where(kcol == 2, krow, 0.0)).astype(BF16)
    ones = jnp.ones((ones_rows, tk), BF16)

    m_ref[...] = jnp.full(m_ref.shape, -jnp.inf, F32)
    acc_ref[...] = jnp.zeros(acc_ref.shape, F32)

    @pl.when(qi == 0)
    def _():
        ones_sq = jnp.ones((dh2, LANES), BF16)
        for g in range(heads):
            def chunk(c, best, g=g):
                kc = k_ref[pl.ds(pl.multiple_of(c * tq, tq), tq), g * dh2:(g + 1) * dh2].astype(F32)
                rows = jnp.dot((kc * kc).astype(BF16), ones_sq, preferred_element_type=F32)
                return jnp.maximum(best, jnp.max(rows, axis=0, keepdims=True))
            kn_ref[g] = lax.fori_loop(0, k_ref.shape[0] // tq, chunk, jnp.zeros((1, LANES), F32))

    def step(j, diagonal):
        start = pl.multiple_of(j * tk, tk)
        delta = qi * tq - j * tk
        head_cols = [slice(g * dh2, (g + 1) * dh2) for g in range(heads)]
        scores, maxima = [], []
        for g, cols in enumerate(head_cols):
            s = jnp.dot(jnp.concatenate([k_ref[pl.ds(start, tk), cols], k_extra], axis=1),
                        qa_ref[g], preferred_element_type=F32)
            if diagonal:
                mrow = lax.broadcasted_iota(jnp.int32, (tk, 2 * tq), 0)
                mcol = lax.broadcasted_iota(jnp.int32, (tk, 2 * tq), 1)
                mq = jnp.where(mcol >= tq, mcol - tq, mcol)
                s = jnp.where(mrow - mq <= delta, s, -jnp.inf)
            scores.append(s)
            maxima.append(jnp.max(s, axis=0, keepdims=True))
        probs, corrs = [], []
        for g, s in enumerate(scores):
            off = delta.astype(F32) * slopes_ref[hg * heads + g]
            m_old = m_ref[g]
            m_new = jnp.maximum(m_old, maxima[g] - off)
            probs.append(jnp.exp(s - (m_new + off)).astype(BF16))
            corrs.append(jnp.exp(m_old - m_new))
            m_ref[g] = m_new
        for g, cols in enumerate(head_cols):
            v_aug = jnp.concatenate([vt_ref[cols, pl.ds(start, tk)], ones], axis=0)
            acc_ref[g] = corrs[g] * acc_ref[g] + jnp.dot(v_aug, probs[g],
                                                         preferred_element_type=F32)

    first_diag = qi * blocks_per_tile
    for d in range(blocks_per_tile):
        step(first_diag + d, True)

    needed = jnp.zeros((1, 2 * tq), F32)
    bound = jnp.zeros((1, 2 * tq), F32)
    for g in range(heads):
        qf = qa_ref[g, :dh2, :].astype(F32)
        q_norm = jnp.sqrt(jnp.sum(qf * qf, axis=0, keepdims=True))
        k_norm = jnp.sqrt(kn_ref[g][:, :1])
        qk = q_norm * k_norm * NORM_SLACK
        reach = (qk - m_ref[g] + EXP_UNDERFLOW) / slopes_ref[hg * heads + g]
        needed = jnp.maximum(needed, jnp.floor((reach - 1.0) / tk) + 1.0)
        bound = jnp.maximum(bound, qk)
    needed = jnp.clip(needed, 0.0, first_diag.astype(F32))
    n_blocks = jnp.max(needed.astype(jnp.int32))

    fixed_shift_ok = jnp.max(bound) * 2.0 < FIXED_SHIFT_LIMIT

    def fixed_shift_step(j):
        start = pl.multiple_of(j * tk, tk)
        delta = qi * tq - j * tk
        scores = [jnp.dot(jnp.concatenate([k_ref[pl.ds(start, tk), g * dh2:(g + 1) * dh2], k_extra],
                                          axis=1), qa_ref[g], preferred_element_type=F32)
                  for g in range(heads)]
        probs = [jnp.exp(s - (m_ref[g] + delta.astype(F32) * slopes_ref[hg * heads + g])).astype(BF16)
                 for g, s in enumerate(scores)]
        for g in range(heads):
            v_aug = jnp.concatenate([vt_ref[g * dh2:(g + 1) * dh2, pl.ds(start, tk)], ones], axis=0)
            acc_ref[g] += jnp.dot(v_aug, probs[g], preferred_element_type=F32)

    @pl.when(fixed_shift_ok)
    def _():
        def body(i, carry):
            fixed_shift_step(first_diag - 1 - i)
            return carry

        lax.fori_loop(0, n_blocks, body, 0)

    @pl.when(jnp.logical_not(fixed_shift_ok))
    def _():
        def body(i, carry):
            step(first_diag - 1 - i, False)
            return carry

        lax.fori_loop(0, n_blocks, body, 0)

    lam = lam_ref[...]
    lam_full = (jnp.exp(jnp.sum(lam[0:1] * lam[1:2], axis=1, keepdims=True))
                - jnp.exp(jnp.sum(lam[2:3] * lam[3:4], axis=1, keepdims=True)) + lambda_init)
    for g in range(heads):
        acc = acc_ref[g]
        on = acc[:dh2] / acc[dh2:dh2 + 1]
        o = on[:, :tq] - lam_full * on[:, tq:]
        o = o * lax.rsqrt(jnp.mean(o * o, axis=0, keepdims=True) + LN_EPS) * g_ref[...]
        o_ref[:, g * dh2:(g + 1) * dh2] = (o * (1.0 - lambda_init)).T.astype(o_ref.dtype)


def _diff_attention(qt, k, vt, lam, subln_g, *, batch, seq, lambda_init, tq=512, tk=256, heads=4):
    d, t = qt.shape
    dh2 = d // ATT_HEADS
    gw = heads * dh2
    nq = seq // tq
    assert tq % tk == 0 and tk <= BF16_EXACT_INT and tq < BF16_EXACT_INT * BF16_EXACT_INT
    slopes = 2.0 ** (-(8.0 / ATT_HEADS) * jnp.arange(1, ATT_HEADS + 1, dtype=F32))
    return pl.pallas_call(
        functools.partial(_attn_kernel, tq=tq, tk=tk, heads=heads, lambda_init=lambda_init),
        grid=(batch, ATT_HEADS // heads, nq),
        in_specs=[
            pl.BlockSpec(memory_space=pltpu.SMEM),
            pl.BlockSpec((gw, tq), lambda b, h, i: (h, b * nq + i)),
            pl.BlockSpec((seq, gw), lambda b, h, i: (b, h)),
            pl.BlockSpec((gw, seq), lambda b, h, i: (h, b)),
            pl.BlockSpec(lam.shape, lambda b, h, i: (0, 0)),
            pl.BlockSpec((dh2, 1), lambda b, h, i: (0, 0)),
        ],
        out_specs=pl.BlockSpec((tq, gw), lambda b, h, i: (b * nq + i, h)),
        out_shape=jax.ShapeDtypeStruct((t, d), BF16),
        scratch_shapes=[
            pltpu.VMEM((heads, 2 * dh2, 2 * tq), BF16),
            pltpu.VMEM((heads, 1, 2 * tq), F32),
            pltpu.VMEM((heads, dh2 + 2 * SUBLANES, 2 * tq), F32),
            pltpu.VMEM((heads, 1, LANES), F32),
        ],
        compiler_params=_params(("arbitrary", "arbitrary", "arbitrary")),
        name="diff_attention",
    )(slopes, qt, k, vt, lam, subln_g.reshape(dh2, 1))


def _oproj_kernel(o_ref, x_ref, mod_ref, modp_ref, w_ref, g_ref, b_ref, out_ref, hp_ref, *, alpha):
    tm, d = x_ref.shape
    _, _, gate = _split_mod(mod_ref[0], d)
    y = jnp.dot(o_ref[...], w_ref[...], preferred_element_type=F32)
    _emit_stage_output(alpha * x_ref[...] + gate * y, g_ref, b_ref, modp_ref, out_ref, hp_ref)


def _out_proj(o, x, mods, mod_row, peer_row, w_o, ln_g, ln_b, *, seq, alpha, tm=512):
    t, d = x.shape
    tiles_per_seq = seq // tm
    return pl.pallas_call(
        functools.partial(_oproj_kernel, alpha=alpha),
        grid=(t // tm,),
        in_specs=[
            pl.BlockSpec((tm, d), lambda i: (i, 0)),
            pl.BlockSpec((tm, d), lambda i: (i, 0)),
            pl.BlockSpec((1, 1, 3 * d), lambda i: (mod_row + i // tiles_per_seq, 0, 0)),
            pl.BlockSpec((1, 1, 3 * d), lambda i: (peer_row + i // tiles_per_seq, 0, 0)),
            _const_spec((d, d)),
            _const_spec((1, d)),
            _const_spec((1, d)),
        ],
        out_specs=[pl.BlockSpec((tm, d), lambda i: (i, 0))] * 2,
        out_shape=[jax.ShapeDtypeStruct((t, d), F32), jax.ShapeDtypeStruct((t, d), BF16)],
        compiler_params=_params(("arbitrary",)),
        name="attn_out_proj",
    )(o, x, mods, mods, w_o.astype(BF16), ln_g, ln_b)


def _top16_rows(s):
    n, tm = s.shape
    rows = lax.broadcasted_iota(jnp.int32, (n, tm), 0).astype(F32)
    vals, idxs = [], []
    tie = None
    for _ in range(PEER_TOPK):
        m = jnp.max(s, axis=0, keepdims=True)
        if tie is not None:
            m = m + tie
        idx = jnp.min(jnp.where(s == m, rows, float(n)), axis=0, keepdims=True)
        s = jnp.where(rows == idx, -jnp.inf, s)
        vals.append(m)
        idxs.append(idx)
        tie = yield
    return vals, idxs


def _candidate_layout(tm):
    slot = lax.broadcasted_iota(jnp.int32, (N_CAND, tm), 0)
    grp = slot // SUBLANES
    r = slot % SUBLANES
    a = jnp.where(grp <= 1, 0, jnp.where(grp == N_CAND_GROUPS - 1, SUBLANES + r, grp - 1))
    b = jnp.where(grp == 0, r, jnp.where(grp == 1, SUBLANES + r,
                                         jnp.where(grp == N_CAND_GROUPS - 1, 0, r)))
    valid = (a + 1) * (b + 1) <= PEER_TOPK
    flat = (a * PEER_TOPK + b).astype(F32)
    return valid, flat


def _candidates(rows0, rows1):
    lo1 = jnp.concatenate(rows1[:SUBLANES], axis=0)
    hi1 = jnp.concatenate(rows1[SUBLANES:], axis=0)
    hi0 = jnp.concatenate(rows0[SUBLANES:], axis=0)
    groups = [rows0[0] + lo1, rows0[0] + hi1]
    groups += [rows0[a] + lo1 for a in range(1, SUBLANES)]
    groups.append(hi0 + rows1[0])
    return jnp.concatenate(groups, axis=0)


def _route_head(q_head, keys_ref, valid, flat):
    scores = [lax.dot_general(keys_ref[p], q_head[:, p * PEER_KEYS:(p + 1) * PEER_KEYS],
                              (((1,), (1,)), ((), ())), preferred_element_type=F32)
              for p in range(2)]
    yield
    vals, idxs = [], []
    for st in scores:
        v, ix = yield from _top16_rows(st)
        vals.append(v)
        idxs.append(ix)
    cand = jnp.where(valid, _candidates(vals[0], vals[1]), -jnp.inf)
    code = _candidates([ix * float(PEER_KEYS) for ix in idxs[0]], idxs[1])
    top_s, top_code = [], []
    tie = None
    for _ in range(PEER_TOPK):
        m = jnp.max(cand, axis=0, keepdims=True)
        if tie is not None:
            m = m + tie
        fmin = jnp.min(jnp.where(cand == m, flat, float(PEER_TOPK * PEER_TOPK)),
                       axis=0, keepdims=True)
        hit = flat == fmin
        top_code.append(jnp.sum(jnp.where(hit, code, 0.0), axis=0, keepdims=True))
        cand = jnp.where(hit, -jnp.inf, cand)
        top_s.append(m)
        tie = yield
    ts = jnp.concatenate(top_s, axis=0)
    e = jnp.exp(ts - top_s[0])
    return (jnp.concatenate(top_code, axis=0).astype(jnp.int32),
            e / jnp.sum(e, axis=0, keepdims=True))


ROUTE_YIELDS = 3 * PEER_TOPK


def _zero_row(x, width):
    bits = lax.bitcast_convert_type(x[:1, :LANES], jnp.uint32)
    zero = lax.bitcast_convert_type((bits >> 16) >> 16, F32)
    return jnp.concatenate([zero] * (width // LANES), axis=1)


def _peer_kernel(h_ref, hn_ref, x_ref, mod_ref, wq_ref, keys_ref, ut_ref, v_ref, g_ref, b_ref,
                 o_ref, q_ref, code_ref, gate_ref, ci_ref, cg_ref, w_ref, acc_ref, hs_ref, us_ref,
                 *, alpha):
    i = pl.program_id(0)
    j = pl.program_id(1)
    nj = pl.num_programs(1)
    tm, d = x_ref.shape
    tn = ut_ref.shape[1]
    nhk = PEER_HEADS * PEER_TOPK
    parts = ROUTE_UNITS // PEER_HEADS
    tp = tm // parts
    slot = i % 2
    valid, flat = _candidate_layout(tp)

    def prepare_queries(h):
        q = jnp.dot(h, wq_ref[...], preferred_element_type=F32).astype(BF16)
        for hd in range(PEER_HEADS):
            q_ref[hd] = q[:, hd * 2 * PEER_KEYS:(hd + 1) * 2 * PEER_KEYS]

    def route_unit(u, dst):
        hd = u // parts
        part = u % parts
        rows = pl.ds(pl.multiple_of(part * tp, tp), tp)
        code, gate = yield from _route_head(q_ref[hd, rows, :], keys_ref, valid, flat)
        krows = pl.ds(pl.multiple_of(hd * PEER_TOPK, PEER_TOPK), PEER_TOPK)
        for g in range(tp // LANES):
            lanes = slice(g * LANES, (g + 1) * LANES)
            code_ref[dst, part * (tp // LANES) + g, krows, :] = code[:, lanes]
            gate_ref[dst, part * (tp // LANES) + g, krows, :] = gate[:, lanes]

    @pl.when((i == 0) & (j == 0))
    def _():
        prepare_queries(h_ref[...])

        def unit(u, carry):
            for _ in route_unit(u, 0):
                pass
            return carry

        lax.fori_loop(0, ROUTE_UNITS, unit, 0)

    @pl.when(j == 0)
    def _():
        for grp in range(tm // LANES):
            code = code_ref[slot, grp].T
            ci_ref[grp * LANES:(grp + 1) * LANES, :] = code // PEER_KEYS
            cg_ref[grp * LANES:(grp + 1) * LANES, :] = gate_ref[slot, grp].T
        key_rows = lax.broadcasted_iota(jnp.int32, (PEER_KEYS, nhk), 0).astype(BF16)
        key_lanes = lax.broadcasted_iota(jnp.int32, (nhk, PEER_KEYS), 1).astype(BF16)
        zero = jnp.zeros((PEER_KEYS, nhk), BF16)
        one = jnp.ones((nhk, PEER_KEYS), BF16)

        def token_group(grp, carry):
            base = pl.multiple_of(grp * LANES, LANES)
            second = (code_ref[slot, grp] % PEER_KEYS).astype(F32)

            def gate_bits(u):
                ri = ci_ref[pl.ds(base + u, 1), :].astype(BF16)
                rg = cg_ref[pl.ds(base + u, 1), :].astype(BF16)
                cj = jnp.broadcast_to(second[:, u:u + 1], (nhk, PEER_KEYS)).astype(BF16)
                pt = jnp.where(key_rows == ri, jnp.broadcast_to(rg, zero.shape), zero)
                qm = jnp.where(key_lanes == cj, one, jnp.zeros_like(one))
                wt = jnp.dot(pt, qm, preferred_element_type=F32)
                return lax.bitcast_convert_type(wt, jnp.uint32)

            for u in range(0, LANES, 2):
                packed = (gate_bits(u) >> 16) | (gate_bits(u + 1) & jnp.uint32(0xFFFF0000))
                pair = (base + u) // 2
                w_ref[pl.ds(pl.multiple_of(pair * W_PITCH, SUBLANES), PEER_KEYS), :] = packed
            return carry

        lax.fori_loop(0, tm // LANES, token_group, 0)
        acc_ref[...] = jnp.zeros(acc_ref.shape, F32)
        hs_ref[...] = h_ref[...]
        prepare_queries(hn_ref[...])

    route = route_unit(j, 1 - slot)
    next(route)
    chunk = 2 * PEER_KEYS
    n_chunks = tn // chunk

    def advance(after, count):
        try:
            route.send(None if after is None else _zero_row(after, tp))
            for _ in range(count - 1):
                next(route)
        except StopIteration:
            pass

    halves = 2
    th = tm // halves
    per_piece = -(-ROUTE_YIELDS // (2 * halves * n_chunks))

    def expert_inputs(half):
        rows = slice(half * th, (half + 1) * th)
        h = hs_ref[rows, :]
        pieces = []
        for r in range(n_chunks):
            c0 = r * chunk
            a = jnp.dot(h, us_ref[:, c0:c0 + chunk], preferred_element_type=F32)
            advance(a, per_piece)
            act = (a * (1.0 + lax.erf(a * (1.0 / math.sqrt(2.0))))).astype(BF16)
            for e in range(chunk // PEER_KEYS):
                first_key = (j * n_chunks + r) * (chunk // PEER_KEYS) + e
                words = w_ref[pl.ds(half * (th // 2) * W_PITCH + first_key, th // 2, stride=W_PITCH), :]
                gates = pltpu.bitcast(words, BF16)
                pieces.append(gates * act[:, e * PEER_KEYS:(e + 1) * PEER_KEYS])
        return jnp.concatenate(pieces, axis=1)

    def expert_outputs(half, z):
        rows = slice(half * th, (half + 1) * th)
        for c in range(d // chunk):
            cols = slice(c * chunk, (c + 1) * chunk)
            zv = jnp.dot(z, vs[:, cols], preferred_element_type=F32)
            acc_ref[rows, cols] += zv
            advance(zv, per_piece)

    us_ref[...] = pltpu.bitcast(ut_ref[...], BF16)
    vs = pltpu.bitcast(v_ref[...], BF16)
    advance(None, per_piece)
    z_prev = expert_inputs(0)
    for half in range(1, halves):
        z_next = expert_inputs(half)
        expert_outputs(half - 1, z_prev)
        z_prev = z_next
    expert_outputs(halves - 1, z_prev)
    for _ in route:
        pass

    @pl.when(j == nj - 1)
    def _():
        _, _, gate = _split_mod(mod_ref[0], d)
        o_ref[...] = _layer_norm(alpha * x_ref[...] + gate * acc_ref[...], g_ref[...], b_ref[...])


def _peer_layer(h, x, mods, mod_row, w_q, sub_keys, u_t, v, ln_g, ln_b, *, seq, alpha, tm=512):
    t, d = x.shape
    n = u_t.shape[1]
    tn = n // ROUTE_UNITS
    nhk = PEER_HEADS * PEER_TOPK
    parts = ROUTE_UNITS // PEER_HEADS
    tiles_per_seq = seq // tm
    last = t // tm - 1
    return pl.pallas_call(
        functools.partial(_peer_kernel, alpha=alpha),
        grid=(t // tm, n // tn),
        in_specs=[
            pl.BlockSpec((tm, d), lambda i, j: (i, 0)),
            pl.BlockSpec((tm, d), lambda i, j: (jnp.minimum(i + 1, last), 0)),
            pl.BlockSpec((tm, d), lambda i, j: (i, 0)),
            pl.BlockSpec((1, 1, 3 * d), lambda i, j: (mod_row + i // tiles_per_seq, 0, 0)),
            _const_spec(w_q.shape),
            _const_spec(sub_keys.shape),
            pl.BlockSpec((d // 2, tn), lambda i, j: (0, j)),
            pl.BlockSpec((tn // 2, d), lambda i, j: (j, 0)),
            _const_spec((1, d)),
            _const_spec((1, d)),
        ],
        out_specs=pl.BlockSpec((tm, d), lambda i, j: (i, 0)),
        out_shape=jax.ShapeDtypeStruct((t, d), F32),
        scratch_shapes=[
            pltpu.VMEM((PEER_HEADS, tm, 2 * PEER_KEYS), BF16),
            pltpu.VMEM((2, tm // LANES, nhk, LANES), jnp.int32),
            pltpu.VMEM((2, tm // LANES, nhk, LANES), F32),
            pltpu.VMEM((tm, nhk), jnp.int32),
            pltpu.VMEM((tm, nhk), F32),
            pltpu.VMEM((tm // 2 * W_PITCH, PEER_KEYS), jnp.uint32),
            pltpu.VMEM((tm, d), F32),
            pltpu.VMEM((tm, d), BF16),
            pltpu.VMEM((d, tn), BF16),
        ],
        compiler_params=_params(("arbitrary", "arbitrary")),
        name="peer_layer",
    )(h, h, x, mods, w_q.astype(BF16), sub_keys.astype(BF16), u_t, v, ln_g, ln_b)


def _pack_kernel(w_ref, o_ref, t_ref, *, transpose, scale):
    w = w_ref[...]
    if scale != 1.0:
        w = w * scale
    if transpose:
        w = w.T
    rows, cols = w.shape
    for c in range(cols // LANES):
        t_ref[c] = w[:, c * LANES:(c + 1) * LANES]
        even = t_ref[c, pl.ds(0, rows // 2, stride=2), :].astype(BF16).astype(F32)
        odd = t_ref[c, pl.ds(1, rows // 2, stride=2), :].astype(BF16).astype(F32)
        o_ref[:, c * LANES:(c + 1) * LANES] = (
            (lax.bitcast_convert_type(even, jnp.uint32) >> 16)
            | (lax.bitcast_convert_type(odd, jnp.uint32) & jnp.uint32(0xFFFF0000)))


def _packed_rows(w, *, transpose, scale=1.0, tb=512):
    n, d = w.shape
    if transpose:
        out_shape, out_block, out_map = (d // 2, n), (d // 2, tb), (lambda i: (0, i))
        t_shape = (tb // LANES, d, LANES)
    else:
        out_shape, out_block, out_map = (n // 2, d), (tb // 2, d), (lambda i: (i, 0))
        t_shape = (d // LANES, tb, LANES)
    return pl.pallas_call(
        functools.partial(_pack_kernel, transpose=transpose, scale=scale),
        grid=(n // tb,),
        in_specs=[pl.BlockSpec((tb, d), lambda i: (i, 0))],
        out_specs=pl.BlockSpec(out_block, out_map),
        out_shape=jax.ShapeDtypeStruct(out_shape, jnp.uint32),
        scratch_shapes=[pltpu.VMEM(t_shape, F32)],
        compiler_params=_params(("arbitrary",)),
        name="pack_weights",
    )(w)


def _lambda_init(layer_idx):
    return 0.8 - 0.6 * math.exp(-0.3 * layer_idx)


def kernel(x, c, ada_w, ada_b, ln_g, ln_b, conv_w_in, conv_w, conv_w_out, attn_w_qkv, attn_lambda,
           attn_subln_g, attn_w_o, peer_w_q, peer_sub_keys, peer_u, peer_v):
    batch, seq, d = x.shape
    depth = ada_w.shape[0]
    alpha = (2.0 * depth) ** 0.25
    head_dim = d // (2 * ATT_HEADS)
    mods = _ada_mods(c, ada_w, ada_b)
    xt = x.reshape(batch * seq, d)
    for i in range(depth):
        j = i // N_MIXERS
        row = (2 * i) * batch
        prow = (2 * i + 1) * batch
        g0, b0 = ln_g[i, 0].reshape(1, d), ln_b[i, 0].reshape(1, d)
        if i % N_MIXERS == 0:
            xt, h = _conv_layer(xt, mods, row, prow, conv_w_in[j], conv_w[j], conv_w_out[j], g0, b0,
                                seq=seq, alpha=alpha)
        else:
            qt, k, vt = _qkv_proj(xt, mods, row, attn_w_qkv[j], seq=seq, q_scale=head_dim ** -0.5)
            o = _diff_attention(qt, k, vt, attn_lambda[j], attn_subln_g[j], batch=batch, seq=seq,
                                lambda_init=_lambda_init(i))
            xt, h = _out_proj(o, xt, mods, row, prow, attn_w_o[j], g0, b0, seq=seq, alpha=alpha)
        g1, b1 = ln_g[i, 1].reshape(1, d), ln_b[i, 1].reshape(1, d)
        xt = _peer_layer(h, xt, mods, prow, peer_w_q[i], peer_sub_keys[i],
                         _packed_rows(peer_u[i], transpose=True),
                         _packed_rows(peer_v[i], transpose=False, scale=0.5), g1, b1,
                         seq=seq, alpha=alpha)
    return xt.reshape(batch, seq, d)
```

```python
import functools
import math

import jax
import jax.numpy as jnp
from jax import lax
from jax.experimental import pallas as pl
from jax.experimental.pallas import tpu as pltpu

F32 = jnp.float32
BF16 = jnp.bfloat16

N_MIXERS = 2
ATT_HEADS = 8
PEER_HEADS = 8
PEER_KEYS = 128
PEER_TOPK = 16
LN_EPS = 1e-5

LANES = 128
SUBLANES = 8
VMEM_LIMIT_BYTES = 56 * 1024 * 1024
BF16_EXACT_INT = 256
EXP_UNDERFLOW = 106.0
NORM_SLACK = 1.02
FIXED_SHIFT_LIMIT = 60.0

N_CAND_GROUPS = 10
N_CAND = N_CAND_GROUPS * SUBLANES
W_PITCH = 136
ROUTE_UNITS = 16


def _layer_norm(r, g, b):
    mu = jnp.mean(r, axis=-1, keepdims=True)
    d = r - mu
    var = jnp.mean(d * d, axis=-1, keepdims=True)
    return d * lax.rsqrt(var + LN_EPS) * g + b


def _split_mod(mod, d):
    return mod[:, :d], 1.0 + mod[:, d:2 * d], 1.0 + mod[:, 2 * d:]


def _params(sem):
    return pltpu.CompilerParams(dimension_semantics=sem, vmem_limit_bytes=VMEM_LIMIT_BYTES)


def _const_spec(shape):
    nd = len(shape)
    return pl.BlockSpec(shape, lambda *_: (0,) * nd, pipeline_mode=pl.Buffered(1))


def _ada_kernel(c_ref, w_ref, b_ref, o_ref):
    c = c_ref[...]
    sc = c / (1.0 + jnp.exp(-c))
    o_ref[0] = jnp.dot(sc, w_ref[0], precision=lax.Precision.HIGHEST,
                       preferred_element_type=F32) + b_ref[0]


def _ada_mods(c, ada_w, ada_b):
    depth, _, d, d3 = ada_w.shape
    nb = c.shape[0]
    nmat = depth * 2
    tn = 1024
    w = ada_w.reshape(nmat, d, d3)
    b = ada_b.reshape(nmat, 1, d3)
    out = pl.pallas_call(
        _ada_kernel,
        grid=(nmat, d3 // tn),
        in_specs=[
            pl.BlockSpec((nb, d), lambda m, n: (0, 0)),
            pl.BlockSpec((1, d, tn), lambda m, n: (m, 0, n)),
            pl.BlockSpec((1, 1, tn), lambda m, n: (m, 0, n)),
        ],
        out_specs=pl.BlockSpec((1, nb, tn), lambda m, n: (m, 0, n)),
        out_shape=jax.ShapeDtypeStruct((nmat, nb, d3), F32),
        compiler_params=_params(("arbitrary", "arbitrary")),
        name="ada_mods",
    )(c, w, b)
    return out.reshape(nmat * nb, 1, d3)


def _emit_stage_output(r, g_ref, b_ref, modp_ref, o_ref, hp_ref):
    xn = _layer_norm(r, g_ref[...], b_ref[...])
    o_ref[...] = xn
    shift, scale, _ = _split_mod(modp_ref[0], xn.shape[1])
    hp_ref[...] = (xn * scale + shift).astype(BF16)


def _conv_kernel(x_ref, mod_ref, modp_ref, win_ref, cw_ref, wout_ref, g_ref, b_ref, o_ref, hp_ref,
                 zbuf, *, tiles_per_seq, alpha):
    i = pl.program_id(0)
    tm, d = x_ref.shape
    x = x_ref[...]
    shift, scale, gate = _split_mod(mod_ref[0], d)
    h = (x * scale + shift).astype(BF16)
    hw = jnp.dot(h, win_ref[...], preferred_element_type=F32)
    gb = hw[:, :d]
    z = hw[:, d:2 * d] * hw[:, 2 * d:]

    @pl.when(i % tiles_per_seq == 0)
    def _():
        zbuf[0:SUBLANES, :] = jnp.zeros((SUBLANES, d), F32)

    zbuf[SUBLANES:, :] = z
    z1 = zbuf[pl.ds(SUBLANES - 1, tm), :]
    z2 = zbuf[pl.ds(SUBLANES - 2, tm), :]
    cw = cw_ref[...]
    zc = cw[2:3] * z + cw[1:2] * z1 + cw[0:1] * z2
    zbuf[0:SUBLANES, :] = zbuf[tm:tm + SUBLANES, :]
    y = jnp.dot((gb * zc).astype(BF16), wout_ref[...], preferred_element_type=F32)
    _emit_stage_output(alpha * x + gate * y, g_ref, b_ref, modp_ref, o_ref, hp_ref)


def _conv_layer(x, mods, mod_row, peer_row, w_in, conv_w, w_out, ln_g, ln_b, *, seq, alpha, tm=512):
    t, d = x.shape
    tiles_per_seq = seq // tm
    return pl.pallas_call(
        functools.partial(_conv_kernel, tiles_per_seq=tiles_per_seq, alpha=alpha),
        grid=(t // tm,),
        in_specs=[
            pl.BlockSpec((tm, d), lambda i: (i, 0)),
            pl.BlockSpec((1, 1, 3 * d), lambda i: (mod_row + i // tiles_per_seq, 0, 0)),
            pl.BlockSpec((1, 1, 3 * d), lambda i: (peer_row + i // tiles_per_seq, 0, 0)),
            _const_spec((d, 3 * d)),
            _const_spec((conv_w.shape[0], d)),
            _const_spec((d, d)),
            _const_spec((1, d)),
            _const_spec((1, d)),
        ],
        out_specs=[pl.BlockSpec((tm, d), lambda i: (i, 0))] * 2,
        out_shape=[jax.ShapeDtypeStruct((t, d), F32), jax.ShapeDtypeStruct((t, d), BF16)],
        scratch_shapes=[pltpu.VMEM((tm + SUBLANES, d), F32)],
        compiler_params=_params(("arbitrary",)),
        name="conv_mixer",
    )(x, mods, mods, w_in.astype(BF16), conv_w, w_out.astype(BF16), ln_g, ln_b)


def _qkv_kernel(x_ref, mod_ref, w_ref, qt_ref, k_ref, vt_ref, *, q_scale):
    tm, d = x_ref.shape
    shift, scale, _ = _split_mod(mod_ref[0], d)
    h = (x_ref[...] * scale + shift).astype(BF16)
    qkv = jnp.dot(h, w_ref[...], preferred_element_type=F32)
    qt_ref[...] = (qkv[:, :d] * q_scale).T.astype(BF16)
    k_ref[...] = qkv[:, d:2 * d].astype(BF16)
    vt_ref[...] = qkv[:, 2 * d:].T.astype(BF16)


def _qkv_proj(x, mods, mod_row, w_qkv, *, seq, q_scale, tm=512):
    t, d = x.shape
    tiles_per_seq = seq // tm
    return pl.pallas_call(
        functools.partial(_qkv_kernel, q_scale=q_scale),
        grid=(t // tm,),
        in_specs=[
            pl.BlockSpec((tm, d), lambda i: (i, 0)),
            pl.BlockSpec((1, 1, 3 * d), lambda i: (mod_row + i // tiles_per_seq, 0, 0)),
            _const_spec((d, 3 * d)),
        ],
        out_specs=[
            pl.BlockSpec((d, tm), lambda i: (0, i)),
            pl.BlockSpec((tm, d), lambda i: (i, 0)),
            pl.BlockSpec((d, tm), lambda i: (0, i)),
        ],
        out_shape=[
            jax.ShapeDtypeStruct((d, t), BF16),
            jax.ShapeDtypeStruct((t, d), BF16),
            jax.ShapeDtypeStruct((d, t), BF16),
        ],
        compiler_params=_params(("arbitrary",)),
        name="attn_qkv",
    )(x, mods, w_qkv.astype(BF16))


def _attn_kernel(slopes_ref, qt_ref, k_ref, vt_ref, lam_ref, g_ref, o_ref, qa_ref, m_ref, acc_ref,
                 kn_ref, *, tq, tk, heads, lambda_init):
    hg = pl.program_id(1)
    qi = pl.program_id(2)
    dh2 = qt_ref.shape[0] // heads
    dh = dh2 // 2
    ones_rows = 2 * SUBLANES
    blocks_per_tile = tq // tk

    feat = lax.broadcasted_iota(jnp.int32, (dh2, tq), 0)
    arow = lax.broadcasted_iota(jnp.int32, (dh2, 2 * tq), 0)
    acol = lax.broadcasted_iota(jnp.int32, (dh2, 2 * tq), 1)
    r = jnp.where(acol >= tq, acol - tq, acol)
    r_lo = (r % BF16_EXACT_INT).astype(F32)
    r_hi = (r - r % BF16_EXACT_INT).astype(F32)
    for g in range(heads):
        slope = slopes_ref[hg * heads + g]
        qt = qt_ref[g * dh2:(g + 1) * dh2, :]
        zero = jnp.zeros_like(qt)
        qa_ref[g, :dh2, :] = jnp.concatenate(
            [jnp.where(feat < dh, qt, zero), jnp.where(feat >= dh, qt, zero)], axis=1)
        qa_ref[g, dh2:, :] = jnp.where(
            arow == 0, -slope * r_lo,
            jnp.where(arow == 1, -slope * r_hi, jnp.where(arow == 2, slope, 0.0))).astype(BF16)
    kcol = lax.broadcasted_iota(jnp.int32, (tk, dh2), 1)
    krow = lax.broadcasted_iota(jnp.int32, (tk, dh2), 0).astype(F32)
    k_extra = jnp.where(kcol <= 1, 1.0, jnp.where(kcol == 2, krow, 0.0)).astype(BF16)
    ones = jnp.ones((ones_rows, tk), BF16)

    m_ref[...] = jnp.full(m_ref.shape, -jnp.inf, F32)
    acc_ref[...] = jnp.zeros(acc_ref.shape, F32)

    @pl.when(qi == 0)
    def _():
        ones_sq = jnp.ones((dh2, LANES), BF16)
        for g in range(heads):
            def chunk(c, best, g=g):
                kc = k_ref[pl.ds(pl.multiple_of(c * tq, tq), tq), g * dh2:(g + 1) * dh2].astype(F32)
                rows = jnp.dot((kc * kc).astype(BF16), ones_sq, preferred_element_type=F32)
                return jnp.maximum(best, jnp.max(rows, axis=0, keepdims=True))
            kn_ref[g] = lax.fori_loop(0, k_ref.shape[0] // tq, chunk, jnp.zeros((1, LANES), F32))

    def step(j, diagonal):
        start = pl.multiple_of(j * tk, tk)
        delta = qi * tq - j * tk
        head_cols = [slice(g * dh2, (g + 1) * dh2) for g in range(heads)]
        scores, maxima = [], []
        for g, cols in enumerate(head_cols):
            s = jnp.dot(jnp.concatenate([k_ref[pl.ds(start, tk), cols], k_extra], axis=1),
                        qa_ref[g], preferred_element_type=F32)
            if diagonal:
                mrow = lax.broadcasted_iota(jnp.int32, (tk, 2 * tq), 0)
                mcol = lax.broadcasted_iota(jnp.int32, (tk, 2 * tq), 1)
                mq = jnp.where(mcol >= tq, mcol - tq, mcol)
                s = jnp.where(mrow - mq <= delta, s, -jnp.inf)
            scores.append(s)
            maxima.append(jnp.max(s, axis=0, keepdims=True))
        probs, corrs = [], []
        for g, s in enumerate(scores):
            off = delta.astype(F32) * slopes_ref[hg * heads + g]
            m_old = m_ref[g]
            m_new = jnp.maximum(m_old, maxima[g] - off)
            probs.append(jnp.exp(s - (m_new + off)).astype(BF16))
            corrs.append(jnp.exp(m_old - m_new))
            m_ref[g] = m_new
        for g, cols in enumerate(head_cols):
            v_aug = jnp.concatenate([vt_ref[cols, pl.ds(start, tk)], ones], axis=0)
            acc_ref[g] = corrs[g] * acc_ref[g] + jnp.dot(v_aug, probs[g],
                                                         preferred_element_type=F32)

    first_diag = qi * blocks_per_tile
    for d in range(blocks_per_tile):
        step(first_diag + d, True)

    needed = jnp.zeros((1, 2 * tq), F32)
    bound = jnp.zeros((1, 2 * tq), F32)
    for g in range(heads):
        qf = qa_ref[g, :dh2, :].astype(F32)
        q_norm = jnp.sqrt(jnp.sum(qf * qf, axis=0, keepdims=True))
        k_norm = jnp.sqrt(kn_ref[g][:, :1])
        qk = q_norm * k_norm * NORM_SLACK
        reach = (qk - m_ref[g] + EXP_UNDERFLOW) / slopes_ref[hg * heads + g]
        needed = jnp.maximum(needed, jnp.floor((reach - 1.0) / tk) + 1.0)
        bound = jnp.maximum(bound, qk)
    needed = jnp.clip(needed, 0.0, first_diag.astype(F32))
    n_blocks = jnp.max(needed.astype(jnp.int32))

    fixed_shift_ok = jnp.max(bound) * 2.0 < FIXED_SHIFT_LIMIT

    def fixed_shift_step(j):
        start = pl.multiple_of(j * tk, tk)
        delta = qi * tq - j * tk
        scores = [jnp.dot(jnp.concatenate([k_ref[pl.ds(start, tk), g * dh2:(g + 1) * dh2], k_extra],
                                          axis=1), qa_ref[g], preferred_element_type=F32)
                  for g in range(heads)]
        probs = [jnp.exp(s - (m_ref[g] + delta.astype(F32) * slopes_ref[hg * heads + g])).astype(BF16)
                 for g, s in enumerate(scores)]
        for g in range(heads):
            v_aug = jnp.concatenate([vt_ref[g * dh2:(g + 1) * dh2, pl.ds(start, tk)], ones], axis=0)
            acc_ref[g] += jnp.dot(v_aug, probs[g], preferred_element_type=F32)

    @pl.when(fixed_shift_ok)
    def _():
        def body(i, carry):
            fixed_shift_step(first_diag - 1 - i)
            return carry

        lax.fori_loop(0, n_blocks, body, 0)

    @pl.when(jnp.logical_not(fixed_shift_ok))
    def _():
        def body(i, carry):
            step(first_diag - 1 - i, False)
            return carry

        lax.fori_loop(0, n_blocks, body, 0)

    lam = lam_ref[...]
    lam_full = (jnp.exp(jnp.sum(lam[0:1] * lam[1:2], axis=1, keepdims=True))
                - jnp.exp(jnp.sum(lam[2:3] * lam[3:4], axis=1, keepdims=True)) + lambda_init)
    for g in range(heads):
        acc = acc_ref[g]
        on = acc[:dh2] / acc[dh2:dh2 + 1]
        o = on[:, :tq] - lam_full * on[:, tq:]
        o = o * lax.rsqrt(jnp.mean(o * o, axis=0, keepdims=True) + LN_EPS) * g_ref[...]
        o_ref[:, g * dh2:(g + 1) * dh2] = (o * (1.0 - lambda_init)).T.astype(o_ref.dtype)


def _diff_attention(qt, k, vt, lam, subln_g, *, batch, seq, lambda_init, tq=512, tk=256, heads=4):
    d, t = qt.shape
    dh2 = d // ATT_HEADS
    gw = heads * dh2
    nq = seq // tq
    assert tq % tk == 0 and tk <= BF16_EXACT_INT and tq < BF16_EXACT_INT * BF16_EXACT_INT
    slopes = 2.0 ** (-(8.0 / ATT_HEADS) * jnp.arange(1, ATT_HEADS + 1, dtype=F32))
    return pl.pallas_call(
        functools.partial(_attn_kernel, tq=tq, tk=tk, heads=heads, lambda_init=lambda_init),
        grid=(batch, ATT_HEADS // heads, nq),
        in_specs=[
            pl.BlockSpec(memory_space=pltpu.SMEM),
            pl.BlockSpec((gw, tq), lambda b, h, i: (h, b * nq + i)),
            pl.BlockSpec((seq, gw), lambda b, h, i: (b, h)),
            pl.BlockSpec((gw, seq), lambda b, h, i: (h, b)),
            pl.BlockSpec(lam.shape, lambda b, h, i: (0, 0)),
            pl.BlockSpec((dh2, 1), lambda b, h, i: (0, 0)),
        ],
        out_specs=pl.BlockSpec((tq, gw), lambda b, h, i: (b * nq + i, h)),
        out_shape=jax.ShapeDtypeStruct((t, d), BF16),
        scratch_shapes=[
            pltpu.VMEM((heads, 2 * dh2, 2 * tq), BF16),
            pltpu.VMEM((heads, 1, 2 * tq), F32),
            pltpu.VMEM((heads, dh2 + 2 * SUBLANES, 2 * tq), F32),
            pltpu.VMEM((heads, 1, LANES), F32),
        ],
        compiler_params=_params(("arbitrary", "arbitrary", "arbitrary")),
        name="diff_attention",
    )(slopes, qt, k, vt, lam, subln_g.reshape(dh2, 1))


def _oproj_kernel(o_ref, x_ref, mod_ref, modp_ref, w_ref, g_ref, b_ref, out_ref, hp_ref, *, alpha):
    tm, d = x_ref.shape
    _, _, gate = _split_mod(mod_ref[0], d)
    y = jnp.dot(o_ref[...], w_ref[...], preferred_element_type=F32)
    _emit_stage_output(alpha * x_ref[...] + gate * y, g_ref, b_ref, modp_ref, out_ref, hp_ref)


def _out_proj(o, x, mods, mod_row, peer_row, w_o, ln_g, ln_b, *, seq, alpha, tm=512):
    t, d = x.shape
    tiles_per_seq = seq // tm
    return pl.pallas_call(
        functools.partial(_oproj_kernel, alpha=alpha),
        grid=(t // tm,),
        in_specs=[
            pl.BlockSpec((tm, d), lambda i: (i, 0)),
            pl.BlockSpec((tm, d), lambda i: (i, 0)),
            pl.BlockSpec((1, 1, 3 * d), lambda i: (mod_row + i // tiles_per_seq, 0, 0)),
            pl.BlockSpec((1, 1, 3 * d), lambda i: (peer_row + i // tiles_per_seq, 0, 0)),
            _const_spec((d, d)),
            _const_spec((1, d)),
            _const_spec((1, d)),
        ],
        out_specs=[pl.BlockSpec((tm, d), lambda i: (i, 0))] * 2,
        out_shape=[jax.ShapeDtypeStruct((t, d), F32), jax.ShapeDtypeStruct((t, d), BF16)],
        compiler_params=_params(("arbitrary",)),
        name="attn_out_proj",
    )(o, x, mods, mods, w_o.astype(BF16), ln_g, ln_b)


def _top16_rows(s):
    n, tm = s.shape
    rows = lax.broadcasted_iota(jnp.int32, (n, tm), 0).astype(F32)
    vals, idxs = [], []
    tie = None
    for _ in range(PEER_TOPK):
        m = jnp.max(s, axis=0, keepdims=True)
        if tie is not None:
            m = m + tie
        idx = jnp.min(jnp.where(s == m, rows, float(n)), axis=0, keepdims=True)
        s = jnp.where(rows == idx, -jnp.inf, s)
        vals.append(m)
        idxs.append(idx)
        tie = yield
    return vals, idxs


def _candidate_layout(tm):
    slot = lax.broadcasted_iota(jnp.int32, (N_CAND, tm), 0)
    grp = slot // SUBLANES
    r = slot % SUBLANES
    a = jnp.where(grp <= 1, 0, jnp.where(grp == N_CAND_GROUPS - 1, SUBLANES + r, grp - 1))
    b = jnp.where(grp == 0, r, jnp.where(grp == 1, SUBLANES + r,
                                         jnp.where(grp == N_CAND_GROUPS - 1, 0, r)))
    valid = (a + 1) * (b + 1) <= PEER_TOPK
    flat = (a * PEER_TOPK + b).astype(F32)
    return valid, flat


def _candidates(rows0, rows1):
    lo1 = jnp.concatenate(rows1[:SUBLANES], axis=0)
    hi1 = jnp.concatenate(rows1[SUBLANES:], axis=0)
    hi0 = jnp.concatenate(rows0[SUBLANES:], axis=0)
    groups = [rows0[0] + lo1, rows0[0] + hi1]
    groups += [rows0[a] + lo1 for a in range(1, SUBLANES)]
    groups.append(hi0 + rows1[0])
    return jnp.concatenate(groups, axis=0)


def _route_head(q_head, keys_ref, valid, flat):
    scores = [lax.dot_general(keys_ref[p], q_head[:, p * PEER_KEYS:(p + 1) * PEER_KEYS],
                              (((1,), (1,)), ((), ())), preferred_element_type=F32)
              for p in range(2)]
    yield
    vals, idxs = [], []
    for st in scores:
        v, ix = yield from _top16_rows(st)
        vals.append(v)
        idxs.append(ix)
    cand = jnp.where(valid, _candidates(vals[0], vals[1]), -jnp.inf)
    code = _candidates([ix * float(PEER_KEYS) for ix in idxs[0]], idxs[1])
    top_s, top_code = [], []
    tie = None
    for _ in range(PEER_TOPK):
        m = jnp.max(cand, axis=0, keepdims=True)
        if tie is not None:
            m = m + tie
        fmin = jnp.min(jnp.where(cand == m, flat, float(PEER_TOPK * PEER_TOPK)),
                       axis=0, keepdims=True)
        hit = flat == fmin
        top_code.append(jnp.sum(jnp.where(hit, code, 0.0), axis=0, keepdims=True))
        cand = jnp.where(hit, -jnp.inf, cand)
        top_s.append(m)
        tie = yield
    ts = jnp.concatenate(top_s, axis=0)
    e = jnp.exp(ts - top_s[0])
    return (jnp.concatenate(top_code, axis=0).astype(jnp.int32),
            e / jnp.sum(e, axis=0, keepdims=True))


ROUTE_YIELDS = 3 * PEER_TOPK


def _zero_row(x, width):
    bits = lax.bitcast_convert_type(x[:1, :LANES], jnp.uint32)
    zero = lax.bitcast_convert_type((bits >> 16) >> 16, F32)
    return jnp.concatenate([zero] * (width // LANES), axis=1)


def _peer_kernel(h_ref, hn_ref, x_ref, mod_ref, wq_ref, keys_ref, ut_ref, v_ref, g_ref, b_ref,
                 o_ref, q_ref, code_ref, gate_ref, ci_ref, cg_ref, w_ref, acc_ref, hs_ref, us_ref,
                 *, alpha):
    i = pl.program_id(0)
    j = pl.program_id(1)
    nj = pl.num_programs(1)
    tm, d = x_ref.shape
    tn = ut_ref.shape[1]
    nhk = PEER_HEADS * PEER_TOPK
    parts = ROUTE_UNITS // PEER_HEADS
    tp = tm // parts
    slot = i % 2
    valid, flat = _candidate_layout(tp)

    def prepare_queries(h):
        q = jnp.dot(h, wq_ref[...], preferred_element_type=F32).astype(BF16)
        for hd in range(PEER_HEADS):
            q_ref[hd] = q[:, hd * 2 * PEER_KEYS:(hd + 1) * 2 * PEER_KEYS]

    def route_unit(u, dst):
        hd = u // parts
        part = u % parts
        rows = pl.ds(pl.multiple_of(part * tp, tp), tp)
        code, gate = yield from _route_head(q_ref[hd, rows, :], keys_ref, valid, flat)
        krows = pl.ds(pl.multiple_of(hd * PEER_TOPK, PEER_TOPK), PEER_TOPK)
        for g in range(tp // LANES):
            lanes = slice(g * LANES, (g + 1) * LANES)
            code_ref[dst, part * (tp // LANES) + g, krows, :] = code[:, lanes]
            gate_ref[dst, part * (tp // LANES) + g, krows, :] = gate[:, lanes]

    @pl.when((i == 0) & (j == 0))
    def _():
        prepare_queries(h_ref[...])

        def unit(u, carry):
            for _ in route_unit(u, 0):
                pass
            return carry

        lax.fori_loop(0, ROUTE_UNITS, unit, 0)

    @pl.when(j == 0)
    def _():
        for grp in range(tm // LANES):
            code = code_ref[slot, grp].T
            ci_ref[grp * LANES:(grp + 1) * LANES, :] = code // PEER_KEYS
            cg_ref[grp * LANES:(grp + 1) * LANES, :] = gate_ref[slot, grp].T
        key_rows = lax.broadcasted_iota(jnp.int32, (PEER_KEYS, nhk), 0).astype(BF16)
        key_lanes = lax.broadcasted_iota(jnp.int32, (nhk, PEER_KEYS), 1).astype(BF16)
        zero = jnp.zeros((PEER_KEYS, nhk), BF16)
        one = jnp.ones((nhk, PEER_KEYS), BF16)

        def token_group(grp, carry):
            base = pl.multiple_of(grp * LANES, LANES)
            second = (code_ref[slot, grp] % PEER_KEYS).astype(F32)

            def gate_bits(u):
                ri = ci_ref[pl.ds(base + u, 1), :].astype(BF16)
                rg = cg_ref[pl.ds(base + u, 1), :].astype(BF16)
                cj = jnp.broadcast_to(second[:, u:u + 1], (nhk, PEER_KEYS)).astype(BF16)
                pt = jnp.where(key_rows == ri, jnp.broadcast_to(rg, zero.shape), zero)
                qm = jnp.where(key_lanes == cj, one, jnp.zeros_like(one))
                wt = jnp.dot(pt, qm, preferred_element_type=F32)
                return lax.bitcast_convert_type(wt, jnp.uint32)

            for u in range(0, LANES, 2):
                packed = (gate_bits(u) >> 16) | (gate_bits(u + 1) & jnp.uint32(0xFFFF0000))
                pair = (base + u) // 2
                w_ref[pl.ds(pl.multiple_of(pair * W_PITCH, SUBLANES), PEER_KEYS), :] = packed
            return carry

        lax.fori_loop(0, tm // LANES, token_group, 0)
        acc_ref[...] = jnp.zeros(acc_ref.shape, F32)
        hs_ref[...] = h_ref[...]
        prepare_queries(hn_ref[...])

    route = route_unit(j, 1 - slot)
    next(route)
    chunk = 2 * PEER_KEYS
    n_chunks = tn // chunk

    def advance(after, count):
        try:
            route.send(None if after is None else _zero_row(after, tp))
            for _ in range(count - 1):
                next(route)
        except StopIteration:
            pass

    halves = 2
    th = tm // halves
    per_piece = -(-ROUTE_YIELDS // (2 * halves * n_chunks))

    def expert_inputs(half):
        rows = slice(half * th, (half + 1) * th)
        h = hs_ref[rows, :]
        pieces = []
        for r in range(n_chunks):
            c0 = r * chunk
            a = jnp.dot(h, us_ref[:, c0:c0 + chunk], preferred_element_type=F32)
            advance(a, per_piece)
            act = (a * (1.0 + lax.erf(a * (1.0 / math.sqrt(2.0))))).astype(BF16)
            for e in range(chunk // PEER_KEYS):
                first_key = (j * n_chunks + r) * (chunk // PEER_KEYS) + e
                words = w_ref[pl.ds(half * (th // 2) * W_PITCH + first_key, th // 2, stride=W_PITCH), :]
                gates = pltpu.bitcast(words, BF16)
                pieces.append(gates * act[:, e * PEER_KEYS:(e + 1) * PEER_KEYS])
        return jnp.concatenate(pieces, axis=1)

    def expert_outputs(half, z):
        rows = slice(half * th, (half + 1) * th)
        for c in range(d // chunk):
            cols = slice(c * chunk, (c + 1) * chunk)
            zv = jnp.dot(z, vs[:, cols], preferred_element_type=F32)
            acc_ref[rows, cols] += zv
            advance(zv, per_piece)

    us_ref[...] = pltpu.bitcast(ut_ref[...], BF16)
    vs = pltpu.bitcast(v_ref[...], BF16)
    advance(None, per_piece)
    z_prev = expert_inputs(0)
    for half in range(1, halves):
        z_next = expert_inputs(half)
        expert_outputs(half - 1, z_prev)
        z_prev = z_next
    expert_outputs(halves - 1, z_prev)
    for _ in route:
        pass

    @pl.when(j == nj - 1)
    def _():
        _, _, gate = _split_mod(mod_ref[0], d)
        o_ref[...] = _layer_norm(alpha * x_ref[...] + gate * acc_ref[...], g_ref[...], b_ref[...])


def _peer_layer(h, x, mods, mod_row, w_q, sub_keys, u_t, v, ln_g, ln_b, *, seq, alpha, tm=512):
    t, d = x.shape
    n = u_t.shape[1]
    tn = n // ROUTE_UNITS
    nhk = PEER_HEADS * PEER_TOPK
    parts = ROUTE_UNITS // PEER_HEADS
    tiles_per_seq = seq // tm
    last = t // tm - 1
    return pl.pallas_call(
        functools.partial(_peer_kernel, alpha=alpha),
        grid=(t // tm, n // tn),
        in_specs=[
            pl.BlockSpec((tm, d), lambda i, j: (i, 0)),
            pl.BlockSpec((tm, d), lambda i, j: (jnp.minimum(i + 1, last), 0)),
            pl.BlockSpec((tm, d), lambda i, j: (i, 0)),
            pl.BlockSpec((1, 1, 3 * d), lambda i, j: (mod_row + i // tiles_per_seq, 0, 0)),
            _const_spec(w_q.shape),
            _const_spec(sub_keys.shape),
            pl.BlockSpec((d // 2, tn), lambda i, j: (0, j)),
            pl.BlockSpec((tn // 2, d), lambda i, j: (j, 0)),
            _const_spec((1, d)),
            _const_spec((1, d)),
        ],
        out_specs=pl.BlockSpec((tm, d), lambda i, j: (i, 0)),
        out_shape=jax.ShapeDtypeStruct((t, d), F32),
        scratch_shapes=[
            pltpu.VMEM((PEER_HEADS, tm, 2 * PEER_KEYS), BF16),
            pltpu.VMEM((2, tm // LANES, nhk, LANES), jnp.int32),
            pltpu.VMEM((2, tm // LANES, nhk, LANES), F32),
            pltpu.VMEM((tm, nhk), jnp.int32),
            pltpu.VMEM((tm, nhk), F32),
            pltpu.VMEM((tm // 2 * W_PITCH, PEER_KEYS), jnp.uint32),
            pltpu.VMEM((tm, d), F32),
            pltpu.VMEM((tm, d), BF16),
            pltpu.VMEM((d, tn), BF16),
        ],
        compiler_params=_params(("arbitrary", "arbitrary")),
        name="peer_layer",
    )(h, h, x, mods, w_q.astype(BF16), sub_keys.astype(BF16), u_t, v, ln_g, ln_b)


def _pack_kernel(w_ref, o_ref, t_ref, *, transpose, scale):
    w = w_ref[0]
    if scale != 1.0:
        w = w * scale
    if transpose:
        w = w.T
    rows, cols = w.shape
    for c in range(cols // LANES):
        t_ref[c] = w[:, c * LANES:(c + 1) * LANES]
        even = t_ref[c, pl.ds(0, rows // 2, stride=2), :].astype(BF16).astype(F32)
        odd = t_ref[c, pl.ds(1, rows // 2, stride=2), :].astype(BF16).astype(F32)
        o_ref[:, c * LANES:(c + 1) * LANES] = (
            (lax.bitcast_convert_type(even, jnp.uint32) >> 16)
            | (lax.bitcast_convert_type(odd, jnp.uint32) & jnp.uint32(0xFFFF0000)))


def _packed_rows(w, layer, *, transpose, scale=1.0, tb=512):
    _, n, d = w.shape
    if transpose:
        out_shape, out_block, out_map = (d // 2, n), (d // 2, tb), (lambda i: (0, i))
        t_shape = (tb // LANES, d, LANES)
    else:
        out_shape, out_block, out_map = (n // 2, d), (tb // 2, d), (lambda i: (i, 0))
        t_shape = (d // LANES, tb, LANES)
    return pl.pallas_call(
        functools.partial(_pack_kernel, transpose=transpose, scale=scale),
        grid=(n // tb,),
        in_specs=[pl.BlockSpec((1, tb, d), lambda i: (layer, i, 0))],
        out_specs=pl.BlockSpec(out_block, out_map),
        out_shape=jax.ShapeDtypeStruct(out_shape, jnp.uint32),
        scratch_shapes=[pltpu.VMEM(t_shape, F32)],
        compiler_params=_params(("arbitrary",)),
        name="pack_weights",
    )(w)


def _lambda_init(layer_idx):
    return 0.8 - 0.6 * math.exp(-0.3 * layer_idx)


def kernel(x, c, ada_w, ada_b, ln_g, ln_b, conv_w_in, conv_w, conv_w_out, attn_w_qkv, attn_lambda,
           attn_subln_g, attn_w_o, peer_w_q, peer_sub_keys, peer_u, peer_v):
    batch, seq, d = x.shape
    depth = ada_w.shape[0]
    alpha = (2.0 * depth) ** 0.25
    head_dim = d // (2 * ATT_HEADS)
    mods = _ada_mods(c, ada_w, ada_b)
    xt = x.reshape(batch * seq, d)
    for i in range(depth):
        j = i // N_MIXERS
        row = (2 * i) * batch
        prow = (2 * i + 1) * batch
        g0, b0 = ln_g[i, 0].reshape(1, d), ln_b[i, 0].reshape(1, d)
        if i % N_MIXERS == 0:
            xt, h = _conv_layer(xt, mods, row, prow, conv_w_in[j], conv_w[j], conv_w_out[j], g0, b0,
                                seq=seq, alpha=alpha)
        else:
            qt, k, vt = _qkv_proj(xt, mods, row, attn_w_qkv[j], seq=seq, q_scale=head_dim ** -0.5)
            o = _diff_attention(qt, k, vt, attn_lambda[j], attn_subln_g[j], batch=batch, seq=seq,
                                lambda_init=_lambda_init(i))
            xt, h = _out_proj(o, xt, mods, row, prow, attn_w_o[j], g0, b0, seq=seq, alpha=alpha)
        g1, b1 = ln_g[i, 1].reshape(1, d), ln_b[i, 1].reshape(1, d)
        xt = _peer_layer(h, xt, mods, prow, peer_w_q[i], peer_sub_keys[i],
                         _packed_rows(peer_u, i, transpose=True),
                         _packed_rows(peer_v, i, transpose=False, scale=0.5), g1, b1,
                         seq=seq, alpha=alpha)
    return xt.reshape(batch, seq, d)
```

```python
import functools
import math

import jax
import jax.numpy as jnp
from jax import lax
from jax.experimental import pallas as pl
from jax.experimental.pallas import tpu as pltpu

F32 = jnp.float32
BF16 = jnp.bfloat16

N_MIXERS = 2
ATT_HEADS = 8
PEER_HEADS = 8
PEER_KEYS = 128
PEER_TOPK = 16
LN_EPS = 1e-5

LANES = 128
SUBLANES = 8
VMEM_LIMIT_BYTES = 56 * 1024 * 1024
BF16_EXACT_INT = 256
EXP_UNDERFLOW = 106.0
NORM_SLACK = 1.02
FIXED_SHIFT_LIMIT = 60.0

N_CAND_GROUPS = 10
N_CAND = N_CAND_GROUPS * SUBLANES
W_PITCH = 136
ROUTE_UNITS = 16


def _layer_norm(r, g, b):
    mu = jnp.mean(r, axis=-1, keepdims=True)
    d = r - mu
    var = jnp.mean(d * d, axis=-1, keepdims=True)
    return d * lax.rsqrt(var + LN_EPS) * g + b


def _split_mod(mod, d):
    return mod[:, :d], 1.0 + mod[:, d:2 * d], 1.0 + mod[:, 2 * d:]


def _params(sem):
    return pltpu.CompilerParams(dimension_semantics=sem, vmem_limit_bytes=VMEM_LIMIT_BYTES)


def _const_spec(shape):
    nd = len(shape)
    return pl.BlockSpec(shape, lambda *_: (0,) * nd, pipeline_mode=pl.Buffered(1))


def _ada_kernel(c_ref, w_ref, b_ref, o_ref):
    c = c_ref[...]
    sc = c / (1.0 + jnp.exp(-c))
    o_ref[0] = jnp.dot(sc, w_ref[0], precision=lax.Precision.HIGHEST,
                       preferred_element_type=F32) + b_ref[0]


def _ada_mods(c, ada_w, ada_b):
    depth, _, d, d3 = ada_w.shape
    nb = c.shape[0]
    nmat = depth * 2
    tn = 1024
    w = ada_w.reshape(nmat, d, d3)
    b = ada_b.reshape(nmat, 1, d3)
    out = pl.pallas_call(
        _ada_kernel,
        grid=(nmat, d3 // tn),
        in_specs=[
            pl.BlockSpec((nb, d), lambda m, n: (0, 0)),
            pl.BlockSpec((1, d, tn), lambda m, n: (m, 0, n)),
            pl.BlockSpec((1, 1, tn), lambda m, n: (m, 0, n)),
        ],
        out_specs=pl.BlockSpec((1, nb, tn), lambda m, n: (m, 0, n)),
        out_shape=jax.ShapeDtypeStruct((nmat, nb, d3), F32),
        compiler_params=_params(("arbitrary", "arbitrary")),
        name="ada_mods",
    )(c, w, b)
    return out.reshape(nmat * nb, 1, d3)


def _emit_stage_output(r, g_ref, b_ref, modp_ref, o_ref, hp_ref):
    xn = _layer_norm(r, g_ref[...], b_ref[...])
    o_ref[...] = xn
    shift, scale, _ = _split_mod(modp_ref[0], xn.shape[1])
    hp_ref[...] = (xn * scale + shift).astype(BF16)


def _conv_kernel(x_ref, mod_ref, modp_ref, win_ref, cw_ref, wout_ref, g_ref, b_ref, o_ref, hp_ref,
                 zbuf, *, tiles_per_seq, alpha):
    i = pl.program_id(0)
    tm, d = x_ref.shape
    x = x_ref[...]
    shift, scale, gate = _split_mod(mod_ref[0], d)
    h = (x * scale + shift).astype(BF16)
    hw = jnp.dot(h, win_ref[...], preferred_element_type=F32)
    gb = hw[:, :d]
    z = hw[:, d:2 * d] * hw[:, 2 * d:]

    @pl.when(i % tiles_per_seq == 0)
    def _():
        zbuf[0:SUBLANES, :] = jnp.zeros((SUBLANES, d), F32)

    zbuf[SUBLANES:, :] = z
    z1 = zbuf[pl.ds(SUBLANES - 1, tm), :]
    z2 = zbuf[pl.ds(SUBLANES - 2, tm), :]
    cw = cw_ref[...]
    zc = cw[2:3] * z + cw[1:2] * z1 + cw[0:1] * z2
    zbuf[0:SUBLANES, :] = zbuf[tm:tm + SUBLANES, :]
    y = jnp.dot((gb * zc).astype(BF16), wout_ref[...], preferred_element_type=F32)
    _emit_stage_output(alpha * x + gate * y, g_ref, b_ref, modp_ref, o_ref, hp_ref)


def _conv_layer(x, mods, mod_row, peer_row, w_in, conv_w, w_out, ln_g, ln_b, *, seq, alpha, tm=512):
    t, d = x.shape
    tiles_per_seq = seq // tm
    return pl.pallas_call(
        functools.partial(_conv_kernel, tiles_per_seq=tiles_per_seq, alpha=alpha),
        grid=(t // tm,),
        in_specs=[
            pl.BlockSpec((tm, d), lambda i: (i, 0)),
            pl.BlockSpec((1, 1, 3 * d), lambda i: (mod_row + i // tiles_per_seq, 0, 0)),
            pl.BlockSpec((1, 1, 3 * d), lambda i: (peer_row + i // tiles_per_seq, 0, 0)),
            _const_spec((d, 3 * d)),
            _const_spec((conv_w.shape[0], d)),
            _const_spec((d, d)),
            _const_spec((1, d)),
            _const_spec((1, d)),
        ],
        out_specs=[pl.BlockSpec((tm, d), lambda i: (i, 0))] * 2,
        out_shape=[jax.ShapeDtypeStruct((t, d), F32), jax.ShapeDtypeStruct((t, d), BF16)],
        scratch_shapes=[pltpu.VMEM((tm + SUBLANES, d), F32)],
        compiler_params=_params(("arbitrary",)),
        name="conv_mixer",
    )(x, mods, mods, w_in.astype(BF16), conv_w, w_out.astype(BF16), ln_g, ln_b)


def _qkv_kernel(x_ref, mod_ref, w_ref, qt_ref, k_ref, vt_ref, *, q_scale):
    tm, d = x_ref.shape
    shift, scale, _ = _split_mod(mod_ref[0], d)
    h = (x_ref[...] * scale + shift).astype(BF16)
    qkv = jnp.dot(h, w_ref[...], preferred_element_type=F32)
    qt_ref[...] = (qkv[:, :d] * q_scale).T.astype(BF16)
    k_ref[...] = qkv[:, d:2 * d].astype(BF16)
    vt_ref[...] = qkv[:, 2 * d:].T.astype(BF16)


def _qkv_proj(x, mods, mod_row, w_qkv, *, seq, q_scale, tm=512):
    t, d = x.shape
    tiles_per_seq = seq // tm
    return pl.pallas_call(
        functools.partial(_qkv_kernel, q_scale=q_scale),
        grid=(t // tm,),
        in_specs=[
            pl.BlockSpec((tm, d), lambda i: (i, 0)),
            pl.BlockSpec((1, 1, 3 * d), lambda i: (mod_row + i // tiles_per_seq, 0, 0)),
            _const_spec((d, 3 * d)),
        ],
        out_specs=[
            pl.BlockSpec((d, tm), lambda i: (0, i)),
            pl.BlockSpec((tm, d), lambda i: (i, 0)),
            pl.BlockSpec((d, tm), lambda i: (0, i)),
        ],
        out_shape=[
            jax.ShapeDtypeStruct((d, t), BF16),
            jax.ShapeDtypeStruct((t, d), BF16),
            jax.ShapeDtypeStruct((d, t), BF16),
        ],
        compiler_params=_params(("arbitrary",)),
        name="attn_qkv",
    )(x, mods, w_qkv.astype(BF16))


def _attn_kernel(slopes_ref, qt_ref, k_ref, vt_ref, lam_ref, g_ref, o_ref, qa_ref, m_ref, acc_ref,
                 kn_ref, *, tq, tk, heads, lambda_init):
    hg = pl.program_id(1)
    qi = pl.program_id(2)
    dh2 = qt_ref.shape[0] // heads
    dh = dh2 // 2
    ones_rows = 2 * SUBLANES
    blocks_per_tile = tq // tk

    feat = lax.broadcasted_iota(jnp.int32, (dh2, tq), 0)
    arow = lax.broadcasted_iota(jnp.int32, (dh2, 2 * tq), 0)
    acol = lax.broadcasted_iota(jnp.int32, (dh2, 2 * tq), 1)
    r = jnp.where(acol >= tq, acol - tq, acol)
    r_lo = (r % BF16_EXACT_INT).astype(F32)
    r_hi = (r - r % BF16_EXACT_INT).astype(F32)
    for g in range(heads):
        slope = slopes_ref[hg * heads + g]
        qt = qt_ref[g * dh2:(g + 1) * dh2, :]
        zero = jnp.zeros_like(qt)
        qa_ref[g, :dh2, :] = jnp.concatenate(
            [jnp.where(feat < dh, qt, zero), jnp.where(feat >= dh, qt, zero)], axis=1)
        qa_ref[g, dh2:, :] = jnp.where(
            arow == 0, -slope * r_lo,
            jnp.where(arow == 1, -slope * r_hi, jnp.where(arow == 2, slope, 0.0))).astype(BF16)
    kcol = lax.broadcasted_iota(jnp.int32, (tk, dh2), 1)
    krow = lax.broadcasted_iota(jnp.int32, (tk, dh2), 0).astype(F32)
    k_extra = jnp.where(kcol <= 1, 1.0, jnp.where(kcol == 2, krow, 0.0)).astype(BF16)
    ones = jnp.ones((ones_rows, tk), BF16)

    m_ref[...] = jnp.full(m_ref.shape, -jnp.inf, F32)
    acc_ref[...] = jnp.zeros(acc_ref.shape, F32)

    @pl.when(qi == 0)
    def _():
        ones_sq = jnp.ones((dh2, LANES), BF16)
        for g in range(heads):
            def chunk(c, best, g=g):
                kc = k_ref[pl.ds(pl.multiple_of(c * tq, tq), tq), g * dh2:(g + 1) * dh2].astype(F32)
                rows = jnp.dot((kc * kc).astype(BF16), ones_sq, preferred_element_type=F32)
                return jnp.maximum(best, jnp.max(rows, axis=0, keepdims=True))
            kn_ref[g] = lax.fori_loop(0, k_ref.shape[0] // tq, chunk, jnp.zeros((1, LANES), F32))

    def step(j, diagonal):
        start = pl.multiple_of(j * tk, tk)
        delta = qi * tq - j * tk
        head_cols = [slice(g * dh2, (g + 1) * dh2) for g in range(heads)]
        scores, maxima = [], []
        for g, cols in enumerate(head_cols):
            s = jnp.dot(jnp.concatenate([k_ref[pl.ds(start, tk), cols], k_extra], axis=1),
                        qa_ref[g], preferred_element_type=F32)
            if diagonal:
                mrow = lax.broadcasted_iota(jnp.int32, (tk, 2 * tq), 0)
                mcol = lax.broadcasted_iota(jnp.int32, (tk, 2 * tq), 1)
                mq = jnp.where(mcol >= tq, mcol - tq, mcol)
                s = jnp.where(mrow - mq <= delta, s, -jnp.inf)
            scores.append(s)
            maxima.append(jnp.max(s, axis=0, keepdims=True))
        probs, corrs = [], []
        for g, s in enumerate(scores):
            off = delta.astype(F32) * slopes_ref[hg * heads + g]
            m_old = m_ref[g]
            m_new = jnp.maximum(m_old, maxima[g] - off)
            probs.append(jnp.exp(s - (m_new + off)).astype(BF16))
            corrs.append(jnp.exp(m_old - m_new))
            m_ref[g] = m_new
        for g, cols in enumerate(head_cols):
            v_aug = jnp.concatenate([vt_ref[cols, pl.ds(start, tk)], ones], axis=0)
            acc_ref[g] = corrs[g] * acc_ref[g] + jnp.dot(v_aug, probs[g],
                                                         preferred_element_type=F32)

    first_diag = qi * blocks_per_tile
    for d in range(blocks_per_tile):
        step(first_diag + d, True)

    needed = jnp.zeros((1, 2 * tq), F32)
    bound = jnp.zeros((1, 2 * tq), F32)
    for g in range(heads):
        qf = qa_ref[g, :dh2, :].astype(F32)
        q_norm = jnp.sqrt(jnp.sum(qf * qf, axis=0, keepdims=True))
        k_norm = jnp.sqrt(kn_ref[g][:, :1])
        qk = q_norm * k_norm * NORM_SLACK
        reach = (qk - m_ref[g] + EXP_UNDERFLOW) / slopes_ref[hg * heads + g]
        needed = jnp.maximum(needed, jnp.floor((reach - 1.0) / tk) + 1.0)
        bound = jnp.maximum(bound, qk)
    needed = jnp.clip(needed, 0.0, first_diag.astype(F32))
    n_blocks = jnp.max(needed.astype(jnp.int32))

    fixed_shift_ok = jnp.max(bound) * 2.0 < FIXED_SHIFT_LIMIT

    def fixed_shift_step(j):
        start = pl.multiple_of(j * tk, tk)
        delta = qi * tq - j * tk
        scores = [jnp.dot(jnp.concatenate([k_ref[pl.ds(start, tk), g * dh2:(g + 1) * dh2], k_extra],
                                          axis=1), qa_ref[g], preferred_element_type=F32)
                  for g in range(heads)]
        probs = [jnp.exp(s - (m_ref[g] + delta.astype(F32) * slopes_ref[hg * heads + g])).astype(BF16)
                 for g, s in enumerate(scores)]
        for g in range(heads):
            v_aug = jnp.concatenate([vt_ref[g * dh2:(g + 1) * dh2, pl.ds(start, tk)], ones], axis=0)
            acc_ref[g] += jnp.dot(v_aug, probs[g], preferred_element_type=F32)

    @pl.when(fixed_shift_ok)
    def _():
        def body(i, carry):
            fixed_shift_step(first_diag - 1 - i)
            return carry

        lax.fori_loop(0, n_blocks, body, 0)

    @pl.when(jnp.logical_not(fixed_shift_ok))
    def _():
        def body(i, carry):
            step(first_diag - 1 - i, False)
            return carry

        lax.fori_loop(0, n_blocks, body, 0)

    lam = lam_ref[...]
    lam_full = (jnp.exp(jnp.sum(lam[0:1] * lam[1:2], axis=1, keepdims=True))
                - jnp.exp(jnp.sum(lam[2:3] * lam[3:4], axis=1, keepdims=True)) + lambda_init)
    for g in range(heads):
        acc = acc_ref[g]
        on = acc[:dh2] / acc[dh2:dh2 + 1]
        o = on[:, :tq] - lam_full * on[:, tq:]
        o = o * lax.rsqrt(jnp.mean(o * o, axis=0, keepdims=True) + LN_EPS) * g_ref[...]
        o_ref[:, g * dh2:(g + 1) * dh2] = (o * (1.0 - lambda_init)).T.astype(o_ref.dtype)


def _diff_attention(qt, k, vt, lam, subln_g, *, batch, seq, lambda_init, tq=512, tk=256, heads=4):
    d, t = qt.shape
    dh2 = d // ATT_HEADS
    gw = heads * dh2
    nq = seq // tq
    assert tq % tk == 0 and tk <= BF16_EXACT_INT and tq < BF16_EXACT_INT * BF16_EXACT_INT
    slopes = 2.0 ** (-(8.0 / ATT_HEADS) * jnp.arange(1, ATT_HEADS + 1, dtype=F32))
    return pl.pallas_call(
        functools.partial(_attn_kernel, tq=tq, tk=tk, heads=heads, lambda_init=lambda_init),
        grid=(batch, ATT_HEADS // heads, nq),
        in_specs=[
            pl.BlockSpec(memory_space=pltpu.SMEM),
            pl.BlockSpec((gw, tq), lambda b, h, i: (h, b * nq + i)),
            pl.BlockSpec((seq, gw), lambda b, h, i: (b, h)),
            pl.BlockSpec((gw, seq), lambda b, h, i: (h, b)),
            pl.BlockSpec(lam.shape, lambda b, h, i: (0, 0)),
            pl.BlockSpec((dh2, 1), lambda b, h, i: (0, 0)),
        ],
        out_specs=pl.BlockSpec((tq, gw), lambda b, h, i: (b * nq + i, h)),
        out_shape=jax.ShapeDtypeStruct((t, d), BF16),
        scratch_shapes=[
            pltpu.VMEM((heads, 2 * dh2, 2 * tq), BF16),
            pltpu.VMEM((heads, 1, 2 * tq), F32),
            pltpu.VMEM((heads, dh2 + 2 * SUBLANES, 2 * tq), F32),
            pltpu.VMEM((heads, 1, LANES), F32),
        ],
        compiler_params=_params(("arbitrary", "arbitrary", "arbitrary")),
        name="diff_attention",
    )(slopes, qt, k, vt, lam, subln_g.reshape(dh2, 1))


def _oproj_kernel(o_ref, x_ref, mod_ref, modp_ref, w_ref, g_ref, b_ref, out_ref, hp_ref, *, alpha):
    tm, d = x_ref.shape
    _, _, gate = _split_mod(mod_ref[0], d)
    y = jnp.dot(o_ref[...], w_ref[...], preferred_element_type=F32)
    _emit_stage_output(alpha * x_ref[...] + gate * y, g_ref, b_ref, modp_ref, out_ref, hp_ref)


def _out_proj(o, x, mods, mod_row, peer_row, w_o, ln_g, ln_b, *, seq, alpha, tm=512):
    t, d = x.shape
    tiles_per_seq = seq // tm
    return pl.pallas_call(
        functools.partial(_oproj_kernel, alpha=alpha),
        grid=(t // tm,),
        in_specs=[
            pl.BlockSpec((tm, d), lambda i: (i, 0)),
            pl.BlockSpec((tm, d), lambda i: (i, 0)),
            pl.BlockSpec((1, 1, 3 * d), lambda i: (mod_row + i // tiles_per_seq, 0, 0)),
            pl.BlockSpec((1, 1, 3 * d), lambda i: (peer_row + i // tiles_per_seq, 0, 0)),
            _const_spec((d, d)),
            _const_spec((1, d)),
            _const_spec((1, d)),
        ],
        out_specs=[pl.BlockSpec((tm, d), lambda i: (i, 0))] * 2,
        out_shape=[jax.ShapeDtypeStruct((t, d), F32), jax.ShapeDtypeStruct((t, d), BF16)],
        compiler_params=_params(("arbitrary",)),
        name="attn_out_proj",
    )(o, x, mods, mods, w_o.astype(BF16), ln_g, ln_b)


def _top16_rows(s):
    n, tm = s.shape
    rows = lax.broadcasted_iota(jnp.int32, (n, tm), 0).astype(F32)
    vals, idxs = [], []
    tie = None
    for _ in range(PEER_TOPK):
        m = jnp.max(s, axis=0, keepdims=True)
        if tie is not None:
            m = m + tie
        idx = jnp.min(jnp.where(s == m, rows, float(n)), axis=0, keepdims=True)
        s = jnp.where(rows == idx, -jnp.inf, s)
        vals.append(m)
        idxs.append(idx)
        tie = yield
    return vals, idxs


def _candidate_layout(tm):
    slot = lax.broadcasted_iota(jnp.int32, (N_CAND, tm), 0)
    grp = slot // SUBLANES
    r = slot % SUBLANES
    a = jnp.where(grp <= 1, 0, jnp.where(grp == N_CAND_GROUPS - 1, SUBLANES + r, grp - 1))
    b = jnp.where(grp == 0, r, jnp.where(grp == 1, SUBLANES + r,
                                         jnp.where(grp == N_CAND_GROUPS - 1, 0, r)))
    valid = (a + 1) * (b + 1) <= PEER_TOPK
    flat = (a * PEER_TOPK + b).astype(F32)
    return valid, flat


def _candidates(rows0, rows1):
    lo1 = jnp.concatenate(rows1[:SUBLANES], axis=0)
    hi1 = jnp.concatenate(rows1[SUBLANES:], axis=0)
    hi0 = jnp.concatenate(rows0[SUBLANES:], axis=0)
    groups = [rows0[0] + lo1, rows0[0] + hi1]
    groups += [rows0[a] + lo1 for a in range(1, SUBLANES)]
    groups.append(hi0 + rows1[0])
    return jnp.concatenate(groups, axis=0)


def _route_head(q_head, keys_ref, valid, flat):
    scores = [lax.dot_general(keys_ref[p], q_head[:, p * PEER_KEYS:(p + 1) * PEER_KEYS],
                              (((1,), (1,)), ((), ())), preferred_element_type=F32)
              for p in range(2)]
    yield
    vals, idxs = [], []
    for st in scores:
        v, ix = yield from _top16_rows(st)
        vals.append(v)
        idxs.append(ix)
    cand = jnp.where(valid, _candidates(vals[0], vals[1]), -jnp.inf)
    code = _candidates([ix * float(PEER_KEYS) for ix in idxs[0]], idxs[1])
    top_s, top_code = [], []
    tie = None
    for _ in range(PEER_TOPK):
        m = jnp.max(cand, axis=0, keepdims=True)
        if tie is not None:
            m = m + tie
        fmin = jnp.min(jnp.where(cand == m, flat, float(PEER_TOPK * PEER_TOPK)),
                       axis=0, keepdims=True)
        hit = flat == fmin
        top_code.append(jnp.sum(jnp.where(hit, code, 0.0), axis=0, keepdims=True))
        cand = jnp.where(hit, -jnp.inf, cand)
        top_s.append(m)
        tie = yield
    ts = jnp.concatenate(top_s, axis=0)
    e = jnp.exp(ts - top_s[0])
    return (jnp.concatenate(top_code, axis=0).astype(jnp.int32),
            e / jnp.sum(e, axis=0, keepdims=True))


ROUTE_YIELDS = 3 * PEER_TOPK


def _zero_row(x, width):
    bits = lax.bitcast_convert_type(x[:1, :LANES], jnp.uint32)
    zero = lax.bitcast_convert_type((bits >> 16) >> 16, F32)
    return jnp.concatenate([zero] * (width // LANES), axis=1)


def _peer_kernel(h_ref, hn_ref, x_ref, mod_ref, wq_ref, keys_ref, ut_ref, v_ref, g_ref, b_ref,
                 o_ref, q_ref, code_ref, gate_ref, ci_ref, cg_ref, w_ref, acc_ref, hs_ref,
                 *, alpha):
    i = pl.program_id(0)
    j = pl.program_id(1)
    nj = pl.num_programs(1)
    tm, d = x_ref.shape
    tn = ut_ref.shape[1]
    nhk = PEER_HEADS * PEER_TOPK
    parts = ROUTE_UNITS // PEER_HEADS
    tp = tm // parts
    slot = i % 2
    valid, flat = _candidate_layout(tp)

    def prepare_queries(h):
        q = jnp.dot(h, wq_ref[...], preferred_element_type=F32).astype(BF16)
        for hd in range(PEER_HEADS):
            q_ref[hd] = q[:, hd * 2 * PEER_KEYS:(hd + 1) * 2 * PEER_KEYS]

    def route_unit(u, dst):
        hd = u // parts
        part = u % parts
        rows = pl.ds(pl.multiple_of(part * tp, tp), tp)
        code, gate = yield from _route_head(q_ref[hd, rows, :], keys_ref, valid, flat)
        krows = pl.ds(pl.multiple_of(hd * PEER_TOPK, PEER_TOPK), PEER_TOPK)
        for g in range(tp // LANES):
            lanes = slice(g * LANES, (g + 1) * LANES)
            code_ref[dst, part * (tp // LANES) + g, krows, :] = code[:, lanes]
            gate_ref[dst, part * (tp // LANES) + g, krows, :] = gate[:, lanes]

    @pl.when((i == 0) & (j == 0))
    def _():
        prepare_queries(h_ref[...])

        def unit(u, carry):
            for _ in route_unit(u, 0):
                pass
            return carry

        lax.fori_loop(0, ROUTE_UNITS, unit, 0)

    @pl.when(j == 0)
    def _():
        for grp in range(tm // LANES):
            code = code_ref[slot, grp].T
            ci_ref[grp * LANES:(grp + 1) * LANES, :] = code // PEER_KEYS
            cg_ref[grp * LANES:(grp + 1) * LANES, :] = gate_ref[slot, grp].T
        key_rows = lax.broadcasted_iota(jnp.int32, (PEER_KEYS, nhk), 0).astype(BF16)
        key_lanes = lax.broadcasted_iota(jnp.int32, (nhk, PEER_KEYS), 1).astype(BF16)
        zero = jnp.zeros((PEER_KEYS, nhk), BF16)
        one = jnp.ones((nhk, PEER_KEYS), BF16)

        def token_group(grp, carry):
            base = pl.multiple_of(grp * LANES, LANES)
            second = (code_ref[slot, grp] % PEER_KEYS).astype(F32)

            def gate_bits(u):
                ri = ci_ref[pl.ds(base + u, 1), :].astype(BF16)
                rg = cg_ref[pl.ds(base + u, 1), :].astype(BF16)
                cj = jnp.broadcast_to(second[:, u:u + 1], (nhk, PEER_KEYS)).astype(BF16)
                pt = jnp.where(key_rows == ri, jnp.broadcast_to(rg, zero.shape), zero)
                qm = jnp.where(key_lanes == cj, one, jnp.zeros_like(one))
                wt = jnp.dot(pt, qm, preferred_element_type=F32)
                return lax.bitcast_convert_type(wt, jnp.uint32)

            for u in range(0, LANES, 2):
                packed = (gate_bits(u) >> 16) | (gate_bits(u + 1) & jnp.uint32(0xFFFF0000))
                pair = (base + u) // 2
                w_ref[pl.ds(pl.multiple_of(pair * W_PITCH, SUBLANES), PEER_KEYS), :] = packed
            return carry

        lax.fori_loop(0, tm // LANES, token_group, 0)
        acc_ref[...] = jnp.zeros(acc_ref.shape, F32)
        hs_ref[...] = h_ref[...]
        prepare_queries(hn_ref[...])

    route = route_unit(j, 1 - slot)
    next(route)
    chunk = 2 * PEER_KEYS
    n_chunks = tn // chunk

    def advance(after, count):
        try:
            route.send(None if after is None else _zero_row(after, tp))
            for _ in range(count - 1):
                next(route)
        except StopIteration:
            pass

    halves = 2
    th = tm // halves
    per_piece = -(-ROUTE_YIELDS // (2 * halves * n_chunks))

    def expert_inputs(half):
        rows = slice(half * th, (half + 1) * th)
        h = hs_ref[rows, :]
        pieces = []
        for r in range(n_chunks):
            c0 = r * chunk
            a = jnp.dot(h, us[:, c0:c0 + chunk], preferred_element_type=F32)
            advance(a, per_piece)
            act = (a * (1.0 + lax.erf(a * (1.0 / math.sqrt(2.0))))).astype(BF16)
            for e in range(chunk // PEER_KEYS):
                first_key = (j * n_chunks + r) * (chunk // PEER_KEYS) + e
                words = w_ref[pl.ds(half * (th // 2) * W_PITCH + first_key, th // 2, stride=W_PITCH), :]
                gates = pltpu.bitcast(words, BF16)
                pieces.append(gates * act[:, e * PEER_KEYS:(e + 1) * PEER_KEYS])
        return jnp.concatenate(pieces, axis=1)

    def expert_outputs(half, z):
        rows = slice(half * th, (half + 1) * th)
        for c in range(d // chunk):
            cols = slice(c * chunk, (c + 1) * chunk)
            zv = jnp.dot(z, vs[:, cols], preferred_element_type=F32)
            acc_ref[rows, cols] += zv
            advance(zv, per_piece)

    us = pltpu.bitcast(ut_ref[...], BF16)
    vs = pltpu.bitcast(v_ref[...], BF16)
    advance(None, per_piece)
    z_prev = expert_inputs(0)
    for half in range(1, halves):
        z_next = expert_inputs(half)
        expert_outputs(half - 1, z_prev)
        z_prev = z_next
    expert_outputs(halves - 1, z_prev)
    for _ in route:
        pass

    @pl.when(j == nj - 1)
    def _():
        _, _, gate = _split_mod(mod_ref[0], d)
        o_ref[...] = _layer_norm(alpha * x_ref[...] + gate * acc_ref[...], g_ref[...], b_ref[...])


def _peer_layer(h, x, mods, mod_row, w_q, sub_keys, u_t, v, ln_g, ln_b, *, seq, alpha, tm=512):
    t, d = x.shape
    n = u_t.shape[1]
    tn = n // ROUTE_UNITS
    nhk = PEER_HEADS * PEER_TOPK
    parts = ROUTE_UNITS // PEER_HEADS
    tiles_per_seq = seq // tm
    last = t // tm - 1
    return pl.pallas_call(
        functools.partial(_peer_kernel, alpha=alpha),
        grid=(t // tm, n // tn),
        in_specs=[
            pl.BlockSpec((tm, d), lambda i, j: (i, 0)),
            pl.BlockSpec((tm, d), lambda i, j: (jnp.minimum(i + 1, last), 0)),
            pl.BlockSpec((tm, d), lambda i, j: (i, 0)),
            pl.BlockSpec((1, 1, 3 * d), lambda i, j: (mod_row + i // tiles_per_seq, 0, 0)),
            _const_spec(w_q.shape),
            _const_spec(sub_keys.shape),
            pl.BlockSpec((d // 2, tn), lambda i, j: (0, j)),
            pl.BlockSpec((tn // 2, d), lambda i, j: (j, 0)),
            _const_spec((1, d)),
            _const_spec((1, d)),
        ],
        out_specs=pl.BlockSpec((tm, d), lambda i, j: (i, 0)),
        out_shape=jax.ShapeDtypeStruct((t, d), F32),
        scratch_shapes=[
            pltpu.VMEM((PEER_HEADS, tm, 2 * PEER_KEYS), BF16),
            pltpu.VMEM((2, tm // LANES, nhk, LANES), jnp.int32),
            pltpu.VMEM((2, tm // LANES, nhk, LANES), F32),
            pltpu.VMEM((tm, nhk), jnp.int32),
            pltpu.VMEM((tm, nhk), F32),
            pltpu.VMEM((tm // 2 * W_PITCH, PEER_KEYS), jnp.uint32),
            pltpu.VMEM((tm, d), F32),
            pltpu.VMEM((tm, d), BF16),
        ],
        compiler_params=_params(("arbitrary", "arbitrary")),
        name="peer_layer",
    )(h, h, x, mods, w_q.astype(BF16), sub_keys.astype(BF16), u_t, v, ln_g, ln_b)


def _pack_kernel(w_ref, o_ref, t_ref, *, transpose, scale):
    w = w_ref[0]
    if scale != 1.0:
        w = w * scale
    if transpose:
        w = w.T
    rows, cols = w.shape
    for c in range(cols // LANES):
        t_ref[c] = w[:, c * LANES:(c + 1) * LANES]
        even = t_ref[c, pl.ds(0, rows // 2, stride=2), :].astype(BF16).astype(F32)
        odd = t_ref[c, pl.ds(1, rows // 2, stride=2), :].astype(BF16).astype(F32)
        o_ref[:, c * LANES:(c + 1) * LANES] = (
            (lax.bitcast_convert_type(even, jnp.uint32) >> 16)
            | (lax.bitcast_convert_type(odd, jnp.uint32) & jnp.uint32(0xFFFF0000)))


def _packed_rows(w, layer, *, transpose, scale=1.0, tb=512):
    _, n, d = w.shape
    if transpose:
        out_shape, out_block, out_map = (d // 2, n), (d // 2, tb), (lambda i: (0, i))
        t_shape = (tb // LANES, d, LANES)
    else:
        out_shape, out_block, out_map = (n // 2, d), (tb // 2, d), (lambda i: (i, 0))
        t_shape = (d // LANES, tb, LANES)
    return pl.pallas_call(
        functools.partial(_pack_kernel, transpose=transpose, scale=scale),
        grid=(n // tb,),
        in_specs=[pl.BlockSpec((1, tb, d), lambda i: (layer, i, 0))],
        out_specs=pl.BlockSpec(out_block, out_map),
        out_shape=jax.ShapeDtypeStruct(out_shape, jnp.uint32),
        scratch_shapes=[pltpu.VMEM(t_shape, F32)],
        compiler_params=_params(("arbitrary",)),
        name="pack_weights",
    )(w)


def _lambda_init(layer_idx):
    return 0.8 - 0.6 * math.exp(-0.3 * layer_idx)


def kernel(x, c, ada_w, ada_b, ln_g, ln_b, conv_w_in, conv_w, conv_w_out, attn_w_qkv, attn_lambda,
           attn_subln_g, attn_w_o, peer_w_q, peer_sub_keys, peer_u, peer_v):
    batch, seq, d = x.shape
    depth = ada_w.shape[0]
    alpha = (2.0 * depth) ** 0.25
    head_dim = d // (2 * ATT_HEADS)
    mods = _ada_mods(c, ada_w, ada_b)
    xt = x.reshape(batch * seq, d)
    for i in range(depth):
        j = i // N_MIXERS
        row = (2 * i) * batch
        prow = (2 * i + 1) * batch
        g0, b0 = ln_g[i, 0].reshape(1, d), ln_b[i, 0].reshape(1, d)
        if i % N_MIXERS == 0:
            xt, h = _conv_layer(xt, mods, row, prow, conv_w_in[j], conv_w[j], conv_w_out[j], g0, b0,
                                seq=seq, alpha=alpha)
        else:
            qt, k, vt = _qkv_proj(xt, mods, row, attn_w_qkv[j], seq=seq, q_scale=head_dim ** -0.5)
            o = _diff_attention(qt, k, vt, attn_lambda[j], attn_subln_g[j], batch=batch, seq=seq,
                                lambda_init=_lambda_init(i))
            xt, h = _out_proj(o, xt, mods, row, prow, attn_w_o[j], g0, b0, seq=seq, alpha=alpha)
        g1, b1 = ln_g[i, 1].reshape(1, d), ln_b[i, 1].reshape(1, d)
        xt = _peer_layer(h, xt, mods, prow, peer_w_q[i], peer_sub_keys[i],
                         _packed_rows(peer_u, i, transpose=True),
                         _packed_rows(peer_v, i, transpose=False, scale=0.5), g1, b1,
                         seq=seq, alpha=alpha)
    return xt.reshape(batch, seq, d)
```

```python
import functools
import math

import jax
import jax.numpy as jnp
from jax import lax
from jax.experimental import pallas as pl
from jax.experimental.pallas import tpu as pltpu

F32 = jnp.float32
BF16 = jnp.bfloat16

N_MIXERS = 2
ATT_HEADS = 8
PEER_HEADS = 8
PEER_KEYS = 128
PEER_TOPK = 16
LN_EPS = 1e-5

LANES = 128
SUBLANES = 8
VMEM_LIMIT_BYTES = 56 * 1024 * 1024
BF16_EXACT_INT = 256
EXP_UNDERFLOW = 106.0
NORM_SLACK = 1.02
FIXED_SHIFT_LIMIT = 60.0

N_CAND_GROUPS = 10
N_CAND = N_CAND_GROUPS * SUBLANES
W_PITCH = 136
ROUTE_UNITS = 16


def _layer_norm(r, g, b):
    mu = jnp.mean(r, axis=-1, keepdims=True)
    d = r - mu
    var = jnp.mean(d * d, axis=-1, keepdims=True)
    return d * lax.rsqrt(var + LN_EPS) * g + b


def _split_mod(mod, d):
    return mod[:, :d], 1.0 + mod[:, d:2 * d], 1.0 + mod[:, 2 * d:]


def _params(sem):
    return pltpu.CompilerParams(dimension_semantics=sem, vmem_limit_bytes=VMEM_LIMIT_BYTES)


def _const_spec(shape):
    nd = len(shape)
    return pl.BlockSpec(shape, lambda *_: (0,) * nd, pipeline_mode=pl.Buffered(1))


def _ada_kernel(c_ref, w_ref, b_ref, o_ref):
    c = c_ref[...]
    sc = c / (1.0 + jnp.exp(-c))
    o_ref[0] = jnp.dot(sc, w_ref[0], precision=lax.Precision.HIGHEST,
                       preferred_element_type=F32) + b_ref[0]


def _ada_mods(c, ada_w, ada_b):
    depth, _, d, d3 = ada_w.shape
    nb = c.shape[0]
    nmat = depth * 2
    tn = 1024
    w = ada_w.reshape(nmat, d, d3)
    b = ada_b.reshape(nmat, 1, d3)
    out = pl.pallas_call(
        _ada_kernel,
        grid=(nmat, d3 // tn),
        in_specs=[
            pl.BlockSpec((nb, d), lambda m, n: (0, 0)),
            pl.BlockSpec((1, d, tn), lambda m, n: (m, 0, n)),
            pl.BlockSpec((1, 1, tn), lambda m, n: (m, 0, n)),
        ],
        out_specs=pl.BlockSpec((1, nb, tn), lambda m, n: (m, 0, n)),
        out_shape=jax.ShapeDtypeStruct((nmat, nb, d3), F32),
        compiler_params=_params(("arbitrary", "arbitrary")),
        name="ada_mods",
    )(c, w, b)
    return out.reshape(nmat * nb, 1, d3)


def _emit_stage_output(r, g_ref, b_ref, modp_ref, o_ref, hp_ref):
    xn = _layer_norm(r, g_ref[...], b_ref[...])
    o_ref[...] = xn
    shift, scale, _ = _split_mod(modp_ref[0], xn.shape[1])
    hp_ref[...] = (xn * scale + shift).astype(BF16)


def _conv_kernel(x_ref, mod_ref, modp_ref, win_ref, cw_ref, wout_ref, g_ref, b_ref, o_ref, hp_ref,
                 zbuf, *, tiles_per_seq, alpha):
    i = pl.program_id(0)
    tm, d = x_ref.shape
    x = x_ref[...]
    shift, scale, gate = _split_mod(mod_ref[0], d)
    h = (x * scale + shift).astype(BF16)
    hw = jnp.dot(h, win_ref[...], preferred_element_type=F32)
    gb = hw[:, :d]
    z = hw[:, d:2 * d] * hw[:, 2 * d:]

    @pl.when(i % tiles_per_seq == 0)
    def _():
        zbuf[0:SUBLANES, :] = jnp.zeros((SUBLANES, d), F32)

    zbuf[SUBLANES:, :] = z
    z1 = zbuf[pl.ds(SUBLANES - 1, tm), :]
    z2 = zbuf[pl.ds(SUBLANES - 2, tm), :]
    cw = cw_ref[...]
    zc = cw[2:3] * z + cw[1:2] * z1 + cw[0:1] * z2
    zbuf[0:SUBLANES, :] = zbuf[tm:tm + SUBLANES, :]
    y = jnp.dot((gb * zc).astype(BF16), wout_ref[...], preferred_element_type=F32)
    _emit_stage_output(alpha * x + gate * y, g_ref, b_ref, modp_ref, o_ref, hp_ref)


def _conv_layer(x, mods, mod_row, peer_row, w_in, conv_w, w_out, ln_g, ln_b, *, seq, alpha, tm=512):
    t, d = x.shape
    tiles_per_seq = seq // tm
    return pl.pallas_call(
        functools.partial(_conv_kernel, tiles_per_seq=tiles_per_seq, alpha=alpha),
        grid=(t // tm,),
        in_specs=[
            pl.BlockSpec((tm, d), lambda i: (i, 0)),
            pl.BlockSpec((1, 1, 3 * d), lambda i: (mod_row + i // tiles_per_seq, 0, 0)),
            pl.BlockSpec((1, 1, 3 * d), lambda i: (peer_row + i // tiles_per_seq, 0, 0)),
            _const_spec((d, 3 * d)),
            _const_spec((conv_w.shape[0], d)),
            _const_spec((d, d)),
            _const_spec((1, d)),
            _const_spec((1, d)),
        ],
        out_specs=[pl.BlockSpec((tm, d), lambda i: (i, 0))] * 2,
        out_shape=[jax.ShapeDtypeStruct((t, d), F32), jax.ShapeDtypeStruct((t, d), BF16)],
        scratch_shapes=[pltpu.VMEM((tm + SUBLANES, d), F32)],
        compiler_params=_params(("arbitrary",)),
        name="conv_mixer",
    )(x, mods, mods, w_in.astype(BF16), conv_w, w_out.astype(BF16), ln_g, ln_b)


def _qkv_kernel(x_ref, mod_ref, w_ref, qt_ref, k_ref, vt_ref, *, q_scale):
    tm, d = x_ref.shape
    shift, scale, _ = _split_mod(mod_ref[0], d)
    h = (x_ref[...] * scale + shift).astype(BF16)
    qkv = jnp.dot(h, w_ref[...], preferred_element_type=F32)
    qt_ref[...] = (qkv[:, :d] * q_scale).T.astype(BF16)
    k_ref[...] = qkv[:, d:2 * d].astype(BF16)
    vt_ref[...] = qkv[:, 2 * d:].T.astype(BF16)


def _qkv_proj(x, mods, mod_row, w_qkv, *, seq, q_scale, tm=512):
    t, d = x.shape
    tiles_per_seq = seq // tm
    return pl.pallas_call(
        functools.partial(_qkv_kernel, q_scale=q_scale),
        grid=(t // tm,),
        in_specs=[
            pl.BlockSpec((tm, d), lambda i: (i, 0)),
            pl.BlockSpec((1, 1, 3 * d), lambda i: (mod_row + i // tiles_per_seq, 0, 0)),
            _const_spec((d, 3 * d)),
        ],
        out_specs=[
            pl.BlockSpec((d, tm), lambda i: (0, i)),
            pl.BlockSpec((tm, d), lambda i: (i, 0)),
            pl.BlockSpec((d, tm), lambda i: (0, i)),
        ],
        out_shape=[
            jax.ShapeDtypeStruct((d, t), BF16),
            jax.ShapeDtypeStruct((t, d), BF16),
            jax.ShapeDtypeStruct((d, t), BF16),
        ],
        compiler_params=_params(("arbitrary",)),
        name="attn_qkv",
    )(x, mods, w_qkv.astype(BF16))


def _attn_kernel(slopes_ref, qt_ref, k_ref, vt_ref, lam_ref, g_ref, o_ref, qa_ref, m_ref, acc_ref,
                 kn_ref, *, tq, tk, heads, lambda_init):
    hg = pl.program_id(1)
    qi = pl.program_id(2)
    dh2 = qt_ref.shape[0] // heads
    dh = dh2 // 2
    ones_rows = 2 * SUBLANES
    blocks_per_tile = tq // tk

    feat = lax.broadcasted_iota(jnp.int32, (dh2, tq), 0)
    arow = lax.broadcasted_iota(jnp.int32, (dh2, 2 * tq), 0)
    acol = lax.broadcasted_iota(jnp.int32, (dh2, 2 * tq), 1)
    r = jnp.where(acol >= tq, acol - tq, acol)
    r_lo = (r % BF16_EXACT_INT).astype(F32)
    r_hi = (r - r % BF16_EXACT_INT).astype(F32)
    for g in range(heads):
        slope = slopes_ref[hg * heads + g]
        qt = qt_ref[g * dh2:(g + 1) * dh2, :]
        zero = jnp.zeros_like(qt)
        qa_ref[g, :dh2, :] = jnp.concatenate(
            [jnp.where(feat < dh, qt, zero), jnp.where(feat >= dh, qt, zero)], axis=1)
        qa_ref[g, dh2:, :] = jnp.where(
            arow == 0, -slope * r_lo,
            jnp.where(arow == 1, -slope * r_hi, jnp.where(arow == 2, slope, 0.0))).astype(BF16)
    kcol = lax.broadcasted_iota(jnp.int32, (tk, dh2), 1)
    krow = lax.broadcasted_iota(jnp.int32, (tk, dh2), 0).astype(F32)
    k_extra = jnp.where(kcol <= 1, 1.0, jnp.where(kcol == 2, krow, 0.0)).astype(BF16)
    ones = jnp.ones((ones_rows, tk), BF16)

    m_ref[...] = jnp.full(m_ref.shape, -jnp.inf, F32)
    acc_ref[...] = jnp.zeros(acc_ref.shape, F32)

    @pl.when(qi == 0)
    def _():
        ones_sq = jnp.ones((dh2, LANES), BF16)
        for g in range(heads):
            def chunk(c, best, g=g):
                kc = k_ref[pl.ds(pl.multiple_of(c * tq, tq), tq), g * dh2:(g + 1) * dh2].astype(F32)
                rows = jnp.dot((kc * kc).astype(BF16), ones_sq, preferred_element_type=F32)
                return jnp.maximum(best, jnp.max(rows, axis=0, keepdims=True))
            kn_ref[g] = lax.fori_loop(0, k_ref.shape[0] // tq, chunk, jnp.zeros((1, LANES), F32))

    def step(j, diagonal):
        start = pl.multiple_of(j * tk, tk)
        delta = qi * tq - j * tk
        head_cols = [slice(g * dh2, (g + 1) * dh2) for g in range(heads)]
        scores, maxima = [], []
        for g, cols in enumerate(head_cols):
            s = jnp.dot(jnp.concatenate([k_ref[pl.ds(start, tk), cols], k_extra], axis=1),
                        qa_ref[g], preferred_element_type=F32)
            if diagonal:
                mrow = lax.broadcasted_iota(jnp.int32, (tk, 2 * tq), 0)
                mcol = lax.broadcasted_iota(jnp.int32, (tk, 2 * tq), 1)
                mq = jnp.where(mcol >= tq, mcol - tq, mcol)
                s = jnp.where(mrow - mq <= delta, s, -jnp.inf)
            scores.append(s)
            maxima.append(jnp.max(s, axis=0, keepdims=True))
        probs, corrs = [], []
        for g, s in enumerate(scores):
            off = delta.astype(F32) * slopes_ref[hg * heads + g]
            m_old = m_ref[g]
            m_new = jnp.maximum(m_old, maxima[g] - off)
            probs.append(jnp.exp(s - (m_new + off)).astype(BF16))
            corrs.append(jnp.exp(m_old - m_new))
            m_ref[g] = m_new
        for g, cols in enumerate(head_cols):
            v_aug = jnp.concatenate([vt_ref[cols, pl.ds(start, tk)], ones], axis=0)
            acc_ref[g] = corrs[g] * acc_ref[g] + jnp.dot(v_aug, probs[g],
                                                         preferred_element_type=F32)

    first_diag = qi * blocks_per_tile
    for d in range(blocks_per_tile):
        step(first_diag + d, True)

    needed = jnp.zeros((1, 2 * tq), F32)
    bound = jnp.zeros((1, 2 * tq), F32)
    for g in range(heads):
        qf = qa_ref[g, :dh2, :].astype(F32)
        q_norm = jnp.sqrt(jnp.sum(qf * qf, axis=0, keepdims=True))
        k_norm = jnp.sqrt(kn_ref[g][:, :1])
        qk = q_norm * k_norm * NORM_SLACK
        reach = (qk - m_ref[g] + EXP_UNDERFLOW) / slopes_ref[hg * heads + g]
        needed = jnp.maximum(needed, jnp.floor((reach - 1.0) / tk) + 1.0)
        bound = jnp.maximum(bound, qk)
    needed = jnp.clip(needed, 0.0, first_diag.astype(F32))
    n_blocks = jnp.max(needed.astype(jnp.int32))

    fixed_shift_ok = jnp.max(bound) * 2.0 < FIXED_SHIFT_LIMIT

    def fixed_shift_step(j):
        start = pl.multiple_of(j * tk, tk)
        delta = qi * tq - j * tk
        scores = [jnp.dot(jnp.concatenate([k_ref[pl.ds(start, tk), g * dh2:(g + 1) * dh2], k_extra],
                                          axis=1), qa_ref[g], preferred_element_type=F32)
                  for g in range(heads)]
        probs = [jnp.exp(s - (m_ref[g] + delta.astype(F32) * slopes_ref[hg * heads + g])).astype(BF16)
                 for g, s in enumerate(scores)]
        for g in range(heads):
            v_aug = jnp.concatenate([vt_ref[g * dh2:(g + 1) * dh2, pl.ds(start, tk)], ones], axis=0)
            acc_ref[g] += jnp.dot(v_aug, probs[g], preferred_element_type=F32)

    @pl.when(fixed_shift_ok)
    def _():
        def body(i, carry):
            fixed_shift_step(first_diag - 1 - i)
            return carry

        lax.fori_loop(0, n_blocks, body, 0)

    @pl.when(jnp.logical_not(fixed_shift_ok))
    def _():
        def body(i, carry):
            step(first_diag - 1 - i, False)
            return carry

        lax.fori_loop(0, n_blocks, body, 0)

    lam = lam_ref[...]
    lam_full = (jnp.exp(jnp.sum(lam[0:1] * lam[1:2], axis=1, keepdims=True))
                - jnp.exp(jnp.sum(lam[2:3] * lam[3:4], axis=1, keepdims=True)) + lambda_init)
    for g in range(heads):
        acc = acc_ref[g]
        on = acc[:dh2] / acc[dh2:dh2 + 1]
        o = on[:, :tq] - lam_full * on[:, tq:]
        o = o * lax.rsqrt(jnp.mean(o * o, axis=0, keepdims=True) + LN_EPS) * g_ref[...]
        o_ref[:, g * dh2:(g + 1) * dh2] = (o * (1.0 - lambda_init)).T.astype(o_ref.dtype)


def _diff_attention(qt, k, vt, lam, subln_g, *, batch, seq, lambda_init, tq=512, tk=256, heads=4):
    d, t = qt.shape
    dh2 = d // ATT_HEADS
    gw = heads * dh2
    nq = seq // tq
    assert tq % tk == 0 and tk <= BF16_EXACT_INT and tq < BF16_EXACT_INT * BF16_EXACT_INT
    slopes = 2.0 ** (-(8.0 / ATT_HEADS) * jnp.arange(1, ATT_HEADS + 1, dtype=F32))
    return pl.pallas_call(
        functools.partial(_attn_kernel, tq=tq, tk=tk, heads=heads, lambda_init=lambda_init),
        grid=(batch, ATT_HEADS // heads, nq),
        in_specs=[
            pl.BlockSpec(memory_space=pltpu.SMEM),
            pl.BlockSpec((gw, tq), lambda b, h, i: (h, b * nq + i)),
            pl.BlockSpec((seq, gw), lambda b, h, i: (b, h)),
            pl.BlockSpec((gw, seq), lambda b, h, i: (h, b)),
            pl.BlockSpec(lam.shape, lambda b, h, i: (0, 0)),
            pl.BlockSpec((dh2, 1), lambda b, h, i: (0, 0)),
        ],
        out_specs=pl.BlockSpec((tq, gw), lambda b, h, i: (b * nq + i, h)),
        out_shape=jax.ShapeDtypeStruct((t, d), BF16),
        scratch_shapes=[
            pltpu.VMEM((heads, 2 * dh2, 2 * tq), BF16),
            pltpu.VMEM((heads, 1, 2 * tq), F32),
            pltpu.VMEM((heads, dh2 + 2 * SUBLANES, 2 * tq), F32),
            pltpu.VMEM((heads, 1, LANES), F32),
        ],
        compiler_params=_params(("arbitrary", "arbitrary", "arbitrary")),
        name="diff_attention",
    )(slopes, qt, k, vt, lam, subln_g.reshape(dh2, 1))


def _oproj_kernel(o_ref, x_ref, mod_ref, modp_ref, w_ref, g_ref, b_ref, out_ref, hp_ref, *, alpha):
    tm, d = x_ref.shape
    _, _, gate = _split_mod(mod_ref[0], d)
    y = jnp.dot(o_ref[...], w_ref[...], preferred_element_type=F32)
    _emit_stage_output(alpha * x_ref[...] + gate * y, g_ref, b_ref, modp_ref, out_ref, hp_ref)


def _out_proj(o, x, mods, mod_row, peer_row, w_o, ln_g, ln_b, *, seq, alpha, tm=512):
    t, d = x.shape
    tiles_per_seq = seq // tm
    return pl.pallas_call(
        functools.partial(_oproj_kernel, alpha=alpha),
        grid=(t // tm,),
        in_specs=[
            pl.BlockSpec((tm, d), lambda i: (i, 0)),
            pl.BlockSpec((tm, d), lambda i: (i, 0)),
            pl.BlockSpec((1, 1, 3 * d), lambda i: (mod_row + i // tiles_per_seq, 0, 0)),
            pl.BlockSpec((1, 1, 3 * d), lambda i: (peer_row + i // tiles_per_seq, 0, 0)),
            _const_spec((d, d)),
            _const_spec((1, d)),
            _const_spec((1, d)),
        ],
        out_specs=[pl.BlockSpec((tm, d), lambda i: (i, 0))] * 2,
        out_shape=[jax.ShapeDtypeStruct((t, d), F32), jax.ShapeDtypeStruct((t, d), BF16)],
        compiler_params=_params(("arbitrary",)),
        name="attn_out_proj",
    )(o, x, mods, mods, w_o.astype(BF16), ln_g, ln_b)


def _top16_rows(s):
    n, tm = s.shape
    rows = lax.broadcasted_iota(jnp.int32, (n, tm), 0).astype(F32)
    vals, idxs = [], []
    tie = None
    for _ in range(PEER_TOPK):
        m = jnp.max(s, axis=0, keepdims=True)
        if tie is not None:
            m = m + tie
        idx = jnp.min(jnp.where(s == m, rows, float(n)), axis=0, keepdims=True)
        s = jnp.where(rows == idx, -jnp.inf, s)
        vals.append(m)
        idxs.append(idx)
        tie = yield
    return vals, idxs


def _candidate_layout(tm):
    slot = lax.broadcasted_iota(jnp.int32, (N_CAND, tm), 0)
    grp = slot // SUBLANES
    r = slot % SUBLANES
    a = jnp.where(grp <= 1, 0, jnp.where(grp == N_CAND_GROUPS - 1, SUBLANES + r, grp - 1))
    b = jnp.where(grp == 0, r, jnp.where(grp == 1, SUBLANES + r,
                                         jnp.where(grp == N_CAND_GROUPS - 1, 0, r)))
    valid = (a + 1) * (b + 1) <= PEER_TOPK
    flat = (a * PEER_TOPK + b).astype(F32)
    return valid, flat


def _candidates(rows0, rows1):
    lo1 = jnp.concatenate(rows1[:SUBLANES], axis=0)
    hi1 = jnp.concatenate(rows1[SUBLANES:], axis=0)
    hi0 = jnp.concatenate(rows0[SUBLANES:], axis=0)
    groups = [rows0[0] + lo1, rows0[0] + hi1]
    groups += [rows0[a] + lo1 for a in range(1, SUBLANES)]
    groups.append(hi0 + rows1[0])
    return jnp.concatenate(groups, axis=0)


def _route_head(q_head, keys_ref, valid, flat):
    scores = [lax.dot_general(keys_ref[p], q_head[:, p * PEER_KEYS:(p + 1) * PEER_KEYS],
                              (((1,), (1,)), ((), ())), preferred_element_type=F32)
              for p in range(2)]
    yield
    vals, idxs = [], []
    for st in scores:
        v, ix = yield from _top16_rows(st)
        vals.append(v)
        idxs.append(ix)
    cand = jnp.where(valid, _candidates(vals[0], vals[1]), -jnp.inf)
    code = _candidates([ix * float(PEER_KEYS) for ix in idxs[0]], idxs[1])
    top_s, top_code = [], []
    tie = None
    for _ in range(PEER_TOPK):
        m = jnp.max(cand, axis=0, keepdims=True)
        if tie is not None:
            m = m + tie
        fmin = jnp.min(jnp.where(cand == m, flat, float(PEER_TOPK * PEER_TOPK)),
                       axis=0, keepdims=True)
        hit = flat == fmin
        top_code.append(jnp.sum(jnp.where(hit, code, 0.0), axis=0, keepdims=True))
        cand = jnp.where(hit, -jnp.inf, cand)
        top_s.append(m)
        tie = yield
    ts = jnp.concatenate(top_s, axis=0)
    e = jnp.exp(ts - top_s[0])
    return (jnp.concatenate(top_code, axis=0).astype(jnp.int32),
            e / jnp.sum(e, axis=0, keepdims=True))


ROUTE_YIELDS = 3 * PEER_TOPK


def _zero_row(x, width):
    bits = lax.bitcast_convert_type(x[:1, :LANES], jnp.uint32)
    zero = lax.bitcast_convert_type((bits >> 16) >> 16, F32)
    return jnp.concatenate([zero] * (width // LANES), axis=1)


def _peer_kernel(h_ref, hn_ref, x_ref, mod_ref, wq_ref, keys_ref, ut_ref, v_ref, g_ref, b_ref,
                 o_ref, q_ref, code_ref, gate_ref, ci_ref, cg_ref, w_ref, acc_ref, hs_ref,
                 *, alpha):
    i = pl.program_id(0)
    j = pl.program_id(1)
    nj = pl.num_programs(1)
    tm, d = x_ref.shape
    tn = ut_ref.shape[1]
    nhk = PEER_HEADS * PEER_TOPK
    parts = ROUTE_UNITS // PEER_HEADS
    tp = tm // parts
    slot = i % 2
    valid, flat = _candidate_layout(tp)

    def prepare_queries(h):
        q = jnp.dot(h, wq_ref[...], preferred_element_type=F32).astype(BF16)
        for hd in range(PEER_HEADS):
            q_ref[hd] = q[:, hd * 2 * PEER_KEYS:(hd + 1) * 2 * PEER_KEYS]

    def route_unit(u, dst):
        hd = u // parts
        part = u % parts
        rows = pl.ds(pl.multiple_of(part * tp, tp), tp)
        code, gate = yield from _route_head(q_ref[hd, rows, :], keys_ref, valid, flat)
        krows = pl.ds(pl.multiple_of(hd * PEER_TOPK, PEER_TOPK), PEER_TOPK)
        for g in range(tp // LANES):
            lanes = slice(g * LANES, (g + 1) * LANES)
            code_ref[dst, part * (tp // LANES) + g, krows, :] = code[:, lanes]
            gate_ref[dst, part * (tp // LANES) + g, krows, :] = gate[:, lanes]

    @pl.when((i == 0) & (j == 0))
    def _():
        prepare_queries(h_ref[...])

        def unit(u, carry):
            for _ in route_unit(u, 0):
                pass
            return carry

        lax.fori_loop(0, ROUTE_UNITS, unit, 0)

    @pl.when(j == 0)
    def _():
        for grp in range(tm // LANES):
            code = code_ref[slot, grp].T
            ci_ref[grp * LANES:(grp + 1) * LANES, :] = code // PEER_KEYS
            cg_ref[grp * LANES:(grp + 1) * LANES, :] = gate_ref[slot, grp].T
        key_rows = lax.broadcasted_iota(jnp.int32, (PEER_KEYS, nhk), 0).astype(BF16)
        key_lanes = lax.broadcasted_iota(jnp.int32, (nhk, PEER_KEYS), 1).astype(BF16)
        zero = jnp.zeros((PEER_KEYS, nhk), BF16)
        one = jnp.ones((nhk, PEER_KEYS), BF16)

        def token_group(grp, carry):
            base = pl.multiple_of(grp * LANES, LANES)
            second = (code_ref[slot, grp] % PEER_KEYS).astype(F32)

            def gate_bits(u):
                ri = ci_ref[pl.ds(base + u, 1), :].astype(BF16)
                rg = cg_ref[pl.ds(base + u, 1), :].astype(BF16)
                cj = jnp.broadcast_to(second[:, u:u + 1], (nhk, PEER_KEYS)).astype(BF16)
                pt = jnp.where(key_rows == ri, jnp.broadcast_to(rg, zero.shape), zero)
                qm = jnp.where(key_lanes == cj, one, jnp.zeros_like(one))
                wt = jnp.dot(pt, qm, preferred_element_type=F32)
                return lax.bitcast_convert_type(wt, jnp.uint32)

            for u in range(0, LANES, 2):
                packed = (gate_bits(u) >> 16) | (gate_bits(u + 1) & jnp.uint32(0xFFFF0000))
                pair = (base + u) // 2
                w_ref[pl.ds(pl.multiple_of(pair * W_PITCH, SUBLANES), PEER_KEYS), :] = packed
            return carry

        lax.fori_loop(0, tm // LANES, token_group, 0)
        acc_ref[...] = jnp.zeros(acc_ref.shape, F32)
        hs_ref[...] = h_ref[...]
        prepare_queries(hn_ref[...])

    route = route_unit(j, 1 - slot)
    next(route)
    chunk = 2 * PEER_KEYS
    n_chunks = tn // chunk

    def advance(after, count):
        try:
            route.send(None if after is None else _zero_row(after, tp))
            for _ in range(count - 1):
                next(route)
        except StopIteration:
            pass

    halves = 2
    th = tm // halves
    per_piece = -(-ROUTE_YIELDS // (2 * halves * n_chunks))

    def expert_inputs(half):
        rows = slice(half * th, (half + 1) * th)
        h = hs_ref[rows, :]
        pieces = []
        for r in range(n_chunks):
            c0 = r * chunk
            a = jnp.dot(h, us[:, c0:c0 + chunk], preferred_element_type=F32)
            advance(a, per_piece)
            act = (a * (1.0 + lax.erf(a * (1.0 / math.sqrt(2.0))))).astype(BF16)
            for e in range(chunk // PEER_KEYS):
                first_key = (j * n_chunks + r) * (chunk // PEER_KEYS) + e
                words = w_ref[pl.ds(half * (th // 2) * W_PITCH + first_key, th // 2, stride=W_PITCH), :]
                gates = pltpu.bitcast(words, BF16)
                pieces.append(gates * act[:, e * PEER_KEYS:(e + 1) * PEER_KEYS])
        return jnp.concatenate(pieces, axis=1)

    def expert_outputs(half, z):
        rows = slice(half * th, (half + 1) * th)
        for c in range(d // chunk):
            cols = slice(c * chunk, (c + 1) * chunk)
            zv = jnp.dot(z, vs[:, cols], preferred_element_type=F32)
            acc_ref[rows, cols] += zv
            advance(zv, per_piece)

    us = pltpu.bitcast(ut_ref[...], BF16)
    vs = pltpu.bitcast(v_ref[...], BF16)
    advance(None, per_piece)
    z_prev = expert_inputs(0)
    for half in range(1, halves):
        z_next = expert_inputs(half)
        expert_outputs(half - 1, z_prev)
        z_prev = z_next
    expert_outputs(halves - 1, z_prev)
    for _ in route:
        pass

    @pl.when(j == nj - 1)
    def _():
        _, _, gate = _split_mod(mod_ref[0], d)
        o_ref[...] = _layer_norm(alpha * x_ref[...] + gate * acc_ref[...], g_ref[...], b_ref[...])


def _peer_layer(h, x, mods, mod_row, w_q, sub_keys, u_t, v, ln_g, ln_b, *, seq, alpha, tm=512):
    t, d = x.shape
    n = u_t.shape[1]
    tn = n // ROUTE_UNITS
    nhk = PEER_HEADS * PEER_TOPK
    parts = ROUTE_UNITS // PEER_HEADS
    tiles_per_seq = seq // tm
    last = t // tm - 1
    return pl.pallas_call(
        functools.partial(_peer_kernel, alpha=alpha),
        grid=(t // tm, n // tn),
        in_specs=[
            pl.BlockSpec((tm, d), lambda i, j: (i, 0)),
            pl.BlockSpec((tm, d), lambda i, j: (jnp.minimum(i + 1, last), 0)),
            pl.BlockSpec((tm, d), lambda i, j: (i, 0)),
            pl.BlockSpec((1, 1, 3 * d), lambda i, j: (mod_row + i // tiles_per_seq, 0, 0)),
            _const_spec(w_q.shape),
            _const_spec(sub_keys.shape),
            pl.BlockSpec((d // 2, tn), lambda i, j: (0, j)),
            pl.BlockSpec((tn // 2, d), lambda i, j: (j, 0)),
            _const_spec((1, d)),
            _const_spec((1, d)),
        ],
        out_specs=pl.BlockSpec((tm, d), lambda i, j: (i, 0)),
        out_shape=jax.ShapeDtypeStruct((t, d), F32),
        scratch_shapes=[
            pltpu.VMEM((PEER_HEADS, tm, 2 * PEER_KEYS), BF16),
            pltpu.VMEM((2, tm // LANES, nhk, LANES), jnp.int32),
            pltpu.VMEM((2, tm // LANES, nhk, LANES), F32),
            pltpu.VMEM((tm, nhk), jnp.int32),
            pltpu.VMEM((tm, nhk), F32),
            pltpu.VMEM((tm // 2 * W_PITCH, PEER_KEYS), jnp.uint32),
            pltpu.VMEM((tm, d), F32),
            pltpu.VMEM((tm, d), BF16),
        ],
        compiler_params=_params(("arbitrary", "arbitrary")),
        name="peer_layer",
    )(h, h, x, mods, w_q.astype(BF16), sub_keys.astype(BF16), u_t, v, ln_g, ln_b)


def _pack_kernel(w_ref, o_ref, t_ref, *, transpose, scale):
    w = w_ref[0]
    if scale != 1.0:
        w = w * scale
    if transpose:
        w = w.T
    rows, cols = w.shape
    for c in range(cols // LANES):
        t_ref[c] = w[:, c * LANES:(c + 1) * LANES]
        even = t_ref[c, pl.ds(0, rows // 2, stride=2), :].astype(BF16).astype(F32)
        odd = t_ref[c, pl.ds(1, rows // 2, stride=2), :].astype(BF16).astype(F32)
        o_ref[:, c * LANES:(c + 1) * LANES] = (
            (lax.bitcast_convert_type(even, jnp.uint32) >> 16)
            | (lax.bitcast_convert_type(odd, jnp.uint32) & jnp.uint32(0xFFFF0000)))


def _packed_rows(w, layer, *, transpose, scale=1.0, tb=2048):
    _, n, d = w.shape
    if transpose:
        out_shape, out_block, out_map = (d // 2, n), (d // 2, tb), (lambda i: (0, i))
        t_shape = (tb // LANES, d, LANES)
    else:
        out_shape, out_block, out_map = (n // 2, d), (tb // 2, d), (lambda i: (i, 0))
        t_shape = (d // LANES, tb, LANES)
    return pl.pallas_call(
        functools.partial(_pack_kernel, transpose=transpose, scale=scale),
        grid=(n // tb,),
        in_specs=[pl.BlockSpec((1, tb, d), lambda i: (layer, i, 0))],
        out_specs=pl.BlockSpec(out_block, out_map),
        out_shape=jax.ShapeDtypeStruct(out_shape, jnp.uint32),
        scratch_shapes=[pltpu.VMEM(t_shape, F32)],
        compiler_params=_params(("arbitrary",)),
        name="pack_weights",
    )(w)


def _lambda_init(layer_idx):
    return 0.8 - 0.6 * math.exp(-0.3 * layer_idx)


def kernel(x, c, ada_w, ada_b, ln_g, ln_b, conv_w_in, conv_w, conv_w_out, attn_w_qkv, attn_lambda,
           attn_subln_g, attn_w_o, peer_w_q, peer_sub_keys, peer_u, peer_v):
    batch, seq, d = x.shape
    depth = ada_w.shape[0]
    alpha = (2.0 * depth) ** 0.25
    head_dim = d // (2 * ATT_HEADS)
    mods = _ada_mods(c, ada_w, ada_b)
    xt = x.reshape(batch * seq, d)
    for i in range(depth):
        j = i // N_MIXERS
        row = (2 * i) * batch
        prow = (2 * i + 1) * batch
        g0, b0 = ln_g[i, 0].reshape(1, d), ln_b[i, 0].reshape(1, d)
        if i % N_MIXERS == 0:
            xt, h = _conv_layer(xt, mods, row, prow, conv_w_in[j], conv_w[j], conv_w_out[j], g0, b0,
                                seq=seq, alpha=alpha)
        else:
            qt, k, vt = _qkv_proj(xt, mods, row, attn_w_qkv[j], seq=seq, q_scale=head_dim ** -0.5)
            o = _diff_attention(qt, k, vt, attn_lambda[j], attn_subln_g[j], batch=batch, seq=seq,
                                lambda_init=_lambda_init(i))
            xt, h = _out_proj(o, xt, mods, row, prow, attn_w_o[j], g0, b0, seq=seq, alpha=alpha)
        g1, b1 = ln_g[i, 1].reshape(1, d), ln_b[i, 1].reshape(1, d)
        xt = _peer_layer(h, xt, mods, prow, peer_w_q[i], peer_sub_keys[i],
                         _packed_rows(peer_u, i, transpose=True),
                         _packed_rows(peer_v, i, transpose=False, scale=0.5), g1, b1,
                         seq=seq, alpha=alpha)
    return xt.reshape(batch, seq, d)
```

```python
import functools
import math

import jax
import jax.numpy as jnp
from jax import lax
from jax.experimental import pallas as pl
from jax.experimental.pallas import tpu as pltpu

F32 = jnp.float32
BF16 = jnp.bfloat16

N_MIXERS = 2
ATT_HEADS = 8
PEER_HEADS = 8
PEER_KEYS = 128
PEER_TOPK = 16
LN_EPS = 1e-5

LANES = 128
SUBLANES = 8
VMEM_LIMIT_BYTES = 56 * 1024 * 1024
BF16_EXACT_INT = 256
EXP_UNDERFLOW = 106.0
NORM_SLACK = 1.02
FIXED_SHIFT_LIMIT = 60.0

N_CAND_GROUPS = 10
N_CAND = N_CAND_GROUPS * SUBLANES
W_PITCH = 136
ROUTE_UNITS = 16


def _layer_norm(r, g, b):
    mu = jnp.mean(r, axis=-1, keepdims=True)
    d = r - mu
    var = jnp.mean(d * d, axis=-1, keepdims=True)
    return d * lax.rsqrt(var + LN_EPS) * g + b


def _split_mod(mod, d):
    return mod[:, :d], 1.0 + mod[:, d:2 * d], 1.0 + mod[:, 2 * d:]


def _params(sem):
    return pltpu.CompilerParams(dimension_semantics=sem, vmem_limit_bytes=VMEM_LIMIT_BYTES)


def _const_spec(shape):
    nd = len(shape)
    return pl.BlockSpec(shape, lambda *_: (0,) * nd, pipeline_mode=pl.Buffered(1))


def _ada_kernel(c_ref, w_ref, b_ref, o_ref):
    c = c_ref[...]
    sc = c / (1.0 + jnp.exp(-c))
    o_ref[0] = jnp.dot(sc, w_ref[0], precision=lax.Precision.HIGHEST,
                       preferred_element_type=F32) + b_ref[0]


def _ada_mods(c, ada_w, ada_b):
    depth, _, d, d3 = ada_w.shape
    nb = c.shape[0]
    nmat = depth * 2
    tn = 1024
    w = ada_w.reshape(nmat, d, d3)
    b = ada_b.reshape(nmat, 1, d3)
    out = pl.pallas_call(
        _ada_kernel,
        grid=(nmat, d3 // tn),
        in_specs=[
            pl.BlockSpec((nb, d), lambda m, n: (0, 0)),
            pl.BlockSpec((1, d, tn), lambda m, n: (m, 0, n)),
            pl.BlockSpec((1, 1, tn), lambda m, n: (m, 0, n)),
        ],
        out_specs=pl.BlockSpec((1, nb, tn), lambda m, n: (m, 0, n)),
        out_shape=jax.ShapeDtypeStruct((nmat, nb, d3), F32),
        compiler_params=_params(("arbitrary", "arbitrary")),
        name="ada_mods",
    )(c, w, b)
    return out.reshape(nmat * nb, 1, d3)


def _emit_stage_output(r, g_ref, b_ref, modp_ref, o_ref, hp_ref):
    xn = _layer_norm(r, g_ref[...], b_ref[...])
    o_ref[...] = xn
    shift, scale, _ = _split_mod(modp_ref[0], xn.shape[1])
    hp_ref[...] = (xn * scale + shift).astype(BF16)


def _conv_kernel(x_ref, mod_ref, modp_ref, win_ref, cw_ref, wout_ref, g_ref, b_ref, o_ref, hp_ref,
                 zbuf, *, tiles_per_seq, alpha):
    i = pl.program_id(0)
    tm, d = x_ref.shape
    x = x_ref[...]
    shift, scale, gate = _split_mod(mod_ref[0], d)
    h = (x * scale + shift).astype(BF16)
    hw = jnp.dot(h, win_ref[...], preferred_element_type=F32)
    gb = hw[:, :d]
    z = hw[:, d:2 * d] * hw[:, 2 * d:]

    @pl.when(i % tiles_per_seq == 0)
    def _():
        zbuf[0:SUBLANES, :] = jnp.zeros((SUBLANES, d), F32)

    zbuf[SUBLANES:, :] = z
    z1 = zbuf[pl.ds(SUBLANES - 1, tm), :]
    z2 = zbuf[pl.ds(SUBLANES - 2, tm), :]
    cw = cw_ref[...]
    zc = cw[2:3] * z + cw[1:2] * z1 + cw[0:1] * z2
    zbuf[0:SUBLANES, :] = zbuf[tm:tm + SUBLANES, :]
    y = jnp.dot((gb * zc).astype(BF16), wout_ref[...], preferred_element_type=F32)
    _emit_stage_output(alpha * x + gate * y, g_ref, b_ref, modp_ref, o_ref, hp_ref)


def _conv_layer(x, mods, mod_row, peer_row, w_in, conv_w, w_out, ln_g, ln_b, *, seq, alpha, tm=512):
    t, d = x.shape
    tiles_per_seq = seq // tm
    return pl.pallas_call(
        functools.partial(_conv_kernel, tiles_per_seq=tiles_per_seq, alpha=alpha),
        grid=(t // tm,),
        in_specs=[
            pl.BlockSpec((tm, d), lambda i: (i, 0)),
            pl.BlockSpec((1, 1, 3 * d), lambda i: (mod_row + i // tiles_per_seq, 0, 0)),
            pl.BlockSpec((1, 1, 3 * d), lambda i: (peer_row + i // tiles_per_seq, 0, 0)),
            _const_spec((d, 3 * d)),
            _const_spec((conv_w.shape[0], d)),
            _const_spec((d, d)),
            _const_spec((1, d)),
            _const_spec((1, d)),
        ],
        out_specs=[pl.BlockSpec((tm, d), lambda i: (i, 0))] * 2,
        out_shape=[jax.ShapeDtypeStruct((t, d), F32), jax.ShapeDtypeStruct((t, d), BF16)],
        scratch_shapes=[pltpu.VMEM((tm + SUBLANES, d), F32)],
        compiler_params=_params(("arbitrary",)),
        name="conv_mixer",
    )(x, mods, mods, w_in.astype(BF16), conv_w, w_out.astype(BF16), ln_g, ln_b)


def _qkv_kernel(x_ref, mod_ref, w_ref, qt_ref, k_ref, vt_ref, *, q_scale):
    tm, d = x_ref.shape
    shift, scale, _ = _split_mod(mod_ref[0], d)
    h = (x_ref[...] * scale + shift).astype(BF16)
    qkv = jnp.dot(h, w_ref[...], preferred_element_type=F32)
    qt_ref[...] = (qkv[:, :d] * q_scale).T.astype(BF16)
    k_ref[...] = qkv[:, d:2 * d].astype(BF16)
    vt_ref[...] = qkv[:, 2 * d:].T.astype(BF16)


def _qkv_proj(x, mods, mod_row, w_qkv, *, seq, q_scale, tm=512):
    t, d = x.shape
    tiles_per_seq = seq // tm
    return pl.pallas_call(
        functools.partial(_qkv_kernel, q_scale=q_scale),
        grid=(t // tm,),
        in_specs=[
            pl.BlockSpec((tm, d), lambda i: (i, 0)),
            pl.BlockSpec((1, 1, 3 * d), lambda i: (mod_row + i // tiles_per_seq, 0, 0)),
            _const_spec((d, 3 * d)),
        ],
        out_specs=[
            pl.BlockSpec((d, tm), lambda i: (0, i)),
            pl.BlockSpec((tm, d), lambda i: (i, 0)),
            pl.BlockSpec((d, tm), lambda i: (0, i)),
        ],
        out_shape=[
            jax.ShapeDtypeStruct((d, t), BF16),
            jax.ShapeDtypeStruct((t, d), BF16),
            jax.ShapeDtypeStruct((d, t), BF16),
        ],
        compiler_params=_params(("arbitrary",)),
        name="attn_qkv",
    )(x, mods, w_qkv.astype(BF16))


def _attn_kernel(slopes_ref, qt_ref, k_ref, vt_ref, lam_ref, g_ref, o_ref, qa_ref, m_ref, acc_ref,
                 kn_ref, *, tq, tk, heads, lambda_init):
    hg = pl.program_id(1)
    qi = pl.program_id(2)
    dh2 = qt_ref.shape[0] // heads
    dh = dh2 // 2
    ones_rows = 2 * SUBLANES
    blocks_per_tile = tq // tk

    feat = lax.broadcasted_iota(jnp.int32, (dh2, tq), 0)
    arow = lax.broadcasted_iota(jnp.int32, (dh2, 2 * tq), 0)
    acol = lax.broadcasted_iota(jnp.int32, (dh2, 2 * tq), 1)
    r = jnp.where(acol >= tq, acol - tq, acol)
    r_lo = (r % BF16_EXACT_INT).astype(F32)
    r_hi = (r - r % BF16_EXACT_INT).astype(F32)
    for g in range(heads):
        slope = slopes_ref[hg * heads + g]
        qt = qt_ref[g * dh2:(g + 1) * dh2, :]
        zero = jnp.zeros_like(qt)
        qa_ref[g, :dh2, :] = jnp.concatenate(
            [jnp.where(feat < dh, qt, zero), jnp.where(feat >= dh, qt, zero)], axis=1)
        qa_ref[g, dh2:, :] = jnp.where(
            arow == 0, -slope * r_lo,
            jnp.where(arow == 1, -slope * r_hi, jnp.where(arow <= 3, slope, 0.0))).astype(BF16)
    kcol = lax.broadcasted_iota(jnp.int32, (tk, dh2), 1)
    krow = lax.broadcasted_iota(jnp.int32, (tk, dh2), 0)
    c_lo = (krow % BF16_EXACT_INT).astype(F32)
    c_hi = (krow - krow % BF16_EXACT_INT).astype(F32)
    k_extra = jnp.where(kcol <= 1, 1.0,
                        jnp.where(kcol == 2, c_lo, jnp.where(kcol == 3, c_hi, 0.0))).astype(BF16)
    ones = jnp.ones((ones_rows, tk), BF16)

    m_ref[...] = jnp.full(m_ref.shape, -jnp.inf, F32)
    acc_ref[...] = jnp.zeros(acc_ref.shape, F32)

    @pl.when(qi == 0)
    def _():
        ones_sq = jnp.ones((dh2, LANES), BF16)
        for g in range(heads):
            def chunk(c, best, g=g):
                kc = k_ref[pl.ds(pl.multiple_of(c * tq, tq), tq), g * dh2:(g + 1) * dh2].astype(F32)
                rows = jnp.dot((kc * kc).astype(BF16), ones_sq, preferred_element_type=F32)
                return jnp.maximum(best, jnp.max(rows, axis=0, keepdims=True))
            kn_ref[g] = lax.fori_loop(0, k_ref.shape[0] // tq, chunk, jnp.zeros((1, LANES), F32))

    def step(j, diagonal):
        start = pl.multiple_of(j * tk, tk)
        delta = qi * tq - j * tk
        head_cols = [slice(g * dh2, (g + 1) * dh2) for g in range(heads)]
        scores, maxima = [], []
        for g, cols in enumerate(head_cols):
            s = jnp.dot(jnp.concatenate([k_ref[pl.ds(start, tk), cols], k_extra], axis=1),
                        qa_ref[g], preferred_element_type=F32)
            if diagonal:
                mrow = lax.broadcasted_iota(jnp.int32, (tk, 2 * tq), 0)
                mcol = lax.broadcasted_iota(jnp.int32, (tk, 2 * tq), 1)
                mq = jnp.where(mcol >= tq, mcol - tq, mcol)
                s = jnp.where(mrow - mq <= delta, s, -jnp.inf)
            scores.append(s)
            maxima.append(jnp.max(s, axis=0, keepdims=True))
        probs, corrs = [], []
        for g, s in enumerate(scores):
            off = delta.astype(F32) * slopes_ref[hg * heads + g]
            m_old = m_ref[g]
            m_new = jnp.maximum(m_old, maxima[g] - off)
            probs.append(jnp.exp(s - (m_new + off)).astype(BF16))
            corrs.append(jnp.exp(m_old - m_new))
            m_ref[g] = m_new
        for g, cols in enumerate(head_cols):
            v_aug = jnp.concatenate([vt_ref[cols, pl.ds(start, tk)], ones], axis=0)
            acc_ref[g] = corrs[g] * acc_ref[g] + jnp.dot(v_aug, probs[g],
                                                         preferred_element_type=F32)

    first_diag = qi * blocks_per_tile
    for d in range(blocks_per_tile):
        step(first_diag + d, True)

    needed = jnp.zeros((1, 2 * tq), F32)
    bound = jnp.zeros((1, 2 * tq), F32)
    for g in range(heads):
        qf = qa_ref[g, :dh2, :].astype(F32)
        q_norm = jnp.sqrt(jnp.sum(qf * qf, axis=0, keepdims=True))
        k_norm = jnp.sqrt(kn_ref[g][:, :1])
        qk = q_norm * k_norm * NORM_SLACK
        reach = (qk - m_ref[g] + EXP_UNDERFLOW) / slopes_ref[hg * heads + g]
        needed = jnp.maximum(needed, jnp.floor((reach - 1.0) / tk) + 1.0)
        bound = jnp.maximum(bound, qk)
    needed = jnp.clip(needed, 0.0, first_diag.astype(F32))
    n_blocks = jnp.max(needed.astype(jnp.int32))

    fixed_shift_ok = jnp.max(bound) * 2.0 < FIXED_SHIFT_LIMIT

    def fixed_shift_step(j):
        start = pl.multiple_of(j * tk, tk)
        delta = qi * tq - j * tk
        scores = [jnp.dot(jnp.concatenate([k_ref[pl.ds(start, tk), g * dh2:(g + 1) * dh2], k_extra],
                                          axis=1), qa_ref[g], preferred_element_type=F32)
                  for g in range(heads)]
        probs = [jnp.exp(s - (m_ref[g] + delta.astype(F32) * slopes_ref[hg * heads + g])).astype(BF16)
                 for g, s in enumerate(scores)]
        for g in range(heads):
            v_aug = jnp.concatenate([vt_ref[g * dh2:(g + 1) * dh2, pl.ds(start, tk)], ones], axis=0)
            acc_ref[g] += jnp.dot(v_aug, probs[g], preferred_element_type=F32)

    @pl.when(fixed_shift_ok)
    def _():
        def body(i, carry):
            fixed_shift_step(first_diag - 1 - i)
            return carry

        lax.fori_loop(0, n_blocks, body, 0)

    @pl.when(jnp.logical_not(fixed_shift_ok))
    def _():
        def body(i, carry):
            step(first_diag - 1 - i, False)
            return carry

        lax.fori_loop(0, n_blocks, body, 0)

    lam = lam_ref[...]
    lam_full = (jnp.exp(jnp.sum(lam[0:1] * lam[1:2], axis=1, keepdims=True))
                - jnp.exp(jnp.sum(lam[2:3] * lam[3:4], axis=1, keepdims=True)) + lambda_init)
    for g in range(heads):
        acc = acc_ref[g]
        on = acc[:dh2] / acc[dh2:dh2 + 1]
        o = on[:, :tq] - lam_full * on[:, tq:]
        o = o * lax.rsqrt(jnp.mean(o * o, axis=0, keepdims=True) + LN_EPS) * g_ref[...]
        o_ref[:, g * dh2:(g + 1) * dh2] = (o * (1.0 - lambda_init)).T.astype(o_ref.dtype)


def _diff_attention(qt, k, vt, lam, subln_g, *, batch, seq, lambda_init, tq=512, tk=512, heads=4):
    d, t = qt.shape
    dh2 = d // ATT_HEADS
    gw = heads * dh2
    nq = seq // tq
    assert tq % tk == 0 and tq < BF16_EXACT_INT * BF16_EXACT_INT
    slopes = 2.0 ** (-(8.0 / ATT_HEADS) * jnp.arange(1, ATT_HEADS + 1, dtype=F32))
    return pl.pallas_call(
        functools.partial(_attn_kernel, tq=tq, tk=tk, heads=heads, lambda_init=lambda_init),
        grid=(batch, ATT_HEADS // heads, nq),
        in_specs=[
            pl.BlockSpec(memory_space=pltpu.SMEM),
            pl.BlockSpec((gw, tq), lambda b, h, i: (h, b * nq + i)),
            pl.BlockSpec((seq, gw), lambda b, h, i: (b, h)),
            pl.BlockSpec((gw, seq), lambda b, h, i: (h, b)),
            pl.BlockSpec(lam.shape, lambda b, h, i: (0, 0)),
            pl.BlockSpec((dh2, 1), lambda b, h, i: (0, 0)),
        ],
        out_specs=pl.BlockSpec((tq, gw), lambda b, h, i: (b * nq + i, h)),
        out_shape=jax.ShapeDtypeStruct((t, d), BF16),
        scratch_shapes=[
            pltpu.VMEM((heads, 2 * dh2, 2 * tq), BF16),
            pltpu.VMEM((heads, 1, 2 * tq), F32),
            pltpu.VMEM((heads, dh2 + 2 * SUBLANES, 2 * tq), F32),
            pltpu.VMEM((heads, 1, LANES), F32),
        ],
        compiler_params=_params(("arbitrary", "arbitrary", "arbitrary")),
        name="diff_attention",
    )(slopes, qt, k, vt, lam, subln_g.reshape(dh2, 1))


def _oproj_kernel(o_ref, x_ref, mod_ref, modp_ref, w_ref, g_ref, b_ref, out_ref, hp_ref, *, alpha):
    tm, d = x_ref.shape
    _, _, gate = _split_mod(mod_ref[0], d)
    y = jnp.dot(o_ref[...], w_ref[...], preferred_element_type=F32)
    _emit_stage_output(alpha * x_ref[...] + gate * y, g_ref, b_ref, modp_ref, out_ref, hp_ref)


def _out_proj(o, x, mods, mod_row, peer_row, w_o, ln_g, ln_b, *, seq, alpha, tm=512):
    t, d = x.shape
    tiles_per_seq = seq // tm
    return pl.pallas_call(
        functools.partial(_oproj_kernel, alpha=alpha),
        grid=(t // tm,),
        in_specs=[
            pl.BlockSpec((tm, d), lambda i: (i, 0)),
            pl.BlockSpec((tm, d), lambda i: (i, 0)),
            pl.BlockSpec((1, 1, 3 * d), lambda i: (mod_row + i // tiles_per_seq, 0, 0)),
            pl.BlockSpec((1, 1, 3 * d), lambda i: (peer_row + i // tiles_per_seq, 0, 0)),
            _const_spec((d, d)),
            _const_spec((1, d)),
            _const_spec((1, d)),
        ],
        out_specs=[pl.BlockSpec((tm, d), lambda i: (i, 0))] * 2,
        out_shape=[jax.ShapeDtypeStruct((t, d), F32), jax.ShapeDtypeStruct((t, d), BF16)],
        compiler_params=_params(("arbitrary",)),
        name="attn_out_proj",
    )(o, x, mods, mods, w_o.astype(BF16), ln_g, ln_b)


def _top16_rows(s):
    n, tm = s.shape
    rows = lax.broadcasted_iota(jnp.int32, (n, tm), 0).astype(F32)
    vals, idxs = [], []
    tie = None
    for _ in range(PEER_TOPK):
        m = jnp.max(s, axis=0, keepdims=True)
        if tie is not None:
            m = m + tie
        idx = jnp.min(jnp.where(s == m, rows, float(n)), axis=0, keepdims=True)
        s = jnp.where(rows == idx, -jnp.inf, s)
        vals.append(m)
        idxs.append(idx)
        tie = yield
    return vals, idxs


def _candidate_layout(tm):
    slot = lax.broadcasted_iota(jnp.int32, (N_CAND, tm), 0)
    grp = slot // SUBLANES
    r = slot % SUBLANES
    a = jnp.where(grp <= 1, 0, jnp.where(grp == N_CAND_GROUPS - 1, SUBLANES + r, grp - 1))
    b = jnp.where(grp == 0, r, jnp.where(grp == 1, SUBLANES + r,
                                         jnp.where(grp == N_CAND_GROUPS - 1, 0, r)))
    valid = (a + 1) * (b + 1) <= PEER_TOPK
    flat = (a * PEER_TOPK + b).astype(F32)
    return valid, flat


def _candidates(rows0, rows1):
    lo1 = jnp.concatenate(rows1[:SUBLANES], axis=0)
    hi1 = jnp.concatenate(rows1[SUBLANES:], axis=0)
    hi0 = jnp.concatenate(rows0[SUBLANES:], axis=0)
    groups = [rows0[0] + lo1, rows0[0] + hi1]
    groups += [rows0[a] + lo1 for a in range(1, SUBLANES)]
    groups.append(hi0 + rows1[0])
    return jnp.concatenate(groups, axis=0)


def _route_head(q_head, keys_ref, valid, flat):
    scores = [lax.dot_general(keys_ref[p], q_head[:, p * PEER_KEYS:(p + 1) * PEER_KEYS],
                              (((1,), (1,)), ((), ())), preferred_element_type=F32)
              for p in range(2)]
    yield
    vals, idxs = [], []
    for st in scores:
        v, ix = yield from _top16_rows(st)
        vals.append(v)
        idxs.append(ix)
    cand = jnp.where(valid, _candidates(vals[0], vals[1]), -jnp.inf)
    code = _candidates([ix * float(PEER_KEYS) for ix in idxs[0]], idxs[1])
    top_s, top_code = [], []
    tie = None
    for _ in range(PEER_TOPK):
        m = jnp.max(cand, axis=0, keepdims=True)
        if tie is not None:
            m = m + tie
        fmin = jnp.min(jnp.where(cand == m, flat, float(PEER_TOPK * PEER_TOPK)),
                       axis=0, keepdims=True)
        hit = flat == fmin
        top_code.append(jnp.sum(jnp.where(hit, code, 0.0), axis=0, keepdims=True))
        cand = jnp.where(hit, -jnp.inf, cand)
        top_s.append(m)
        tie = yield
    ts = jnp.concatenate(top_s, axis=0)
    e = jnp.exp(ts - top_s[0])
    return (jnp.concatenate(top_code, axis=0).astype(jnp.int32),
            e / jnp.sum(e, axis=0, keepdims=True))


ROUTE_YIELDS = 3 * PEER_TOPK


def _zero_row(x, width):
    bits = lax.bitcast_convert_type(x[:1, :LANES], jnp.uint32)
    zero = lax.bitcast_convert_type((bits >> 16) >> 16, F32)
    return jnp.concatenate([zero] * (width // LANES), axis=1)


def _peer_kernel(h_ref, hn_ref, x_ref, mod_ref, wq_ref, keys_ref, ut_ref, v_ref, g_ref, b_ref,
                 o_ref, q_ref, code_ref, gate_ref, ci_ref, cg_ref, w_ref, acc_ref, hs_ref,
                 *, alpha):
    i = pl.program_id(0)
    j = pl.program_id(1)
    nj = pl.num_programs(1)
    tm, d = x_ref.shape
    tn = ut_ref.shape[1]
    nhk = PEER_HEADS * PEER_TOPK
    parts = ROUTE_UNITS // PEER_HEADS
    tp = tm // parts
    slot = i % 2
    valid, flat = _candidate_layout(tp)

    def prepare_queries(h):
        q = jnp.dot(h, wq_ref[...], preferred_element_type=F32).astype(BF16)
        for hd in range(PEER_HEADS):
            q_ref[hd] = q[:, hd * 2 * PEER_KEYS:(hd + 1) * 2 * PEER_KEYS]

    def route_unit(u, dst):
        hd = u // parts
        part = u % parts
        rows = pl.ds(pl.multiple_of(part * tp, tp), tp)
        code, gate = yield from _route_head(q_ref[hd, rows, :], keys_ref, valid, flat)
        krows = pl.ds(pl.multiple_of(hd * PEER_TOPK, PEER_TOPK), PEER_TOPK)
        for g in range(tp // LANES):
            lanes = slice(g * LANES, (g + 1) * LANES)
            code_ref[dst, part * (tp // LANES) + g, krows, :] = code[:, lanes]
            gate_ref[dst, part * (tp // LANES) + g, krows, :] = gate[:, lanes]

    @pl.when((i == 0) & (j == 0))
    def _():
        prepare_queries(h_ref[...])

        def unit(u, carry):
            for _ in route_unit(u, 0):
                pass
            return carry

        lax.fori_loop(0, ROUTE_UNITS, unit, 0)

    @pl.when(j == 0)
    def _():
        for grp in range(tm // LANES):
            code = code_ref[slot, grp].T
            ci_ref[grp * LANES:(grp + 1) * LANES, :] = code // PEER_KEYS
            cg_ref[grp * LANES:(grp + 1) * LANES, :] = gate_ref[slot, grp].T
        key_rows = lax.broadcasted_iota(jnp.int32, (PEER_KEYS, nhk), 0).astype(BF16)
        key_lanes = lax.broadcasted_iota(jnp.int32, (nhk, PEER_KEYS), 1).astype(BF16)
        zero = jnp.zeros((PEER_KEYS, nhk), BF16)
        one = jnp.ones((nhk, PEER_KEYS), BF16)

        def token_group(grp, carry):
            base = pl.multiple_of(grp * LANES, LANES)
            second = (code_ref[slot, grp] % PEER_KEYS).astype(F32)

            def gate_bits(u):
                ri = ci_ref[pl.ds(base + u, 1), :].astype(BF16)
                rg = cg_ref[pl.ds(base + u, 1), :].astype(BF16)
                cj = jnp.broadcast_to(second[:, u:u + 1], (nhk, PEER_KEYS)).astype(BF16)
                pt = jnp.where(key_rows == ri, jnp.broadcast_to(rg, zero.shape), zero)
                qm = jnp.where(key_lanes == cj, one, jnp.zeros_like(one))
                wt = jnp.dot(pt, qm, preferred_element_type=F32)
                return lax.bitcast_convert_type(wt, jnp.uint32)

            for u in range(0, LANES, 2):
                packed = (gate_bits(u) >> 16) | (gate_bits(u + 1) & jnp.uint32(0xFFFF0000))
                pair = (base + u) // 2
                w_ref[pl.ds(pl.multiple_of(pair * W_PITCH, SUBLANES), PEER_KEYS), :] = packed
            return carry

        lax.fori_loop(0, tm // LANES, token_group, 0)
        acc_ref[...] = jnp.zeros(acc_ref.shape, F32)
        hs_ref[...] = h_ref[...]
        prepare_queries(hn_ref[...])

    route = route_unit(j, 1 - slot)
    next(route)
    chunk = 2 * PEER_KEYS
    n_chunks = tn // chunk

    def advance(after, count):
        try:
            route.send(None if after is None else _zero_row(after, tp))
            for _ in range(count - 1):
                next(route)
        except StopIteration:
            pass

    halves = 2
    th = tm // halves
    per_piece = -(-ROUTE_YIELDS // (2 * halves * n_chunks))

    def expert_inputs(half):
        rows = slice(half * th, (half + 1) * th)
        h = hs_ref[rows, :]
        pieces = []
        for r in range(n_chunks):
            c0 = r * chunk
            a = jnp.dot(h, us[:, c0:c0 + chunk], preferred_element_type=F32)
            advance(a, per_piece)
            act = (a * (1.0 + lax.erf(a * (1.0 / math.sqrt(2.0))))).astype(BF16)
            for e in range(chunk // PEER_KEYS):
                first_key = (j * n_chunks + r) * (chunk // PEER_KEYS) + e
                words = w_ref[pl.ds(half * (th // 2) * W_PITCH + first_key, th // 2, stride=W_PITCH), :]
                gates = pltpu.bitcast(words, BF16)
                pieces.append(gates * act[:, e * PEER_KEYS:(e + 1) * PEER_KEYS])
        return jnp.concatenate(pieces, axis=1)

    def expert_outputs(half, z):
        rows = slice(half * th, (half + 1) * th)
        for c in range(d // chunk):
            cols = slice(c * chunk, (c + 1) * chunk)
            zv = jnp.dot(z, vs[:, cols], preferred_element_type=F32)
            acc_ref[rows, cols] += zv
            advance(zv, per_piece)

    us = pltpu.bitcast(ut_ref[...], BF16)
    vs = pltpu.bitcast(v_ref[...], BF16)
    advance(None, per_piece)
    z_prev = expert_inputs(0)
    for half in range(1, halves):
        z_next = expert_inputs(half)
        expert_outputs(half - 1, z_prev)
        z_prev = z_next
    expert_outputs(halves - 1, z_prev)
    for _ in route:
        pass

    @pl.when(j == nj - 1)
    def _():
        _, _, gate = _split_mod(mod_ref[0], d)
        o_ref[...] = _layer_norm(alpha * x_ref[...] + gate * acc_ref[...], g_ref[...], b_ref[...])


def _peer_layer(h, x, mods, mod_row, w_q, sub_keys, u_t, v, ln_g, ln_b, *, seq, alpha, tm=512):
    t, d = x.shape
    n = u_t.shape[1]
    tn = n // ROUTE_UNITS
    nhk = PEER_HEADS * PEER_TOPK
    parts = ROUTE_UNITS // PEER_HEADS
    tiles_per_seq = seq // tm
    last = t // tm - 1
    return pl.pallas_call(
        functools.partial(_peer_kernel, alpha=alpha),
        grid=(t // tm, n // tn),
        in_specs=[
            pl.BlockSpec((tm, d), lambda i, j: (i, 0)),
            pl.BlockSpec((tm, d), lambda i, j: (jnp.minimum(i + 1, last), 0)),
            pl.BlockSpec((tm, d), lambda i, j: (i, 0)),
            pl.BlockSpec((1, 1, 3 * d), lambda i, j: (mod_row + i // tiles_per_seq, 0, 0)),
            _const_spec(w_q.shape),
            _const_spec(sub_keys.shape),
            pl.BlockSpec((d // 2, tn), lambda i, j: (0, j)),
            pl.BlockSpec((tn // 2, d), lambda i, j: (j, 0)),
            _const_spec((1, d)),
            _const_spec((1, d)),
        ],
        out_specs=pl.BlockSpec((tm, d), lambda i, j: (i, 0)),
        out_shape=jax.ShapeDtypeStruct((t, d), F32),
        scratch_shapes=[
            pltpu.VMEM((PEER_HEADS, tm, 2 * PEER_KEYS), BF16),
            pltpu.VMEM((2, tm // LANES, nhk, LANES), jnp.int32),
            pltpu.VMEM((2, tm // LANES, nhk, LANES), F32),
            pltpu.VMEM((tm, nhk), jnp.int32),
            pltpu.VMEM((tm, nhk), F32),
            pltpu.VMEM((tm // 2 * W_PITCH, PEER_KEYS), jnp.uint32),
            pltpu.VMEM((tm, d), F32),
            pltpu.VMEM((tm, d), BF16),
        ],
        compiler_params=_params(("arbitrary", "arbitrary")),
        name="peer_layer",
    )(h, h, x, mods, w_q.astype(BF16), sub_keys.astype(BF16), u_t, v, ln_g, ln_b)


def _pack_kernel(w_ref, o_ref, t_ref, *, transpose, scale):
    w = w_ref[0]
    if scale != 1.0:
        w = w * scale
    if transpose:
        w = w.T
    rows, cols = w.shape
    for c in range(cols // LANES):
        t_ref[c] = w[:, c * LANES:(c + 1) * LANES]
        even = t_ref[c, pl.ds(0, rows // 2, stride=2), :].astype(BF16).astype(F32)
        odd = t_ref[c, pl.ds(1, rows // 2, stride=2), :].astype(BF16).astype(F32)
        o_ref[:, c * LANES:(c + 1) * LANES] = (
            (lax.bitcast_convert_type(even, jnp.uint32) >> 16)
            | (lax.bitcast_convert_type(odd, jnp.uint32) & jnp.uint32(0xFFFF0000)))


def _packed_rows(w, layer, *, transpose, scale=1.0, tb=2048):
    _, n, d = w.shape
    if transpose:
        out_shape, out_block, out_map = (d // 2, n), (d // 2, tb), (lambda i: (0, i))
        t_shape = (tb // LANES, d, LANES)
    else:
        out_shape, out_block, out_map = (n // 2, d), (tb // 2, d), (lambda i: (i, 0))
        t_shape = (d // LANES, tb, LANES)
    return pl.pallas_call(
        functools.partial(_pack_kernel, transpose=transpose, scale=scale),
        grid=(n // tb,),
        in_specs=[pl.BlockSpec((1, tb, d), lambda i: (layer, i, 0))],
        out_specs=pl.BlockSpec(out_block, out_map),
        out_shape=jax.ShapeDtypeStruct(out_shape, jnp.uint32),
        scratch_shapes=[pltpu.VMEM(t_shape, F32)],
        compiler_params=_params(("arbitrary",)),
        name="pack_weights",
    )(w)


def _lambda_init(layer_idx):
    return 0.8 - 0.6 * math.exp(-0.3 * layer_idx)


def kernel(x, c, ada_w, ada_b, ln_g, ln_b, conv_w_in, conv_w, conv_w_out, attn_w_qkv, attn_lambda,
           attn_subln_g, attn_w_o, peer_w_q, peer_sub_keys, peer_u, peer_v):
    batch, seq, d = x.shape
    depth = ada_w.shape[0]
    alpha = (2.0 * depth) ** 0.25
    head_dim = d // (2 * ATT_HEADS)
    assert 2 * head_dim == LANES, "one attention head must fill one 128-lane block"
    assert peer_u.shape[1] == PEER_KEYS * PEER_KEYS and peer_sub_keys.shape[2:] == (PEER_KEYS, PEER_KEYS)
    assert peer_w_q.shape[2] == PEER_HEADS * 2 * PEER_KEYS
    mods = _ada_mods(c, ada_w, ada_b)
    xt = x.reshape(batch * seq, d)
    for i in range(depth):
        j = i // N_MIXERS
        row = (2 * i) * batch
        prow = (2 * i + 1) * batch
        g0, b0 = ln_g[i, 0].reshape(1, d), ln_b[i, 0].reshape(1, d)
        if i % N_MIXERS == 0:
            xt, h = _conv_layer(xt, mods, row, prow, conv_w_in[j], conv_w[j], conv_w_out[j], g0, b0,
                                seq=seq, alpha=alpha)
        else:
            qt, k, vt = _qkv_proj(xt, mods, row, attn_w_qkv[j], seq=seq, q_scale=head_dim ** -0.5)
            o = _diff_attention(qt, k, vt, attn_lambda[j], attn_subln_g[j], batch=batch, seq=seq,
                                lambda_init=_lambda_init(i))
            xt, h = _out_proj(o, xt, mods, row, prow, attn_w_o[j], g0, b0, seq=seq, alpha=alpha)
        g1, b1 = ln_g[i, 1].reshape(1, d), ln_b[i, 1].reshape(1, d)
        xt = _peer_layer(h, xt, mods, prow, peer_w_q[i], peer_sub_keys[i],
                         _packed_rows(peer_u, i, transpose=True),
                         _packed_rows(peer_v, i, transpose=False, scale=0.5), g1, b1,
                         seq=seq, alpha=alpha)
    return xt.reshape(batch, seq, d)
```

```python
import functools
import math

import jax
import jax.numpy as jnp
from jax import lax
from jax.experimental import pallas as pl
from jax.experimental.pallas import tpu as pltpu

F32 = jnp.float32
BF16 = jnp.bfloat16

N_MIXERS = 2
ATT_HEADS = 8
PEER_HEADS = 8
PEER_KEYS = 128
PEER_TOPK = 16
LN_EPS = 1e-5

LANES = 128
SUBLANES = 8
VMEM_LIMIT_BYTES = 56 * 1024 * 1024
BF16_EXACT_INT = 256
EXP_UNDERFLOW = 106.0
NORM_SLACK = 1.02
FIXED_SHIFT_LIMIT = 60.0

N_CAND_GROUPS = 10
N_CAND = N_CAND_GROUPS * SUBLANES
W_PITCH = 136
ROUTE_UNITS = 16


def _layer_norm(r, g, b):
    mu = jnp.mean(r, axis=-1, keepdims=True)
    d = r - mu
    var = jnp.mean(d * d, axis=-1, keepdims=True)
    return d * lax.rsqrt(var + LN_EPS) * g + b


def _split_mod(mod, d):
    return mod[:, :d], 1.0 + mod[:, d:2 * d], 1.0 + mod[:, 2 * d:]


def _params(sem):
    return pltpu.CompilerParams(dimension_semantics=sem, vmem_limit_bytes=VMEM_LIMIT_BYTES)


def _const_spec(shape):
    nd = len(shape)
    return pl.BlockSpec(shape, lambda *_: (0,) * nd, pipeline_mode=pl.Buffered(1))


def _ada_kernel(c_ref, w_ref, b_ref, o_ref):
    c = c_ref[...]
    sc = c / (1.0 + jnp.exp(-c))
    o_ref[0] = jnp.dot(sc, w_ref[0], precision=lax.Precision.HIGHEST,
                       preferred_element_type=F32) + b_ref[0]


def _ada_mods(c, ada_w, ada_b):
    depth, _, d, d3 = ada_w.shape
    nb = c.shape[0]
    nmat = depth * 2
    tn = 1024
    w = ada_w.reshape(nmat, d, d3)
    b = ada_b.reshape(nmat, 1, d3)
    out = pl.pallas_call(
        _ada_kernel,
        grid=(nmat, d3 // tn),
        in_specs=[
            pl.BlockSpec((nb, d), lambda m, n: (0, 0)),
            pl.BlockSpec((1, d, tn), lambda m, n: (m, 0, n)),
            pl.BlockSpec((1, 1, tn), lambda m, n: (m, 0, n)),
        ],
        out_specs=pl.BlockSpec((1, nb, tn), lambda m, n: (m, 0, n)),
        out_shape=jax.ShapeDtypeStruct((nmat, nb, d3), F32),
        compiler_params=_params(("arbitrary", "arbitrary")),
        name="ada_mods",
    )(c, w, b)
    return out.reshape(nmat * nb, 1, d3)


def _emit_stage_output(r, g_ref, b_ref, modp_ref, o_ref, hp_ref):
    xn = _layer_norm(r, g_ref[...], b_ref[...])
    o_ref[...] = xn
    shift, scale, _ = _split_mod(modp_ref[0], xn.shape[1])
    hp_ref[...] = (xn * scale + shift).astype(BF16)


def _conv_kernel(x_ref, mod_ref, modp_ref, win_ref, cw_ref, wout_ref, g_ref, b_ref, o_ref, hp_ref,
                 zbuf, *, tiles_per_seq, alpha):
    i = pl.program_id(0)
    tm, d = x_ref.shape
    x = x_ref[...]
    shift, scale, gate = _split_mod(mod_ref[0], d)
    h = (x * scale + shift).astype(BF16)
    hw = jnp.dot(h, win_ref[...], preferred_element_type=F32)
    gb = hw[:, :d]
    z = hw[:, d:2 * d] * hw[:, 2 * d:]

    @pl.when(i % tiles_per_seq == 0)
    def _():
        zbuf[0:SUBLANES, :] = jnp.zeros((SUBLANES, d), F32)

    zbuf[SUBLANES:, :] = z
    z1 = zbuf[pl.ds(SUBLANES - 1, tm), :]
    z2 = zbuf[pl.ds(SUBLANES - 2, tm), :]
    cw = cw_ref[...]
    zc = cw[2:3] * z + cw[1:2] * z1 + cw[0:1] * z2
    zbuf[0:SUBLANES, :] = zbuf[tm:tm + SUBLANES, :]
    y = jnp.dot((gb * zc).astype(BF16), wout_ref[...], preferred_element_type=F32)
    _emit_stage_output(alpha * x + gate * y, g_ref, b_ref, modp_ref, o_ref, hp_ref)


def _conv_layer(x, mods, mod_row, peer_row, w_in, conv_w, w_out, ln_g, ln_b, *, seq, alpha, tm=512):
    t, d = x.shape
    tiles_per_seq = seq // tm
    return pl.pallas_call(
        functools.partial(_conv_kernel, tiles_per_seq=tiles_per_seq, alpha=alpha),
        grid=(t // tm,),
        in_specs=[
            pl.BlockSpec((tm, d), lambda i: (i, 0)),
            pl.BlockSpec((1, 1, 3 * d), lambda i: (mod_row + i // tiles_per_seq, 0, 0)),
            pl.BlockSpec((1, 1, 3 * d), lambda i: (peer_row + i // tiles_per_seq, 0, 0)),
            _const_spec((d, 3 * d)),
            _const_spec((conv_w.shape[0], d)),
            _const_spec((d, d)),
            _const_spec((1, d)),
            _const_spec((1, d)),
        ],
        out_specs=[pl.BlockSpec((tm, d), lambda i: (i, 0))] * 2,
        out_shape=[jax.ShapeDtypeStruct((t, d), F32), jax.ShapeDtypeStruct((t, d), BF16)],
        scratch_shapes=[pltpu.VMEM((tm + SUBLANES, d), F32)],
        compiler_params=_params(("arbitrary",)),
        name="conv_mixer",
    )(x, mods, mods, w_in.astype(BF16), conv_w, w_out.astype(BF16), ln_g, ln_b)


def _qkv_kernel(x_ref, mod_ref, w_ref, qt_ref, k_ref, vt_ref, *, q_scale):
    tm, d = x_ref.shape
    shift, scale, _ = _split_mod(mod_ref[0], d)
    h = (x_ref[...] * scale + shift).astype(BF16)
    qkv = jnp.dot(h, w_ref[...], preferred_element_type=F32)
    qt_ref[...] = (qkv[:, :d] * q_scale).T.astype(BF16)
    k_ref[...] = qkv[:, d:2 * d].astype(BF16)
    vt_ref[...] = qkv[:, 2 * d:].T.astype(BF16)


def _qkv_proj(x, mods, mod_row, w_qkv, *, seq, q_scale, tm=512):
    t, d = x.shape
    tiles_per_seq = seq // tm
    return pl.pallas_call(
        functools.partial(_qkv_kernel, q_scale=q_scale),
        grid=(t // tm,),
        in_specs=[
            pl.BlockSpec((tm, d), lambda i: (i, 0)),
            pl.BlockSpec((1, 1, 3 * d), lambda i: (mod_row + i // tiles_per_seq, 0, 0)),
            _const_spec((d, 3 * d)),
        ],
        out_specs=[
            pl.BlockSpec((d, tm), lambda i: (0, i)),
            pl.BlockSpec((tm, d), lambda i: (i, 0)),
            pl.BlockSpec((d, tm), lambda i: (0, i)),
        ],
        out_shape=[
            jax.ShapeDtypeStruct((d, t), BF16),
            jax.ShapeDtypeStruct((t, d), BF16),
            jax.ShapeDtypeStruct((d, t), BF16),
        ],
        compiler_params=_params(("arbitrary",)),
        name="attn_qkv",
    )(x, mods, w_qkv.astype(BF16))


def _attn_kernel(slopes_ref, qt_ref, k_ref, vt_ref, lam_ref, g_ref, o_ref, qa_ref, m_ref, acc_ref,
                 kn_ref, *, tq, tk, heads, lambda_init):
    hg = pl.program_id(1)
    qi = pl.program_id(2)
    dh2 = qt_ref.shape[0] // heads
    dh = dh2 // 2
    ones_rows = 2 * SUBLANES
    blocks_per_tile = tq // tk

    feat = lax.broadcasted_iota(jnp.int32, (dh2, tq), 0)
    arow = lax.broadcasted_iota(jnp.int32, (dh2, 2 * tq), 0)
    acol = lax.broadcasted_iota(jnp.int32, (dh2, 2 * tq), 1)
    r = jnp.where(acol >= tq, acol - tq, acol)
    r_lo = (r % BF16_EXACT_INT).astype(F32)
    r_hi = (r - r % BF16_EXACT_INT).astype(F32)
    for g in range(heads):
        slope = slopes_ref[hg * heads + g]
        qt = qt_ref[g * dh2:(g + 1) * dh2, :]
        zero = jnp.zeros_like(qt)
        qa_ref[g, :dh2, :] = jnp.concatenate(
            [jnp.where(feat < dh, qt, zero), jnp.where(feat >= dh, qt, zero)], axis=1)
        qa_ref[g, dh2:, :] = jnp.where(
            arow == 0, -slope * r_lo,
            jnp.where(arow == 1, -slope * r_hi, jnp.where(arow <= 3, slope, 0.0))).astype(BF16)
    kcol = lax.broadcasted_iota(jnp.int32, (tk, dh2), 1)
    krow = lax.broadcasted_iota(jnp.int32, (tk, dh2), 0)
    c_lo = (krow % BF16_EXACT_INT).astype(F32)
    c_hi = (krow - krow % BF16_EXACT_INT).astype(F32)
    k_extra = jnp.where(kcol <= 1, 1.0,
                        jnp.where(kcol == 2, c_lo, jnp.where(kcol == 3, c_hi, 0.0))).astype(BF16)
    ones = jnp.ones((ones_rows, tk), BF16)

    m_ref[...] = jnp.full(m_ref.shape, -jnp.inf, F32)
    acc_ref[...] = jnp.zeros(acc_ref.shape, F32)

    @pl.when(qi == 0)
    def _():
        ones_sq = jnp.ones((dh2, LANES), BF16)
        for g in range(heads):
            def chunk(c, best, g=g):
                kc = k_ref[pl.ds(pl.multiple_of(c * tq, tq), tq), g * dh2:(g + 1) * dh2].astype(F32)
                rows = jnp.dot((kc * kc).astype(BF16), ones_sq, preferred_element_type=F32)
                return jnp.maximum(best, jnp.max(rows, axis=0, keepdims=True))
            kn_ref[g] = lax.fori_loop(0, k_ref.shape[0] // tq, chunk, jnp.zeros((1, LANES), F32))

    def step(j, diagonal):
        start = pl.multiple_of(j * tk, tk)
        delta = qi * tq - j * tk
        head_cols = [slice(g * dh2, (g + 1) * dh2) for g in range(heads)]
        scores, maxima = [], []
        for g, cols in enumerate(head_cols):
            s = jnp.dot(jnp.concatenate([k_ref[pl.ds(start, tk), cols], k_extra], axis=1),
                        qa_ref[g], preferred_element_type=F32)
            if diagonal:
                mrow = lax.broadcasted_iota(jnp.int32, (tk, 2 * tq), 0)
                mcol = lax.broadcasted_iota(jnp.int32, (tk, 2 * tq), 1)
                mq = jnp.where(mcol >= tq, mcol - tq, mcol)
                s = jnp.where(mrow - mq <= delta, s, -jnp.inf)
            scores.append(s)
            maxima.append(jnp.max(s, axis=0, keepdims=True))
        probs, corrs = [], []
        for g, s in enumerate(scores):
            off = delta.astype(F32) * slopes_ref[hg * heads + g]
            m_old = m_ref[g]
            m_new = jnp.maximum(m_old, maxima[g] - off)
            probs.append(jnp.exp(s - (m_new + off)).astype(BF16))
            corrs.append(jnp.exp(m_old - m_new))
            m_ref[g] = m_new
        for g, cols in enumerate(head_cols):
            v_aug = jnp.concatenate([vt_ref[cols, pl.ds(start, tk)], ones], axis=0)
            acc_ref[g] = corrs[g] * acc_ref[g] + jnp.dot(v_aug, probs[g],
                                                         preferred_element_type=F32)

    first_diag = qi * blocks_per_tile
    for d in range(blocks_per_tile):
        step(first_diag + d, True)

    needed = jnp.zeros((1, 2 * tq), F32)
    bound = jnp.zeros((1, 2 * tq), F32)
    for g in range(heads):
        qf = qa_ref[g, :dh2, :].astype(F32)
        q_norm = jnp.sqrt(jnp.sum(qf * qf, axis=0, keepdims=True))
        k_norm = jnp.sqrt(kn_ref[g][:, :1])
        qk = q_norm * k_norm * NORM_SLACK
        reach = (qk - m_ref[g] + EXP_UNDERFLOW) / slopes_ref[hg * heads + g]
        needed = jnp.maximum(needed, jnp.floor((reach - 1.0) / tk) + 1.0)
        bound = jnp.maximum(bound, qk)
    needed = jnp.clip(needed, 0.0, first_diag.astype(F32))
    n_blocks = jnp.max(needed.astype(jnp.int32))

    fixed_shift_ok = jnp.max(bound) * 2.0 < FIXED_SHIFT_LIMIT

    def fixed_shift_step(j):
        start = pl.multiple_of(j * tk, tk)
        delta = qi * tq - j * tk
        scores = [jnp.dot(jnp.concatenate([k_ref[pl.ds(start, tk), g * dh2:(g + 1) * dh2], k_extra],
                                          axis=1), qa_ref[g], preferred_element_type=F32)
                  for g in range(heads)]
        probs = [jnp.exp(s - (m_ref[g] + delta.astype(F32) * slopes_ref[hg * heads + g])).astype(BF16)
                 for g, s in enumerate(scores)]
        for g in range(heads):
            v_aug = jnp.concatenate([vt_ref[g * dh2:(g + 1) * dh2, pl.ds(start, tk)], ones], axis=0)
            acc_ref[g] += jnp.dot(v_aug, probs[g], preferred_element_type=F32)

    @pl.when(fixed_shift_ok)
    def _():
        def body(i, carry):
            fixed_shift_step(first_diag - 1 - i)
            return carry

        lax.fori_loop(0, n_blocks, body, 0)

    @pl.when(jnp.logical_not(fixed_shift_ok))
    def _():
        def body(i, carry):
            step(first_diag - 1 - i, False)
            return carry

        lax.fori_loop(0, n_blocks, body, 0)

    lam = lam_ref[...]
    lam_full = (jnp.exp(jnp.sum(lam[0:1] * lam[1:2], axis=1, keepdims=True))
                - jnp.exp(jnp.sum(lam[2:3] * lam[3:4], axis=1, keepdims=True)) + lambda_init)
    for g in range(heads):
        acc = acc_ref[g]
        on = acc[:dh2] / acc[dh2:dh2 + 1]
        o = on[:, :tq] - lam_full * on[:, tq:]
        o = o * lax.rsqrt(jnp.mean(o * o, axis=0, keepdims=True) + LN_EPS) * g_ref[...]
        o_ref[:, g * dh2:(g + 1) * dh2] = (o * (1.0 - lambda_init)).T.astype(o_ref.dtype)


def _diff_attention(qt, k, vt, lam, subln_g, *, batch, seq, lambda_init, tq=512, tk=512, heads=4):
    d, t = qt.shape
    dh2 = d // ATT_HEADS
    gw = heads * dh2
    nq = seq // tq
    assert tq % tk == 0 and tq < BF16_EXACT_INT * BF16_EXACT_INT
    slopes = 2.0 ** (-(8.0 / ATT_HEADS) * jnp.arange(1, ATT_HEADS + 1, dtype=F32))
    return pl.pallas_call(
        functools.partial(_attn_kernel, tq=tq, tk=tk, heads=heads, lambda_init=lambda_init),
        grid=(batch, ATT_HEADS // heads, nq),
        in_specs=[
            pl.BlockSpec(memory_space=pltpu.SMEM),
            pl.BlockSpec((gw, tq), lambda b, h, i: (h, b * nq + i)),
            pl.BlockSpec((seq, gw), lambda b, h, i: (b, h)),
            pl.BlockSpec((gw, seq), lambda b, h, i: (h, b)),
            pl.BlockSpec(lam.shape, lambda b, h, i: (0, 0)),
            pl.BlockSpec((dh2, 1), lambda b, h, i: (0, 0)),
        ],
        out_specs=pl.BlockSpec((tq, gw), lambda b, h, i: (b * nq + i, h)),
        out_shape=jax.ShapeDtypeStruct((t, d), BF16),
        scratch_shapes=[
            pltpu.VMEM((heads, 2 * dh2, 2 * tq), BF16),
            pltpu.VMEM((heads, 1, 2 * tq), F32),
            pltpu.VMEM((heads, dh2 + 2 * SUBLANES, 2 * tq), F32),
            pltpu.VMEM((heads, 1, LANES), F32),
        ],
        compiler_params=_params(("arbitrary", "arbitrary", "arbitrary")),
        name="diff_attention",
    )(slopes, qt, k, vt, lam, subln_g.reshape(dh2, 1))


def _oproj_kernel(o_ref, x_ref, mod_ref, modp_ref, w_ref, g_ref, b_ref, out_ref, hp_ref, *, alpha):
    tm, d = x_ref.shape
    _, _, gate = _split_mod(mod_ref[0], d)
    y = jnp.dot(o_ref[...], w_ref[...], preferred_element_type=F32)
    _emit_stage_output(alpha * x_ref[...] + gate * y, g_ref, b_ref, modp_ref, out_ref, hp_ref)


def _out_proj(o, x, mods, mod_row, peer_row, w_o, ln_g, ln_b, *, seq, alpha, tm=512):
    t, d = x.shape
    tiles_per_seq = seq // tm
    return pl.pallas_call(
        functools.partial(_oproj_kernel, alpha=alpha),
        grid=(t // tm,),
        in_specs=[
            pl.BlockSpec((tm, d), lambda i: (i, 0)),
            pl.BlockSpec((tm, d), lambda i: (i, 0)),
            pl.BlockSpec((1, 1, 3 * d), lambda i: (mod_row + i // tiles_per_seq, 0, 0)),
            pl.BlockSpec((1, 1, 3 * d), lambda i: (peer_row + i // tiles_per_seq, 0, 0)),
            _const_spec((d, d)),
            _const_spec((1, d)),
            _const_spec((1, d)),
        ],
        out_specs=[pl.BlockSpec((tm, d), lambda i: (i, 0))] * 2,
        out_shape=[jax.ShapeDtypeStruct((t, d), F32), jax.ShapeDtypeStruct((t, d), BF16)],
        compiler_params=_params(("arbitrary",)),
        name="attn_out_proj",
    )(o, x, mods, mods, w_o.astype(BF16), ln_g, ln_b)


def _top16_rows(s):
    n, tm = s.shape
    rows = lax.broadcasted_iota(jnp.int32, (n, tm), 0).astype(F32)
    vals, idxs = [], []
    tie = None
    for _ in range(PEER_TOPK):
        m = jnp.max(s, axis=0, keepdims=True)
        if tie is not None:
            m = m + tie
        idx = jnp.min(jnp.where(s == m, rows, float(n)), axis=0, keepdims=True)
        s = jnp.where(rows == idx, -jnp.inf, s)
        vals.append(m)
        idxs.append(idx)
        tie = yield
    return vals, idxs


def _candidate_layout(tm):
    slot = lax.broadcasted_iota(jnp.int32, (N_CAND, tm), 0)
    grp = slot // SUBLANES
    r = slot % SUBLANES
    a = jnp.where(grp <= 1, 0, jnp.where(grp == N_CAND_GROUPS - 1, SUBLANES + r, grp - 1))
    b = jnp.where(grp == 0, r, jnp.where(grp == 1, SUBLANES + r,
                                         jnp.where(grp == N_CAND_GROUPS - 1, 0, r)))
    valid = (a + 1) * (b + 1) <= PEER_TOPK
    flat = (a * PEER_TOPK + b).astype(F32)
    return valid, flat


def _candidates(rows0, rows1):
    lo1 = jnp.concatenate(rows1[:SUBLANES], axis=0)
    hi1 = jnp.concatenate(rows1[SUBLANES:], axis=0)
    hi0 = jnp.concatenate(rows0[SUBLANES:], axis=0)
    groups = [rows0[0] + lo1, rows0[0] + hi1]
    groups += [rows0[a] + lo1 for a in range(1, SUBLANES)]
    groups.append(hi0 + rows1[0])
    return jnp.concatenate(groups, axis=0)


def _route_head(q_head, keys_ref, valid, flat):
    scores = [lax.dot_general(keys_ref[p], q_head[:, p * PEER_KEYS:(p + 1) * PEER_KEYS],
                              (((1,), (1,)), ((), ())), preferred_element_type=F32)
              for p in range(2)]
    yield
    vals, idxs = [], []
    for st in scores:
        v, ix = yield from _top16_rows(st)
        vals.append(v)
        idxs.append(ix)
    cand = jnp.where(valid, _candidates(vals[0], vals[1]), -jnp.inf)
    code = _candidates([ix * float(PEER_KEYS) for ix in idxs[0]], idxs[1])
    top_s, top_code = [], []
    tie = None
    for _ in range(PEER_TOPK):
        m = jnp.max(cand, axis=0, keepdims=True)
        if tie is not None:
            m = m + tie
        fmin = jnp.min(jnp.where(cand == m, flat, float(PEER_TOPK * PEER_TOPK)),
                       axis=0, keepdims=True)
        hit = flat == fmin
        top_code.append(jnp.sum(jnp.where(hit, code, 0.0), axis=0, keepdims=True))
        cand = jnp.where(hit, -jnp.inf, cand)
        top_s.append(m)
        tie = yield
    ts = jnp.concatenate(top_s, axis=0)
    e = jnp.exp(ts - top_s[0])
    return (jnp.concatenate(top_code, axis=0).astype(jnp.int32),
            e / jnp.sum(e, axis=0, keepdims=True))


ROUTE_YIELDS = 3 * PEER_TOPK


def _zero_row(x, width):
    bits = lax.bitcast_convert_type(x[:1, :LANES], jnp.uint32)
    zero = lax.bitcast_convert_type((bits >> 16) >> 16, F32)
    return jnp.concatenate([zero] * (width // LANES), axis=1)


def _peer_kernel(h_ref, hn_ref, x_ref, mod_ref, wq_ref, keys_ref, ut_ref, v_ref, g_ref, b_ref,
                 o_ref, q_ref, code_ref, gate_ref, ci_ref, cg_ref, w_ref, acc_ref, hs_ref,
                 *, alpha):
    i = pl.program_id(0)
    j = pl.program_id(1)
    nj = pl.num_programs(1)
    tm, d = x_ref.shape
    tn = ut_ref.shape[1]
    nhk = PEER_HEADS * PEER_TOPK
    parts = ROUTE_UNITS // PEER_HEADS
    tp = tm // parts
    slot = i % 2
    valid, flat = _candidate_layout(tp)

    def prepare_queries(h):
        q = jnp.dot(h, wq_ref[...], preferred_element_type=F32).astype(BF16)
        for hd in range(PEER_HEADS):
            q_ref[hd] = q[:, hd * 2 * PEER_KEYS:(hd + 1) * 2 * PEER_KEYS]

    def route_unit(u, dst):
        hd = u // parts
        part = u % parts
        rows = pl.ds(pl.multiple_of(part * tp, tp), tp)
        code, gate = yield from _route_head(q_ref[hd, rows, :], keys_ref, valid, flat)
        krows = pl.ds(pl.multiple_of(hd * PEER_TOPK, PEER_TOPK), PEER_TOPK)
        for g in range(tp // LANES):
            lanes = slice(g * LANES, (g + 1) * LANES)
            code_ref[dst, part * (tp // LANES) + g, krows, :] = code[:, lanes]
            gate_ref[dst, part * (tp // LANES) + g, krows, :] = gate[:, lanes]

    @pl.when((i == 0) & (j == 0))
    def _():
        prepare_queries(h_ref[...])

        def unit(u, carry):
            for _ in route_unit(u, 0):
                pass
            return carry

        lax.fori_loop(0, ROUTE_UNITS, unit, 0)

    @pl.when(j == 0)
    def _():
        for grp in range(tm // LANES):
            code = code_ref[slot, grp].T
            ci_ref[grp * LANES:(grp + 1) * LANES, :] = code // PEER_KEYS
            cg_ref[grp * LANES:(grp + 1) * LANES, :] = gate_ref[slot, grp].T
        key_rows = lax.broadcasted_iota(jnp.int32, (PEER_KEYS, nhk), 0).astype(BF16)
        key_lanes = lax.broadcasted_iota(jnp.int32, (nhk, PEER_KEYS), 1).astype(BF16)
        zero = jnp.zeros((PEER_KEYS, nhk), BF16)
        one = jnp.ones((nhk, PEER_KEYS), BF16)

        def token_group(grp, carry):
            base = pl.multiple_of(grp * LANES, LANES)
            second = (code_ref[slot, grp] % PEER_KEYS).astype(F32)

            def gate_bits(u):
                ri = ci_ref[pl.ds(base + u, 1), :].astype(BF16)
                rg = cg_ref[pl.ds(base + u, 1), :].astype(BF16)
                cj = jnp.broadcast_to(second[:, u:u + 1], (nhk, PEER_KEYS)).astype(BF16)
                pt = jnp.where(key_rows == ri, jnp.broadcast_to(rg, zero.shape), zero)
                qm = jnp.where(key_lanes == cj, one, jnp.zeros_like(one))
                wt = jnp.dot(pt, qm, preferred_element_type=F32)
                return lax.bitcast_convert_type(wt, jnp.uint32)

            for u in range(0, LANES, 2):
                packed = (gate_bits(u) >> 16) | (gate_bits(u + 1) & jnp.uint32(0xFFFF0000))
                pair = (base + u) // 2
                w_ref[pl.ds(pl.multiple_of(pair * W_PITCH, SUBLANES), PEER_KEYS), :] = packed
            return carry

        lax.fori_loop(0, tm // LANES, token_group, 0)
        acc_ref[...] = jnp.zeros(acc_ref.shape, F32)
        hs_ref[...] = h_ref[...]
        prepare_queries(hn_ref[...])

    route = route_unit(j, 1 - slot)
    next(route)
    chunk = 4 * PEER_KEYS
    n_chunks = tn // chunk

    def advance(after, count):
        try:
            route.send(None if after is None else _zero_row(after, tp))
            for _ in range(count - 1):
                next(route)
        except StopIteration:
            pass

    halves = 2
    th = tm // halves
    per_piece = -(-ROUTE_YIELDS // (2 * halves * n_chunks))

    def expert_inputs(half):
        rows = slice(half * th, (half + 1) * th)
        h = hs_ref[rows, :]
        pieces = []
        for r in range(n_chunks):
            c0 = r * chunk
            a = jnp.dot(h, us[:, c0:c0 + chunk], preferred_element_type=F32)
            advance(a, per_piece)
            act = (a * (1.0 + lax.erf(a * (1.0 / math.sqrt(2.0))))).astype(BF16)
            for e in range(chunk // PEER_KEYS):
                first_key = (j * n_chunks + r) * (chunk // PEER_KEYS) + e
                words = w_ref[pl.ds(half * (th // 2) * W_PITCH + first_key, th // 2, stride=W_PITCH), :]
                gates = pltpu.bitcast(words, BF16)
                pieces.append(gates * act[:, e * PEER_KEYS:(e + 1) * PEER_KEYS])
        return jnp.concatenate(pieces, axis=1)

    def expert_outputs(half, z):
        rows = slice(half * th, (half + 1) * th)
        for c in range(d // chunk):
            cols = slice(c * chunk, (c + 1) * chunk)
            zv = jnp.dot(z, vs[:, cols], preferred_element_type=F32)
            acc_ref[rows, cols] += zv
            advance(zv, per_piece)

    us = pltpu.bitcast(ut_ref[...], BF16)
    vs = pltpu.bitcast(v_ref[...], BF16)
    advance(None, per_piece)
    z_prev = expert_inputs(0)
    for half in range(1, halves):
        z_next = expert_inputs(half)
        expert_outputs(half - 1, z_prev)
        z_prev = z_next
    expert_outputs(halves - 1, z_prev)
    for _ in route:
        pass

    @pl.when(j == nj - 1)
    def _():
        _, _, gate = _split_mod(mod_ref[0], d)
        o_ref[...] = _layer_norm(alpha * x_ref[...] + gate * acc_ref[...], g_ref[...], b_ref[...])


def _peer_layer(h, x, mods, mod_row, w_q, sub_keys, u_t, v, ln_g, ln_b, *, seq, alpha, tm=512):
    t, d = x.shape
    n = u_t.shape[1]
    tn = n // ROUTE_UNITS
    nhk = PEER_HEADS * PEER_TOPK
    parts = ROUTE_UNITS // PEER_HEADS
    tiles_per_seq = seq // tm
    last = t // tm - 1
    return pl.pallas_call(
        functools.partial(_peer_kernel, alpha=alpha),
        grid=(t // tm, n // tn),
        in_specs=[
            pl.BlockSpec((tm, d), lambda i, j: (i, 0)),
            pl.BlockSpec((tm, d), lambda i, j: (jnp.minimum(i + 1, last), 0)),
            pl.BlockSpec((tm, d), lambda i, j: (i, 0)),
            pl.BlockSpec((1, 1, 3 * d), lambda i, j: (mod_row + i // tiles_per_seq, 0, 0)),
            _const_spec(w_q.shape),
            _const_spec(sub_keys.shape),
            pl.BlockSpec((d // 2, tn), lambda i, j: (0, j)),
            pl.BlockSpec((tn // 2, d), lambda i, j: (j, 0)),
            _const_spec((1, d)),
            _const_spec((1, d)),
        ],
        out_specs=pl.BlockSpec((tm, d), lambda i, j: (i, 0)),
        out_shape=jax.ShapeDtypeStruct((t, d), F32),
        scratch_shapes=[
            pltpu.VMEM((PEER_HEADS, tm, 2 * PEER_KEYS), BF16),
            pltpu.VMEM((2, tm // LANES, nhk, LANES), jnp.int32),
            pltpu.VMEM((2, tm // LANES, nhk, LANES), F32),
            pltpu.VMEM((tm, nhk), jnp.int32),
            pltpu.VMEM((tm, nhk), F32),
            pltpu.VMEM((tm // 2 * W_PITCH, PEER_KEYS), jnp.uint32),
            pltpu.VMEM((tm, d), F32),
            pltpu.VMEM((tm, d), BF16),
        ],
        compiler_params=_params(("arbitrary", "arbitrary")),
        name="peer_layer",
    )(h, h, x, mods, w_q.astype(BF16), sub_keys.astype(BF16), u_t, v, ln_g, ln_b)


def _pack_kernel(w_ref, o_ref, t_ref, *, transpose, scale):
    w = w_ref[0]
    if scale != 1.0:
        w = w * scale
    if transpose:
        w = w.T
    rows, cols = w.shape
    for c in range(cols // LANES):
        t_ref[c] = w[:, c * LANES:(c + 1) * LANES]
        even = t_ref[c, pl.ds(0, rows // 2, stride=2), :].astype(BF16).astype(F32)
        odd = t_ref[c, pl.ds(1, rows // 2, stride=2), :].astype(BF16).astype(F32)
        o_ref[:, c * LANES:(c + 1) * LANES] = (
            (lax.bitcast_convert_type(even, jnp.uint32) >> 16)
            | (lax.bitcast_convert_type(odd, jnp.uint32) & jnp.uint32(0xFFFF0000)))


def _packed_rows(w, layer, *, transpose, scale=1.0, tb=2048):
    _, n, d = w.shape
    if transpose:
        out_shape, out_block, out_map = (d // 2, n), (d // 2, tb), (lambda i: (0, i))
        t_shape = (tb // LANES, d, LANES)
    else:
        out_shape, out_block, out_map = (n // 2, d), (tb // 2, d), (lambda i: (i, 0))
        t_shape = (d // LANES, tb, LANES)
    return pl.pallas_call(
        functools.partial(_pack_kernel, transpose=transpose, scale=scale),
        grid=(n // tb,),
        in_specs=[pl.BlockSpec((1, tb, d), lambda i: (layer, i, 0))],
        out_specs=pl.BlockSpec(out_block, out_map),
        out_shape=jax.ShapeDtypeStruct(out_shape, jnp.uint32),
        scratch_shapes=[pltpu.VMEM(t_shape, F32)],
        compiler_params=_params(("arbitrary",)),
        name="pack_weights",
    )(w)


def _lambda_init(layer_idx):
    return 0.8 - 0.6 * math.exp(-0.3 * layer_idx)


def kernel(x, c, ada_w, ada_b, ln_g, ln_b, conv_w_in, conv_w, conv_w_out, attn_w_qkv, attn_lambda,
           attn_subln_g, attn_w_o, peer_w_q, peer_sub_keys, peer_u, peer_v):
    batch, seq, d = x.shape
    depth = ada_w.shape[0]
    alpha = (2.0 * depth) ** 0.25
    head_dim = d // (2 * ATT_HEADS)
    assert 2 * head_dim == LANES, "one attention head must fill one 128-lane block"
    assert peer_u.shape[1] == PEER_KEYS * PEER_KEYS and peer_sub_keys.shape[2:] == (PEER_KEYS, PEER_KEYS)
    assert peer_w_q.shape[2] == PEER_HEADS * 2 * PEER_KEYS
    mods = _ada_mods(c, ada_w, ada_b)
    xt = x.reshape(batch * seq, d)
    for i in range(depth):
        j = i // N_MIXERS
        row = (2 * i) * batch
        prow = (2 * i + 1) * batch
        g0, b0 = ln_g[i, 0].reshape(1, d), ln_b[i, 0].reshape(1, d)
        if i % N_MIXERS == 0:
            xt, h = _conv_layer(xt, mods, row, prow, conv_w_in[j], conv_w[j], conv_w_out[j], g0, b0,
                                seq=seq, alpha=alpha)
        else:
            qt, k, vt = _qkv_proj(xt, mods, row, attn_w_qkv[j], seq=seq, q_scale=head_dim ** -0.5)
            o = _diff_attention(qt, k, vt, attn_lambda[j], attn_subln_g[j], batch=batch, seq=seq,
                                lambda_init=_lambda_init(i))
            xt, h = _out_proj(o, xt, mods, row, prow, attn_w_o[j], g0, b0, seq=seq, alpha=alpha)
        g1, b1 = ln_g[i, 1].reshape(1, d), ln_b[i, 1].reshape(1, d)
        xt = _peer_layer(h, xt, mods, prow, peer_w_q[i], peer_sub_keys[i],
                         _packed_rows(peer_u, i, transpose=True),
                         _packed_rows(peer_v, i, transpose=False, scale=0.5), g1, b1,
                         seq=seq, alpha=alpha)
    return xt.reshape(batch, seq, d)
```

```python
import functools
import math

import jax
import jax.numpy as jnp
from jax import lax
from jax.experimental import pallas as pl
from jax.experimental.pallas import tpu as pltpu

F32 = jnp.float32
BF16 = jnp.bfloat16

N_MIXERS = 2
ATT_HEADS = 8
PEER_HEADS = 8
PEER_KEYS = 128
PEER_TOPK = 16
LN_EPS = 1e-5

LANES = 128
SUBLANES = 8
VMEM_LIMIT_BYTES = 56 * 1024 * 1024
BF16_EXACT_INT = 256
EXP_UNDERFLOW = 106.0
NORM_SLACK = 1.02
FIXED_SHIFT_LIMIT = 60.0

N_CAND_GROUPS = 7
N_CAND = N_CAND_GROUPS * SUBLANES
W_PITCH = 136
ROUTE_UNITS = 16


def _layer_norm(r, g, b):
    mu = jnp.mean(r, axis=-1, keepdims=True)
    d = r - mu
    var = jnp.mean(d * d, axis=-1, keepdims=True)
    return d * lax.rsqrt(var + LN_EPS) * g + b


def _split_mod(mod, d):
    return mod[:, :d], 1.0 + mod[:, d:2 * d], 1.0 + mod[:, 2 * d:]


def _params(sem):
    return pltpu.CompilerParams(dimension_semantics=sem, vmem_limit_bytes=VMEM_LIMIT_BYTES)


def _const_spec(shape):
    nd = len(shape)
    return pl.BlockSpec(shape, lambda *_: (0,) * nd, pipeline_mode=pl.Buffered(1))


def _ada_kernel(c_ref, w_ref, b_ref, o_ref):
    c = c_ref[...]
    sc = c / (1.0 + jnp.exp(-c))
    o_ref[0] = jnp.dot(sc, w_ref[0], precision=lax.Precision.HIGHEST,
                       preferred_element_type=F32) + b_ref[0]


def _ada_mods(c, ada_w, ada_b):
    depth, _, d, d3 = ada_w.shape
    nb = c.shape[0]
    nmat = depth * 2
    tn = 1024
    w = ada_w.reshape(nmat, d, d3)
    b = ada_b.reshape(nmat, 1, d3)
    out = pl.pallas_call(
        _ada_kernel,
        grid=(nmat, d3 // tn),
        in_specs=[
            pl.BlockSpec((nb, d), lambda m, n: (0, 0)),
            pl.BlockSpec((1, d, tn), lambda m, n: (m, 0, n)),
            pl.BlockSpec((1, 1, tn), lambda m, n: (m, 0, n)),
        ],
        out_specs=pl.BlockSpec((1, nb, tn), lambda m, n: (m, 0, n)),
        out_shape=jax.ShapeDtypeStruct((nmat, nb, d3), F32),
        compiler_params=_params(("arbitrary", "arbitrary")),
        name="ada_mods",
    )(c, w, b)
    return out.reshape(nmat * nb, 1, d3)


def _emit_stage_output(r, g_ref, b_ref, modp_ref, o_ref, hp_ref):
    xn = _layer_norm(r, g_ref[...], b_ref[...])
    o_ref[...] = xn
    shift, scale, _ = _split_mod(modp_ref[0], xn.shape[1])
    hp_ref[...] = (xn * scale + shift).astype(BF16)


def _conv_kernel(x_ref, mod_ref, modp_ref, win_ref, cw_ref, wout_ref, g_ref, b_ref, o_ref, hp_ref,
                 zbuf, *, tiles_per_seq, alpha):
    i = pl.program_id(0)
    tm, d = x_ref.shape
    x = x_ref[...]
    shift, scale, gate = _split_mod(mod_ref[0], d)
    h = (x * scale + shift).astype(BF16)
    hw = jnp.dot(h, win_ref[...], preferred_element_type=F32)
    gb = hw[:, :d]
    z = hw[:, d:2 * d] * hw[:, 2 * d:]

    @pl.when(i % tiles_per_seq == 0)
    def _():
        zbuf[0:SUBLANES, :] = jnp.zeros((SUBLANES, d), F32)

    zbuf[SUBLANES:, :] = z
    z1 = zbuf[pl.ds(SUBLANES - 1, tm), :]
    z2 = zbuf[pl.ds(SUBLANES - 2, tm), :]
    cw = cw_ref[...]
    zc = cw[2:3] * z + cw[1:2] * z1 + cw[0:1] * z2
    zbuf[0:SUBLANES, :] = zbuf[tm:tm + SUBLANES, :]
    y = jnp.dot((gb * zc).astype(BF16), wout_ref[...], preferred_element_type=F32)
    _emit_stage_output(alpha * x + gate * y, g_ref, b_ref, modp_ref, o_ref, hp_ref)


def _conv_layer(x, mods, mod_row, peer_row, w_in, conv_w, w_out, ln_g, ln_b, *, seq, alpha, tm=512):
    t, d = x.shape
    tiles_per_seq = seq // tm
    return pl.pallas_call(
        functools.partial(_conv_kernel, tiles_per_seq=tiles_per_seq, alpha=alpha),
        grid=(t // tm,),
        in_specs=[
            pl.BlockSpec((tm, d), lambda i: (i, 0)),
            pl.BlockSpec((1, 1, 3 * d), lambda i: (mod_row + i // tiles_per_seq, 0, 0)),
            pl.BlockSpec((1, 1, 3 * d), lambda i: (peer_row + i // tiles_per_seq, 0, 0)),
            _const_spec((d, 3 * d)),
            _const_spec((conv_w.shape[0], d)),
            _const_spec((d, d)),
            _const_spec((1, d)),
            _const_spec((1, d)),
        ],
        out_specs=[pl.BlockSpec((tm, d), lambda i: (i, 0))] * 2,
        out_shape=[jax.ShapeDtypeStruct((t, d), F32), jax.ShapeDtypeStruct((t, d), BF16)],
        scratch_shapes=[pltpu.VMEM((tm + SUBLANES, d), F32)],
        compiler_params=_params(("arbitrary",)),
        name="conv_mixer",
    )(x, mods, mods, w_in.astype(BF16), conv_w, w_out.astype(BF16), ln_g, ln_b)


def _qkv_kernel(x_ref, mod_ref, w_ref, qt_ref, k_ref, vt_ref, *, q_scale):
    tm, d = x_ref.shape
    shift, scale, _ = _split_mod(mod_ref[0], d)
    h = (x_ref[...] * scale + shift).astype(BF16)
    qkv = jnp.dot(h, w_ref[...], preferred_element_type=F32)
    qt_ref[...] = (qkv[:, :d] * q_scale).T.astype(BF16)
    k_ref[...] = qkv[:, d:2 * d].astype(BF16)
    vt_ref[...] = qkv[:, 2 * d:].T.astype(BF16)


def _qkv_proj(x, mods, mod_row, w_qkv, *, seq, q_scale, tm=512):
    t, d = x.shape
    tiles_per_seq = seq // tm
    return pl.pallas_call(
        functools.partial(_qkv_kernel, q_scale=q_scale),
        grid=(t // tm,),
        in_specs=[
            pl.BlockSpec((tm, d), lambda i: (i, 0)),
            pl.BlockSpec((1, 1, 3 * d), lambda i: (mod_row + i // tiles_per_seq, 0, 0)),
            _const_spec((d, 3 * d)),
        ],
        out_specs=[
            pl.BlockSpec((d, tm), lambda i: (0, i)),
            pl.BlockSpec((tm, d), lambda i: (i, 0)),
            pl.BlockSpec((d, tm), lambda i: (0, i)),
        ],
        out_shape=[
            jax.ShapeDtypeStruct((d, t), BF16),
            jax.ShapeDtypeStruct((t, d), BF16),
            jax.ShapeDtypeStruct((d, t), BF16),
        ],
        compiler_params=_params(("arbitrary",)),
        name="attn_qkv",
    )(x, mods, w_qkv.astype(BF16))


def _attn_kernel(slopes_ref, qt_ref, k_ref, vt_ref, lam_ref, g_ref, o_ref, qa_ref, m_ref, acc_ref,
                 kn_ref, *, tq, tk, heads, lambda_init):
    hg = pl.program_id(1)
    qi = pl.program_id(2)
    dh2 = qt_ref.shape[0] // heads
    dh = dh2 // 2
    ones_rows = 2 * SUBLANES
    blocks_per_tile = tq // tk

    feat = lax.broadcasted_iota(jnp.int32, (dh2, tq), 0)
    arow = lax.broadcasted_iota(jnp.int32, (dh2, 2 * tq), 0)
    acol = lax.broadcasted_iota(jnp.int32, (dh2, 2 * tq), 1)
    r = jnp.where(acol >= tq, acol - tq, acol)
    r_lo = (r % BF16_EXACT_INT).astype(F32)
    r_hi = (r - r % BF16_EXACT_INT).astype(F32)
    for g in range(heads):
        slope = slopes_ref[hg * heads + g]
        qt = qt_ref[g * dh2:(g + 1) * dh2, :]
        zero = jnp.zeros_like(qt)
        qa_ref[g, :dh2, :] = jnp.concatenate(
            [jnp.where(feat < dh, qt, zero), jnp.where(feat >= dh, qt, zero)], axis=1)
        qa_ref[g, dh2:, :] = jnp.where(
            arow == 0, -slope * r_lo,
            jnp.where(arow == 1, -slope * r_hi, jnp.where(arow <= 3, slope, 0.0))).astype(BF16)
    kcol = lax.broadcasted_iota(jnp.int32, (tk, dh2), 1)
    krow = lax.broadcasted_iota(jnp.int32, (tk, dh2), 0)
    c_lo = (krow % BF16_EXACT_INT).astype(F32)
    c_hi = (krow - krow % BF16_EXACT_INT).astype(F32)
    k_extra = jnp.where(kcol <= 1, 1.0,
                        jnp.where(kcol == 2, c_lo, jnp.where(kcol == 3, c_hi, 0.0))).astype(BF16)
    ones = jnp.ones((ones_rows, tk), BF16)

    m_ref[...] = jnp.full(m_ref.shape, -jnp.inf, F32)
    acc_ref[...] = jnp.zeros(acc_ref.shape, F32)

    @pl.when(qi == 0)
    def _():
        ones_sq = jnp.ones((dh2, LANES), BF16)
        for g in range(heads):
            def chunk(c, best, g=g):
                kc = k_ref[pl.ds(pl.multiple_of(c * tq, tq), tq), g * dh2:(g + 1) * dh2].astype(F32)
                rows = jnp.dot((kc * kc).astype(BF16), ones_sq, preferred_element_type=F32)
                return jnp.maximum(best, jnp.max(rows, axis=0, keepdims=True))
            kn_ref[g] = lax.fori_loop(0, k_ref.shape[0] // tq, chunk, jnp.zeros((1, LANES), F32))

    def step(j, diagonal):
        start = pl.multiple_of(j * tk, tk)
        delta = qi * tq - j * tk
        head_cols = [slice(g * dh2, (g + 1) * dh2) for g in range(heads)]
        scores, maxima = [], []
        for g, cols in enumerate(head_cols):
            s = jnp.dot(jnp.concatenate([k_ref[pl.ds(start, tk), cols], k_extra], axis=1),
                        qa_ref[g], preferred_element_type=F32)
            if diagonal:
                mrow = lax.broadcasted_iota(jnp.int32, (tk, 2 * tq), 0)
                mcol = lax.broadcasted_iota(jnp.int32, (tk, 2 * tq), 1)
                mq = jnp.where(mcol >= tq, mcol - tq, mcol)
                s = jnp.where(mrow - mq <= delta, s, -jnp.inf)
            scores.append(s)
            maxima.append(jnp.max(s, axis=0, keepdims=True))
        probs, corrs = [], []
        for g, s in enumerate(scores):
            off = delta.astype(F32) * slopes_ref[hg * heads + g]
            m_old = m_ref[g]
            m_new = jnp.maximum(m_old, maxima[g] - off)
            probs.append(jnp.exp(s - (m_new + off)).astype(BF16))
            corrs.append(jnp.exp(m_old - m_new))
            m_ref[g] = m_new
        for g, cols in enumerate(head_cols):
            v_aug = jnp.concatenate([vt_ref[cols, pl.ds(start, tk)], ones], axis=0)
            acc_ref[g] = corrs[g] * acc_ref[g] + jnp.dot(v_aug, probs[g],
                                                         preferred_element_type=F32)

    first_diag = qi * blocks_per_tile
    for d in range(blocks_per_tile):
        step(first_diag + d, True)

    needed = jnp.zeros((1, 2 * tq), F32)
    bound = jnp.zeros((1, 2 * tq), F32)
    for g in range(heads):
        qf = qa_ref[g, :dh2, :].astype(F32)
        q_norm = jnp.sqrt(jnp.sum(qf * qf, axis=0, keepdims=True))
        k_norm = jnp.sqrt(kn_ref[g][:, :1])
        qk = q_norm * k_norm * NORM_SLACK
        reach = (qk - m_ref[g] + EXP_UNDERFLOW) / slopes_ref[hg * heads + g]
        needed = jnp.maximum(needed, jnp.floor((reach - 1.0) / tk) + 1.0)
        bound = jnp.maximum(bound, qk)
    needed = jnp.clip(needed, 0.0, first_diag.astype(F32))
    n_blocks = jnp.max(needed.astype(jnp.int32))

    fixed_shift_ok = jnp.max(bound) * 2.0 < FIXED_SHIFT_LIMIT

    def fixed_shift_step(j):
        start = pl.multiple_of(j * tk, tk)
        delta = qi * tq - j * tk
        scores = [jnp.dot(jnp.concatenate([k_ref[pl.ds(start, tk), g * dh2:(g + 1) * dh2], k_extra],
                                          axis=1), qa_ref[g], preferred_element_type=F32)
                  for g in range(heads)]
        probs = [jnp.exp(s - (m_ref[g] + delta.astype(F32) * slopes_ref[hg * heads + g])).astype(BF16)
                 for g, s in enumerate(scores)]
        for g in range(heads):
            v_aug = jnp.concatenate([vt_ref[g * dh2:(g + 1) * dh2, pl.ds(start, tk)], ones], axis=0)
            acc_ref[g] += jnp.dot(v_aug, probs[g], preferred_element_type=F32)

    @pl.when(fixed_shift_ok)
    def _():
        def body(i, carry):
            fixed_shift_step(first_diag - 1 - i)
            return carry

        lax.fori_loop(0, n_blocks, body, 0)

    @pl.when(jnp.logical_not(fixed_shift_ok))
    def _():
        def body(i, carry):
            step(first_diag - 1 - i, False)
            return carry

        lax.fori_loop(0, n_blocks, body, 0)

    lam = lam_ref[...]
    lam_full = (jnp.exp(jnp.sum(lam[0:1] * lam[1:2], axis=1, keepdims=True))
                - jnp.exp(jnp.sum(lam[2:3] * lam[3:4], axis=1, keepdims=True)) + lambda_init)
    for g in range(heads):
        acc = acc_ref[g]
        on = acc[:dh2] / acc[dh2:dh2 + 1]
        o = on[:, :tq] - lam_full * on[:, tq:]
        o = o * lax.rsqrt(jnp.mean(o * o, axis=0, keepdims=True) + LN_EPS) * g_ref[...]
        o_ref[:, g * dh2:(g + 1) * dh2] = (o * (1.0 - lambda_init)).T.astype(o_ref.dtype)


def _diff_attention(qt, k, vt, lam, subln_g, *, batch, seq, lambda_init, tq=512, tk=512, heads=4):
    d, t = qt.shape
    dh2 = d // ATT_HEADS
    gw = heads * dh2
    nq = seq // tq
    assert tq % tk == 0 and tq < BF16_EXACT_INT * BF16_EXACT_INT
    slopes = 2.0 ** (-(8.0 / ATT_HEADS) * jnp.arange(1, ATT_HEADS + 1, dtype=F32))
    return pl.pallas_call(
        functools.partial(_attn_kernel, tq=tq, tk=tk, heads=heads, lambda_init=lambda_init),
        grid=(batch, ATT_HEADS // heads, nq),
        in_specs=[
            pl.BlockSpec(memory_space=pltpu.SMEM),
            pl.BlockSpec((gw, tq), lambda b, h, i: (h, b * nq + i)),
            pl.BlockSpec((seq, gw), lambda b, h, i: (b, h)),
            pl.BlockSpec((gw, seq), lambda b, h, i: (h, b)),
            pl.BlockSpec(lam.shape, lambda b, h, i: (0, 0)),
            pl.BlockSpec((dh2, 1), lambda b, h, i: (0, 0)),
        ],
        out_specs=pl.BlockSpec((tq, gw), lambda b, h, i: (b * nq + i, h)),
        out_shape=jax.ShapeDtypeStruct((t, d), BF16),
        scratch_shapes=[
            pltpu.VMEM((heads, 2 * dh2, 2 * tq), BF16),
            pltpu.VMEM((heads, 1, 2 * tq), F32),
            pltpu.VMEM((heads, dh2 + 2 * SUBLANES, 2 * tq), F32),
            pltpu.VMEM((heads, 1, LANES), F32),
        ],
        compiler_params=_params(("arbitrary", "arbitrary", "arbitrary")),
        name="diff_attention",
    )(slopes, qt, k, vt, lam, subln_g.reshape(dh2, 1))


def _oproj_kernel(o_ref, x_ref, mod_ref, modp_ref, w_ref, g_ref, b_ref, out_ref, hp_ref, *, alpha):
    tm, d = x_ref.shape
    _, _, gate = _split_mod(mod_ref[0], d)
    y = jnp.dot(o_ref[...], w_ref[...], preferred_element_type=F32)
    _emit_stage_output(alpha * x_ref[...] + gate * y, g_ref, b_ref, modp_ref, out_ref, hp_ref)


def _out_proj(o, x, mods, mod_row, peer_row, w_o, ln_g, ln_b, *, seq, alpha, tm=512):
    t, d = x.shape
    tiles_per_seq = seq // tm
    return pl.pallas_call(
        functools.partial(_oproj_kernel, alpha=alpha),
        grid=(t // tm,),
        in_specs=[
            pl.BlockSpec((tm, d), lambda i: (i, 0)),
            pl.BlockSpec((tm, d), lambda i: (i, 0)),
            pl.BlockSpec((1, 1, 3 * d), lambda i: (mod_row + i // tiles_per_seq, 0, 0)),
            pl.BlockSpec((1, 1, 3 * d), lambda i: (peer_row + i // tiles_per_seq, 0, 0)),
            _const_spec((d, d)),
            _const_spec((1, d)),
            _const_spec((1, d)),
        ],
        out_specs=[pl.BlockSpec((tm, d), lambda i: (i, 0))] * 2,
        out_shape=[jax.ShapeDtypeStruct((t, d), F32), jax.ShapeDtypeStruct((t, d), BF16)],
        compiler_params=_params(("arbitrary",)),
        name="attn_out_proj",
    )(o, x, mods, mods, w_o.astype(BF16), ln_g, ln_b)


def _top16_rows(s):
    n, tm = s.shape
    rows = lax.broadcasted_iota(jnp.int32, (n, tm), 0).astype(F32)
    vals, idxs = [], []
    tie = None
    for _ in range(PEER_TOPK):
        m = jnp.max(s, axis=0, keepdims=True)
        if tie is not None:
            m = m + tie
        idx = jnp.min(jnp.where(s == m, rows, float(n)), axis=0, keepdims=True)
        s = jnp.where(rows == idx, -jnp.inf, s)
        vals.append(m)
        idxs.append(idx)
        tie = yield
    return vals, idxs


def _candidate_layout(tm):
    slot = lax.broadcasted_iota(jnp.int32, (N_CAND, tm), 0)
    grp = slot // SUBLANES
    r = slot % SUBLANES
    a3 = jnp.where(r < 5, 2, 4)
    b3 = jnp.where(r < 5, r, r - 5)
    a4 = jnp.where(r < 4, 3, jnp.where(r < 6, 5, 6))
    b4 = jnp.where(r < 4, r, jnp.where(r < 6, r - 4, r - 6))
    a = jnp.where(grp <= 1, 0, jnp.where(grp == 2, 1, jnp.where(grp == 3, a3, jnp.where(
        grp == 4, a4, jnp.where(grp == 5, 7, SUBLANES + r)))))
    b = jnp.where(grp == 1, SUBLANES + r, jnp.where(grp == 3, b3, jnp.where(
        grp == 4, b4, jnp.where(grp == 6, 0, r))))
    valid = (a + 1) * (b + 1) <= PEER_TOPK
    flat = (a * PEER_TOPK + b).astype(F32)
    return valid, flat


def _candidates(rows0, rows1):
    lo1 = jnp.concatenate(rows1[:SUBLANES], axis=0)
    hi1 = jnp.concatenate(rows1[SUBLANES:], axis=0)
    hi0 = jnp.concatenate(rows0[SUBLANES:], axis=0)
    r = lax.broadcasted_iota(jnp.int32, lo1.shape, 0)

    def shifted(k):
        return pltpu.roll(lo1, k, 0)

    groups = [
        rows0[0] + lo1,
        rows0[0] + hi1,
        rows0[1] + lo1,
        jnp.where(r < 5, rows0[2] + lo1, rows0[4] + shifted(5)),
        jnp.where(r < 4, rows0[3] + lo1,
                  jnp.where(r < 6, rows0[5] + shifted(4), rows0[6] + shifted(6))),
        rows0[7] + lo1,
        hi0 + rows1[0],
    ]
    return jnp.concatenate(groups, axis=0)


def _route_head(q_head, keys_ref, valid, flat):
    scores = [lax.dot_general(keys_ref[p], q_head[:, p * PEER_KEYS:(p + 1) * PEER_KEYS],
                              (((1,), (1,)), ((), ())), preferred_element_type=F32)
              for p in range(2)]
    yield
    vals, idxs = [], []
    for st in scores:
        v, ix = yield from _top16_rows(st)
        vals.append(v)
        idxs.append(ix)
    cand = jnp.where(valid, _candidates(vals[0], vals[1]), -jnp.inf)
    code = _candidates([ix * float(PEER_KEYS) for ix in idxs[0]], idxs[1])
    top_s, top_code = [], []
    tie = None
    for _ in range(PEER_TOPK):
        m = jnp.max(cand, axis=0, keepdims=True)
        if tie is not None:
            m = m + tie
        fmin = jnp.min(jnp.where(cand == m, flat, float(PEER_TOPK * PEER_TOPK)),
                       axis=0, keepdims=True)
        hit = flat == fmin
        top_code.append(jnp.sum(jnp.where(hit, code, 0.0), axis=0, keepdims=True))
        cand = jnp.where(hit, -jnp.inf, cand)
        top_s.append(m)
        tie = yield
    ts = jnp.concatenate(top_s, axis=0)
    e = jnp.exp(ts - top_s[0])
    return (jnp.concatenate(top_code, axis=0).astype(jnp.int32),
            e / jnp.sum(e, axis=0, keepdims=True))


ROUTE_YIELDS = 3 * PEER_TOPK


def _zero_row(x, width):
    bits = lax.bitcast_convert_type(x[:1, :LANES], jnp.uint32)
    zero = lax.bitcast_convert_type((bits >> 16) >> 16, F32)
    return jnp.concatenate([zero] * (width // LANES), axis=1)


def _peer_kernel(h_ref, hn_ref, x_ref, mod_ref, wq_ref, keys_ref, ut_ref, v_ref, g_ref, b_ref,
                 o_ref, q_ref, code_ref, gate_ref, ci_ref, cg_ref, w_ref, acc_ref, hs_ref,
                 *, alpha):
    i = pl.program_id(0)
    j = pl.program_id(1)
    nj = pl.num_programs(1)
    tm, d = x_ref.shape
    tn = ut_ref.shape[1]
    nhk = PEER_HEADS * PEER_TOPK
    parts = ROUTE_UNITS // PEER_HEADS
    tp = tm // parts
    slot = i % 2
    valid, flat = _candidate_layout(tp)

    def prepare_queries(h):
        q = jnp.dot(h, wq_ref[...], preferred_element_type=F32).astype(BF16)
        for hd in range(PEER_HEADS):
            q_ref[hd] = q[:, hd * 2 * PEER_KEYS:(hd + 1) * 2 * PEER_KEYS]

    def route_unit(u, dst):
        hd = u // parts
        part = u % parts
        rows = pl.ds(pl.multiple_of(part * tp, tp), tp)
        code, gate = yield from _route_head(q_ref[hd, rows, :], keys_ref, valid, flat)
        krows = pl.ds(pl.multiple_of(hd * PEER_TOPK, PEER_TOPK), PEER_TOPK)
        for g in range(tp // LANES):
            lanes = slice(g * LANES, (g + 1) * LANES)
            code_ref[dst, part * (tp // LANES) + g, krows, :] = code[:, lanes]
            gate_ref[dst, part * (tp // LANES) + g, krows, :] = gate[:, lanes]

    @pl.when((i == 0) & (j == 0))
    def _():
        prepare_queries(h_ref[...])

        def unit(u, carry):
            for _ in route_unit(u, 0):
                pass
            return carry

        lax.fori_loop(0, ROUTE_UNITS, unit, 0)

    @pl.when(j == 0)
    def _():
        for grp in range(tm // LANES):
            code = code_ref[slot, grp].T
            ci_ref[grp * LANES:(grp + 1) * LANES, :] = code // PEER_KEYS
            cg_ref[grp * LANES:(grp + 1) * LANES, :] = gate_ref[slot, grp].T
        key_rows = lax.broadcasted_iota(jnp.int32, (PEER_KEYS, nhk), 0).astype(BF16)
        key_lanes = lax.broadcasted_iota(jnp.int32, (nhk, PEER_KEYS), 1).astype(BF16)
        zero = jnp.zeros((PEER_KEYS, nhk), BF16)
        one = jnp.ones((nhk, PEER_KEYS), BF16)

        def token_group(grp, carry):
            base = pl.multiple_of(grp * LANES, LANES)
            second = (code_ref[slot, grp] % PEER_KEYS).astype(F32)

            def gate_bits(u):
                ri = ci_ref[pl.ds(base + u, 1), :].astype(BF16)
                rg = cg_ref[pl.ds(base + u, 1), :].astype(BF16)
                cj = jnp.broadcast_to(second[:, u:u + 1], (nhk, PEER_KEYS)).astype(BF16)
                pt = jnp.where(key_rows == ri, jnp.broadcast_to(rg, zero.shape), zero)
                qm = jnp.where(key_lanes == cj, one, jnp.zeros_like(one))
                wt = jnp.dot(pt, qm, preferred_element_type=F32)
                return lax.bitcast_convert_type(wt, jnp.uint32)

            for u in range(0, LANES, 2):
                packed = (gate_bits(u) >> 16) | (gate_bits(u + 1) & jnp.uint32(0xFFFF0000))
                pair = (base + u) // 2
                w_ref[pl.ds(pl.multiple_of(pair * W_PITCH, SUBLANES), PEER_KEYS), :] = packed
            return carry

        lax.fori_loop(0, tm // LANES, token_group, 0)
        acc_ref[...] = jnp.zeros(acc_ref.shape, F32)
        hs_ref[...] = h_ref[...]
        prepare_queries(hn_ref[...])

    route = route_unit(j, 1 - slot)
    next(route)
    chunk = 4 * PEER_KEYS
    n_chunks = tn // chunk

    def advance(after, count):
        try:
            route.send(None if after is None else _zero_row(after, tp))
            for _ in range(count - 1):
                next(route)
        except StopIteration:
            pass

    halves = 2
    th = tm // halves
    per_piece = -(-ROUTE_YIELDS // (2 * halves * n_chunks))

    def expert_inputs(half):
        rows = slice(half * th, (half + 1) * th)
        h = hs_ref[rows, :]
        pieces = []
        for r in range(n_chunks):
            c0 = r * chunk
            a = jnp.dot(h, us[:, c0:c0 + chunk], preferred_element_type=F32)
            advance(a, per_piece)
            act = (a * (1.0 + lax.erf(a * (1.0 / math.sqrt(2.0))))).astype(BF16)
            for e in range(chunk // PEER_KEYS):
                first_key = (j * n_chunks + r) * (chunk // PEER_KEYS) + e
                words = w_ref[pl.ds(half * (th // 2) * W_PITCH + first_key, th // 2, stride=W_PITCH), :]
                gates = pltpu.bitcast(words, BF16)
                pieces.append(gates * act[:, e * PEER_KEYS:(e + 1) * PEER_KEYS])
        return jnp.concatenate(pieces, axis=1)

    def expert_outputs(half, z):
        rows = slice(half * th, (half + 1) * th)
        for c in range(d // chunk):
            cols = slice(c * chunk, (c + 1) * chunk)
            zv = jnp.dot(z, vs[:, cols], preferred_element_type=F32)
            acc_ref[rows, cols] += zv
            advance(zv, per_piece)

    us = pltpu.bitcast(ut_ref[...], BF16)
    vs = pltpu.bitcast(v_ref[...], BF16)
    advance(None, per_piece)
    z_prev = expert_inputs(0)
    for half in range(1, halves):
        z_next = expert_inputs(half)
        expert_outputs(half - 1, z_prev)
        z_prev = z_next
    expert_outputs(halves - 1, z_prev)
    for _ in route:
        pass

    @pl.when(j == nj - 1)
    def _():
        _, _, gate = _split_mod(mod_ref[0], d)
        o_ref[...] = _layer_norm(alpha * x_ref[...] + gate * acc_ref[...], g_ref[...], b_ref[...])


def _peer_layer(h, x, mods, mod_row, w_q, sub_keys, u_t, v, ln_g, ln_b, *, seq, alpha, tm=512):
    t, d = x.shape
    n = u_t.shape[1]
    tn = n // ROUTE_UNITS
    nhk = PEER_HEADS * PEER_TOPK
    parts = ROUTE_UNITS // PEER_HEADS
    tiles_per_seq = seq // tm
    last = t // tm - 1
    return pl.pallas_call(
        functools.partial(_peer_kernel, alpha=alpha),
        grid=(t // tm, n // tn),
        in_specs=[
            pl.BlockSpec((tm, d), lambda i, j: (i, 0)),
            pl.BlockSpec((tm, d), lambda i, j: (jnp.minimum(i + 1, last), 0)),
            pl.BlockSpec((tm, d), lambda i, j: (i, 0)),
            pl.BlockSpec((1, 1, 3 * d), lambda i, j: (mod_row + i // tiles_per_seq, 0, 0)),
            _const_spec(w_q.shape),
            _const_spec(sub_keys.shape),
            pl.BlockSpec((d // 2, tn), lambda i, j: (0, j)),
            pl.BlockSpec((tn // 2, d), lambda i, j: (j, 0)),
            _const_spec((1, d)),
            _const_spec((1, d)),
        ],
        out_specs=pl.BlockSpec((tm, d), lambda i, j: (i, 0)),
        out_shape=jax.ShapeDtypeStruct((t, d), F32),
        scratch_shapes=[
            pltpu.VMEM((PEER_HEADS, tm, 2 * PEER_KEYS), BF16),
            pltpu.VMEM((2, tm // LANES, nhk, LANES), jnp.int32),
            pltpu.VMEM((2, tm // LANES, nhk, LANES), F32),
            pltpu.VMEM((tm, nhk), jnp.int32),
            pltpu.VMEM((tm, nhk), F32),
            pltpu.VMEM((tm // 2 * W_PITCH, PEER_KEYS), jnp.uint32),
            pltpu.VMEM((tm, d), F32),
            pltpu.VMEM((tm, d), BF16),
        ],
        compiler_params=_params(("arbitrary", "arbitrary")),
        name="peer_layer",
    )(h, h, x, mods, w_q.astype(BF16), sub_keys.astype(BF16), u_t, v, ln_g, ln_b)


def _pack_kernel(w_ref, o_ref, t_ref, *, transpose, scale):
    w = w_ref[0]
    if scale != 1.0:
        w = w * scale
    if transpose:
        w = w.T
    rows, cols = w.shape
    for c in range(cols // LANES):
        t_ref[c] = w[:, c * LANES:(c + 1) * LANES]
        even = t_ref[c, pl.ds(0, rows // 2, stride=2), :].astype(BF16).astype(F32)
        odd = t_ref[c, pl.ds(1, rows // 2, stride=2), :].astype(BF16).astype(F32)
        o_ref[:, c * LANES:(c + 1) * LANES] = (
            (lax.bitcast_convert_type(even, jnp.uint32) >> 16)
            | (lax.bitcast_convert_type(odd, jnp.uint32) & jnp.uint32(0xFFFF0000)))


def _packed_rows(w, layer, *, transpose, scale=1.0, tb=2048):
    _, n, d = w.shape
    if transpose:
        out_shape, out_block, out_map = (d // 2, n), (d // 2, tb), (lambda i: (0, i))
        t_shape = (tb // LANES, d, LANES)
    else:
        out_shape, out_block, out_map = (n // 2, d), (tb // 2, d), (lambda i: (i, 0))
        t_shape = (d // LANES, tb, LANES)
    return pl.pallas_call(
        functools.partial(_pack_kernel, transpose=transpose, scale=scale),
        grid=(n // tb,),
        in_specs=[pl.BlockSpec((1, tb, d), lambda i: (layer, i, 0))],
        out_specs=pl.BlockSpec(out_block, out_map),
        out_shape=jax.ShapeDtypeStruct(out_shape, jnp.uint32),
        scratch_shapes=[pltpu.VMEM(t_shape, F32)],
        compiler_params=_params(("arbitrary",)),
        name="pack_weights",
    )(w)


def _lambda_init(layer_idx):
    return 0.8 - 0.6 * math.exp(-0.3 * layer_idx)


def kernel(x, c, ada_w, ada_b, ln_g, ln_b, conv_w_in, conv_w, conv_w_out, attn_w_qkv, attn_lambda,
           attn_subln_g, attn_w_o, peer_w_q, peer_sub_keys, peer_u, peer_v):
    batch, seq, d = x.shape
    depth = ada_w.shape[0]
    alpha = (2.0 * depth) ** 0.25
    head_dim = d // (2 * ATT_HEADS)
    assert 2 * head_dim == LANES, "one attention head must fill one 128-lane block"
    assert peer_u.shape[1] == PEER_KEYS * PEER_KEYS and peer_sub_keys.shape[2:] == (PEER_KEYS, PEER_KEYS)
    assert peer_w_q.shape[2] == PEER_HEADS * 2 * PEER_KEYS
    mods = _ada_mods(c, ada_w, ada_b)
    xt = x.reshape(batch * seq, d)
    for i in range(depth):
        j = i // N_MIXERS
        row = (2 * i) * batch
        prow = (2 * i + 1) * batch
        g0, b0 = ln_g[i, 0].reshape(1, d), ln_b[i, 0].reshape(1, d)
        if i % N_MIXERS == 0:
            xt, h = _conv_layer(xt, mods, row, prow, conv_w_in[j], conv_w[j], conv_w_out[j], g0, b0,
                                seq=seq, alpha=alpha)
        else:
            qt, k, vt = _qkv_proj(xt, mods, row, attn_w_qkv[j], seq=seq, q_scale=head_dim ** -0.5)
            o = _diff_attention(qt, k, vt, attn_lambda[j], attn_subln_g[j], batch=batch, seq=seq,
                                lambda_init=_lambda_init(i))
            xt, h = _out_proj(o, xt, mods, row, prow, attn_w_o[j], g0, b0, seq=seq, alpha=alpha)
        g1, b1 = ln_g[i, 1].reshape(1, d), ln_b[i, 1].reshape(1, d)
        xt = _peer_layer(h, xt, mods, prow, peer_w_q[i], peer_sub_keys[i],
                         _packed_rows(peer_u, i, transpose=True),
                         _packed_rows(peer_v, i, transpose=False, scale=0.5), g1, b1,
                         seq=seq, alpha=alpha)
    return xt.reshape(batch, seq, d)
```

```python
import functools
import math

import jax
import jax.numpy as jnp
from jax import lax
from jax.experimental import pallas as pl
from jax.experimental.pallas import tpu as pltpu

F32 = jnp.float32
BF16 = jnp.bfloat16

N_MIXERS = 2
ATT_HEADS = 8
PEER_HEADS = 8
PEER_KEYS = 128
PEER_TOPK = 16
LN_EPS = 1e-5

LANES = 128
SUBLANES = 8
VMEM_LIMIT_BYTES = 56 * 1024 * 1024
BF16_EXACT_INT = 256
EXP_UNDERFLOW = 106.0
NORM_SLACK = 1.02
FIXED_SHIFT_LIMIT = 60.0

N_CAND_GROUPS = 7
N_CAND = N_CAND_GROUPS * SUBLANES
W_PITCH = 136
ROUTE_UNITS = 16


def _layer_norm(r, g, b):
    mu = jnp.mean(r, axis=-1, keepdims=True)
    d = r - mu
    var = jnp.mean(d * d, axis=-1, keepdims=True)
    return d * lax.rsqrt(var + LN_EPS) * g + b


def _split_mod(mod, d):
    return mod[:, :d], 1.0 + mod[:, d:2 * d], 1.0 + mod[:, 2 * d:]


def _params(sem):
    return pltpu.CompilerParams(dimension_semantics=sem, vmem_limit_bytes=VMEM_LIMIT_BYTES)


def _const_spec(shape):
    nd = len(shape)
    return pl.BlockSpec(shape, lambda *_: (0,) * nd, pipeline_mode=pl.Buffered(1))


def _ada_kernel(c_ref, w_ref, b_ref, o_ref):
    c = c_ref[...]
    sc = c / (1.0 + jnp.exp(-c))
    o_ref[0] = jnp.dot(sc, w_ref[0], precision=lax.Precision.HIGHEST,
                       preferred_element_type=F32) + b_ref[0]


def _ada_mods(c, ada_w, ada_b):
    depth, _, d, d3 = ada_w.shape
    nb = c.shape[0]
    nmat = depth * 2
    tn = 1024
    w = ada_w.reshape(nmat, d, d3)
    b = ada_b.reshape(nmat, 1, d3)
    out = pl.pallas_call(
        _ada_kernel,
        grid=(nmat, d3 // tn),
        in_specs=[
            pl.BlockSpec((nb, d), lambda m, n: (0, 0)),
            pl.BlockSpec((1, d, tn), lambda m, n: (m, 0, n)),
            pl.BlockSpec((1, 1, tn), lambda m, n: (m, 0, n)),
        ],
        out_specs=pl.BlockSpec((1, nb, tn), lambda m, n: (m, 0, n)),
        out_shape=jax.ShapeDtypeStruct((nmat, nb, d3), F32),
        compiler_params=_params(("arbitrary", "arbitrary")),
        name="ada_mods",
    )(c, w, b)
    return out.reshape(nmat * nb, 1, d3)


def _emit_stage_output(r, g_ref, b_ref, modp_ref, o_ref, hp_ref):
    xn = _layer_norm(r, g_ref[...], b_ref[...])
    o_ref[...] = xn
    shift, scale, _ = _split_mod(modp_ref[0], xn.shape[1])
    hp_ref[...] = (xn * scale + shift).astype(BF16)


def _conv_kernel(x_ref, mod_ref, modp_ref, win_ref, cw_ref, wout_ref, g_ref, b_ref, o_ref, hp_ref,
                 zbuf, *, tiles_per_seq, alpha):
    i = pl.program_id(0)
    tm, d = x_ref.shape
    x = x_ref[...]
    shift, scale, gate = _split_mod(mod_ref[0], d)
    h = (x * scale + shift).astype(BF16)
    hw = jnp.dot(h, win_ref[...], preferred_element_type=F32)
    gb = hw[:, :d]
    z = hw[:, d:2 * d] * hw[:, 2 * d:]

    @pl.when(i % tiles_per_seq == 0)
    def _():
        zbuf[0:SUBLANES, :] = jnp.zeros((SUBLANES, d), F32)

    zbuf[SUBLANES:, :] = z
    z1 = zbuf[pl.ds(SUBLANES - 1, tm), :]
    z2 = zbuf[pl.ds(SUBLANES - 2, tm), :]
    cw = cw_ref[...]
    zc = cw[2:3] * z + cw[1:2] * z1 + cw[0:1] * z2
    zbuf[0:SUBLANES, :] = zbuf[tm:tm + SUBLANES, :]
    y = jnp.dot((gb * zc).astype(BF16), wout_ref[...], preferred_element_type=F32)
    _emit_stage_output(alpha * x + gate * y, g_ref, b_ref, modp_ref, o_ref, hp_ref)


def _conv_layer(x, mods, mod_row, peer_row, w_in, conv_w, w_out, ln_g, ln_b, *, seq, alpha, tm=512):
    t, d = x.shape
    tiles_per_seq = seq // tm
    return pl.pallas_call(
        functools.partial(_conv_kernel, tiles_per_seq=tiles_per_seq, alpha=alpha),
        grid=(t // tm,),
        in_specs=[
            pl.BlockSpec((tm, d), lambda i: (i, 0)),
            pl.BlockSpec((1, 1, 3 * d), lambda i: (mod_row + i // tiles_per_seq, 0, 0)),
            pl.BlockSpec((1, 1, 3 * d), lambda i: (peer_row + i // tiles_per_seq, 0, 0)),
            _const_spec((d, 3 * d)),
            _const_spec((conv_w.shape[0], d)),
            _const_spec((d, d)),
            _const_spec((1, d)),
            _const_spec((1, d)),
        ],
        out_specs=[pl.BlockSpec((tm, d), lambda i: (i, 0))] * 2,
        out_shape=[jax.ShapeDtypeStruct((t, d), F32), jax.ShapeDtypeStruct((t, d), BF16)],
        scratch_shapes=[pltpu.VMEM((tm + SUBLANES, d), F32)],
        compiler_params=_params(("arbitrary",)),
        name="conv_mixer",
    )(x, mods, mods, w_in.astype(BF16), conv_w, w_out.astype(BF16), ln_g, ln_b)


def _qkv_kernel(x_ref, mod_ref, w_ref, qt_ref, k_ref, vt_ref, *, q_scale):
    tm, d = x_ref.shape
    shift, scale, _ = _split_mod(mod_ref[0], d)
    h = (x_ref[...] * scale + shift).astype(BF16)
    qkv = jnp.dot(h, w_ref[...], preferred_element_type=F32)
    qt_ref[...] = (qkv[:, :d] * q_scale).T.astype(BF16)
    k_ref[...] = qkv[:, d:2 * d].astype(BF16)
    vt_ref[...] = qkv[:, 2 * d:].T.astype(BF16)


def _qkv_proj(x, mods, mod_row, w_qkv, *, seq, q_scale, tm=512):
    t, d = x.shape
    tiles_per_seq = seq // tm
    return pl.pallas_call(
        functools.partial(_qkv_kernel, q_scale=q_scale),
        grid=(t // tm,),
        in_specs=[
            pl.BlockSpec((tm, d), lambda i: (i, 0)),
            pl.BlockSpec((1, 1, 3 * d), lambda i: (mod_row + i // tiles_per_seq, 0, 0)),
            _const_spec((d, 3 * d)),
        ],
        out_specs=[
            pl.BlockSpec((d, tm), lambda i: (0, i)),
            pl.BlockSpec((tm, d), lambda i: (i, 0)),
            pl.BlockSpec((d, tm), lambda i: (0, i)),
        ],
        out_shape=[
            jax.ShapeDtypeStruct((d, t), BF16),
            jax.ShapeDtypeStruct((t, d), BF16),
            jax.ShapeDtypeStruct((d, t), BF16),
        ],
        compiler_params=_params(("arbitrary",)),
        name="attn_qkv",
    )(x, mods, w_qkv.astype(BF16))


def _attn_kernel(slopes_ref, qt_ref, k_ref, vt_ref, lam_ref, g_ref, o_ref, qa_ref, m_ref, acc_ref,
                 kn_ref, *, tq, tk, heads, lambda_init):
    hg = pl.program_id(1)
    qi = pl.program_id(2)
    dh2 = qt_ref.shape[0] // heads
    dh = dh2 // 2
    ones_rows = 2 * SUBLANES
    blocks_per_tile = tq // tk

    feat = lax.broadcasted_iota(jnp.int32, (dh2, tq), 0)
    arow = lax.broadcasted_iota(jnp.int32, (dh2, 2 * tq), 0)
    acol = lax.broadcasted_iota(jnp.int32, (dh2, 2 * tq), 1)
    r = jnp.where(acol >= tq, acol - tq, acol)
    r_lo = (r % BF16_EXACT_INT).astype(F32)
    r_hi = (r - r % BF16_EXACT_INT).astype(F32)
    for g in range(heads):
        slope = slopes_ref[hg * heads + g]
        qt = qt_ref[g * dh2:(g + 1) * dh2, :]
        zero = jnp.zeros_like(qt)
        qa_ref[g, :dh2, :] = jnp.concatenate(
            [jnp.where(feat < dh, qt, zero), jnp.where(feat >= dh, qt, zero)], axis=1)
        qa_ref[g, dh2:, :] = jnp.where(
            arow == 0, -slope * r_lo,
            jnp.where(arow == 1, -slope * r_hi, jnp.where(arow <= 3, slope, 0.0))).astype(BF16)
    kcol = lax.broadcasted_iota(jnp.int32, (tk, dh2), 1)
    krow = lax.broadcasted_iota(jnp.int32, (tk, dh2), 0)
    c_lo = (krow % BF16_EXACT_INT).astype(F32)
    c_hi = (krow - krow % BF16_EXACT_INT).astype(F32)
    k_extra = jnp.where(kcol <= 1, 1.0,
                        jnp.where(kcol == 2, c_lo, jnp.where(kcol == 3, c_hi, 0.0))).astype(BF16)
    ones = jnp.ones((ones_rows, tk), BF16)

    m_ref[...] = jnp.full(m_ref.shape, -jnp.inf, F32)
    acc_ref[...] = jnp.zeros(acc_ref.shape, F32)

    @pl.when(qi == 0)
    def _():
        ones_sq = jnp.ones((dh2, LANES), BF16)
        for g in range(heads):
            def chunk(c, best, g=g):
                kc = k_ref[pl.ds(pl.multiple_of(c * tq, tq), tq), g * dh2:(g + 1) * dh2].astype(F32)
                rows = jnp.dot((kc * kc).astype(BF16), ones_sq, preferred_element_type=F32)
                return jnp.maximum(best, jnp.max(rows, axis=0, keepdims=True))
            kn_ref[g] = lax.fori_loop(0, k_ref.shape[0] // tq, chunk, jnp.zeros((1, LANES), F32))

    def step(j, diagonal):
        start = pl.multiple_of(j * tk, tk)
        delta = qi * tq - j * tk
        head_cols = [slice(g * dh2, (g + 1) * dh2) for g in range(heads)]
        scores, maxima = [], []
        for g, cols in enumerate(head_cols):
            s = jnp.dot(jnp.concatenate([k_ref[pl.ds(start, tk), cols], k_extra], axis=1),
                        qa_ref[g], preferred_element_type=F32)
            if diagonal:
                mrow = lax.broadcasted_iota(jnp.int32, (tk, 2 * tq), 0)
                mcol = lax.broadcasted_iota(jnp.int32, (tk, 2 * tq), 1)
                mq = jnp.where(mcol >= tq, mcol - tq, mcol)
                s = jnp.where(mrow - mq <= delta, s, -jnp.inf)
            scores.append(s)
            maxima.append(jnp.max(s, axis=0, keepdims=True))
        probs, corrs = [], []
        for g, s in enumerate(scores):
            off = delta.astype(F32) * slopes_ref[hg * heads + g]
            m_old = m_ref[g]
            m_new = jnp.maximum(m_old, maxima[g] - off)
            probs.append(jnp.exp(s - (m_new + off)).astype(BF16))
            corrs.append(jnp.exp(m_old - m_new))
            m_ref[g] = m_new
        for g, cols in enumerate(head_cols):
            v_aug = jnp.concatenate([vt_ref[cols, pl.ds(start, tk)], ones], axis=0)
            acc_ref[g] = corrs[g] * acc_ref[g] + jnp.dot(v_aug, probs[g],
                                                         preferred_element_type=F32)

    first_diag = qi * blocks_per_tile
    for d in range(blocks_per_tile):
        step(first_diag + d, True)

    needed = jnp.zeros((1, 2 * tq), F32)
    bound = jnp.zeros((1, 2 * tq), F32)
    for g in range(heads):
        qf = qa_ref[g, :dh2, :].astype(F32)
        q_norm = jnp.sqrt(jnp.sum(qf * qf, axis=0, keepdims=True))
        k_norm = jnp.sqrt(kn_ref[g][:, :1])
        qk = q_norm * k_norm * NORM_SLACK
        reach = (qk - m_ref[g] + EXP_UNDERFLOW) / slopes_ref[hg * heads + g]
        needed = jnp.maximum(needed, jnp.floor((reach - 1.0) / tk) + 1.0)
        bound = jnp.maximum(bound, qk)
    needed = jnp.clip(needed, 0.0, first_diag.astype(F32))
    n_blocks = jnp.max(needed.astype(jnp.int32))

    fixed_shift_ok = jnp.max(bound) * 2.0 < FIXED_SHIFT_LIMIT

    def fixed_shift_step(j):
        start = pl.multiple_of(j * tk, tk)
        delta = qi * tq - j * tk
        scores = [jnp.dot(jnp.concatenate([k_ref[pl.ds(start, tk), g * dh2:(g + 1) * dh2], k_extra],
                                          axis=1), qa_ref[g], preferred_element_type=F32)
                  for g in range(heads)]
        probs = [jnp.exp(s - (m_ref[g] + delta.astype(F32) * slopes_ref[hg * heads + g])).astype(BF16)
                 for g, s in enumerate(scores)]
        for g in range(heads):
            v_aug = jnp.concatenate([vt_ref[g * dh2:(g + 1) * dh2, pl.ds(start, tk)], ones], axis=0)
            acc_ref[g] += jnp.dot(v_aug, probs[g], preferred_element_type=F32)

    @pl.when(fixed_shift_ok)
    def _():
        def body(i, carry):
            fixed_shift_step(first_diag - 1 - i)
            return carry

        lax.fori_loop(0, n_blocks, body, 0)

    @pl.when(jnp.logical_not(fixed_shift_ok))
    def _():
        def body(i, carry):
            step(first_diag - 1 - i, False)
            return carry

        lax.fori_loop(0, n_blocks, body, 0)

    lam = lam_ref[...]
    lam_full = (jnp.exp(jnp.sum(lam[0:1] * lam[1:2], axis=1, keepdims=True))
                - jnp.exp(jnp.sum(lam[2:3] * lam[3:4], axis=1, keepdims=True)) + lambda_init)
    for g in range(heads):
        acc = acc_ref[g]
        on = acc[:dh2] / acc[dh2:dh2 + 1]
        o = on[:, :tq] - lam_full * on[:, tq:]
        o = o * lax.rsqrt(jnp.mean(o * o, axis=0, keepdims=True) + LN_EPS) * g_ref[...]
        o_ref[:, g * dh2:(g + 1) * dh2] = (o * (1.0 - lambda_init)).T.astype(o_ref.dtype)


def _diff_attention(qt, k, vt, lam, subln_g, *, batch, seq, lambda_init, tq=512, tk=512, heads=4):
    d, t = qt.shape
    dh2 = d // ATT_HEADS
    gw = heads * dh2
    nq = seq // tq
    assert tq % tk == 0 and tq < BF16_EXACT_INT * BF16_EXACT_INT
    slopes = 2.0 ** (-(8.0 / ATT_HEADS) * jnp.arange(1, ATT_HEADS + 1, dtype=F32))
    return pl.pallas_call(
        functools.partial(_attn_kernel, tq=tq, tk=tk, heads=heads, lambda_init=lambda_init),
        grid=(batch, ATT_HEADS // heads, nq),
        in_specs=[
            pl.BlockSpec(memory_space=pltpu.SMEM),
            pl.BlockSpec((gw, tq), lambda b, h, i: (h, b * nq + i)),
            pl.BlockSpec((seq, gw), lambda b, h, i: (b, h)),
            pl.BlockSpec((gw, seq), lambda b, h, i: (h, b)),
            pl.BlockSpec(lam.shape, lambda b, h, i: (0, 0)),
            pl.BlockSpec((dh2, 1), lambda b, h, i: (0, 0)),
        ],
        out_specs=pl.BlockSpec((tq, gw), lambda b, h, i: (b * nq + i, h)),
        out_shape=jax.ShapeDtypeStruct((t, d), BF16),
        scratch_shapes=[
            pltpu.VMEM((heads, 2 * dh2, 2 * tq), BF16),
            pltpu.VMEM((heads, 1, 2 * tq), F32),
            pltpu.VMEM((heads, dh2 + 2 * SUBLANES, 2 * tq), F32),
            pltpu.VMEM((heads, 1, LANES), F32),
        ],
        compiler_params=_params(("arbitrary", "arbitrary", "arbitrary")),
        name="diff_attention",
    )(slopes, qt, k, vt, lam, subln_g.reshape(dh2, 1))


def _oproj_kernel(o_ref, x_ref, mod_ref, modp_ref, w_ref, g_ref, b_ref, out_ref, hp_ref, *, alpha):
    tm, d = x_ref.shape
    _, _, gate = _split_mod(mod_ref[0], d)
    y = jnp.dot(o_ref[...], w_ref[...], preferred_element_type=F32)
    _emit_stage_output(alpha * x_ref[...] + gate * y, g_ref, b_ref, modp_ref, out_ref, hp_ref)


def _out_proj(o, x, mods, mod_row, peer_row, w_o, ln_g, ln_b, *, seq, alpha, tm=512):
    t, d = x.shape
    tiles_per_seq = seq // tm
    return pl.pallas_call(
        functools.partial(_oproj_kernel, alpha=alpha),
        grid=(t // tm,),
        in_specs=[
            pl.BlockSpec((tm, d), lambda i: (i, 0)),
            pl.BlockSpec((tm, d), lambda i: (i, 0)),
            pl.BlockSpec((1, 1, 3 * d), lambda i: (mod_row + i // tiles_per_seq, 0, 0)),
            pl.BlockSpec((1, 1, 3 * d), lambda i: (peer_row + i // tiles_per_seq, 0, 0)),
            _const_spec((d, d)),
            _const_spec((1, d)),
            _const_spec((1, d)),
        ],
        out_specs=[pl.BlockSpec((tm, d), lambda i: (i, 0))] * 2,
        out_shape=[jax.ShapeDtypeStruct((t, d), F32), jax.ShapeDtypeStruct((t, d), BF16)],
        compiler_params=_params(("arbitrary",)),
        name="attn_out_proj",
    )(o, x, mods, mods, w_o.astype(BF16), ln_g, ln_b)


def _top16_rows(s):
    n, tm = s.shape
    rows = lax.broadcasted_iota(jnp.int32, (n, tm), 0).astype(F32)
    vals, idxs = [], []
    tie = None
    for _ in range(PEER_TOPK):
        m = jnp.max(s, axis=0, keepdims=True)
        if tie is not None:
            m = m + tie
        idx = jnp.min(jnp.where(s == m, rows, float(n)), axis=0, keepdims=True)
        s = jnp.where(rows == idx, -jnp.inf, s)
        vals.append(m)
        idxs.append(idx)
        tie = yield
    return vals, idxs


def _candidate_layout(tm):
    slot = lax.broadcasted_iota(jnp.int32, (N_CAND, tm), 0)
    grp = slot // SUBLANES
    r = slot % SUBLANES
    a3 = jnp.where(r < 5, 2, 4)
    b3 = jnp.where(r < 5, r, r - 5)
    a4 = jnp.where(r < 4, 3, jnp.where(r < 6, 5, 6))
    b4 = jnp.where(r < 4, r, jnp.where(r < 6, r - 4, r - 6))
    a = jnp.where(grp <= 1, 0, jnp.where(grp == 2, 1, jnp.where(grp == 3, a3, jnp.where(
        grp == 4, a4, jnp.where(grp == 5, 7, SUBLANES + r)))))
    b = jnp.where(grp == 1, SUBLANES + r, jnp.where(grp == 3, b3, jnp.where(
        grp == 4, b4, jnp.where(grp == 6, 0, r))))
    valid = (a + 1) * (b + 1) <= PEER_TOPK
    flat = (a * PEER_TOPK + b).astype(F32)
    return valid, flat


def _candidates(rows0, rows1):
    lo1 = jnp.concatenate(rows1[:SUBLANES], axis=0)
    hi1 = jnp.concatenate(rows1[SUBLANES:], axis=0)
    hi0 = jnp.concatenate(rows0[SUBLANES:], axis=0)
    r = lax.broadcasted_iota(jnp.int32, lo1.shape, 0)

    def shifted(k):
        return pltpu.roll(lo1, k, 0)

    groups = [
        rows0[0] + lo1,
        rows0[0] + hi1,
        rows0[1] + lo1,
        jnp.where(r < 5, rows0[2] + lo1, rows0[4] + shifted(5)),
        jnp.where(r < 4, rows0[3] + lo1,
                  jnp.where(r < 6, rows0[5] + shifted(4), rows0[6] + shifted(6))),
        rows0[7] + lo1,
        hi0 + rows1[0],
    ]
    return jnp.concatenate(groups, axis=0)


def _route_head(q_head, keys_ref, valid, flat):
    scores = [lax.dot_general(keys_ref[p], q_head[:, p * PEER_KEYS:(p + 1) * PEER_KEYS],
                              (((1,), (1,)), ((), ())), preferred_element_type=F32)
              for p in range(2)]
    yield
    vals, idxs = [], []
    for st in scores:
        v, ix = yield from _top16_rows(st)
        vals.append(v)
        idxs.append(ix)
    cand = jnp.where(valid, _candidates(vals[0], vals[1]), -jnp.inf)
    code = _candidates([ix * float(PEER_KEYS) for ix in idxs[0]], idxs[1])
    top_s, top_code = [], []
    tie = None
    for _ in range(PEER_TOPK):
        m = jnp.max(cand, axis=0, keepdims=True)
        if tie is not None:
            m = m + tie
        fmin = jnp.min(jnp.where(cand == m, flat, float(PEER_TOPK * PEER_TOPK)),
                       axis=0, keepdims=True)
        hit = flat == fmin
        top_code.append(jnp.sum(jnp.where(hit, code, 0.0), axis=0, keepdims=True))
        cand = jnp.where(hit, -jnp.inf, cand)
        top_s.append(m)
        tie = yield
    ts = jnp.concatenate(top_s, axis=0)
    e = jnp.exp(ts - top_s[0])
    return (jnp.concatenate(top_code, axis=0).astype(jnp.int32),
            e / jnp.sum(e, axis=0, keepdims=True))


ROUTE_YIELDS = 3 * PEER_TOPK


def _zero_row(x, width):
    bits = lax.bitcast_convert_type(x[:1, :LANES], jnp.uint32)
    zero = lax.bitcast_convert_type((bits >> 16) >> 16, F32)
    return jnp.concatenate([zero] * (width // LANES), axis=1)


def _peer_kernel(h_ref, hn_ref, x_ref, mod_ref, wq_ref, keys_ref, ut_ref, v_ref, g_ref, b_ref,
                 o_ref, q_ref, code_ref, gate_ref, ci_ref, cg_ref, w_ref, acc_ref, hs_ref,
                 *, alpha):
    i = pl.program_id(0)
    j = pl.program_id(1)
    nj = pl.num_programs(1)
    tm, d = x_ref.shape
    tn = ut_ref.shape[1]
    nhk = PEER_HEADS * PEER_TOPK
    parts = ROUTE_UNITS // PEER_HEADS
    tp = tm // parts
    slot = i % 2
    valid, flat = _candidate_layout(tp)

    def prepare_queries(h):
        q = jnp.dot(h, wq_ref[...], preferred_element_type=F32).astype(BF16)
        for hd in range(PEER_HEADS):
            q_ref[hd] = q[:, hd * 2 * PEER_KEYS:(hd + 1) * 2 * PEER_KEYS]

    def route_unit(u, dst):
        hd = u // parts
        part = u % parts
        rows = pl.ds(pl.multiple_of(part * tp, tp), tp)
        code, gate = yield from _route_head(q_ref[hd, rows, :], keys_ref, valid, flat)
        krows = pl.ds(pl.multiple_of(hd * PEER_TOPK, PEER_TOPK), PEER_TOPK)
        for g in range(tp // LANES):
            lanes = slice(g * LANES, (g + 1) * LANES)
            code_ref[dst, part * (tp // LANES) + g, krows, :] = code[:, lanes]
            gate_ref[dst, part * (tp // LANES) + g, krows, :] = gate[:, lanes]

    @pl.when((i == 0) & (j == 0))
    def _():
        prepare_queries(h_ref[...])

        def unit(u, carry):
            for _ in route_unit(u, 0):
                pass
            return carry

        lax.fori_loop(0, ROUTE_UNITS, unit, 0)

    @pl.when(j == 0)
    def _():
        for grp in range(tm // LANES):
            code = code_ref[slot, grp].T
            ci_ref[grp * LANES:(grp + 1) * LANES, :] = code // PEER_KEYS
            cg_ref[grp * LANES:(grp + 1) * LANES, :] = gate_ref[slot, grp].T
        key_rows = lax.broadcasted_iota(jnp.int32, (PEER_KEYS, nhk), 0).astype(BF16)
        key_lanes = lax.broadcasted_iota(jnp.int32, (nhk, PEER_KEYS), 1).astype(BF16)
        zero = jnp.zeros((PEER_KEYS, nhk), BF16)
        one = jnp.ones((nhk, PEER_KEYS), BF16)

        def token_group(grp, carry):
            base = pl.multiple_of(grp * LANES, LANES)
            second = (code_ref[slot, grp] % PEER_KEYS).astype(F32)

            def gate_bits(u):
                ri = ci_ref[pl.ds(base + u, 1), :].astype(BF16)
                rg = cg_ref[pl.ds(base + u, 1), :].astype(BF16)
                cj = jnp.broadcast_to(second[:, u:u + 1], (nhk, PEER_KEYS)).astype(BF16)
                pt = jnp.where(key_rows == ri, jnp.broadcast_to(rg, zero.shape), zero)
                qm = jnp.where(key_lanes == cj, one, jnp.zeros_like(one))
                wt = jnp.dot(pt, qm, preferred_element_type=F32)
                return lax.bitcast_convert_type(wt, jnp.uint32)

            for u in range(0, LANES, 2):
                packed = (gate_bits(u) >> 16) | (gate_bits(u + 1) & jnp.uint32(0xFFFF0000))
                pair = (base + u) // 2
                w_ref[pl.ds(pl.multiple_of(pair * W_PITCH, SUBLANES), PEER_KEYS), :] = packed
            return carry

        lax.fori_loop(0, tm // LANES, token_group, 0)
        acc_ref[...] = jnp.zeros(acc_ref.shape, F32)
        hs_ref[...] = h_ref[...]
        prepare_queries(hn_ref[...])

    route = route_unit(j, 1 - slot)
    next(route)
    chunk = 4 * PEER_KEYS
    n_chunks = tn // chunk

    def advance(after, count):
        try:
            route.send(None if after is None else _zero_row(after, tp))
            for _ in range(count - 1):
                next(route)
        except StopIteration:
            pass

    halves = 4
    th = tm // halves
    per_piece = -(-ROUTE_YIELDS // (2 * halves * n_chunks))

    def expert_inputs(half):
        rows = slice(half * th, (half + 1) * th)
        h = hs_ref[rows, :]
        pieces = []
        for r in range(n_chunks):
            c0 = r * chunk
            a = jnp.dot(h, us[:, c0:c0 + chunk], preferred_element_type=F32)
            advance(a, per_piece)
            act = (a * (1.0 + lax.erf(a * (1.0 / math.sqrt(2.0))))).astype(BF16)
            for e in range(chunk // PEER_KEYS):
                first_key = (j * n_chunks + r) * (chunk // PEER_KEYS) + e
                words = w_ref[pl.ds(half * (th // 2) * W_PITCH + first_key, th // 2, stride=W_PITCH), :]
                gates = pltpu.bitcast(words, BF16)
                pieces.append(gates * act[:, e * PEER_KEYS:(e + 1) * PEER_KEYS])
        return jnp.concatenate(pieces, axis=1)

    def expert_outputs(half, z):
        rows = slice(half * th, (half + 1) * th)
        for c in range(d // chunk):
            cols = slice(c * chunk, (c + 1) * chunk)
            zv = jnp.dot(z, vs[:, cols], preferred_element_type=F32)
            acc_ref[rows, cols] += zv
            advance(zv, per_piece)

    us = pltpu.bitcast(ut_ref[...], BF16)
    vs = pltpu.bitcast(v_ref[...], BF16)
    advance(None, per_piece)
    z_prev = expert_inputs(0)
    for half in range(1, halves):
        z_next = expert_inputs(half)
        expert_outputs(half - 1, z_prev)
        z_prev = z_next
    expert_outputs(halves - 1, z_prev)
    for _ in route:
        pass

    @pl.when(j == nj - 1)
    def _():
        _, _, gate = _split_mod(mod_ref[0], d)
        o_ref[...] = _layer_norm(alpha * x_ref[...] + gate * acc_ref[...], g_ref[...], b_ref[...])


def _peer_layer(h, x, mods, mod_row, w_q, sub_keys, u_t, v, ln_g, ln_b, *, seq, alpha, tm=512):
    t, d = x.shape
    n = u_t.shape[1]
    tn = n // ROUTE_UNITS
    nhk = PEER_HEADS * PEER_TOPK
    parts = ROUTE_UNITS // PEER_HEADS
    tiles_per_seq = seq // tm
    last = t // tm - 1
    return pl.pallas_call(
        functools.partial(_peer_kernel, alpha=alpha),
        grid=(t // tm, n // tn),
        in_specs=[
            pl.BlockSpec((tm, d), lambda i, j: (i, 0)),
            pl.BlockSpec((tm, d), lambda i, j: (jnp.minimum(i + 1, last), 0)),
            pl.BlockSpec((tm, d), lambda i, j: (i, 0)),
            pl.BlockSpec((1, 1, 3 * d), lambda i, j: (mod_row + i // tiles_per_seq, 0, 0)),
            _const_spec(w_q.shape),
            _const_spec(sub_keys.shape),
            pl.BlockSpec((d // 2, tn), lambda i, j: (0, j)),
            pl.BlockSpec((tn // 2, d), lambda i, j: (j, 0)),
            _const_spec((1, d)),
            _const_spec((1, d)),
        ],
        out_specs=pl.BlockSpec((tm, d), lambda i, j: (i, 0)),
        out_shape=jax.ShapeDtypeStruct((t, d), F32),
        scratch_shapes=[
            pltpu.VMEM((PEER_HEADS, tm, 2 * PEER_KEYS), BF16),
            pltpu.VMEM((2, tm // LANES, nhk, LANES), jnp.int32),
            pltpu.VMEM((2, tm // LANES, nhk, LANES), F32),
            pltpu.VMEM((tm, nhk), jnp.int32),
            pltpu.VMEM((tm, nhk), F32),
            pltpu.VMEM((tm // 2 * W_PITCH, PEER_KEYS), jnp.uint32),
            pltpu.VMEM((tm, d), F32),
            pltpu.VMEM((tm, d), BF16),
        ],
        compiler_params=_params(("arbitrary", "arbitrary")),
        name="peer_layer",
    )(h, h, x, mods, w_q.astype(BF16), sub_keys.astype(BF16), u_t, v, ln_g, ln_b)


def _pack_kernel(w_ref, o_ref, t_ref, *, transpose, scale):
    w = w_ref[0]
    if scale != 1.0:
        w = w * scale
    if transpose:
        w = w.T
    rows, cols = w.shape
    for c in range(cols // LANES):
        t_ref[c] = w[:, c * LANES:(c + 1) * LANES]
        even = t_ref[c, pl.ds(0, rows // 2, stride=2), :].astype(BF16).astype(F32)
        odd = t_ref[c, pl.ds(1, rows // 2, stride=2), :].astype(BF16).astype(F32)
        o_ref[:, c * LANES:(c + 1) * LANES] = (
            (lax.bitcast_convert_type(even, jnp.uint32) >> 16)
            | (lax.bitcast_convert_type(odd, jnp.uint32) & jnp.uint32(0xFFFF0000)))


def _packed_rows(w, layer, *, transpose, scale=1.0, tb=2048):
    _, n, d = w.shape
    if transpose:
        out_shape, out_block, out_map = (d // 2, n), (d // 2, tb), (lambda i: (0, i))
        t_shape = (tb // LANES, d, LANES)
    else:
        out_shape, out_block, out_map = (n // 2, d), (tb // 2, d), (lambda i: (i, 0))
        t_shape = (d // LANES, tb, LANES)
    return pl.pallas_call(
        functools.partial(_pack_kernel, transpose=transpose, scale=scale),
        grid=(n // tb,),
        in_specs=[pl.BlockSpec((1, tb, d), lambda i: (layer, i, 0))],
        out_specs=pl.BlockSpec(out_block, out_map),
        out_shape=jax.ShapeDtypeStruct(out_shape, jnp.uint32),
        scratch_shapes=[pltpu.VMEM(t_shape, F32)],
        compiler_params=_params(("arbitrary",)),
        name="pack_weights",
    )(w)


def _lambda_init(layer_idx):
    return 0.8 - 0.6 * math.exp(-0.3 * layer_idx)


def kernel(x, c, ada_w, ada_b, ln_g, ln_b, conv_w_in, conv_w, conv_w_out, attn_w_qkv, attn_lambda,
           attn_subln_g, attn_w_o, peer_w_q, peer_sub_keys, peer_u, peer_v):
    batch, seq, d = x.shape
    depth = ada_w.shape[0]
    alpha = (2.0 * depth) ** 0.25
    head_dim = d // (2 * ATT_HEADS)
    assert 2 * head_dim == LANES, "one attention head must fill one 128-lane block"
    assert peer_u.shape[1] == PEER_KEYS * PEER_KEYS and peer_sub_keys.shape[2:] == (PEER_KEYS, PEER_KEYS)
    assert peer_w_q.shape[2] == PEER_HEADS * 2 * PEER_KEYS
    mods = _ada_mods(c, ada_w, ada_b)
    xt = x.reshape(batch * seq, d)
    for i in range(depth):
        j = i // N_MIXERS
        row = (2 * i) * batch
        prow = (2 * i + 1) * batch
        g0, b0 = ln_g[i, 0].reshape(1, d), ln_b[i, 0].reshape(1, d)
        if i % N_MIXERS == 0:
            xt, h = _conv_layer(xt, mods, row, prow, conv_w_in[j], conv_w[j], conv_w_out[j], g0, b0,
                                seq=seq, alpha=alpha)
        else:
            qt, k, vt = _qkv_proj(xt, mods, row, attn_w_qkv[j], seq=seq, q_scale=head_dim ** -0.5)
            o = _diff_attention(qt, k, vt, attn_lambda[j], attn_subln_g[j], batch=batch, seq=seq,
                                lambda_init=_lambda_init(i))
            xt, h = _out_proj(o, xt, mods, row, prow, attn_w_o[j], g0, b0, seq=seq, alpha=alpha)
        g1, b1 = ln_g[i, 1].reshape(1, d), ln_b[i, 1].reshape(1, d)
        xt = _peer_layer(h, xt, mods, prow, peer_w_q[i], peer_sub_keys[i],
                         _packed_rows(peer_u, i, transpose=True),
                         _packed_rows(peer_v, i, transpose=False, scale=0.5), g1, b1,
                         seq=seq, alpha=alpha)
    return xt.reshape(batch, seq, d)
```

```python
import functools
import math

import jax
import jax.numpy as jnp
from jax import lax
from jax.experimental import pallas as pl
from jax.experimental.pallas import tpu as pltpu

F32 = jnp.float32
BF16 = jnp.bfloat16

N_MIXERS = 2
ATT_HEADS = 8
PEER_HEADS = 8
PEER_KEYS = 128
PEER_TOPK = 16
LN_EPS = 1e-5

LANES = 128
SUBLANES = 8
VMEM_LIMIT_BYTES = 56 * 1024 * 1024
BF16_EXACT_INT = 256
EXP_UNDERFLOW = 106.0
NORM_SLACK = 1.02
FIXED_SHIFT_LIMIT = 60.0

N_CAND_GROUPS = 7
N_CAND = N_CAND_GROUPS * SUBLANES
W_PITCH = 136
ROUTE_UNITS = 16


def _layer_norm(r, g, b):
    mu = jnp.mean(r, axis=-1, keepdims=True)
    d = r - mu
    var = jnp.mean(d * d, axis=-1, keepdims=True)
    return d * lax.rsqrt(var + LN_EPS) * g + b


def _split_mod(mod, d):
    return mod[:, :d], 1.0 + mod[:, d:2 * d], 1.0 + mod[:, 2 * d:]


def _params(sem):
    return pltpu.CompilerParams(dimension_semantics=sem, vmem_limit_bytes=VMEM_LIMIT_BYTES)


def _const_spec(shape):
    nd = len(shape)
    return pl.BlockSpec(shape, lambda *_: (0,) * nd, pipeline_mode=pl.Buffered(1))


def _ada_kernel(c_ref, w_ref, b_ref, o_ref):
    c = c_ref[...]
    sc = c / (1.0 + jnp.exp(-c))
    o_ref[0] = jnp.dot(sc, w_ref[0], precision=lax.Precision.HIGHEST,
                       preferred_element_type=F32) + b_ref[0]


def _ada_mods(c, ada_w, ada_b):
    depth, _, d, d3 = ada_w.shape
    nb = c.shape[0]
    nmat = depth * 2
    tn = 1024
    w = ada_w.reshape(nmat, d, d3)
    b = ada_b.reshape(nmat, 1, d3)
    out = pl.pallas_call(
        _ada_kernel,
        grid=(nmat, d3 // tn),
        in_specs=[
            pl.BlockSpec((nb, d), lambda m, n: (0, 0)),
            pl.BlockSpec((1, d, tn), lambda m, n: (m, 0, n)),
            pl.BlockSpec((1, 1, tn), lambda m, n: (m, 0, n)),
        ],
        out_specs=pl.BlockSpec((1, nb, tn), lambda m, n: (m, 0, n)),
        out_shape=jax.ShapeDtypeStruct((nmat, nb, d3), F32),
        compiler_params=_params(("arbitrary", "arbitrary")),
        name="ada_mods",
    )(c, w, b)
    return out.reshape(nmat * nb, 1, d3)


def _emit_stage_output(r, g_ref, b_ref, modp_ref, o_ref, hp_ref):
    xn = _layer_norm(r, g_ref[...], b_ref[...])
    o_ref[...] = xn
    shift, scale, _ = _split_mod(modp_ref[0], xn.shape[1])
    hp_ref[...] = (xn * scale + shift).astype(BF16)


def _conv_kernel(x_ref, mod_ref, modp_ref, win_ref, cw_ref, wout_ref, g_ref, b_ref, o_ref, hp_ref,
                 zbuf, *, tiles_per_seq, alpha):
    i = pl.program_id(0)
    tm, d = x_ref.shape
    x = x_ref[...]
    shift, scale, gate = _split_mod(mod_ref[0], d)
    h = (x * scale + shift).astype(BF16)
    hw = jnp.dot(h, win_ref[...], preferred_element_type=F32)
    gb = hw[:, :d]
    z = hw[:, d:2 * d] * hw[:, 2 * d:]

    @pl.when(i % tiles_per_seq == 0)
    def _():
        zbuf[0:SUBLANES, :] = jnp.zeros((SUBLANES, d), F32)

    zbuf[SUBLANES:, :] = z
    z1 = zbuf[pl.ds(SUBLANES - 1, tm), :]
    z2 = zbuf[pl.ds(SUBLANES - 2, tm), :]
    cw = cw_ref[...]
    zc = cw[2:3] * z + cw[1:2] * z1 + cw[0:1] * z2
    zbuf[0:SUBLANES, :] = zbuf[tm:tm + SUBLANES, :]
    y = jnp.dot((gb * zc).astype(BF16), wout_ref[...], preferred_element_type=F32)
    _emit_stage_output(alpha * x + gate * y, g_ref, b_ref, modp_ref, o_ref, hp_ref)


def _conv_layer(x, mods, mod_row, peer_row, w_in, conv_w, w_out, ln_g, ln_b, *, seq, alpha, tm=512):
    t, d = x.shape
    tiles_per_seq = seq // tm
    return pl.pallas_call(
        functools.partial(_conv_kernel, tiles_per_seq=tiles_per_seq, alpha=alpha),
        grid=(t // tm,),
        in_specs=[
            pl.BlockSpec((tm, d), lambda i: (i, 0)),
            pl.BlockSpec((1, 1, 3 * d), lambda i: (mod_row + i // tiles_per_seq, 0, 0)),
            pl.BlockSpec((1, 1, 3 * d), lambda i: (peer_row + i // tiles_per_seq, 0, 0)),
            _const_spec((d, 3 * d)),
            _const_spec((conv_w.shape[0], d)),
            _const_spec((d, d)),
            _const_spec((1, d)),
            _const_spec((1, d)),
        ],
        out_specs=[pl.BlockSpec((tm, d), lambda i: (i, 0))] * 2,
        out_shape=[jax.ShapeDtypeStruct((t, d), F32), jax.ShapeDtypeStruct((t, d), BF16)],
        scratch_shapes=[pltpu.VMEM((tm + SUBLANES, d), F32)],
        compiler_params=_params(("arbitrary",)),
        name="conv_mixer",
    )(x, mods, mods, w_in.astype(BF16), conv_w, w_out.astype(BF16), ln_g, ln_b)


def _qkv_kernel(x_ref, mod_ref, w_ref, qt_ref, k_ref, vt_ref, *, q_scale):
    tm, d = x_ref.shape
    shift, scale, _ = _split_mod(mod_ref[0], d)
    h = (x_ref[...] * scale + shift).astype(BF16)
    qkv = jnp.dot(h, w_ref[...], preferred_element_type=F32)
    qt_ref[...] = (qkv[:, :d] * q_scale).T.astype(BF16)
    k_ref[...] = qkv[:, d:2 * d].astype(BF16)
    vt_ref[...] = qkv[:, 2 * d:].T.astype(BF16)


def _qkv_proj(x, mods, mod_row, w_qkv, *, seq, q_scale, tm=512):
    t, d = x.shape
    tiles_per_seq = seq // tm
    return pl.pallas_call(
        functools.partial(_qkv_kernel, q_scale=q_scale),
        grid=(t // tm,),
        in_specs=[
            pl.BlockSpec((tm, d), lambda i: (i, 0)),
            pl.BlockSpec((1, 1, 3 * d), lambda i: (mod_row + i // tiles_per_seq, 0, 0)),
            _const_spec((d, 3 * d)),
        ],
        out_specs=[
            pl.BlockSpec((d, tm), lambda i: (0, i)),
            pl.BlockSpec((tm, d), lambda i: (i, 0)),
            pl.BlockSpec((d, tm), lambda i: (0, i)),
        ],
        out_shape=[
            jax.ShapeDtypeStruct((d, t), BF16),
            jax.ShapeDtypeStruct((t, d), BF16),
            jax.ShapeDtypeStruct((d, t), BF16),
        ],
        compiler_params=_params(("arbitrary",)),
        name="attn_qkv",
    )(x, mods, w_qkv.astype(BF16))


def _attn_kernel(slopes_ref, qt_ref, k_ref, vt_ref, lam_ref, g_ref, o_ref, qa_ref, m_ref, acc_ref,
                 kn_ref, *, tq, tk, heads, lambda_init):
    hg = pl.program_id(1)
    qi = pl.program_id(2)
    dh2 = qt_ref.shape[0] // heads
    dh = dh2 // 2
    ones_rows = 2 * SUBLANES
    blocks_per_tile = tq // tk

    feat = lax.broadcasted_iota(jnp.int32, (dh2, tq), 0)
    arow = lax.broadcasted_iota(jnp.int32, (dh2, 2 * tq), 0)
    acol = lax.broadcasted_iota(jnp.int32, (dh2, 2 * tq), 1)
    r = jnp.where(acol >= tq, acol - tq, acol)
    r_lo = (r % BF16_EXACT_INT).astype(F32)
    r_hi = (r - r % BF16_EXACT_INT).astype(F32)
    for g in range(heads):
        slope = slopes_ref[hg * heads + g]
        qt = qt_ref[g * dh2:(g + 1) * dh2, :]
        zero = jnp.zeros_like(qt)
        qa_ref[g, :dh2, :] = jnp.concatenate(
            [jnp.where(feat < dh, qt, zero), jnp.where(feat >= dh, qt, zero)], axis=1)
        qa_ref[g, dh2:, :] = jnp.where(
            arow == 0, -slope * r_lo,
            jnp.where(arow == 1, -slope * r_hi, jnp.where(arow <= 3, slope, 0.0))).astype(BF16)
    kcol = lax.broadcasted_iota(jnp.int32, (tk, dh2), 1)
    krow = lax.broadcasted_iota(jnp.int32, (tk, dh2), 0)
    c_lo = (krow % BF16_EXACT_INT).astype(F32)
    c_hi = (krow - krow % BF16_EXACT_INT).astype(F32)
    k_extra = jnp.where(kcol <= 1, 1.0,
                        jnp.where(kcol == 2, c_lo, jnp.where(kcol == 3, c_hi, 0.0))).astype(BF16)
    ones = jnp.ones((ones_rows, tk), BF16)

    m_ref[...] = jnp.full(m_ref.shape, -jnp.inf, F32)
    acc_ref[...] = jnp.zeros(acc_ref.shape, F32)

    @pl.when(qi == 0)
    def _():
        ones_sq = jnp.ones((dh2, LANES), BF16)
        for g in range(heads):
            def chunk(c, best, g=g):
                kc = k_ref[pl.ds(pl.multiple_of(c * tq, tq), tq), g * dh2:(g + 1) * dh2].astype(F32)
                rows = jnp.dot((kc * kc).astype(BF16), ones_sq, preferred_element_type=F32)
                return jnp.maximum(best, jnp.max(rows, axis=0, keepdims=True))
            kn_ref[g] = lax.fori_loop(0, k_ref.shape[0] // tq, chunk, jnp.zeros((1, LANES), F32))

    def step(j, diagonal):
        start = pl.multiple_of(j * tk, tk)
        delta = qi * tq - j * tk
        head_cols = [slice(g * dh2, (g + 1) * dh2) for g in range(heads)]
        scores, maxima = [], []
        for g, cols in enumerate(head_cols):
            s = jnp.dot(jnp.concatenate([k_ref[pl.ds(start, tk), cols], k_extra], axis=1),
                        qa_ref[g], preferred_element_type=F32)
            if diagonal:
                mrow = lax.broadcasted_iota(jnp.int32, (tk, 2 * tq), 0)
                mcol = lax.broadcasted_iota(jnp.int32, (tk, 2 * tq), 1)
                mq = jnp.where(mcol >= tq, mcol - tq, mcol)
                s = jnp.where(mrow - mq <= delta, s, -jnp.inf)
            scores.append(s)
            maxima.append(jnp.max(s, axis=0, keepdims=True))
        probs, corrs = [], []
        for g, s in enumerate(scores):
            off = delta.astype(F32) * slopes_ref[hg * heads + g]
            m_old = m_ref[g]
            m_new = jnp.maximum(m_old, maxima[g] - off)
            probs.append(jnp.exp(s - (m_new + off)).astype(BF16))
            corrs.append(jnp.exp(m_old - m_new))
            m_ref[g] = m_new
        for g, cols in enumerate(head_cols):
            v_aug = jnp.concatenate([vt_ref[cols, pl.ds(start, tk)], ones], axis=0)
            acc_ref[g] = corrs[g] * acc_ref[g] + jnp.dot(v_aug, probs[g],
                                                         preferred_element_type=F32)

    first_diag = qi * blocks_per_tile
    for d in range(blocks_per_tile):
        step(first_diag + d, True)

    needed = jnp.zeros((1, 2 * tq), F32)
    bound = jnp.zeros((1, 2 * tq), F32)
    for g in range(heads):
        qf = qa_ref[g, :dh2, :].astype(F32)
        q_norm = jnp.sqrt(jnp.sum(qf * qf, axis=0, keepdims=True))
        k_norm = jnp.sqrt(kn_ref[g][:, :1])
        qk = q_norm * k_norm * NORM_SLACK
        reach = (qk - m_ref[g] + EXP_UNDERFLOW) / slopes_ref[hg * heads + g]
        needed = jnp.maximum(needed, jnp.floor((reach - 1.0) / tk) + 1.0)
        bound = jnp.maximum(bound, qk)
    needed = jnp.clip(needed, 0.0, first_diag.astype(F32))
    n_blocks = jnp.max(needed.astype(jnp.int32))

    fixed_shift_ok = jnp.max(bound) * 2.0 < FIXED_SHIFT_LIMIT

    def fixed_shift_step(j):
        start = pl.multiple_of(j * tk, tk)
        delta = qi * tq - j * tk
        scores = [jnp.dot(jnp.concatenate([k_ref[pl.ds(start, tk), g * dh2:(g + 1) * dh2], k_extra],
                                          axis=1), qa_ref[g], preferred_element_type=F32)
                  for g in range(heads)]
        probs = [jnp.exp(s - (m_ref[g] + delta.astype(F32) * slopes_ref[hg * heads + g])).astype(BF16)
                 for g, s in enumerate(scores)]
        for g in range(heads):
            v_aug = jnp.concatenate([vt_ref[g * dh2:(g + 1) * dh2, pl.ds(start, tk)], ones], axis=0)
            acc_ref[g] += jnp.dot(v_aug, probs[g], preferred_element_type=F32)

    @pl.when(fixed_shift_ok)
    def _():
        def body(i, carry):
            fixed_shift_step(first_diag - 1 - i)
            return carry

        lax.fori_loop(0, n_blocks, body, 0)

    @pl.when(jnp.logical_not(fixed_shift_ok))
    def _():
        def body(i, carry):
            step(first_diag - 1 - i, False)
            return carry

        lax.fori_loop(0, n_blocks, body, 0)

    lam = lam_ref[...]
    lam_full = (jnp.exp(jnp.sum(lam[0:1] * lam[1:2], axis=1, keepdims=True))
                - jnp.exp(jnp.sum(lam[2:3] * lam[3:4], axis=1, keepdims=True)) + lambda_init)
    for g in range(heads):
        acc = acc_ref[g]
        on = acc[:dh2] / acc[dh2:dh2 + 1]
        o = on[:, :tq] - lam_full * on[:, tq:]
        o = o * lax.rsqrt(jnp.mean(o * o, axis=0, keepdims=True) + LN_EPS) * g_ref[...]
        o_ref[:, g * dh2:(g + 1) * dh2] = (o * (1.0 - lambda_init)).T.astype(o_ref.dtype)


def _diff_attention(qt, k, vt, lam, subln_g, *, batch, seq, lambda_init, tq=512, tk=512, heads=4):
    d, t = qt.shape
    dh2 = d // ATT_HEADS
    gw = heads * dh2
    nq = seq // tq
    assert tq % tk == 0 and tq < BF16_EXACT_INT * BF16_EXACT_INT
    slopes = 2.0 ** (-(8.0 / ATT_HEADS) * jnp.arange(1, ATT_HEADS + 1, dtype=F32))
    return pl.pallas_call(
        functools.partial(_attn_kernel, tq=tq, tk=tk, heads=heads, lambda_init=lambda_init),
        grid=(batch, ATT_HEADS // heads, nq),
        in_specs=[
            pl.BlockSpec(memory_space=pltpu.SMEM),
            pl.BlockSpec((gw, tq), lambda b, h, i: (h, b * nq + i)),
            pl.BlockSpec((seq, gw), lambda b, h, i: (b, h)),
            pl.BlockSpec((gw, seq), lambda b, h, i: (h, b)),
            pl.BlockSpec(lam.shape, lambda b, h, i: (0, 0)),
            pl.BlockSpec((dh2, 1), lambda b, h, i: (0, 0)),
        ],
        out_specs=pl.BlockSpec((tq, gw), lambda b, h, i: (b * nq + i, h)),
        out_shape=jax.ShapeDtypeStruct((t, d), BF16),
        scratch_shapes=[
            pltpu.VMEM((heads, 2 * dh2, 2 * tq), BF16),
            pltpu.VMEM((heads, 1, 2 * tq), F32),
            pltpu.VMEM((heads, dh2 + 2 * SUBLANES, 2 * tq), F32),
            pltpu.VMEM((heads, 1, LANES), F32),
        ],
        compiler_params=_params(("arbitrary", "arbitrary", "arbitrary")),
        name="diff_attention",
    )(slopes, qt, k, vt, lam, subln_g.reshape(dh2, 1))


def _oproj_kernel(o_ref, x_ref, mod_ref, modp_ref, w_ref, g_ref, b_ref, out_ref, hp_ref, *, alpha):
    tm, d = x_ref.shape
    _, _, gate = _split_mod(mod_ref[0], d)
    y = jnp.dot(o_ref[...], w_ref[...], preferred_element_type=F32)
    _emit_stage_output(alpha * x_ref[...] + gate * y, g_ref, b_ref, modp_ref, out_ref, hp_ref)


def _out_proj(o, x, mods, mod_row, peer_row, w_o, ln_g, ln_b, *, seq, alpha, tm=512):
    t, d = x.shape
    tiles_per_seq = seq // tm
    return pl.pallas_call(
        functools.partial(_oproj_kernel, alpha=alpha),
        grid=(t // tm,),
        in_specs=[
            pl.BlockSpec((tm, d), lambda i: (i, 0)),
            pl.BlockSpec((tm, d), lambda i: (i, 0)),
            pl.BlockSpec((1, 1, 3 * d), lambda i: (mod_row + i // tiles_per_seq, 0, 0)),
            pl.BlockSpec((1, 1, 3 * d), lambda i: (peer_row + i // tiles_per_seq, 0, 0)),
            _const_spec((d, d)),
            _const_spec((1, d)),
            _const_spec((1, d)),
        ],
        out_specs=[pl.BlockSpec((tm, d), lambda i: (i, 0))] * 2,
        out_shape=[jax.ShapeDtypeStruct((t, d), F32), jax.ShapeDtypeStruct((t, d), BF16)],
        compiler_params=_params(("arbitrary",)),
        name="attn_out_proj",
    )(o, x, mods, mods, w_o.astype(BF16), ln_g, ln_b)


def _top16_rows(s):
    n, tm = s.shape
    rows = lax.broadcasted_iota(jnp.int32, (n, tm), 0).astype(F32)
    vals, idxs = [], []
    tie = None
    for _ in range(PEER_TOPK):
        m = jnp.max(s, axis=0, keepdims=True)
        if tie is not None:
            m = m + tie
        idx = jnp.min(jnp.where(s == m, rows, float(n)), axis=0, keepdims=True)
        s = jnp.where(rows == idx, -jnp.inf, s)
        vals.append(m)
        idxs.append(idx)
        tie = yield
    return vals, idxs


def _candidate_layout(tm):
    slot = lax.broadcasted_iota(jnp.int32, (N_CAND, tm), 0)
    grp = slot // SUBLANES
    r = slot % SUBLANES
    a3 = jnp.where(r < 5, 2, 4)
    b3 = jnp.where(r < 5, r, r - 5)
    a4 = jnp.where(r < 4, 3, jnp.where(r < 6, 5, 6))
    b4 = jnp.where(r < 4, r, jnp.where(r < 6, r - 4, r - 6))
    a = jnp.where(grp <= 1, 0, jnp.where(grp == 2, 1, jnp.where(grp == 3, a3, jnp.where(
        grp == 4, a4, jnp.where(grp == 5, 7, SUBLANES + r)))))
    b = jnp.where(grp == 1, SUBLANES + r, jnp.where(grp == 3, b3, jnp.where(
        grp == 4, b4, jnp.where(grp == 6, 0, r))))
    valid = (a + 1) * (b + 1) <= PEER_TOPK
    flat = (a * PEER_TOPK + b).astype(F32)
    return valid, flat


def _candidates(rows0, rows1):
    lo1 = jnp.concatenate(rows1[:SUBLANES], axis=0)
    hi1 = jnp.concatenate(rows1[SUBLANES:], axis=0)
    hi0 = jnp.concatenate(rows0[SUBLANES:], axis=0)
    r = lax.broadcasted_iota(jnp.int32, lo1.shape, 0)

    def shifted(k):
        return pltpu.roll(lo1, k, 0)

    groups = [
        rows0[0] + lo1,
        rows0[0] + hi1,
        rows0[1] + lo1,
        jnp.where(r < 5, rows0[2] + lo1, rows0[4] + shifted(5)),
        jnp.where(r < 4, rows0[3] + lo1,
                  jnp.where(r < 6, rows0[5] + shifted(4), rows0[6] + shifted(6))),
        rows0[7] + lo1,
        hi0 + rows1[0],
    ]
    return jnp.concatenate(groups, axis=0)


def _route_head(q_head, keys_ref, valid, flat):
    scores = [lax.dot_general(keys_ref[p], q_head[:, p * PEER_KEYS:(p + 1) * PEER_KEYS],
                              (((1,), (1,)), ((), ())), preferred_element_type=F32)
              for p in range(2)]
    yield
    vals, idxs = [], []
    for st in scores:
        v, ix = yield from _top16_rows(st)
        vals.append(v)
        idxs.append(ix)
    cand = jnp.where(valid, _candidates(vals[0], vals[1]), -jnp.inf)
    code = _candidates([ix * float(PEER_KEYS) for ix in idxs[0]], idxs[1])
    top_s, top_code = [], []
    tie = None
    for _ in range(PEER_TOPK):
        m = jnp.max(cand, axis=0, keepdims=True)
        if tie is not None:
            m = m + tie
        fmin = jnp.min(jnp.where(cand == m, flat, float(PEER_TOPK * PEER_TOPK)),
                       axis=0, keepdims=True)
        hit = flat == fmin
        top_code.append(jnp.sum(jnp.where(hit, code, 0.0), axis=0, keepdims=True))
        cand = jnp.where(hit, -jnp.inf, cand)
        top_s.append(m)
        tie = yield
    ts = jnp.concatenate(top_s, axis=0)
    e = jnp.exp(ts - top_s[0])
    return (jnp.concatenate(top_code, axis=0).astype(jnp.int32),
            e / jnp.sum(e, axis=0, keepdims=True))


ROUTE_YIELDS = 3 * PEER_TOPK


def _zero_row(x, width):
    bits = lax.bitcast_convert_type(x[:1, :LANES], jnp.uint32)
    zero = lax.bitcast_convert_type((bits >> 16) >> 16, F32)
    return jnp.concatenate([zero] * (width // LANES), axis=1)


def _peer_kernel(h_ref, hn_ref, x_ref, mod_ref, wq_ref, keys_ref, ut_ref, v_ref, g_ref, b_ref,
                 o_ref, q_ref, code_ref, gate_ref, ci_ref, cg_ref, w_ref, acc_ref, hs_ref,
                 *, alpha):
    i = pl.program_id(0)
    j = pl.program_id(1)
    nj = pl.num_programs(1)
    tm, d = x_ref.shape
    tn = ut_ref.shape[1]
    nhk = PEER_HEADS * PEER_TOPK
    parts = ROUTE_UNITS // PEER_HEADS
    tp = tm // parts
    slot = i % 2
    valid, flat = _candidate_layout(tp)

    def prepare_queries(h):
        q = jnp.dot(h, wq_ref[...], preferred_element_type=F32).astype(BF16)
        for hd in range(PEER_HEADS):
            q_ref[hd] = q[:, hd * 2 * PEER_KEYS:(hd + 1) * 2 * PEER_KEYS]

    def route_unit(u, dst):
        hd = u // parts
        part = u % parts
        rows = pl.ds(pl.multiple_of(part * tp, tp), tp)
        code, gate = yield from _route_head(q_ref[hd, rows, :], keys_ref, valid, flat)
        krows = pl.ds(pl.multiple_of(hd * PEER_TOPK, PEER_TOPK), PEER_TOPK)
        for g in range(tp // LANES):
            lanes = slice(g * LANES, (g + 1) * LANES)
            code_ref[dst, part * (tp // LANES) + g, krows, :] = code[:, lanes]
            gate_ref[dst, part * (tp // LANES) + g, krows, :] = gate[:, lanes]

    @pl.when((i == 0) & (j == 0))
    def _():
        prepare_queries(h_ref[...])

        def unit(u, carry):
            pending = [route_unit(2 * u, 0), route_unit(2 * u + 1, 0)]
            while pending:
                pending = [r for r in pending if next(r, pending) is not pending]
            return carry

        lax.fori_loop(0, ROUTE_UNITS // 2, unit, 0)

    @pl.when(j == 0)
    def _():
        for grp in range(tm // LANES):
            code = code_ref[slot, grp].T
            ci_ref[grp * LANES:(grp + 1) * LANES, :] = code // PEER_KEYS
            cg_ref[grp * LANES:(grp + 1) * LANES, :] = gate_ref[slot, grp].T
        key_rows = lax.broadcasted_iota(jnp.int32, (PEER_KEYS, nhk), 0).astype(BF16)
        key_lanes = lax.broadcasted_iota(jnp.int32, (nhk, PEER_KEYS), 1).astype(BF16)
        zero = jnp.zeros((PEER_KEYS, nhk), BF16)
        one = jnp.ones((nhk, PEER_KEYS), BF16)

        def token_group(grp, carry):
            base = pl.multiple_of(grp * LANES, LANES)
            second = (code_ref[slot, grp] % PEER_KEYS).astype(F32)

            def gate_bits(u):
                ri = ci_ref[pl.ds(base + u, 1), :].astype(BF16)
                rg = cg_ref[pl.ds(base + u, 1), :].astype(BF16)
                cj = jnp.broadcast_to(second[:, u:u + 1], (nhk, PEER_KEYS)).astype(BF16)
                pt = jnp.where(key_rows == ri, jnp.broadcast_to(rg, zero.shape), zero)
                qm = jnp.where(key_lanes == cj, one, jnp.zeros_like(one))
                wt = jnp.dot(pt, qm, preferred_element_type=F32)
                return lax.bitcast_convert_type(wt, jnp.uint32)

            for u in range(0, LANES, 2):
                packed = (gate_bits(u) >> 16) | (gate_bits(u + 1) & jnp.uint32(0xFFFF0000))
                pair = (base + u) // 2
                w_ref[pl.ds(pl.multiple_of(pair * W_PITCH, SUBLANES), PEER_KEYS), :] = packed
            return carry

        lax.fori_loop(0, tm // LANES, token_group, 0)
        acc_ref[...] = jnp.zeros(acc_ref.shape, F32)
        hs_ref[...] = h_ref[...]
        prepare_queries(hn_ref[...])

    route = route_unit(j, 1 - slot)
    next(route)
    chunk = 4 * PEER_KEYS
    n_chunks = tn // chunk

    def advance(after, count):
        try:
            route.send(None if after is None else _zero_row(after, tp))
            for _ in range(count - 1):
                next(route)
        except StopIteration:
            pass

    halves = 4
    th = tm // halves
    per_piece = -(-ROUTE_YIELDS // (2 * halves * n_chunks))

    def expert_inputs(half):
        rows = slice(half * th, (half + 1) * th)
        h = hs_ref[rows, :]
        pieces = []
        for r in range(n_chunks):
            c0 = r * chunk
            a = jnp.dot(h, us[:, c0:c0 + chunk], preferred_element_type=F32)
            advance(a, per_piece)
            act = (a * (1.0 + lax.erf(a * (1.0 / math.sqrt(2.0))))).astype(BF16)
            for e in range(chunk // PEER_KEYS):
                first_key = (j * n_chunks + r) * (chunk // PEER_KEYS) + e
                words = w_ref[pl.ds(half * (th // 2) * W_PITCH + first_key, th // 2, stride=W_PITCH), :]
                gates = pltpu.bitcast(words, BF16)
                pieces.append(gates * act[:, e * PEER_KEYS:(e + 1) * PEER_KEYS])
        return jnp.concatenate(pieces, axis=1)

    def expert_outputs(half, z):
        rows = slice(half * th, (half + 1) * th)
        for c in range(d // chunk):
            cols = slice(c * chunk, (c + 1) * chunk)
            zv = jnp.dot(z, vs[:, cols], preferred_element_type=F32)
            acc_ref[rows, cols] += zv
            advance(zv, per_piece)

    us = pltpu.bitcast(ut_ref[...], BF16)
    vs = pltpu.bitcast(v_ref[...], BF16)
    advance(None, per_piece)
    z_prev = expert_inputs(0)
    for half in range(1, halves):
        z_next = expert_inputs(half)
        expert_outputs(half - 1, z_prev)
        z_prev = z_next
    expert_outputs(halves - 1, z_prev)
    for _ in route:
        pass

    @pl.when(j == nj - 1)
    def _():
        _, _, gate = _split_mod(mod_ref[0], d)
        o_ref[...] = _layer_norm(alpha * x_ref[...] + gate * acc_ref[...], g_ref[...], b_ref[...])


def _peer_layer(h, x, mods, mod_row, w_q, sub_keys, u_t, v, ln_g, ln_b, *, seq, alpha, tm=512):
    t, d = x.shape
    n = u_t.shape[1]
    tn = n // ROUTE_UNITS
    nhk = PEER_HEADS * PEER_TOPK
    parts = ROUTE_UNITS // PEER_HEADS
    tiles_per_seq = seq // tm
    last = t // tm - 1
    return pl.pallas_call(
        functools.partial(_peer_kernel, alpha=alpha),
        grid=(t // tm, n // tn),
        in_specs=[
            pl.BlockSpec((tm, d), lambda i, j: (i, 0)),
            pl.BlockSpec((tm, d), lambda i, j: (jnp.minimum(i + 1, last), 0)),
            pl.BlockSpec((tm, d), lambda i, j: (i, 0)),
            pl.BlockSpec((1, 1, 3 * d), lambda i, j: (mod_row + i // tiles_per_seq, 0, 0)),
            _const_spec(w_q.shape),
            _const_spec(sub_keys.shape),
            pl.BlockSpec((d // 2, tn), lambda i, j: (0, j)),
            pl.BlockSpec((tn // 2, d), lambda i, j: (j, 0)),
            _const_spec((1, d)),
            _const_spec((1, d)),
        ],
        out_specs=pl.BlockSpec((tm, d), lambda i, j: (i, 0)),
        out_shape=jax.ShapeDtypeStruct((t, d), F32),
        scratch_shapes=[
            pltpu.VMEM((PEER_HEADS, tm, 2 * PEER_KEYS), BF16),
            pltpu.VMEM((2, tm // LANES, nhk, LANES), jnp.int32),
            pltpu.VMEM((2, tm // LANES, nhk, LANES), F32),
            pltpu.VMEM((tm, nhk), jnp.int32),
            pltpu.VMEM((tm, nhk), F32),
            pltpu.VMEM((tm // 2 * W_PITCH, PEER_KEYS), jnp.uint32),
            pltpu.VMEM((tm, d), F32),
            pltpu.VMEM((tm, d), BF16),
        ],
        compiler_params=_params(("arbitrary", "arbitrary")),
        name="peer_layer",
    )(h, h, x, mods, w_q.astype(BF16), sub_keys.astype(BF16), u_t, v, ln_g, ln_b)


def _pack_kernel(w_ref, o_ref, t_ref, *, transpose, scale):
    w = w_ref[0]
    if scale != 1.0:
        w = w * scale
    if transpose:
        w = w.T
    rows, cols = w.shape
    for c in range(cols // LANES):
        t_ref[c] = w[:, c * LANES:(c + 1) * LANES]
        even = t_ref[c, pl.ds(0, rows // 2, stride=2), :].astype(BF16).astype(F32)
        odd = t_ref[c, pl.ds(1, rows // 2, stride=2), :].astype(BF16).astype(F32)
        o_ref[:, c * LANES:(c + 1) * LANES] = (
            (lax.bitcast_convert_type(even, jnp.uint32) >> 16)
            | (lax.bitcast_convert_type(odd, jnp.uint32) & jnp.uint32(0xFFFF0000)))


def _packed_rows(w, layer, *, transpose, scale=1.0, tb=2048):
    _, n, d = w.shape
    if transpose:
        out_shape, out_block, out_map = (d // 2, n), (d // 2, tb), (lambda i: (0, i))
        t_shape = (tb // LANES, d, LANES)
    else:
        out_shape, out_block, out_map = (n // 2, d), (tb // 2, d), (lambda i: (i, 0))
        t_shape = (d // LANES, tb, LANES)
    return pl.pallas_call(
        functools.partial(_pack_kernel, transpose=transpose, scale=scale),
        grid=(n // tb,),
        in_specs=[pl.BlockSpec((1, tb, d), lambda i: (layer, i, 0))],
        out_specs=pl.BlockSpec(out_block, out_map),
        out_shape=jax.ShapeDtypeStruct(out_shape, jnp.uint32),
        scratch_shapes=[pltpu.VMEM(t_shape, F32)],
        compiler_params=_params(("arbitrary",)),
        name="pack_weights",
    )(w)


def _lambda_init(layer_idx):
    return 0.8 - 0.6 * math.exp(-0.3 * layer_idx)


def kernel(x, c, ada_w, ada_b, ln_g, ln_b, conv_w_in, conv_w, conv_w_out, attn_w_qkv, attn_lambda,
           attn_subln_g, attn_w_o, peer_w_q, peer_sub_keys, peer_u, peer_v):
    batch, seq, d = x.shape
    depth = ada_w.shape[0]
    alpha = (2.0 * depth) ** 0.25
    head_dim = d // (2 * ATT_HEADS)
    assert 2 * head_dim == LANES, "one attention head must fill one 128-lane block"
    assert peer_u.shape[1] == PEER_KEYS * PEER_KEYS and peer_sub_keys.shape[2:] == (PEER_KEYS, PEER_KEYS)
    assert peer_w_q.shape[2] == PEER_HEADS * 2 * PEER_KEYS
    mods = _ada_mods(c, ada_w, ada_b)
    xt = x.reshape(batch * seq, d)
    for i in range(depth):
        j = i // N_MIXERS
        row = (2 * i) * batch
        prow = (2 * i + 1) * batch
        g0, b0 = ln_g[i, 0].reshape(1, d), ln_b[i, 0].reshape(1, d)
        if i % N_MIXERS == 0:
            xt, h = _conv_layer(xt, mods, row, prow, conv_w_in[j], conv_w[j], conv_w_out[j], g0, b0,
                                seq=seq, alpha=alpha)
        else:
            qt, k, vt = _qkv_proj(xt, mods, row, attn_w_qkv[j], seq=seq, q_scale=head_dim ** -0.5)
            o = _diff_attention(qt, k, vt, attn_lambda[j], attn_subln_g[j], batch=batch, seq=seq,
                                lambda_init=_lambda_init(i))
            xt, h = _out_proj(o, xt, mods, row, prow, attn_w_o[j], g0, b0, seq=seq, alpha=alpha)
        g1, b1 = ln_g[i, 1].reshape(1, d), ln_b[i, 1].reshape(1, d)
        xt = _peer_layer(h, xt, mods, prow, peer_w_q[i], peer_sub_keys[i],
                         _packed_rows(peer_u, i, transpose=True),
                         _packed_rows(peer_v, i, transpose=False, scale=0.5), g1, b1,
                         seq=seq, alpha=alpha)
    return xt.reshape(batch, seq, d)
```
